```python
import jax, jax.numpy as jnp
from jax import lax
import numpy as np

D_MODEL = 1024
BATCH = 2
SEQ = 8192
DEPTH = 2
DEC_BATCH = 128
DEC_SEQ = 4
PAST_LEN = 16384
PAGE_SIZE = 128

N_A = DEPTH // 2
N_B = DEPTH - N_A
N_DENSE = (DEPTH + 1) // 2
N_MOE = DEPTH // 2

LRU_WIDTH = D_MODEL
LRU_BLOCK_W = 256
LRU_BLOCKS = LRU_WIDTH // LRU_BLOCK_W
CONV_W = 4
LRU_C = 8.0

HEAD_DIM = 64
N_HEADS = D_MODEL // HEAD_DIM
N_KV = max(1, N_HEADS // 8)
GROUP = N_HEADS // N_KV
WINDOW = 128

NUM_BUCKETS = 32
MAX_DISTANCE = 128

D_FF = 3 * D_MODEL
N_EXPERTS = 8
TOP_K = 2
D_EXPERT = 7 * D_MODEL // 2

EPS = 1e-6
NEG = -1e30

kernel_name = "yoco_rglru_swa_sink_step"


def rmsnorm(x, g):
    xf = x.astype(jnp.float32)
    y = xf * lax.rsqrt(jnp.mean(xf * xf, axis=-1, keepdims=True) + EPS)
    return (y * g.astype(jnp.float32)).astype(x.dtype)


def swiglu(x, w_gate, w_up, w_down):
    return (jax.nn.silu(x @ w_gate) * (x @ w_up)) @ w_down


def moe_swiglu(x, w_router, w_gate, w_up, w_down):
    b, t, d = x.shape
    xt = x.reshape(b * t, d)
    logits = (xt @ w_router).astype(jnp.float32)
    top_val, top_idx = lax.top_k(logits, TOP_K)
    top_w = jax.nn.softmax(top_val, axis=-1)
    gates = jnp.sum(jax.nn.one_hot(top_idx, N_EXPERTS, dtype=jnp.float32) * top_w[..., None], axis=1)
    out = jnp.zeros((b * t, d), jnp.float32)
    for e in range(N_EXPERTS):
        out = out + gates[:, e:e + 1] * swiglu(xt, w_gate[e], w_up[e], w_down[e]).astype(jnp.float32)
    return out.astype(x.dtype).reshape(b, t, d)


def _lru_combine(c1, c2):
    a1, b1 = c1
    a2, b2 = c2
    return a1 * a2, a2 * b1 + b2


def rglru_block(x, conv_state, h0, w_gate, b_gate, w_in, b_in, conv_w, conv_b,
                w_r, b_r, w_i, b_i, lam, w_out, b_out):
    b, t, _ = x.shape
    gate = jax.nn.gelu(x @ w_gate + b_gate)
    xr = x @ w_in + b_in
    xpad = jnp.concatenate([conv_state.astype(xr.dtype), xr], axis=1)
    xc = conv_b + sum(conv_w[k] * xpad[:, k:k + t] for k in range(CONV_W))
    new_conv = xpad[:, -(CONV_W - 1):]
    xb = xc.reshape(b, t, LRU_BLOCKS, LRU_BLOCK_W)
    r = jax.nn.sigmoid(jnp.einsum('btnc,ncd->btnd', xb, w_r).reshape(b, t, LRU_WIDTH) + b_r)
    i = jax.nn.sigmoid(jnp.einsum('btnc,ncd->btnd', xb, w_i).reshape(b, t, LRU_WIDTH) + b_i)
    log_a = LRU_C * r.astype(jnp.float32) * jax.nn.log_sigmoid(lam.astype(jnp.float32))
    a = jnp.exp(log_a)
    mult = jnp.sqrt(-jnp.expm1(2.0 * log_a))
    u = mult * (i * xc).astype(jnp.float32)
    u = u.at[:, 0].add(a[:, 0] * h0.astype(jnp.float32))
    _, h = lax.associative_scan(_lru_combine, (a, u), axis=1)
    y = (h.astype(x.dtype) * gate) @ w_out + b_out
    return y, new_conv, h[:, -1].astype(h0.dtype)


def rel_bucket(dist):
    max_exact = NUM_BUCKETS // 2
    n = jnp.maximum(dist, 0)
    nf = jnp.maximum(n, max_exact).astype(jnp.float32)
    large = max_exact + (jnp.log(nf / max_exact) / np.log(MAX_DISTANCE / max_exact)
                         * (NUM_BUCKETS - max_exact)).astype(jnp.int32)
    large = jnp.minimum(large, NUM_BUCKETS - 1)
    return jnp.where(n < max_exact, n, large)


def band_bias_mask(rel_table, q_len):
    s_len = WINDOW + q_len
    dist = jnp.arange(q_len)[:, None] + WINDOW - jnp.arange(s_len)[None, :]
    band = (dist >= 0) & (dist < WINDOW)
    bias = rel_table.astype(jnp.float32)[rel_bucket(dist)]
    bias = jnp.transpose(bias, (2, 0, 1)).reshape(N_KV, GROUP, q_len, s_len)
    return bias, band


def window_attention(q, k_band, v_band, key_valid, rel_table, sinks):
    b, n, q_len = q.shape[:3]
    qg = q.reshape(b, n, q_len, N_KV, GROUP, HEAD_DIM).astype(jnp.float32)
    s = jnp.einsum('bnqkgd,bnskd->bnkgqs', qg, k_band.astype(jnp.float32)) * (HEAD_DIM ** -0.5)
    bias, band = band_bias_mask(rel_table, q_len)
    mask = band[None, :, :] & key_valid[:, None, :]
    s = jnp.where(mask[None, :, None, None], s + bias, NEG)
    sink = sinks.astype(jnp.float32).reshape(N_KV, GROUP)[..., None]
    m = jnp.maximum(jnp.max(s, axis=-1), sink)
    p = jnp.exp(s - m[..., None])
    denom = jnp.sum(p, axis=-1) + jnp.exp(sink - m)
    o = jnp.einsum('bnkgqs,bnskd->bnqkgd', p / denom[..., None], v_band.astype(jnp.float32))
    return o.reshape(b, n, q_len, N_HEADS * HEAD_DIM).astype(q.dtype)


def trunk(x, conv0, h0, past_k, past_v, p):
    b, t, _ = x.shape
    h = x
    new_conv, new_h = [], []
    k_band = v_band = key_valid = None
    new_k = new_v = None
    for layer in range(DEPTH):
        u = rmsnorm(h, p['g_mix'][layer])
        if layer < N_A:
            j = layer
            y, c_new, h_new = rglru_block(
                u, conv0[j], h0[j], p['a_w_gate'][j], p['a_b_gate'][j], p['a_w_in'][j], p['a_b_in'][j],
                p['a_conv_w'][j], p['a_conv_b'][j], p['a_w_r'][j], p['a_b_r'][j], p['a_w_i'][j],
                p['a_b_i'][j], p['a_lam'][j], p['a_w_out'][j], p['a_b_out'][j])
            new_conv.append(c_new)
            new_h.append(h_new)
        else:
            j = layer - N_A
            if j == 0:
                kv = rmsnorm(h, p['g_kv']) @ p['w_kv'] + p['b_kv']
                k = kv[..., :N_KV * HEAD_DIM].reshape(b, t, N_KV, HEAD_DIM)
                v = kv[..., N_KV * HEAD_DIM:].reshape(b, t, N_KV, HEAD_DIM)
                if past_k is None:
                    nb = t // WINDOW
                    kb = k.reshape(b, nb, WINDOW, N_KV, HEAD_DIM)
                    vb = v.reshape(b, nb, WINDOW, N_KV, HEAD_DIM)
                    k_prev = jnp.concatenate([jnp.zeros_like(kb[:, :1]), kb[:, :-1]], axis=1)
                    v_prev = jnp.concatenate([jnp.zeros_like(vb[:, :1]), vb[:, :-1]], axis=1)
                    k_band = jnp.concatenate([k_prev, kb], axis=2)
                    v_band = jnp.concatenate([v_prev, vb], axis=2)
                    key_pos = jnp.arange(nb)[:, None] * WINDOW - WINDOW + jnp.arange(2 * WINDOW)[None, :]
                    key_valid = key_pos >= 0
                    new_k, new_v = k[:, -WINDOW:], v[:, -WINDOW:]
                else:
                    k_all = jnp.concatenate([past_k.astype(k.dtype), k], axis=1)
                    v_all = jnp.concatenate([past_v.astype(v.dtype), v], axis=1)
                    k_band, v_band = k_all[:, None], v_all[:, None]
                    key_valid = jnp.ones((1, WINDOW + t), dtype=bool)
                    new_k, new_v = k_all[:, -WINDOW:], v_all[:, -WINDOW:]
            n_blk = k_band.shape[1]
            q = (u @ p['b_w_q'][j] + p['b_b_q'][j]).reshape(b, n_blk, t // n_blk, N_HEADS, HEAD_DIM)
            o = window_attention(q, k_band, v_band, key_valid, p['rel_bias'], p['b_sinks'][j])
            y = o.reshape(b, t, N_HEADS * HEAD_DIM) @ p['b_w_o'][j] + p['b_b_o'][j]
        h = h + y
        u = rmsnorm(h, p['g_ffn'][layer])
        f = layer // 2
        if layer % 2 == 0:
            h = h + swiglu(u, p['f_w_gate'][f], p['f_w_up'][f], p['f_w_down'][f])
        else:
            h = h + moe_swiglu(u, p['m_w_router'][f], p['m_w_gate'][f], p['m_w_up'][f], p['m_w_down'][f])
    y_out = rmsnorm(h, p['g_final'])
    return y_out, jnp.stack(new_conv, axis=0), jnp.stack(new_h, axis=0), new_k, new_v


def setup_inputs(seed: int = 0) -> dict:
    key = jax.random.key(seed)
    ks = iter(jax.random.split(key, 64))
    f32 = jnp.float32

    def nrm(shape, scale):
        return jax.random.normal(next(ks), shape, f32) * scale

    D, L = D_MODEL, LRU_WIDTH
    QD, KVD = N_HEADS * HEAD_DIM, N_KV * HEAD_DIM
    lam_u = jax.random.uniform(next(ks), (N_A, L), f32, minval=0.9, maxval=0.999)
    lam_s = lam_u ** (1.0 / LRU_C)
    a_lam = jnp.log(lam_s) - jnp.log1p(-lam_s)
    return {
        'x_prompt': nrm((BATCH, SEQ, D), 1.0),
        'x_sample': nrm((DEC_BATCH, DEC_SEQ, D), 1.0),
        'state_conv': nrm((N_A, DEC_BATCH, CONV_W - 1, L), 1.0),
        'state_h': nrm((N_A, DEC_BATCH, L), 0.5),
        'cache_k': nrm((DEC_BATCH, WINDOW, N_KV, HEAD_DIM), 1.0),
        'cache_v': nrm((DEC_BATCH, WINDOW, N_KV, HEAD_DIM), 1.0),
        'g_mix': 1.0 + nrm((DEPTH, D), 0.05),
        'g_ffn': 1.0 + nrm((DEPTH, D), 0.05),
        'g_kv': 1.0 + nrm((D,), 0.05),
        'g_final': 1.0 + nrm((D,), 0.05),
        'a_w_gate': nrm((N_A, D, L), D ** -0.5),
        'a_b_gate': nrm((N_A, L), 0.02),
        'a_w_in': nrm((N_A, D, L), D ** -0.5),
        'a_b_in': nrm((N_A, L), 0.02),
        'a_conv_w': nrm((N_A, CONV_W, L), CONV_W ** -0.5),
        'a_conv_b': nrm((N_A, L), 0.02),
        'a_w_r': nrm((N_A, LRU_BLOCKS, LRU_BLOCK_W, LRU_BLOCK_W), LRU_BLOCK_W ** -0.5),
        'a_b_r': nrm((N_A, L), 0.1),
        'a_w_i': nrm((N_A, LRU_BLOCKS, LRU_BLOCK_W, LRU_BLOCK_W), LRU_BLOCK_W ** -0.5),
        'a_b_i': nrm((N_A, L), 0.1),
        'a_lam': a_lam,
        'a_w_out': nrm((N_A, L, D), L ** -0.5),
        'a_b_out': nrm((N_A, D), 0.02),
        'w_kv': nrm((D, 2 * KVD), D ** -0.5),
        'b_kv': nrm((2 * KVD,), 0.02),
        'rel_bias': nrm((NUM_BUCKETS, N_HEADS), 0.5),
        'b_w_q': nrm((N_B, D, QD), D ** -0.5),
        'b_b_q': nrm((N_B, QD), 0.02),
        'b_sinks': nrm((N_B, N_HEADS), 1.0),
        'b_w_o': nrm((N_B, QD, D), QD ** -0.5),
        'b_b_o': nrm((N_B, D), 0.02),
        'f_w_gate': nrm((N_DENSE, D, D_FF), D ** -0.5),
        'f_w_up': nrm((N_DENSE, D, D_FF), D ** -0.5),
        'f_w_down': nrm((N_DENSE, D_FF, D), D_FF ** -0.5),
        'm_w_router': nrm((N_MOE, D, N_EXPERTS), D ** -0.5),
        'm_w_gate': nrm((N_MOE, N_EXPERTS, D, D_EXPERT), D ** -0.5),
        'm_w_up': nrm((N_MOE, N_EXPERTS, D, D_EXPERT), D ** -0.5),
        'm_w_down': nrm((N_MOE, N_EXPERTS, D_EXPERT, D), D_EXPERT ** -0.5),
    }


def reference(x_prompt, x_sample, state_conv, state_h, cache_k, cache_v,
              g_mix, g_ffn, g_kv, g_final,
              a_w_gate, a_b_gate, a_w_in, a_b_in, a_conv_w, a_conv_b,
              a_w_r, a_b_r, a_w_i, a_b_i, a_lam, a_w_out, a_b_out,
              w_kv, b_kv, rel_bias, b_w_q, b_b_q, b_sinks, b_w_o, b_b_o,
              f_w_gate, f_w_up, f_w_down,
              m_w_router, m_w_gate, m_w_up, m_w_down):
    p = dict(g_mix=g_mix, g_ffn=g_ffn, g_kv=g_kv, g_final=g_final,
             a_w_gate=a_w_gate, a_b_gate=a_b_gate, a_w_in=a_w_in, a_b_in=a_b_in,
             a_conv_w=a_conv_w, a_conv_b=a_conv_b, a_w_r=a_w_r, a_b_r=a_b_r,
             a_w_i=a_w_i, a_b_i=a_b_i, a_lam=a_lam, a_w_out=a_w_out, a_b_out=a_b_out,
             w_kv=w_kv, b_kv=b_kv, rel_bias=rel_bias,
             b_w_q=b_w_q, b_b_q=b_b_q, b_sinks=b_sinks, b_w_o=b_w_o, b_b_o=b_b_o,
             f_w_gate=f_w_gate, f_w_up=f_w_up, f_w_down=f_w_down,
             m_w_router=m_w_router, m_w_gate=m_w_gate, m_w_up=m_w_up, m_w_down=m_w_down)
    b_p = x_prompt.shape[0]
    conv0_p = jnp.zeros((N_A, b_p, CONV_W - 1, LRU_WIDTH), x_prompt.dtype)
    h0_p = jnp.zeros((N_A, b_p, LRU_WIDTH), x_prompt.dtype)
    y_prompt, p_conv, p_h, p_k, p_v = trunk(x_prompt, conv0_p, h0_p, None, None, p)
    y_sample, s_conv, s_h, s_k, s_v = trunk(x_sample, state_conv, state_h, cache_k, cache_v, p)
    return (y_prompt, y_sample, p_conv, p_h, p_k, p_v, s_conv, s_h, s_k, s_v)
```

```python
import math

import jax
import jax.numpy as jnp
from jax import lax
from jax.experimental import pallas as pl
from jax.experimental.pallas import tpu as pltpu

D_MODEL = 1024
LRU_WIDTH = D_MODEL
LRU_BLOCK_W = 256
LRU_BLOCKS = LRU_WIDTH // LRU_BLOCK_W
CONV_W = 4
LRU_C = 8.0
HEAD_DIM = 64
N_HEADS = D_MODEL // HEAD_DIM
N_KV = 2
GROUP = N_HEADS // N_KV
KV_DIM = N_KV * HEAD_DIM
WINDOW = 128
NUM_BUCKETS = 32
MAX_DISTANCE = 128
D_FF = 3 * D_MODEL
N_EXPERTS = 8
TOP_K = 2
D_EXPERT = 7 * D_MODEL // 2
EPS = 1e-6
NEG = -1e30

BF16 = jnp.bfloat16
F32 = jnp.float32
I32 = jnp.int32

SUBLANES = 8
LANES = 128
VMEM_LIMIT_BYTES = 56 * 1024 * 1024

ROW_TILE = 512
MIX_TILE = 256
FF_CHUNK = 1024
MOE_TILE = 512
MOE_CHUNKS = 2
DISPATCH_TILE = 256


def _params(*semantics):
    return pltpu.CompilerParams(dimension_semantics=semantics, vmem_limit_bytes=VMEM_LIMIT_BYTES)


def _const_spec(shape):
    zeros = (0,) * len(shape)
    return pl.BlockSpec(shape, lambda *_: zeros, pipeline_mode=pl.Buffered(1))


def _dot(a, b):
    return jnp.dot(a, b, preferred_element_type=F32)


def _rms(x, g):
    ms = jnp.mean(x * x, axis=-1, keepdims=True)
    return x * lax.rsqrt(ms + EPS) * g


def _sigmoid(x):
    return 1.0 / (1.0 + jnp.exp(-x))


def _gelu_tanh(x):
    return 0.5 * x * (1.0 + jnp.tanh(0.7978845608028654 * (x + 0.044715 * (x * x * x))))


def _log_sigmoid(x):
    return jnp.minimum(x, 0.0) - jnp.log1p(jnp.exp(-jnp.abs(x)))


def _lru_gates(xc, wr_ref, br, wi_ref, bi, lam):
    xcb = xc.astype(BF16)
    rs, gs = [], []
    for n in range(LRU_BLOCKS):
        xn = xcb[:, n * LRU_BLOCK_W:(n + 1) * LRU_BLOCK_W]
        rs.append(_dot(xn, wr_ref[n]))
        gs.append(_dot(xn, wi_ref[n]))
    r = _sigmoid(jnp.concatenate(rs, axis=1) + br)
    i = _sigmoid(jnp.concatenate(gs, axis=1) + bi)
    log_a = LRU_C * r * _log_sigmoid(lam)
    a = jnp.exp(log_a)
    mult = jnp.sqrt(1.0 - a * a)
    return a, mult * (i * xc)


def _mixer_prompt_kernel(x_ref, g_ref, wg_ref, bg_ref, win_ref, bin_ref, cw_ref, cb_ref,
                         wr_ref, br_ref, wi_ref, bi_ref, lam_ref, wout_ref, bout_ref,
                         h1_ref, conv_ref, hlast_ref, xr_buf, h_carry):
    t = pl.program_id(1)
    tt = x_ref.shape[1]
    pad = SUBLANES

    @pl.when(t == 0)
    def _():
        xr_buf[0:pad, :] = jnp.zeros((pad, LRU_WIDTH), F32)
        h_carry[...] = jnp.zeros_like(h_carry)

    x = x_ref[0]
    u = _rms(x, g_ref[...]).astype(BF16)
    gate = _gelu_tanh(_dot(u, wg_ref[...]) + bg_ref[...])
    xr = _dot(u, win_ref[...]) + bin_ref[...]
    xr_buf[pad:pad + tt, :] = xr
    xc = cb_ref[...] + cw_ref[CONV_W - 1:CONV_W, :] * xr
    for k in range(CONV_W - 1):
        back = CONV_W - 1 - k
        xc = xc + cw_ref[k:k + 1, :] * xr_buf[pad - back:pad - back + tt, :]
    xr_buf[0:pad, :] = xr[tt - pad:tt, :]
    conv_ref[0] = xr[tt - (CONV_W - 1):tt, :]

    a, b = _lru_gates(xc, wr_ref, br_ref[...], wi_ref, bi_ref[...], lam_ref[...])

    groups = tt // SUBLANES
    a3 = a.reshape(groups, SUBLANES, LRU_WIDTH)
    b3 = b.reshape(groups, SUBLANES, LRU_WIDTH)
    row = lax.broadcasted_iota(I32, (1, SUBLANES, LRU_WIDTH), 1)
    step = 1
    while step < SUBLANES:
        keep = row >= step
        a_prev = jnp.where(keep, pltpu.roll(a3, step, axis=1), 1.0)
        b_prev = jnp.where(keep, pltpu.roll(b3, step, axis=1), 0.0)
        b3 = b3 + a3 * b_prev
        a3 = a3 * a_prev
        step *= 2
    h_prev = h_carry[0:1, :]
    hs = []
    for gi in range(groups):
        hg = b3[gi] + a3[gi] * h_prev
        hs.append(hg)
        h_prev = hg[SUBLANES - 1:SUBLANES, :]
    h = jnp.concatenate(hs, axis=0)
    h_carry[0:1, :] = h_prev
    hlast_ref[0] = h_prev

    y = _dot((h * gate).astype(BF16), wout_ref[...]) + bout_ref[...]
    h1_ref[...] = x + y


def _mixer_prompt(x, w):
    b, t, d = x.shape
    tt = MIX_TILE
    nt = t // tt
    vec = lambda n: _const_spec((1, n))
    in_specs = [
        pl.BlockSpec((1, tt, d), lambda bi, ti: (bi, ti, 0)),
        vec(d), _const_spec((d, LRU_WIDTH)), vec(LRU_WIDTH), _const_spec((d, LRU_WIDTH)), vec(LRU_WIDTH),
        _const_spec((CONV_W, LRU_WIDTH)), vec(LRU_WIDTH),
        _const_spec((LRU_BLOCKS, LRU_BLOCK_W, LRU_BLOCK_W)), vec(LRU_WIDTH),
        _const_spec((LRU_BLOCKS, LRU_BLOCK_W, LRU_BLOCK_W)), vec(LRU_WIDTH),
        vec(LRU_WIDTH), _const_spec((LRU_WIDTH, d)), vec(d),
    ]
    out_specs = [
        pl.BlockSpec((tt, d), lambda bi, ti: (bi * nt + ti, 0)),
        pl.BlockSpec((1, CONV_W - 1, LRU_WIDTH), lambda bi, ti: (bi, 0, 0)),
        pl.BlockSpec((1, 1, LRU_WIDTH), lambda bi, ti: (bi, 0, 0)),
    ]
    out_shape = [
        jax.ShapeDtypeStruct((b * t, d), F32),
        jax.ShapeDtypeStruct((b, CONV_W - 1, LRU_WIDTH), F32),
        jax.ShapeDtypeStruct((b, 1, LRU_WIDTH), F32),
    ]
    return pl.pallas_call(
        _mixer_prompt_kernel,
        grid=(b, nt),
        in_specs=in_specs,
        out_specs=out_specs,
        out_shape=out_shape,
        scratch_shapes=[pltpu.VMEM((SUBLANES + tt, LRU_WIDTH), F32), pltpu.VMEM((SUBLANES, LRU_WIDTH), F32)],
        compiler_params=_params("arbitrary", "arbitrary"),
        name="mixer_prompt",
    )(x, w["g"], w["wg"], w["bg"], w["win"], w["bin"], w["cw"], w["cb"], w["wr"], w["br"],
      w["wi"], w["bi"], w["lam"], w["wout"], w["bout"])


def _mixer_sample_kernel(x_ref, cs_ref, h0_ref, g_ref, wg_ref, bg_ref, win_ref, bin_ref, cw_ref, cb_ref,
                         wr_ref, br_ref, wi_ref, bi_ref, lam_ref, wout_ref, bout_ref,
                         h1_ref, conv_ref, hlast_ref):
    steps, nb, d = x_ref.shape
    x = x_ref[...].reshape(steps * nb, d)
    u = _rms(x, g_ref[...]).astype(BF16)
    gate = _gelu_tanh(_dot(u, wg_ref[...]) + bg_ref[...])
    xr = _dot(u, win_ref[...]) + bin_ref[...]
    xpad = [cs_ref[k] for k in range(CONV_W - 1)] + [xr[s * nb:(s + 1) * nb, :] for s in range(steps)]
    xcs = []
    for s in range(steps):
        acc = cb_ref[...] + cw_ref[0:1, :] * xpad[s]
        for k in range(1, CONV_W):
            acc = acc + cw_ref[k:k + 1, :] * xpad[s + k]
        xcs.append(acc)
    for k in range(CONV_W - 1):
        conv_ref[k] = xpad[steps + k]
    xc = jnp.concatenate(xcs, axis=0)
    a, b = _lru_gates(xc, wr_ref, br_ref[...], wi_ref, bi_ref[...], lam_ref[...])
    h = h0_ref[...]
    hs = []
    for s in range(steps):
        h = a[s * nb:(s + 1) * nb, :] * h + b[s * nb:(s + 1) * nb, :]
        hs.append(h)
    hlast_ref[...] = h
    hcat = jnp.concatenate(hs, axis=0)
    y = _dot((hcat * gate).astype(BF16), wout_ref[...]) + bout_ref[...]
    h1_ref[...] = (x + y).reshape(steps, nb, d)


def _mixer_sample(x_tm, cs_tm, h0, w):
    steps, nb, d = x_tm.shape
    full = lambda shape: pl.BlockSpec(shape, lambda i: (0,) * len(shape))
    vec = lambda n: full((1, n))
    in_specs = [
        full((steps, nb, d)), full((CONV_W - 1, nb, LRU_WIDTH)), full((nb, LRU_WIDTH)),
        vec(d), full((d, LRU_WIDTH)), vec(LRU_WIDTH), full((d, LRU_WIDTH)), vec(LRU_WIDTH),
        full((CONV_W, LRU_WIDTH)), vec(LRU_WIDTH),
        full((LRU_BLOCKS, LRU_BLOCK_W, LRU_BLOCK_W)), vec(LRU_WIDTH),
        full((LRU_BLOCKS, LRU_BLOCK_W, LRU_BLOCK_W)), vec(LRU_WIDTH),
        vec(LRU_WIDTH), full((LRU_WIDTH, d)), vec(d),
    ]
    out_specs = [full((steps, nb, d)), full((CONV_W - 1, nb, LRU_WIDTH)), full((nb, LRU_WIDTH))]
    out_shape = [
        jax.ShapeDtypeStruct((steps, nb, d), F32),
        jax.ShapeDtypeStruct((CONV_W - 1, nb, LRU_WIDTH), F32),
        jax.ShapeDtypeStruct((nb, LRU_WIDTH), F32),
    ]
    return pl.pallas_call(
        _mixer_sample_kernel, grid=(1,), in_specs=in_specs, out_specs=out_specs, out_shape=out_shape,
        compiler_params=_params("arbitrary"), name="mixer_sample",
    )(x_tm, cs_tm, h0, w["g"], w["wg"], w["bg"], w["win"], w["bin"], w["cw"], w["cb"], w["wr"], w["br"],
      w["wi"], w["bi"], w["lam"], w["wout"], w["bout"])


def _two_part_specs(n_prompt, n_sample, width):
    assert n_sample == ROW_TILE and n_prompt % ROW_TILE == 0
    last_prompt = n_prompt // ROW_TILE - 1
    return [pl.BlockSpec((ROW_TILE, width), lambda i: (jnp.minimum(i, last_prompt), 0)),
            pl.BlockSpec((ROW_TILE, width), lambda i: (0, 0))]


def _two_part_tile(prompt_ref, sample_ref):
    is_sample = pl.program_id(0) == pl.num_programs(0) - 1
    return jnp.where(is_sample, sample_ref[...], prompt_ref[...])


def _ffn_kvq_kernel(h1p_ref, h1s_ref, gf_ref, wg_ref, wu_ref, wd_ref, gkv_ref, wkv_ref, bkv_ref,
                    gq_ref, wq_ref, bq_ref, h2_ref, kv_ref, q_ref):
    h1 = _two_part_tile(h1p_ref, h1s_ref)
    u = _rms(h1, gf_ref[...]).astype(BF16)
    acc = h1
    for c in range(D_FF // FF_CHUNK):
        cols = slice(c * FF_CHUNK, (c + 1) * FF_CHUNK)
        g = _dot(u, wg_ref[:, cols])
        up = _dot(u, wu_ref[:, cols])
        mid = (g * _sigmoid(g) * up).astype(BF16)
        acc = acc + _dot(mid, wd_ref[cols, :])
    h2_ref[...] = acc
    kv_ref[...] = _dot(_rms(acc, gkv_ref[...]).astype(BF16), wkv_ref[...]) + bkv_ref[...]
    q = _dot(_rms(acc, gq_ref[...]).astype(BF16), wq_ref[...]) + bq_ref[...]
    q_ref[...] = (q * (HEAD_DIM ** -0.5)).astype(BF16)


def _ffn_kvq(h1_p, h1_s, w):
    d = h1_p.shape[1]
    n = h1_p.shape[0] + h1_s.shape[0]
    tm = ROW_TILE
    row = lambda width: pl.BlockSpec((tm, width), lambda i: (i, 0))
    vec = lambda width: _const_spec((1, width))
    in_specs = _two_part_specs(h1_p.shape[0], h1_s.shape[0], d) + [
        vec(d), _const_spec((d, D_FF)), _const_spec((d, D_FF)), _const_spec((D_FF, d)),
        vec(d), _const_spec((d, 2 * KV_DIM)), vec(2 * KV_DIM),
        vec(d), _const_spec((d, d)), vec(d),
    ]
    return pl.pallas_call(
        _ffn_kvq_kernel,
        grid=(n // tm,),
        in_specs=in_specs,
        out_specs=[row(d), row(2 * KV_DIM), row(d)],
        out_shape=[jax.ShapeDtypeStruct((n, d), F32), jax.ShapeDtypeStruct((n, 2 * KV_DIM), F32),
                   jax.ShapeDtypeStruct((n, d), BF16)],
        compiler_params=_params("arbitrary"),
        name="ffn_kvq",
    )(h1_p, h1_s, w["gf"], w["wg"], w["wu"], w["wd"], w["gkv"], w["wkv"], w["bkv"], w["gq"], w["wq"], w["bq"])


def _bias_band_kernel(table_ref, out_ref):
    _, q_len, s_len = out_ref.shape
    qi = lax.broadcasted_iota(I32, (q_len, s_len), 0)
    si = lax.broadcasted_iota(I32, (q_len, s_len), 1)
    dist = qi + WINDOW - si
    max_exact = NUM_BUCKETS // 2
    n = jnp.maximum(dist, 0)
    nf = jnp.maximum(n, max_exact).astype(F32)
    log_ratio = math.log(MAX_DISTANCE / max_exact)
    large = max_exact + (jnp.log(nf / max_exact) / log_ratio * (NUM_BUCKETS - max_exact)).astype(I32)
    large = jnp.minimum(large, NUM_BUCKETS - 1)
    bucket = jnp.where(n < max_exact, n, large)
    for h in range(N_HEADS):
        acc = jnp.zeros((q_len, s_len), F32)
        for bkt in range(NUM_BUCKETS):
            acc = jnp.where(bucket == bkt, table_ref[bkt, h], acc)
        out_ref[h] = acc


def _bias_band(rel_table):
    shape = (N_HEADS, WINDOW, 2 * WINDOW)
    return pl.pallas_call(
        _bias_band_kernel,
        grid=(1,),
        in_specs=[pl.BlockSpec(memory_space=pltpu.SMEM)],
        out_specs=pl.BlockSpec(shape, lambda i: (0, 0, 0)),
        out_shape=jax.ShapeDtypeStruct(shape, F32),
        compiler_params=_params("arbitrary"),
        name="bias_band",
    )(rel_table)


def _softmax_sink_pv(s, sink, v):
    m = jnp.maximum(jnp.max(s, axis=-1, keepdims=True), sink)
    p = jnp.exp(s - m)
    denom = jnp.sum(p, axis=-1, keepdims=True) + jnp.exp(sink - m)
    return _dot(p.astype(BF16), v) / denom


def _attn_prompt_kernel(sink_ref, q_ref, kvp_ref, kvc_ref, bias_ref, o_ref):
    nblk = pl.program_id(1)
    w = WINDOW
    qi = lax.broadcasted_iota(I32, (w, 2 * w), 0)
    si = lax.broadcasted_iota(I32, (w, 2 * w), 1)
    dist = qi + w - si
    mask = (dist >= 0) & (dist < w) & ((si >= w) | (nblk > 0))
    kv_prev = kvp_ref[...].astype(BF16)
    kv_cur = kvc_ref[...].astype(BF16)
    kv = jnp.concatenate([kv_prev, kv_cur], axis=0)
    q = q_ref[...]
    for j in range(N_KV):
        k = kv[:, j * HEAD_DIM:(j + 1) * HEAD_DIM]
        v = kv[:, KV_DIM + j * HEAD_DIM:KV_DIM + (j + 1) * HEAD_DIM]
        for pair in range(GROUP // 2):
            outs = []
            for sub in range(2):
                h = j * GROUP + 2 * pair + sub
                qh = q[:, h * HEAD_DIM:(h + 1) * HEAD_DIM]
                s = lax.dot_general(qh, k, (((1,), (1,)), ((), ())), preferred_element_type=F32)
                s = jnp.where(mask, s + bias_ref[h], NEG)
                outs.append(_softmax_sink_pv(s, sink_ref[h], v))
            h0 = j * GROUP + 2 * pair
            o_ref[:, h0 * HEAD_DIM:(h0 + 2) * HEAD_DIM] = jnp.concatenate(outs, axis=1).astype(BF16)


def _attn_prompt(q, kv, bias, sinks, batch, seq):
    nb = seq // WINDOW
    d = q.shape[1]
    in_specs = [
        pl.BlockSpec(memory_space=pltpu.SMEM),
        pl.BlockSpec((WINDOW, d), lambda b, n: (b * nb + n, 0)),
        pl.BlockSpec((WINDOW, 2 * KV_DIM), lambda b, n: (b * nb + jnp.maximum(n - 1, 0), 0)),
        pl.BlockSpec((WINDOW, 2 * KV_DIM), lambda b, n: (b * nb + n, 0)),
        _const_spec((N_HEADS, WINDOW, 2 * WINDOW)),
    ]
    return pl.pallas_call(
        _attn_prompt_kernel,
        grid=(batch, nb),
        in_specs=in_specs,
        out_specs=pl.BlockSpec((WINDOW, d), lambda b, n: (b * nb + n, 0)),
        out_shape=jax.ShapeDtypeStruct((batch * seq, d), BF16),
        compiler_params=_params("arbitrary", "arbitrary"),
        name="attn_prompt",
    )(sinks, q, kv, kv, bias)


SAMPLE_GROUP = 8


def _attn_sample_kernel(q_ref, ck_ref, cv_ref, kn_ref, vn_ref, bias_ref, sink_ref, o_ref):
    gb = q_ref.shape[0]
    steps = kn_ref.shape[1]
    rows = steps * GROUP
    pad = (-steps) % SUBLANES
    keys = WINDOW + steps + pad
    ri = lax.broadcasted_iota(I32, (rows, keys), 0)
    si = lax.broadcasted_iota(I32, (rows, keys), 1)
    dist = ri // GROUP + WINDOW - si
    mask = (dist >= 0) & (dist < WINDOW)
    zeros = jnp.zeros((pad, KV_DIM), F32)
    for b in range(gb):
        k_all = jnp.concatenate([ck_ref[b], kn_ref[b], zeros], axis=0).astype(BF16)
        v_all = jnp.concatenate([cv_ref[b], vn_ref[b], zeros], axis=0).astype(BF16)
        for j in range(N_KV):
            k = k_all[:, j * HEAD_DIM:(j + 1) * HEAD_DIM]
            v = v_all[:, j * HEAD_DIM:(j + 1) * HEAD_DIM]
            s = lax.dot_general(q_ref[b, j], k, (((1,), (1,)), ((), ())), preferred_element_type=F32)
            s = jnp.where(mask, s + bias_ref[j], NEG)
            o_ref[b, j] = _softmax_sink_pv(s, sink_ref[j], v).astype(BF16)


def _attn_sample(q4, ck, cv, kn, vn, bias_rows, sink_rows):
    nb, _, rows, hd = q4.shape
    steps = kn.shape[1]
    keys = bias_rows.shape[2]
    gb = SAMPLE_GROUP
    in_specs = [
        pl.BlockSpec((gb, N_KV, rows, hd), lambda i: (i, 0, 0, 0)),
        pl.BlockSpec((gb, WINDOW, KV_DIM), lambda i: (i, 0, 0)),
        pl.BlockSpec((gb, WINDOW, KV_DIM), lambda i: (i, 0, 0)),
        pl.BlockSpec((gb, steps, KV_DIM), lambda i: (i, 0, 0)),
        pl.BlockSpec((gb, steps, KV_DIM), lambda i: (i, 0, 0)),
        pl.BlockSpec((N_KV, rows, keys), lambda i: (0, 0, 0)),
        pl.BlockSpec((N_KV, rows, 1), lambda i: (0, 0, 0)),
    ]
    return pl.pallas_call(
        _attn_sample_kernel,
        grid=(nb // gb,),
        in_specs=in_specs,
        out_specs=pl.BlockSpec((gb, N_KV, rows, hd), lambda i: (i, 0, 0, 0)),
        out_shape=jax.ShapeDtypeStruct(q4.shape, BF16),
        compiler_params=_params("arbitrary"),
        name="attn_sample",
    )(q4, ck, cv, kn, vn, bias_rows, sink_rows)


def _split_bf16(x):
    hi = x.astype(BF16)
    lo = (x - hi.astype(F32)).astype(BF16)
    return hi, lo


def _oproj_router_kernel(op_ref, os_ref, h2_ref, wo_ref, bo_ref, gf_ref, wrh_ref, wrl_ref,
                         h3_ref, rinfo_ref, gates_ref, counts_ref, carry):
    i = pl.program_id(0)
    tm = h2_ref.shape[0]

    @pl.when(i == 0)
    def _():
        carry[...] = jnp.zeros_like(carry)

    h3 = h2_ref[...] + _dot(_two_part_tile(op_ref, os_ref), wo_ref[...]) + bo_ref[...]
    h3_ref[...] = h3
    u_hi, u_lo = _split_bf16(_rms(h3, gf_ref[...]))
    logits = _dot(u_hi, wrh_ref[...]) + (_dot(u_lo, wrh_ref[...]) + _dot(u_hi, wrl_ref[...]))

    lane = lax.broadcasted_iota(I32, (tm, LANES), 1)
    lanef = lane.astype(F32)
    logits = jnp.where(lane < N_EXPERTS, logits, -jnp.inf)
    v1 = jnp.max(logits, axis=-1, keepdims=True)
    e1 = jnp.min(jnp.where(logits == v1, lanef, float(LANES)), axis=-1, keepdims=True)
    rest = jnp.where(lanef == e1, -jnp.inf, logits)
    v2 = jnp.max(rest, axis=-1, keepdims=True)
    e2 = jnp.min(jnp.where(rest == v2, lanef, float(LANES)), axis=-1, keepdims=True)
    ex = jnp.exp(v2 - v1)
    w1 = 1.0 / (1.0 + ex)
    w2 = ex / (1.0 + ex)
    pick1 = lanef == e1
    pick2 = lanef == e2
    sel = jnp.where(pick1 | pick2, 1.0, 0.0)

    ri = lax.broadcasted_iota(I32, (tm, tm), 0)
    ci = lax.broadcasted_iota(I32, (tm, tm), 1)
    earlier = jnp.where(ri > ci, 1.0, 0.0).astype(BF16)
    before = _dot(earlier, sel.astype(BF16)) + carry[0:1, :]
    r1 = jnp.sum(jnp.where(pick1, before, 0.0), axis=-1, keepdims=True)
    r2 = jnp.sum(jnp.where(pick2, before, 0.0), axis=-1, keepdims=True)
    carry[0:1, :] = carry[0:1, :] + jnp.sum(sel, axis=0, keepdims=True)
    counts_ref[...] = carry[...]

    info = jnp.where(lane == 0, e1, jnp.where(lane == 1, e2, jnp.where(lane == 2, r1, jnp.where(lane == 3, r2, 0.0))))
    rinfo_ref[...] = info.astype(I32)
    gates_ref[...] = jnp.where(lane == 0, w1, jnp.where(lane == 1, w2, 0.0))


def _oproj_router(o_p, o_s, h2, w):
    n, d = h2.shape
    tm = ROW_TILE
    row = lambda width: pl.BlockSpec((tm, width), lambda i: (i, 0))
    in_specs = _two_part_specs(o_p.shape[0], o_s.shape[0], d) + [
        row(d), _const_spec((d, d)), _const_spec((1, d)), _const_spec((1, d)),
        _const_spec((d, LANES)), _const_spec((d, LANES))]
    return pl.pallas_call(
        _oproj_router_kernel,
        grid=(n // tm,),
        in_specs=in_specs,
        out_specs=[row(d), row(LANES), row(LANES), pl.BlockSpec((SUBLANES, LANES), lambda i: (0, 0))],
        out_shape=[jax.ShapeDtypeStruct((n, d), F32), jax.ShapeDtypeStruct((n, LANES), I32),
                   jax.ShapeDtypeStruct((n, LANES), F32), jax.ShapeDtypeStruct((SUBLANES, LANES), F32)],
        scratch_shapes=[pltpu.VMEM((SUBLANES, LANES), F32)],
        compiler_params=_params("arbitrary"),
        name="oproj_router",
    )(o_p, o_s, h2, w["wo"], w["bo"], w["gf"], w["wr_hi"], w["wr_lo"])


def _row_copy(src_ref, src_row, dst_ref, dst_row, sem):
    return pltpu.make_async_copy(src_ref.at[pl.ds(src_row, 1), :], dst_ref.at[pl.ds(dst_row, 1), :], sem)


def _dispatch_kernel(pos_ref, h3_ref, xs_in_ref, xs_ref, sem):
    del xs_in_ref
    tm = h3_ref.shape[0]

    def start(r, c):
        for k in range(TOP_K):
            _row_copy(h3_ref, r, xs_ref, pos_ref[0, 0, TOP_K * r + k], sem).start()
        return c

    lax.fori_loop(0, tm, start, 0)

    def wait(r, c):
        for k in range(TOP_K):
            _row_copy(h3_ref, r, xs_ref, pos_ref[0, 0, TOP_K * r + k], sem).wait()
        return c

    lax.fori_loop(0, tm, wait, 0)


def _dispatch(h3, pos, xs_init):
    n, d = h3.shape
    tm = DISPATCH_TILE
    pos3 = pos.reshape(n // tm, 1, TOP_K * tm)
    return pl.pallas_call(
        _dispatch_kernel,
        grid=(n // tm,),
        in_specs=[pl.BlockSpec((1, 1, TOP_K * tm), lambda i: (i, 0, 0), memory_space=pltpu.SMEM),
                  pl.BlockSpec((tm, d), lambda i: (i, 0)),
                  pl.BlockSpec(memory_space=pl.ANY)],
        out_specs=pl.BlockSpec(memory_space=pl.ANY),
        out_shape=jax.ShapeDtypeStruct(xs_init.shape, xs_init.dtype),
        scratch_shapes=[pltpu.SemaphoreType.DMA(())],
        input_output_aliases={2: 0},
        compiler_params=_params("arbitrary"),
        name="moe_dispatch",
    )(pos3, h3, xs_init)


def _moe_kernel(te_ref, nu_ref, xs_ref, g_ref, wg_ref, wu_ref, wd_ref, y_ref, xn_ref):
    i = pl.program_id(0)
    c = pl.program_id(1)
    used = i < nu_ref[0]

    @pl.when(used & (c == 0))
    def _():
        xn_ref[...] = _rms(xs_ref[...], g_ref[...]).astype(BF16)

    @pl.when(used)
    def _():
        xn = xn_ref[...]
        g = _dot(xn, wg_ref[0])
        up = _dot(xn, wu_ref[0])
        mid = (g * _sigmoid(g) * up).astype(BF16)
        y = _dot(mid, wd_ref[0])

        @pl.when(c == 0)
        def _():
            y_ref[...] = y

        @pl.when(c > 0)
        def _():
            y_ref[...] = y_ref[...] + y

    @pl.when(jnp.logical_not(used) & (c == 0))
    def _():
        y_ref[...] = jnp.zeros_like(y_ref)


def _moe_experts(xs, tile_expert, n_used, g, wg, wu, wd):
    p_rows, d = xs.shape
    tm = MOE_TILE
    n_tiles = p_rows // tm
    ch = D_EXPERT // MOE_CHUNKS
    last = MOE_CHUNKS - 1

    def chunk(i, c, nu):
        return jnp.where(i < nu[0], c, last)

    grid_spec = pltpu.PrefetchScalarGridSpec(
        num_scalar_prefetch=2,
        grid=(n_tiles, MOE_CHUNKS),
        in_specs=[
            pl.BlockSpec((tm, d), lambda i, c, te, nu: (i, 0)),
            pl.BlockSpec((1, d), lambda i, c, te, nu: (0, 0)),
            pl.BlockSpec((1, d, ch), lambda i, c, te, nu: (te[i], 0, chunk(i, c, nu))),
            pl.BlockSpec((1, d, ch), lambda i, c, te, nu: (te[i], 0, chunk(i, c, nu))),
            pl.BlockSpec((1, ch, d), lambda i, c, te, nu: (te[i], chunk(i, c, nu), 0)),
        ],
        out_specs=pl.BlockSpec((tm, d), lambda i, c, te, nu: (i, 0)),
        scratch_shapes=[pltpu.VMEM((tm, d), BF16)],
    )
    return pl.pallas_call(
        _moe_kernel,
        grid_spec=grid_spec,
        out_shape=jax.ShapeDtypeStruct((p_rows, d), F32),
        compiler_params=_params("arbitrary", "arbitrary"),
        name="moe_experts",
    )(tile_expert, n_used, xs, g, wg, wu, wd)


def _combine_kernel(pos_ref, h3_ref, gates_ref, ys_ref, g_ref, out_ref, ybuf, sem):
    tm = h3_ref.shape[0]

    def start(r, c):
        for k in range(TOP_K):
            _row_copy(ys_ref, pos_ref[0, 0, TOP_K * r + k], ybuf.at[k], r, sem).start()
        return c

    lax.fori_loop(0, tm, start, 0)

    def wait(r, c):
        for k in range(TOP_K):
            _row_copy(ys_ref, pos_ref[0, 0, TOP_K * r + k], ybuf.at[k], r, sem).wait()
        return c

    lax.fori_loop(0, tm, wait, 0)

    gates = gates_ref[...]
    h4 = h3_ref[...] + (gates[:, 0:1] * ybuf[0] + gates[:, 1:2] * ybuf[1])
    out_ref[...] = _rms(h4, g_ref[...])


def _combine_final(h3, gates, pos, ys, g_final):
    n, d = h3.shape
    tm = DISPATCH_TILE
    pos3 = pos.reshape(n // tm, 1, TOP_K * tm)
    return pl.pallas_call(
        _combine_kernel,
        grid=(n // tm,),
        in_specs=[pl.BlockSpec((1, 1, TOP_K * tm), lambda i: (i, 0, 0), memory_space=pltpu.SMEM),
                  pl.BlockSpec((tm, d), lambda i: (i, 0)),
                  pl.BlockSpec((tm, LANES), lambda i: (i, 0)),
                  pl.BlockSpec(memory_space=pl.ANY),
                  pl.BlockSpec((1, d), lambda i: (0, 0))],
        out_specs=pl.BlockSpec((tm, d), lambda i: (i, 0)),
        out_shape=jax.ShapeDtypeStruct((n, d), F32),
        scratch_shapes=[pltpu.VMEM((TOP_K, tm, d), F32), pltpu.SemaphoreType.DMA(())],
        compiler_params=_params("arbitrary"),
        name="moe_combine",
    )(pos3, h3, gates, ys, g_final)


def _row(v):
    return v.reshape(1, -1).astype(F32)


def _routing_tables(rinfo, counts):
    n = rinfo.shape[0]
    n_tiles = (n * TOP_K) // MOE_TILE + N_EXPERTS
    cnt = counts[0, :N_EXPERTS].astype(I32)
    padded = ((cnt + MOE_TILE - 1) // MOE_TILE) * MOE_TILE
    ends = jnp.cumsum(padded)
    offs = ends - padded
    experts = rinfo[:, 0:TOP_K]
    ranks = rinfo[:, TOP_K:2 * TOP_K]
    onehot = experts[..., None] == jnp.arange(N_EXPERTS, dtype=I32)
    pos = jnp.sum(jnp.where(onehot, offs, 0), axis=-1) + ranks
    tile_start = jnp.arange(n_tiles, dtype=I32) * MOE_TILE
    tile_expert = jnp.minimum(jnp.sum((tile_start[:, None] >= ends[None, :]).astype(I32), axis=1), N_EXPERTS - 1)
    n_used = (ends[-1] // MOE_TILE).reshape(1).astype(I32)
    return pos.astype(I32), tile_expert.astype(I32), n_used, n_tiles


def kernel(x_prompt, x_sample, state_conv, state_h, cache_k, cache_v, g_mix, g_ffn, g_kv, g_final, a_w_gate, a_b_gate, a_w_in, a_b_in, a_conv_w, a_conv_b, a_w_r, a_b_r, a_w_i, a_b_i, a_lam, a_w_out, a_b_out, w_kv, b_kv, rel_bias, b_w_q, b_b_q, b_sinks, b_w_o, b_b_o, f_w_gate, f_w_up, f_w_down, m_w_router, m_w_gate, m_w_up, m_w_down):
    bp, seq, d = x_prompt.shape
    bs, steps, _ = x_sample.shape
    n_prompt = bp * seq
    n_sample = bs * steps
    n = n_prompt + n_sample
    assert seq % MIX_TILE == 0 and seq % WINDOW == 0 and n_prompt % ROW_TILE == 0
    assert n_sample == ROW_TILE and n % DISPATCH_TILE == 0 and bs % SAMPLE_GROUP == 0

    mix_w = dict(g=_row(g_mix[0]), wg=a_w_gate[0].astype(BF16), bg=_row(a_b_gate[0]),
                 win=a_w_in[0].astype(BF16), bin=_row(a_b_in[0]), cw=a_conv_w[0], cb=_row(a_conv_b[0]),
                 wr=a_w_r[0].astype(BF16), br=_row(a_b_r[0]), wi=a_w_i[0].astype(BF16), bi=_row(a_b_i[0]),
                 lam=_row(a_lam[0]), wout=a_w_out[0].astype(BF16), bout=_row(a_b_out[0]))

    h1_p, p_conv, p_h = _mixer_prompt(x_prompt, mix_w)
    h1_s, s_conv_tm, s_h = _mixer_sample(jnp.transpose(x_sample, (1, 0, 2)),
                                         jnp.transpose(state_conv[0], (1, 0, 2)), state_h[0], mix_w)
    h1_s = jnp.transpose(h1_s, (1, 0, 2)).reshape(n_sample, d)

    ffn_w = dict(gf=_row(g_ffn[0]), wg=f_w_gate[0].astype(BF16), wu=f_w_up[0].astype(BF16),
                 wd=f_w_down[0].astype(BF16), gkv=_row(g_kv), wkv=w_kv.astype(BF16), bkv=_row(b_kv),
                 gq=_row(g_mix[1]), wq=b_w_q[0].astype(BF16), bq=_row(b_b_q[0]))
    h2, kv, q = _ffn_kvq(h1_p, h1_s, ffn_w)

    bias = _bias_band(rel_bias)
    sinks = b_sinks[0].astype(F32)
    o_p = _attn_prompt(q, kv, bias, sinks, bp, seq)
    keys = WINDOW + steps + (-steps) % SUBLANES
    q4 = q[n_prompt:].reshape(bs, steps, N_KV, GROUP, HEAD_DIM).transpose(0, 2, 1, 3, 4)
    q4 = q4.reshape(bs, N_KV, steps * GROUP, HEAD_DIM)
    kv_s = kv[n_prompt:].reshape(bs, steps, 2 * KV_DIM)
    k_new, v_new = kv_s[:, :, :KV_DIM], kv_s[:, :, KV_DIM:]
    bias_rows = bias[:, :steps, :keys].reshape(N_KV, GROUP, steps, keys).transpose(0, 2, 1, 3)
    bias_rows = bias_rows.reshape(N_KV, steps * GROUP, keys)
    sink_rows = jnp.broadcast_to(sinks.reshape(N_KV, 1, GROUP), (N_KV, steps, GROUP)).reshape(N_KV, steps * GROUP, 1)
    o4 = _attn_sample(q4, cache_k.reshape(bs, WINDOW, KV_DIM), cache_v.reshape(bs, WINDOW, KV_DIM),
                      k_new, v_new, bias_rows, sink_rows)
    o_s = o4.reshape(bs, N_KV, steps, GROUP, HEAD_DIM).transpose(0, 2, 1, 3, 4).reshape(n_sample, d)

    wr_pad = jnp.zeros((d, LANES), F32).at[:, :N_EXPERTS].set(m_w_router[0])
    wr_hi = wr_pad.astype(BF16)
    wr_lo = (wr_pad - wr_hi.astype(F32)).astype(BF16)
    h3, rinfo, gates, counts = _oproj_router(
        o_p, o_s, h2, dict(wo=b_w_o[0].astype(BF16), bo=_row(b_b_o[0]), gf=_row(g_ffn[1]), wr_hi=wr_hi, wr_lo=wr_lo))

    pos, tile_expert, n_used, n_tiles = _routing_tables(rinfo, counts)
    xs = _dispatch(h3, pos, jnp.zeros((n_tiles * MOE_TILE, d), F32))
    ys = _moe_experts(xs, tile_expert, n_used, _row(g_ffn[1]), m_w_gate[0].astype(BF16),
                      m_w_up[0].astype(BF16), m_w_down[0].astype(BF16))
    y = _combine_final(h3, gates, pos, ys, _row(g_final))

    y_prompt = y[:n_prompt].reshape(bp, seq, d)
    y_sample = y[n_prompt:].reshape(bs, steps, d)
    p_k = kv[:n_prompt, :KV_DIM].reshape(bp, seq, N_KV, HEAD_DIM)[:, -WINDOW:]
    p_v = kv[:n_prompt, KV_DIM:].reshape(bp, seq, N_KV, HEAD_DIM)[:, -WINDOW:]
    s_k = jnp.concatenate([cache_k, k_new.reshape(bs, steps, N_KV, HEAD_DIM)], axis=1)[:, -WINDOW:]
    s_v = jnp.concatenate([cache_v, v_new.reshape(bs, steps, N_KV, HEAD_DIM)], axis=1)[:, -WINDOW:]
    return (y_prompt, y_sample, p_conv[None], p_h.reshape(1, bp, LRU_WIDTH), p_k, p_v,
            jnp.transpose(s_conv_tm, (1, 0, 2))[None], s_h[None], s_k, s_v)
```

```python
import math

import jax
import jax.numpy as jnp
from jax import lax
from jax.experimental import pallas as pl
from jax.experimental.pallas import tpu as pltpu

D_MODEL = 1024
LRU_WIDTH = D_MODEL
LRU_BLOCK_W = 256
LRU_BLOCKS = LRU_WIDTH // LRU_BLOCK_W
CONV_W = 4
LRU_C = 8.0
HEAD_DIM = 64
N_HEADS = D_MODEL // HEAD_DIM
N_KV = 2
GROUP = N_HEADS // N_KV
KV_DIM = N_KV * HEAD_DIM
WINDOW = 128
NUM_BUCKETS = 32
MAX_DISTANCE = 128
D_FF = 3 * D_MODEL
N_EXPERTS = 8
TOP_K = 2
D_EXPERT = 7 * D_MODEL // 2
EPS = 1e-6
NEG = -1e30

BF16 = jnp.bfloat16
F32 = jnp.float32
I32 = jnp.int32

SUBLANES = 8
LANES = 128
VMEM_LIMIT_BYTES = 56 * 1024 * 1024

ROW_TILE = 512
MIX_TILE = 256
FF_CHUNK = 1024
MOE_TILE = 512
MOE_CHUNKS = 2


def _params(*semantics):
    return pltpu.CompilerParams(dimension_semantics=semantics, vmem_limit_bytes=VMEM_LIMIT_BYTES)


def _const_spec(shape):
    zeros = (0,) * len(shape)
    return pl.BlockSpec(shape, lambda *_: zeros, pipeline_mode=pl.Buffered(1))


def _dot(a, b):
    return jnp.dot(a, b, preferred_element_type=F32)


def _rms(x, g):
    ms = jnp.mean(x * x, axis=-1, keepdims=True)
    return x * lax.rsqrt(ms + EPS) * g


def _sigmoid(x):
    return 1.0 / (1.0 + jnp.exp(-x))


def _gelu_tanh(x):
    return 0.5 * x * (1.0 + jnp.tanh(0.7978845608028654 * (x + 0.044715 * (x * x * x))))


def _log_sigmoid(x):
    return jnp.minimum(x, 0.0) - jnp.log1p(jnp.exp(-jnp.abs(x)))


def _lru_gates(xc, wr_ref, br, wi_ref, bi, lam):
    xcb = xc.astype(BF16)
    rs, gs = [], []
    for n in range(LRU_BLOCKS):
        xn = xcb[:, n * LRU_BLOCK_W:(n + 1) * LRU_BLOCK_W]
        rs.append(_dot(xn, wr_ref[n]))
        gs.append(_dot(xn, wi_ref[n]))
    r = _sigmoid(jnp.concatenate(rs, axis=1) + br)
    i = _sigmoid(jnp.concatenate(gs, axis=1) + bi)
    log_a = LRU_C * r * _log_sigmoid(lam)
    a = jnp.exp(log_a)
    mult = jnp.sqrt(1.0 - a * a)
    return a, mult * (i * xc)


def _mixer_prompt_kernel(x_ref, g_ref, wg_ref, bg_ref, win_ref, bin_ref, cw_ref, cb_ref,
                         wr_ref, br_ref, wi_ref, bi_ref, lam_ref, wout_ref, bout_ref,
                         h1_ref, conv_ref, hlast_ref, xr_buf, h_carry):
    t = pl.program_id(1)
    tt = x_ref.shape[1]
    pad = SUBLANES

    @pl.when(t == 0)
    def _():
        xr_buf[0:pad, :] = jnp.zeros((pad, LRU_WIDTH), F32)
        h_carry[...] = jnp.zeros_like(h_carry)

    x = x_ref[0]
    u = _rms(x, g_ref[...]).astype(BF16)
    gate = _gelu_tanh(_dot(u, wg_ref[...]) + bg_ref[...])
    xr = _dot(u, win_ref[...]) + bin_ref[...]
    xr_buf[pad:pad + tt, :] = xr
    xc = cb_ref[...] + cw_ref[CONV_W - 1:CONV_W, :] * xr
    for k in range(CONV_W - 1):
        back = CONV_W - 1 - k
        xc = xc + cw_ref[k:k + 1, :] * xr_buf[pad - back:pad - back + tt, :]
    xr_buf[0:pad, :] = xr[tt - pad:tt, :]
    conv_ref[0] = xr[tt - (CONV_W - 1):tt, :]

    a, b = _lru_gates(xc, wr_ref, br_ref[...], wi_ref, bi_ref[...], lam_ref[...])

    groups = tt // SUBLANES
    a3 = a.reshape(groups, SUBLANES, LRU_WIDTH)
    b3 = b.reshape(groups, SUBLANES, LRU_WIDTH)
    row = lax.broadcasted_iota(I32, (1, SUBLANES, LRU_WIDTH), 1)
    step = 1
    while step < SUBLANES:
        keep = row >= step
        a_prev = jnp.where(keep, pltpu.roll(a3, step, axis=1), 1.0)
        b_prev = jnp.where(keep, pltpu.roll(b3, step, axis=1), 0.0)
        b3 = b3 + a3 * b_prev
        a3 = a3 * a_prev
        step *= 2
    h_prev = h_carry[0:1, :]
    hs = []
    for gi in range(groups):
        hg = b3[gi] + a3[gi] * h_prev
        hs.append(hg)
        h_prev = hg[SUBLANES - 1:SUBLANES, :]
    h = jnp.concatenate(hs, axis=0)
    h_carry[0:1, :] = h_prev
    hlast_ref[0] = h_prev

    y = _dot((h * gate).astype(BF16), wout_ref[...]) + bout_ref[...]
    h1_ref[...] = x + y


def _mixer_prompt(x, w):
    b, t, d = x.shape
    tt = MIX_TILE
    nt = t // tt
    vec = lambda n: _const_spec((1, n))
    in_specs = [
        pl.BlockSpec((1, tt, d), lambda bi, ti: (bi, ti, 0)),
        vec(d), _const_spec((d, LRU_WIDTH)), vec(LRU_WIDTH), _const_spec((d, LRU_WIDTH)), vec(LRU_WIDTH),
        _const_spec((CONV_W, LRU_WIDTH)), vec(LRU_WIDTH),
        _const_spec((LRU_BLOCKS, LRU_BLOCK_W, LRU_BLOCK_W)), vec(LRU_WIDTH),
        _const_spec((LRU_BLOCKS, LRU_BLOCK_W, LRU_BLOCK_W)), vec(LRU_WIDTH),
        vec(LRU_WIDTH), _const_spec((LRU_WIDTH, d)), vec(d),
    ]
    out_specs = [
        pl.BlockSpec((tt, d), lambda bi, ti: (bi * nt + ti, 0)),
        pl.BlockSpec((1, CONV_W - 1, LRU_WIDTH), lambda bi, ti: (bi, 0, 0)),
        pl.BlockSpec((1, 1, LRU_WIDTH), lambda bi, ti: (bi, 0, 0)),
    ]
    out_shape = [
        jax.ShapeDtypeStruct((b * t, d), F32),
        jax.ShapeDtypeStruct((b, CONV_W - 1, LRU_WIDTH), F32),
        jax.ShapeDtypeStruct((b, 1, LRU_WIDTH), F32),
    ]
    return pl.pallas_call(
        _mixer_prompt_kernel,
        grid=(b, nt),
        in_specs=in_specs,
        out_specs=out_specs,
        out_shape=out_shape,
        scratch_shapes=[pltpu.VMEM((SUBLANES + tt, LRU_WIDTH), F32), pltpu.VMEM((SUBLANES, LRU_WIDTH), F32)],
        compiler_params=_params("arbitrary", "arbitrary"),
        name="mixer_prompt",
    )(x, w["g"], w["wg"], w["bg"], w["win"], w["bin"], w["cw"], w["cb"], w["wr"], w["br"],
      w["wi"], w["bi"], w["lam"], w["wout"], w["bout"])


def _mixer_sample_kernel(x_ref, cs_ref, h0_ref, g_ref, wg_ref, bg_ref, win_ref, bin_ref, cw_ref, cb_ref,
                         wr_ref, br_ref, wi_ref, bi_ref, lam_ref, wout_ref, bout_ref,
                         h1_ref, conv_ref, hlast_ref):
    steps, nb, d = x_ref.shape
    x = x_ref[...].reshape(steps * nb, d)
    u = _rms(x, g_ref[...]).astype(BF16)
    gate = _gelu_tanh(_dot(u, wg_ref[...]) + bg_ref[...])
    xr = _dot(u, win_ref[...]) + bin_ref[...]
    xpad = [cs_ref[k] for k in range(CONV_W - 1)] + [xr[s * nb:(s + 1) * nb, :] for s in range(steps)]
    xcs = []
    for s in range(steps):
        acc = cb_ref[...] + cw_ref[0:1, :] * xpad[s]
        for k in range(1, CONV_W):
            acc = acc + cw_ref[k:k + 1, :] * xpad[s + k]
        xcs.append(acc)
    for k in range(CONV_W - 1):
        conv_ref[k] = xpad[steps + k]
    xc = jnp.concatenate(xcs, axis=0)
    a, b = _lru_gates(xc, wr_ref, br_ref[...], wi_ref, bi_ref[...], lam_ref[...])
    h = h0_ref[...]
    hs = []
    for s in range(steps):
        h = a[s * nb:(s + 1) * nb, :] * h + b[s * nb:(s + 1) * nb, :]
        hs.append(h)
    hlast_ref[...] = h
    hcat = jnp.concatenate(hs, axis=0)
    y = _dot((hcat * gate).astype(BF16), wout_ref[...]) + bout_ref[...]
    h1_ref[...] = (x + y).reshape(steps, nb, d)


def _mixer_sample(x_tm, cs_tm, h0, w):
    steps, nb, d = x_tm.shape
    full = lambda shape: pl.BlockSpec(shape, lambda i: (0,) * len(shape))
    vec = lambda n: full((1, n))
    in_specs = [
        full((steps, nb, d)), full((CONV_W - 1, nb, LRU_WIDTH)), full((nb, LRU_WIDTH)),
        vec(d), full((d, LRU_WIDTH)), vec(LRU_WIDTH), full((d, LRU_WIDTH)), vec(LRU_WIDTH),
        full((CONV_W, LRU_WIDTH)), vec(LRU_WIDTH),
        full((LRU_BLOCKS, LRU_BLOCK_W, LRU_BLOCK_W)), vec(LRU_WIDTH),
        full((LRU_BLOCKS, LRU_BLOCK_W, LRU_BLOCK_W)), vec(LRU_WIDTH),
        vec(LRU_WIDTH), full((LRU_WIDTH, d)), vec(d),
    ]
    out_specs = [full((steps, nb, d)), full((CONV_W - 1, nb, LRU_WIDTH)), full((nb, LRU_WIDTH))]
    out_shape = [
        jax.ShapeDtypeStruct((steps, nb, d), F32),
        jax.ShapeDtypeStruct((CONV_W - 1, nb, LRU_WIDTH), F32),
        jax.ShapeDtypeStruct((nb, LRU_WIDTH), F32),
    ]
    return pl.pallas_call(
        _mixer_sample_kernel, grid=(1,), in_specs=in_specs, out_specs=out_specs, out_shape=out_shape,
        compiler_params=_params("arbitrary"), name="mixer_sample",
    )(x_tm, cs_tm, h0, w["g"], w["wg"], w["bg"], w["win"], w["bin"], w["cw"], w["cb"], w["wr"], w["br"],
      w["wi"], w["bi"], w["lam"], w["wout"], w["bout"])


def _two_part_specs(n_prompt, n_sample, width):
    assert n_sample == ROW_TILE and n_prompt % ROW_TILE == 0
    last_prompt = n_prompt // ROW_TILE - 1
    return [pl.BlockSpec((ROW_TILE, width), lambda i: (jnp.minimum(i, last_prompt), 0)),
            pl.BlockSpec((ROW_TILE, width), lambda i: (0, 0))]


def _two_part_tile(prompt_ref, sample_ref):
    is_sample = pl.program_id(0) == pl.num_programs(0) - 1
    return jnp.where(is_sample, sample_ref[...], prompt_ref[...])


def _ffn_kvq_kernel(h1p_ref, h1s_ref, gf_ref, wg_ref, wu_ref, wd_ref, gkv_ref, wkv_ref, bkv_ref,
                    gq_ref, wq_ref, bq_ref, h2_ref, kv_ref, q_ref):
    h1 = _two_part_tile(h1p_ref, h1s_ref)
    u = _rms(h1, gf_ref[...]).astype(BF16)
    acc = h1
    for c in range(D_FF // FF_CHUNK):
        cols = slice(c * FF_CHUNK, (c + 1) * FF_CHUNK)
        g = _dot(u, wg_ref[:, cols])
        up = _dot(u, wu_ref[:, cols])
        mid = (g * _sigmoid(g) * up).astype(BF16)
        acc = acc + _dot(mid, wd_ref[cols, :])
    h2_ref[...] = acc
    kv_ref[...] = _dot(_rms(acc, gkv_ref[...]).astype(BF16), wkv_ref[...]) + bkv_ref[...]
    q = _dot(_rms(acc, gq_ref[...]).astype(BF16), wq_ref[...]) + bq_ref[...]
    q_ref[...] = (q * (HEAD_DIM ** -0.5)).astype(BF16)


def _ffn_kvq(h1_p, h1_s, w):
    d = h1_p.shape[1]
    n = h1_p.shape[0] + h1_s.shape[0]
    tm = ROW_TILE
    row = lambda width: pl.BlockSpec((tm, width), lambda i: (i, 0))
    vec = lambda width: _const_spec((1, width))
    in_specs = _two_part_specs(h1_p.shape[0], h1_s.shape[0], d) + [
        vec(d), _const_spec((d, D_FF)), _const_spec((d, D_FF)), _const_spec((D_FF, d)),
        vec(d), _const_spec((d, 2 * KV_DIM)), vec(2 * KV_DIM),
        vec(d), _const_spec((d, d)), vec(d),
    ]
    return pl.pallas_call(
        _ffn_kvq_kernel,
        grid=(n // tm,),
        in_specs=in_specs,
        out_specs=[row(d), row(2 * KV_DIM), row(d)],
        out_shape=[jax.ShapeDtypeStruct((n, d), F32), jax.ShapeDtypeStruct((n, 2 * KV_DIM), F32),
                   jax.ShapeDtypeStruct((n, d), BF16)],
        compiler_params=_params("arbitrary"),
        name="ffn_kvq",
    )(h1_p, h1_s, w["gf"], w["wg"], w["wu"], w["wd"], w["gkv"], w["wkv"], w["bkv"], w["gq"], w["wq"], w["bq"])


def _bias_band_kernel(table_ref, out_ref):
    _, q_len, s_len = out_ref.shape
    qi = lax.broadcasted_iota(I32, (q_len, s_len), 0)
    si = lax.broadcasted_iota(I32, (q_len, s_len), 1)
    dist = qi + WINDOW - si
    max_exact = NUM_BUCKETS // 2
    n = jnp.maximum(dist, 0)
    nf = jnp.maximum(n, max_exact).astype(F32)
    log_ratio = math.log(MAX_DISTANCE / max_exact)
    large = max_exact + (jnp.log(nf / max_exact) / log_ratio * (NUM_BUCKETS - max_exact)).astype(I32)
    large = jnp.minimum(large, NUM_BUCKETS - 1)
    bucket = jnp.where(n < max_exact, n, large)
    for h in range(N_HEADS):
        acc = jnp.zeros((q_len, s_len), F32)
        for bkt in range(NUM_BUCKETS):
            acc = jnp.where(bucket == bkt, table_ref[bkt, h], acc)
        out_ref[h] = acc


def _bias_band(rel_table):
    shape = (N_HEADS, WINDOW, 2 * WINDOW)
    return pl.pallas_call(
        _bias_band_kernel,
        grid=(1,),
        in_specs=[pl.BlockSpec(memory_space=pltpu.SMEM)],
        out_specs=pl.BlockSpec(shape, lambda i: (0, 0, 0)),
        out_shape=jax.ShapeDtypeStruct(shape, F32),
        compiler_params=_params("arbitrary"),
        name="bias_band",
    )(rel_table)


def _softmax_sink_pv(s, sink, v):
    m = jnp.maximum(jnp.max(s, axis=-1, keepdims=True), sink)
    p = jnp.exp(s - m)
    denom = jnp.sum(p, axis=-1, keepdims=True) + jnp.exp(sink - m)
    return _dot(p.astype(BF16), v) / denom


def _attn_prompt_kernel(sink_ref, q_ref, kvp_ref, kvc_ref, bias_ref, o_ref):
    nblk = pl.program_id(1)
    w = WINDOW
    qi = lax.broadcasted_iota(I32, (w, 2 * w), 0)
    si = lax.broadcasted_iota(I32, (w, 2 * w), 1)
    dist = qi + w - si
    mask = (dist >= 0) & (dist < w) & ((si >= w) | (nblk > 0))
    kv_prev = kvp_ref[...].astype(BF16)
    kv_cur = kvc_ref[...].astype(BF16)
    kv = jnp.concatenate([kv_prev, kv_cur], axis=0)
    q = q_ref[...]
    for j in range(N_KV):
        k = kv[:, j * HEAD_DIM:(j + 1) * HEAD_DIM]
        v = kv[:, KV_DIM + j * HEAD_DIM:KV_DIM + (j + 1) * HEAD_DIM]
        for pair in range(GROUP // 2):
            outs = []
            for sub in range(2):
                h = j * GROUP + 2 * pair + sub
                qh = q[:, h * HEAD_DIM:(h + 1) * HEAD_DIM]
                s = lax.dot_general(qh, k, (((1,), (1,)), ((), ())), preferred_element_type=F32)
                s = jnp.where(mask, s + bias_ref[h], NEG)
                outs.append(_softmax_sink_pv(s, sink_ref[h], v))
            h0 = j * GROUP + 2 * pair
            o_ref[:, h0 * HEAD_DIM:(h0 + 2) * HEAD_DIM] = jnp.concatenate(outs, axis=1).astype(BF16)


def _attn_prompt(q, kv, bias, sinks, batch, seq):
    nb = seq // WINDOW
    d = q.shape[1]
    in_specs = [
        pl.BlockSpec(memory_space=pltpu.SMEM),
        pl.BlockSpec((WINDOW, d), lambda b, n: (b * nb + n, 0)),
        pl.BlockSpec((WINDOW, 2 * KV_DIM), lambda b, n: (b * nb + jnp.maximum(n - 1, 0), 0)),
        pl.BlockSpec((WINDOW, 2 * KV_DIM), lambda b, n: (b * nb + n, 0)),
        _const_spec((N_HEADS, WINDOW, 2 * WINDOW)),
    ]
    return pl.pallas_call(
        _attn_prompt_kernel,
        grid=(batch, nb),
        in_specs=in_specs,
        out_specs=pl.BlockSpec((WINDOW, d), lambda b, n: (b * nb + n, 0)),
        out_shape=jax.ShapeDtypeStruct((batch * seq, d), BF16),
        compiler_params=_params("arbitrary", "arbitrary"),
        name="attn_prompt",
    )(sinks, q, kv, kv, bias)


SAMPLE_GROUP = 8


def _attn_sample_kernel(q_ref, ck_ref, cv_ref, kn_ref, vn_ref, bias_ref, sink_ref, o_ref):
    gb = q_ref.shape[0]
    steps = kn_ref.shape[1]
    rows = steps * GROUP
    pad = (-steps) % SUBLANES
    keys = WINDOW + steps + pad
    ri = lax.broadcasted_iota(I32, (rows, keys), 0)
    si = lax.broadcasted_iota(I32, (rows, keys), 1)
    dist = ri // GROUP + WINDOW - si
    mask = (dist >= 0) & (dist < WINDOW)
    zeros = jnp.zeros((pad, KV_DIM), F32)
    for b in range(gb):
        k_all = jnp.concatenate([ck_ref[b], kn_ref[b], zeros], axis=0).astype(BF16)
        v_all = jnp.concatenate([cv_ref[b], vn_ref[b], zeros], axis=0).astype(BF16)
        for j in range(N_KV):
            k = k_all[:, j * HEAD_DIM:(j + 1) * HEAD_DIM]
            v = v_all[:, j * HEAD_DIM:(j + 1) * HEAD_DIM]
            s = lax.dot_general(q_ref[b, j], k, (((1,), (1,)), ((), ())), preferred_element_type=F32)
            s = jnp.where(mask, s + bias_ref[j], NEG)
            o_ref[b, j] = _softmax_sink_pv(s, sink_ref[j], v).astype(BF16)


def _attn_sample(q4, ck, cv, kn, vn, bias_rows, sink_rows):
    nb, _, rows, hd = q4.shape
    steps = kn.shape[1]
    keys = bias_rows.shape[2]
    gb = SAMPLE_GROUP
    in_specs = [
        pl.BlockSpec((gb, N_KV, rows, hd), lambda i: (i, 0, 0, 0)),
        pl.BlockSpec((gb, WINDOW, KV_DIM), lambda i: (i, 0, 0)),
        pl.BlockSpec((gb, WINDOW, KV_DIM), lambda i: (i, 0, 0)),
        pl.BlockSpec((gb, steps, KV_DIM), lambda i: (i, 0, 0)),
        pl.BlockSpec((gb, steps, KV_DIM), lambda i: (i, 0, 0)),
        pl.BlockSpec((N_KV, rows, keys), lambda i: (0, 0, 0)),
        pl.BlockSpec((N_KV, rows, 1), lambda i: (0, 0, 0)),
    ]
    return pl.pallas_call(
        _attn_sample_kernel,
        grid=(nb // gb,),
        in_specs=in_specs,
        out_specs=pl.BlockSpec((gb, N_KV, rows, hd), lambda i: (i, 0, 0, 0)),
        out_shape=jax.ShapeDtypeStruct(q4.shape, BF16),
        compiler_params=_params("arbitrary"),
        name="attn_sample",
    )(q4, ck, cv, kn, vn, bias_rows, sink_rows)


def _split_bf16(x):
    hi = x.astype(BF16)
    lo = (x - hi.astype(F32)).astype(BF16)
    return hi, lo


def _oproj_router_kernel(op_ref, os_ref, h2_ref, wo_ref, bo_ref, gf_ref, wrh_ref, wrl_ref,
                         h3_ref, route_ref, counts_ref):
    tm = h2_ref.shape[0]
    h3 = h2_ref[...] + _dot(_two_part_tile(op_ref, os_ref), wo_ref[...]) + bo_ref[...]
    h3_ref[...] = h3
    u_hi, u_lo = _split_bf16(_rms(h3, gf_ref[...]))
    logits = _dot(u_hi, wrh_ref[...]) + (_dot(u_lo, wrh_ref[...]) + _dot(u_hi, wrl_ref[...]))

    lane = lax.broadcasted_iota(I32, (tm, LANES), 1)
    lanef = lane.astype(F32)
    logits = jnp.where(lane < N_EXPERTS, logits, -jnp.inf)
    v1 = jnp.max(logits, axis=-1, keepdims=True)
    e1 = jnp.min(jnp.where(logits == v1, lanef, float(LANES)), axis=-1, keepdims=True)
    rest = jnp.where(lanef == e1, -jnp.inf, logits)
    v2 = jnp.max(rest, axis=-1, keepdims=True)
    e2 = jnp.min(jnp.where(rest == v2, lanef, float(LANES)), axis=-1, keepdims=True)
    ex = jnp.exp(v2 - v1)
    w1 = 1.0 / (1.0 + ex)
    w2 = ex / (1.0 + ex)
    pick1 = lanef == e1
    pick2 = lanef == e2
    sel = jnp.where(pick1 | pick2, 1.0, 0.0)

    ri = lax.broadcasted_iota(I32, (tm, tm), 0)
    ci = lax.broadcasted_iota(I32, (tm, tm), 1)
    earlier = jnp.where(ri > ci, 1.0, 0.0).astype(BF16)
    rank = _dot(earlier, sel.astype(BF16))
    cnt = jnp.sum(sel, axis=0, keepdims=True)
    seg = jnp.floor((cnt + (SUBLANES - 1)) * (1.0 / SUBLANES)) * SUBLANES
    ek = lax.broadcasted_iota(I32, (LANES, LANES), 0)
    el = lax.broadcasted_iota(I32, (LANES, LANES), 1)
    lower_experts = jnp.where(ek < el, 1.0, 0.0).astype(BF16)
    seg_start = _dot(jnp.broadcast_to(seg, (SUBLANES, LANES)).astype(BF16), lower_experts)[0:1, :]
    local = rank + seg_start
    lr1 = jnp.sum(jnp.where(pick1, local, 0.0), axis=-1, keepdims=True)
    lr2 = jnp.sum(jnp.where(pick2, local, 0.0), axis=-1, keepdims=True)
    route_ref[...] = jnp.where(lane == 0, lr1, jnp.where(lane == 1, lr2, jnp.where(lane == 2, w1, jnp.where(lane == 3, w2, 0.0))))
    counts_ref[0] = jnp.broadcast_to(cnt, (SUBLANES, LANES))


def _oproj_router(o_p, o_s, h2, w):
    n, d = h2.shape
    tm = ROW_TILE
    row = lambda width: pl.BlockSpec((tm, width), lambda i: (i, 0))
    in_specs = _two_part_specs(o_p.shape[0], o_s.shape[0], d) + [
        row(d), _const_spec((d, d)), _const_spec((1, d)), _const_spec((1, d)),
        _const_spec((d, LANES)), _const_spec((d, LANES))]
    return pl.pallas_call(
        _oproj_router_kernel,
        grid=(n // tm,),
        in_specs=in_specs,
        out_specs=[row(d), row(LANES), pl.BlockSpec((1, SUBLANES, LANES), lambda i: (i, 0, 0))],
        out_shape=[jax.ShapeDtypeStruct((n, d), F32), jax.ShapeDtypeStruct((n, LANES), F32),
                   jax.ShapeDtypeStruct((n // tm, SUBLANES, LANES), F32)],
        compiler_params=_params("arbitrary"),
        name="oproj_router",
    )(o_p, o_s, h2, w["wo"], w["bo"], w["gf"], w["wr_hi"], w["wr_lo"])


LOCAL_ROWS = 1152
XS_WIDTH = D_MODEL + LANES
SEG_TABLE = 3 * N_EXPERTS
SEG_PIECES = tuple(SUBLANES << b for b in reversed(range(7)))


def _segment_copies(src_ref, src_start, dst_ref, dst_start, length, sem, act):
    for piece in SEG_PIECES:
        done = length & (-2 * piece)

        @pl.when((length & piece) != 0)
        def _():
            s = pl.multiple_of(src_start + done, SUBLANES)
            t = pl.multiple_of(dst_start + done, SUBLANES)
            act(pltpu.make_async_copy(src_ref.at[pl.ds(s, piece), :], dst_ref.at[pl.ds(t, piece), :], sem))


def _sort_place_kernel(tbl_ref, tail_ref, nu_ref, h3_ref, route_ref, g_ref, xs_ref, stage, zeros, sem):
    j = pl.program_id(0)
    tm = h3_ref.shape[0]
    xn = _rms(h3_ref[...], g_ref[...]).astype(BF16)
    rt = jnp.transpose(route_ref[...])
    row = lax.broadcasted_iota(I32, (LOCAL_ROWS, tm), 0).astype(F32)
    m1 = row == rt[0:1, :]
    m2 = row == rt[1:2, :]
    onehot = jnp.where(m1 | m2, 1.0, 0.0).astype(BF16)
    stage[:, 0:D_MODEL] = _dot(onehot, xn)
    gate = jnp.sum(jnp.where(m1, rt[2:3, :], 0.0) + jnp.where(m2, rt[3:4, :], 0.0), axis=-1, keepdims=True)
    stage[:, D_MODEL:XS_WIDTH] = jnp.broadcast_to(gate, (LOCAL_ROWS, LANES))

    def segments(act):
        for e in range(N_EXPERTS):
            base = j * SEG_TABLE
            _segment_copies(stage, tbl_ref[base + e], xs_ref, tbl_ref[base + 2 * N_EXPERTS + e],
                            tbl_ref[base + N_EXPERTS + e], sem, act)

    segments(lambda cp: cp.start())
    segments(lambda cp: cp.wait())

    @pl.when(j == pl.num_programs(0) - 1)
    def _():
        zeros[...] = jnp.zeros_like(zeros)

        def tails(act):
            for e in range(N_EXPERTS):
                _segment_copies(zeros, 0, xs_ref, tail_ref[e], tail_ref[N_EXPERTS + e], sem, act)

        def unused(act):
            def body(i, c):
                t = pl.multiple_of(i * MOE_TILE, MOE_TILE)
                act(pltpu.make_async_copy(zeros, xs_ref.at[pl.ds(t, MOE_TILE), :], sem))
                return c
            lax.fori_loop(nu_ref[0], xs_ref.shape[0] // MOE_TILE, body, 0)

        tails(lambda cp: cp.start())
        unused(lambda cp: cp.start())
        tails(lambda cp: cp.wait())
        unused(lambda cp: cp.wait())


def _sort_place(h3, route, g, tbl, tail, n_used, p_rows):
    n, d = h3.shape
    tm = ROW_TILE
    grid_spec = pltpu.PrefetchScalarGridSpec(
        num_scalar_prefetch=3,
        grid=(n // tm,),
        in_specs=[pl.BlockSpec((tm, d), lambda j, *_: (j, 0)),
                  pl.BlockSpec((tm, LANES), lambda j, *_: (j, 0)),
                  pl.BlockSpec((1, d), lambda j, *_: (0, 0))],
        out_specs=pl.BlockSpec(memory_space=pl.ANY),
        scratch_shapes=[pltpu.VMEM((LOCAL_ROWS, XS_WIDTH), F32), pltpu.VMEM((MOE_TILE, XS_WIDTH), F32),
                        pltpu.SemaphoreType.DMA(())],
    )
    return pl.pallas_call(
        _sort_place_kernel,
        grid_spec=grid_spec,
        out_shape=jax.ShapeDtypeStruct((p_rows, XS_WIDTH), F32),
        compiler_params=_params("arbitrary"),
        name="moe_sort_place",
    )(tbl, tail, n_used, h3, route, g)


def _moe_kernel(te_ref, nu_ref, xs_ref, wg_ref, wu_ref, wd_ref, y_ref):
    i = pl.program_id(0)
    c = pl.program_id(1)
    used = i < nu_ref[0]

    @pl.when(used)
    def _():
        xn = xs_ref[:, 0:D_MODEL].astype(BF16)
        g = _dot(xn, wg_ref[0])
        up = _dot(xn, wu_ref[0])
        mid = (g * _sigmoid(g) * up).astype(BF16)
        y = _dot(mid, wd_ref[0]) * xs_ref[:, D_MODEL:D_MODEL + 1]

        @pl.when(c == 0)
        def _():
            y_ref[...] = y

        @pl.when(c > 0)
        def _():
            y_ref[...] = y_ref[...] + y

    @pl.when(jnp.logical_not(used) & (c == 0))
    def _():
        y_ref[...] = jnp.zeros_like(y_ref)


def _moe_experts(xs, tile_expert, n_used, wg, wu, wd):
    p_rows = xs.shape[0]
    d = D_MODEL
    tm = MOE_TILE
    n_tiles = p_rows // tm
    ch = D_EXPERT // MOE_CHUNKS
    last = MOE_CHUNKS - 1

    def chunk(i, c, nu):
        return jnp.where(i < nu[0], c, last)

    grid_spec = pltpu.PrefetchScalarGridSpec(
        num_scalar_prefetch=2,
        grid=(n_tiles, MOE_CHUNKS),
        in_specs=[
            pl.BlockSpec((tm, XS_WIDTH), lambda i, c, te, nu: (i, 0)),
            pl.BlockSpec((1, d, ch), lambda i, c, te, nu: (te[i], 0, chunk(i, c, nu))),
            pl.BlockSpec((1, d, ch), lambda i, c, te, nu: (te[i], 0, chunk(i, c, nu))),
            pl.BlockSpec((1, ch, d), lambda i, c, te, nu: (te[i], chunk(i, c, nu), 0)),
        ],
        out_specs=pl.BlockSpec((tm, d), lambda i, c, te, nu: (i, 0)),
    )
    return pl.pallas_call(
        _moe_kernel,
        grid_spec=grid_spec,
        out_shape=jax.ShapeDtypeStruct((p_rows, d), F32),
        compiler_params=_params("arbitrary", "arbitrary"),
        name="moe_experts",
    )(tile_expert, n_used, xs, wg, wu, wd)


def _combine_kernel(tbl_ref, h3_ref, route_ref, ys_ref, g_ref, outp_ref, outs_ref, ybuf, sem):
    j = pl.program_id(0)
    last = pl.num_programs(0) - 1
    tm = h3_ref.shape[0]

    @pl.when(j == 0)
    def _():
        ybuf[...] = jnp.zeros_like(ybuf)

    def segments(act):
        for e in range(N_EXPERTS):
            base = j * SEG_TABLE
            _segment_copies(ys_ref, tbl_ref[base + 2 * N_EXPERTS + e], ybuf, tbl_ref[base + e],
                            tbl_ref[base + N_EXPERTS + e], sem, act)

    segments(lambda cp: cp.start())
    segments(lambda cp: cp.wait())

    route = route_ref[...]
    col = lax.broadcasted_iota(I32, (tm, LOCAL_ROWS), 1).astype(F32)
    picks = jnp.where((col == route[:, 0:1]) | (col == route[:, 1:2]), 1.0, 0.0).astype(BF16)
    h4 = h3_ref[...] + _dot(picks, ybuf[...].astype(BF16))
    out = _rms(h4, g_ref[...])

    @pl.when(j < last)
    def _():
        outp_ref[...] = out

    @pl.when(j == last)
    def _():
        outs_ref[...] = out


def _combine_final(h3, route, tbl, ys, g_final, n_prompt):
    n, d = h3.shape
    tm = ROW_TILE
    n_sample = n - n_prompt
    assert n_sample == tm and n_prompt % tm == 0
    last_prompt = n_prompt // tm - 1
    grid_spec = pltpu.PrefetchScalarGridSpec(
        num_scalar_prefetch=1,
        grid=(n // tm,),
        in_specs=[pl.BlockSpec((tm, d), lambda j, *_: (j, 0)),
                  pl.BlockSpec((tm, LANES), lambda j, *_: (j, 0)),
                  pl.BlockSpec(memory_space=pl.ANY),
                  pl.BlockSpec((1, d), lambda j, *_: (0, 0))],
        out_specs=[pl.BlockSpec((tm, d), lambda j, *_: (jnp.minimum(j, last_prompt), 0)),
                   pl.BlockSpec((tm, d), lambda j, *_: (0, 0))],
        scratch_shapes=[pltpu.VMEM((LOCAL_ROWS, d), F32), pltpu.SemaphoreType.DMA(())],
    )
    return pl.pallas_call(
        _combine_kernel,
        grid_spec=grid_spec,
        out_shape=[jax.ShapeDtypeStruct((n_prompt, d), F32), jax.ShapeDtypeStruct((n_sample, d), F32)],
        compiler_params=_params("arbitrary"),
        name="moe_combine",
    )(tbl, h3, route, ys, g_final)


def _row(v):
    return v.reshape(1, -1).astype(F32)


def _routing_tables(counts):
    n_row_tiles = counts.shape[0]
    cnt = counts[:, 0, :N_EXPERTS].astype(I32)
    seg = ((cnt + SUBLANES - 1) // SUBLANES) * SUBLANES
    local_start = jnp.cumsum(seg, axis=1) - seg
    rows = jnp.sum(seg, axis=0)
    padded = ((rows + MOE_TILE - 1) // MOE_TILE) * MOE_TILE
    ends = jnp.cumsum(padded)
    starts = ends - padded
    sorted_start = starts[None, :] + jnp.cumsum(seg, axis=0) - seg
    tbl = jnp.concatenate([local_start, seg, sorted_start], axis=1).reshape(-1).astype(I32)
    tail = jnp.concatenate([starts + rows, padded - rows]).astype(I32)
    max_rows = n_row_tiles * (TOP_K * ROW_TILE + N_EXPERTS * (SUBLANES - 1)) + N_EXPERTS * (MOE_TILE - SUBLANES)
    n_tiles = -(-max_rows // MOE_TILE)
    tile_start = jnp.arange(n_tiles, dtype=I32) * MOE_TILE
    tile_expert = jnp.minimum(jnp.sum((tile_start[:, None] >= ends[None, :]).astype(I32), axis=1), N_EXPERTS - 1)
    n_used = (ends[-1] // MOE_TILE).reshape(1).astype(I32)
    return tbl, tail, tile_expert.astype(I32), n_used, n_tiles


def kernel(x_prompt, x_sample, state_conv, state_h, cache_k, cache_v, g_mix, g_ffn, g_kv, g_final, a_w_gate, a_b_gate, a_w_in, a_b_in, a_conv_w, a_conv_b, a_w_r, a_b_r, a_w_i, a_b_i, a_lam, a_w_out, a_b_out, w_kv, b_kv, rel_bias, b_w_q, b_b_q, b_sinks, b_w_o, b_b_o, f_w_gate, f_w_up, f_w_down, m_w_router, m_w_gate, m_w_up, m_w_down):
    bp, seq, d = x_prompt.shape
    bs, steps, _ = x_sample.shape
    n_prompt = bp * seq
    n_sample = bs * steps
    n = n_prompt + n_sample
    assert seq % MIX_TILE == 0 and seq % WINDOW == 0 and n_prompt % ROW_TILE == 0
    assert n_sample == ROW_TILE and bs % SAMPLE_GROUP == 0

    mix_w = dict(g=_row(g_mix[0]), wg=a_w_gate[0].astype(BF16), bg=_row(a_b_gate[0]),
                 win=a_w_in[0].astype(BF16), bin=_row(a_b_in[0]), cw=a_conv_w[0], cb=_row(a_conv_b[0]),
                 wr=a_w_r[0].astype(BF16), br=_row(a_b_r[0]), wi=a_w_i[0].astype(BF16), bi=_row(a_b_i[0]),
                 lam=_row(a_lam[0]), wout=a_w_out[0].astype(BF16), bout=_row(a_b_out[0]))

    h1_p, p_conv, p_h = _mixer_prompt(x_prompt, mix_w)
    h1_s, s_conv_tm, s_h = _mixer_sample(jnp.transpose(x_sample, (1, 0, 2)),
                                         jnp.transpose(state_conv[0], (1, 0, 2)), state_h[0], mix_w)
    h1_s = jnp.transpose(h1_s, (1, 0, 2)).reshape(n_sample, d)

    ffn_w = dict(gf=_row(g_ffn[0]), wg=f_w_gate[0].astype(BF16), wu=f_w_up[0].astype(BF16),
                 wd=f_w_down[0].astype(BF16), gkv=_row(g_kv), wkv=w_kv.astype(BF16), bkv=_row(b_kv),
                 gq=_row(g_mix[1]), wq=b_w_q[0].astype(BF16), bq=_row(b_b_q[0]))
    h2, kv, q = _ffn_kvq(h1_p, h1_s, ffn_w)

    bias = _bias_band(rel_bias)
    sinks = b_sinks[0].astype(F32)
    o_p = _attn_prompt(q, kv, bias, sinks, bp, seq)
    keys = WINDOW + steps + (-steps) % SUBLANES
    q4 = q[n_prompt:].reshape(bs, steps, N_KV, GROUP, HEAD_DIM).transpose(0, 2, 1, 3, 4)
    q4 = q4.reshape(bs, N_KV, steps * GROUP, HEAD_DIM)
    kv_s = kv[n_prompt:].reshape(bs, steps, 2 * KV_DIM)
    k_new, v_new = kv_s[:, :, :KV_DIM], kv_s[:, :, KV_DIM:]
    bias_rows = bias[:, :steps, :keys].reshape(N_KV, GROUP, steps, keys).transpose(0, 2, 1, 3)
    bias_rows = bias_rows.reshape(N_KV, steps * GROUP, keys)
    sink_rows = jnp.broadcast_to(sinks.reshape(N_KV, 1, GROUP), (N_KV, steps, GROUP)).reshape(N_KV, steps * GROUP, 1)
    o4 = _attn_sample(q4, cache_k.reshape(bs, WINDOW, KV_DIM), cache_v.reshape(bs, WINDOW, KV_DIM),
                      k_new, v_new, bias_rows, sink_rows)
    o_s = o4.reshape(bs, N_KV, steps, GROUP, HEAD_DIM).transpose(0, 2, 1, 3, 4).reshape(n_sample, d)

    wr_pad = jnp.zeros((d, LANES), F32).at[:, :N_EXPERTS].set(m_w_router[0])
    wr_hi = wr_pad.astype(BF16)
    wr_lo = (wr_pad - wr_hi.astype(F32)).astype(BF16)
    h3, route, counts = _oproj_router(
        o_p, o_s, h2, dict(wo=b_w_o[0].astype(BF16), bo=_row(b_b_o[0]), gf=_row(g_ffn[1]), wr_hi=wr_hi, wr_lo=wr_lo))

    tbl, tail, tile_expert, n_used, n_tiles = _routing_tables(counts)
    xs = _sort_place(h3, route, _row(g_ffn[1]), tbl, tail, n_used, n_tiles * MOE_TILE)
    ys = _moe_experts(xs, tile_expert, n_used, m_w_gate[0].astype(BF16),
                      m_w_up[0].astype(BF16), m_w_down[0].astype(BF16))
    y_p, y_s = _combine_final(h3, route, tbl, ys, _row(g_final), n_prompt)

    y_prompt = y_p.reshape(bp, seq, d)
    y_sample = y_s.reshape(bs, steps, d)
    p_k = kv[:n_prompt, :KV_DIM].reshape(bp, seq, N_KV, HEAD_DIM)[:, -WINDOW:]
    p_v = kv[:n_prompt, KV_DIM:].reshape(bp, seq, N_KV, HEAD_DIM)[:, -WINDOW:]
    s_k = jnp.concatenate([cache_k, k_new.reshape(bs, steps, N_KV, HEAD_DIM)], axis=1)[:, -WINDOW:]
    s_v = jnp.concatenate([cache_v, v_new.reshape(bs, steps, N_KV, HEAD_DIM)], axis=1)[:, -WINDOW:]
    return (y_prompt, y_sample, p_conv[None], p_h.reshape(1, bp, LRU_WIDTH), p_k, p_v,
            jnp.transpose(s_conv_tm, (1, 0, 2))[None], s_h[None], s_k, s_v)
```

```python
import math

import jax
import jax.numpy as jnp
from jax import lax
from jax.experimental import pallas as pl
from jax.experimental.pallas import tpu as pltpu

D_MODEL = 1024
LRU_WIDTH = D_MODEL
LRU_BLOCK_W = 256
LRU_BLOCKS = LRU_WIDTH // LRU_BLOCK_W
CONV_W = 4
LRU_C = 8.0
HEAD_DIM = 64
N_HEADS = D_MODEL // HEAD_DIM
N_KV = 2
GROUP = N_HEADS // N_KV
KV_DIM = N_KV * HEAD_DIM
WINDOW = 128
NUM_BUCKETS = 32
MAX_DISTANCE = 128
D_FF = 3 * D_MODEL
N_EXPERTS = 8
TOP_K = 2
D_EXPERT = 7 * D_MODEL // 2
EPS = 1e-6
NEG = -1e30

BF16 = jnp.bfloat16
F32 = jnp.float32
I32 = jnp.int32

SUBLANES = 8
LANES = 128
VMEM_LIMIT_BYTES = 56 * 1024 * 1024

ROW_TILE = 512
MIX_TILE = 256
FF_CHUNK = 1024
MOE_TILE = 512
MOE_CHUNKS = 2


def _params(*semantics):
    return pltpu.CompilerParams(dimension_semantics=semantics, vmem_limit_bytes=VMEM_LIMIT_BYTES)


def _const_spec(shape):
    zeros = (0,) * len(shape)
    return pl.BlockSpec(shape, lambda *_: zeros, pipeline_mode=pl.Buffered(1))


def _dot(a, b):
    return jnp.dot(a, b, preferred_element_type=F32)


def _rms(x, g):
    ms = jnp.mean(x * x, axis=-1, keepdims=True)
    return x * lax.rsqrt(ms + EPS) * g


def _sigmoid(x):
    return 1.0 / (1.0 + jnp.exp(-x))


def _gelu_tanh(x):
    return 0.5 * x * (1.0 + jnp.tanh(0.7978845608028654 * (x + 0.044715 * (x * x * x))))


def _log_sigmoid(x):
    return jnp.minimum(x, 0.0) - jnp.log1p(jnp.exp(-jnp.abs(x)))


def _lru_gates(xc, wr_ref, br, wi_ref, bi, lam):
    xcb = xc.astype(BF16)
    rs, gs = [], []
    for n in range(LRU_BLOCKS):
        xn = xcb[:, n * LRU_BLOCK_W:(n + 1) * LRU_BLOCK_W]
        rs.append(_dot(xn, wr_ref[n]))
        gs.append(_dot(xn, wi_ref[n]))
    r = _sigmoid(jnp.concatenate(rs, axis=1) + br)
    i = _sigmoid(jnp.concatenate(gs, axis=1) + bi)
    log_a = LRU_C * r * _log_sigmoid(lam)
    a = jnp.exp(log_a)
    mult = jnp.sqrt(1.0 - a * a)
    return a, mult * (i * xc)


def _mixer_prompt_kernel(x_ref, g_ref, wg_ref, bg_ref, win_ref, bin_ref, cw_ref, cb_ref,
                         wr_ref, br_ref, wi_ref, bi_ref, lam_ref, wout_ref, bout_ref,
                         h1_ref, conv_ref, hlast_ref, xr_buf, h_carry):
    t = pl.program_id(1)
    tt = x_ref.shape[1]
    pad = SUBLANES

    @pl.when(t == 0)
    def _():
        xr_buf[0:pad, :] = jnp.zeros((pad, LRU_WIDTH), F32)
        h_carry[...] = jnp.zeros_like(h_carry)

    x = x_ref[0]
    u = _rms(x, g_ref[...]).astype(BF16)
    gate = _gelu_tanh(_dot(u, wg_ref[...]) + bg_ref[...])
    xr = _dot(u, win_ref[...]) + bin_ref[...]
    xr_buf[pad:pad + tt, :] = xr
    xc = cb_ref[...] + cw_ref[CONV_W - 1:CONV_W, :] * xr
    for k in range(CONV_W - 1):
        back = CONV_W - 1 - k
        xc = xc + cw_ref[k:k + 1, :] * xr_buf[pad - back:pad - back + tt, :]
    xr_buf[0:pad, :] = xr[tt - pad:tt, :]
    conv_ref[0] = xr[tt - (CONV_W - 1):tt, :]

    a, b = _lru_gates(xc, wr_ref, br_ref[...], wi_ref, bi_ref[...], lam_ref[...])

    groups = tt // SUBLANES
    a3 = a.reshape(groups, SUBLANES, LRU_WIDTH)
    b3 = b.reshape(groups, SUBLANES, LRU_WIDTH)
    row = lax.broadcasted_iota(I32, (1, SUBLANES, LRU_WIDTH), 1)
    step = 1
    while step < SUBLANES:
        keep = row >= step
        a_prev = jnp.where(keep, pltpu.roll(a3, step, axis=1), 1.0)
        b_prev = jnp.where(keep, pltpu.roll(b3, step, axis=1), 0.0)
        b3 = b3 + a3 * b_prev
        a3 = a3 * a_prev
        step *= 2
    h_prev = h_carry[0:1, :]
    hs = []
    for gi in range(groups):
        hg = b3[gi] + a3[gi] * h_prev
        hs.append(hg)
        h_prev = hg[SUBLANES - 1:SUBLANES, :]
    h = jnp.concatenate(hs, axis=0)
    h_carry[0:1, :] = h_prev
    hlast_ref[0] = h_prev

    y = _dot((h * gate).astype(BF16), wout_ref[...]) + bout_ref[...]
    h1_ref[...] = x + y


def _mixer_prompt(x, w):
    b, t, d = x.shape
    tt = MIX_TILE
    nt = t // tt
    vec = lambda n: _const_spec((1, n))
    in_specs = [
        pl.BlockSpec((1, tt, d), lambda bi, ti: (bi, ti, 0)),
        vec(d), _const_spec((d, LRU_WIDTH)), vec(LRU_WIDTH), _const_spec((d, LRU_WIDTH)), vec(LRU_WIDTH),
        _const_spec((CONV_W, LRU_WIDTH)), vec(LRU_WIDTH),
        _const_spec((LRU_BLOCKS, LRU_BLOCK_W, LRU_BLOCK_W)), vec(LRU_WIDTH),
        _const_spec((LRU_BLOCKS, LRU_BLOCK_W, LRU_BLOCK_W)), vec(LRU_WIDTH),
        vec(LRU_WIDTH), _const_spec((LRU_WIDTH, d)), vec(d),
    ]
    out_specs = [
        pl.BlockSpec((tt, d), lambda bi, ti: (bi * nt + ti, 0)),
        pl.BlockSpec((1, CONV_W - 1, LRU_WIDTH), lambda bi, ti: (bi, 0, 0)),
        pl.BlockSpec((1, 1, LRU_WIDTH), lambda bi, ti: (bi, 0, 0)),
    ]
    out_shape = [
        jax.ShapeDtypeStruct((b * t, d), F32),
        jax.ShapeDtypeStruct((b, CONV_W - 1, LRU_WIDTH), F32),
        jax.ShapeDtypeStruct((b, 1, LRU_WIDTH), F32),
    ]
    return pl.pallas_call(
        _mixer_prompt_kernel,
        grid=(b, nt),
        in_specs=in_specs,
        out_specs=out_specs,
        out_shape=out_shape,
        scratch_shapes=[pltpu.VMEM((SUBLANES + tt, LRU_WIDTH), F32), pltpu.VMEM((SUBLANES, LRU_WIDTH), F32)],
        compiler_params=_params("arbitrary", "arbitrary"),
        name="mixer_prompt",
    )(x, w["g"], w["wg"], w["bg"], w["win"], w["bin"], w["cw"], w["cb"], w["wr"], w["br"],
      w["wi"], w["bi"], w["lam"], w["wout"], w["bout"])


def _mixer_sample_kernel(x_ref, cs_ref, h0_ref, g_ref, wg_ref, bg_ref, win_ref, bin_ref, cw_ref, cb_ref,
                         wr_ref, br_ref, wi_ref, bi_ref, lam_ref, wout_ref, bout_ref,
                         h1_ref, conv_ref, hlast_ref):
    steps, nb, d = x_ref.shape
    x = x_ref[...].reshape(steps * nb, d)
    u = _rms(x, g_ref[...]).astype(BF16)
    gate = _gelu_tanh(_dot(u, wg_ref[...]) + bg_ref[...])
    xr = _dot(u, win_ref[...]) + bin_ref[...]
    xpad = [cs_ref[k] for k in range(CONV_W - 1)] + [xr[s * nb:(s + 1) * nb, :] for s in range(steps)]
    xcs = []
    for s in range(steps):
        acc = cb_ref[...] + cw_ref[0:1, :] * xpad[s]
        for k in range(1, CONV_W):
            acc = acc + cw_ref[k:k + 1, :] * xpad[s + k]
        xcs.append(acc)
    for k in range(CONV_W - 1):
        conv_ref[k] = xpad[steps + k]
    xc = jnp.concatenate(xcs, axis=0)
    a, b = _lru_gates(xc, wr_ref, br_ref[...], wi_ref, bi_ref[...], lam_ref[...])
    h = h0_ref[...]
    hs = []
    for s in range(steps):
        h = a[s * nb:(s + 1) * nb, :] * h + b[s * nb:(s + 1) * nb, :]
        hs.append(h)
    hlast_ref[...] = h
    hcat = jnp.concatenate(hs, axis=0)
    y = _dot((hcat * gate).astype(BF16), wout_ref[...]) + bout_ref[...]
    h1_ref[...] = (x + y).reshape(steps, nb, d)


def _mixer_sample(x_tm, cs_tm, h0, w):
    steps, nb, d = x_tm.shape
    full = lambda shape: pl.BlockSpec(shape, lambda i: (0,) * len(shape))
    vec = lambda n: full((1, n))
    in_specs = [
        full((steps, nb, d)), full((CONV_W - 1, nb, LRU_WIDTH)), full((nb, LRU_WIDTH)),
        vec(d), full((d, LRU_WIDTH)), vec(LRU_WIDTH), full((d, LRU_WIDTH)), vec(LRU_WIDTH),
        full((CONV_W, LRU_WIDTH)), vec(LRU_WIDTH),
        full((LRU_BLOCKS, LRU_BLOCK_W, LRU_BLOCK_W)), vec(LRU_WIDTH),
        full((LRU_BLOCKS, LRU_BLOCK_W, LRU_BLOCK_W)), vec(LRU_WIDTH),
        vec(LRU_WIDTH), full((LRU_WIDTH, d)), vec(d),
    ]
    out_specs = [full((steps, nb, d)), full((CONV_W - 1, nb, LRU_WIDTH)), full((nb, LRU_WIDTH))]
    out_shape = [
        jax.ShapeDtypeStruct((steps, nb, d), F32),
        jax.ShapeDtypeStruct((CONV_W - 1, nb, LRU_WIDTH), F32),
        jax.ShapeDtypeStruct((nb, LRU_WIDTH), F32),
    ]
    return pl.pallas_call(
        _mixer_sample_kernel, grid=(1,), in_specs=in_specs, out_specs=out_specs, out_shape=out_shape,
        compiler_params=_params("arbitrary"), name="mixer_sample",
    )(x_tm, cs_tm, h0, w["g"], w["wg"], w["bg"], w["win"], w["bin"], w["cw"], w["cb"], w["wr"], w["br"],
      w["wi"], w["bi"], w["lam"], w["wout"], w["bout"])


def _two_part_specs(n_prompt, n_sample, width):
    assert n_sample == ROW_TILE and n_prompt % ROW_TILE == 0
    last_prompt = n_prompt // ROW_TILE - 1
    return [pl.BlockSpec((ROW_TILE, width), lambda i: (jnp.minimum(i, last_prompt), 0)),
            pl.BlockSpec((ROW_TILE, width), lambda i: (0, 0))]


def _two_part_tile(prompt_ref, sample_ref):
    is_sample = pl.program_id(0) == pl.num_programs(0) - 1
    return jnp.where(is_sample, sample_ref[...], prompt_ref[...])


def _side_cast_specs(w, n_steps, step_of):
    ne, rows, cols = w.shape
    per_expert = n_steps // ne
    blk = rows // per_expert
    assert per_expert * ne == n_steps and blk * per_expert == rows and blk % (2 * SUBLANES) == 0
    index = lambda *ids: (step_of(*ids) // per_expert, step_of(*ids) % per_expert, 0)
    spec = pl.BlockSpec((1, blk, cols), index)
    return spec, spec, jax.ShapeDtypeStruct(w.shape, BF16)


def _ffn_kvq_kernel(h1p_ref, h1s_ref, gf_ref, wg_ref, wu_ref, wd_ref, gkv_ref, wkv_ref, bkv_ref,
                    gq_ref, wq_ref, bq_ref, cast_ref, h2_ref, kv_ref, q_ref, cast_out_ref):
    cast_out_ref[...] = cast_ref[...].astype(BF16)
    h1 = _two_part_tile(h1p_ref, h1s_ref)
    u = _rms(h1, gf_ref[...]).astype(BF16)
    acc = h1
    for c in range(D_FF // FF_CHUNK):
        cols = slice(c * FF_CHUNK, (c + 1) * FF_CHUNK)
        g = _dot(u, wg_ref[:, cols])
        up = _dot(u, wu_ref[:, cols])
        mid = (g * _sigmoid(g) * up).astype(BF16)
        acc = acc + _dot(mid, wd_ref[cols, :])
    h2_ref[...] = acc
    kv_ref[...] = _dot(_rms(acc, gkv_ref[...]).astype(BF16), wkv_ref[...]) + bkv_ref[...]
    q = _dot(_rms(acc, gq_ref[...]).astype(BF16), wq_ref[...]) + bq_ref[...]
    q_ref[...] = (q * (HEAD_DIM ** -0.5)).astype(BF16)


def _ffn_kvq(h1_p, h1_s, w, cast_w):
    d = h1_p.shape[1]
    n = h1_p.shape[0] + h1_s.shape[0]
    tm = ROW_TILE
    row = lambda width: pl.BlockSpec((tm, width), lambda i: (i, 0))
    vec = lambda width: _const_spec((1, width))
    prompt_steps = h1_p.shape[0] // tm
    cast_in, cast_out, cast_shape = _side_cast_specs(cast_w, prompt_steps, lambda i: jnp.minimum(i, prompt_steps - 1))
    in_specs = _two_part_specs(h1_p.shape[0], h1_s.shape[0], d) + [
        vec(d), _const_spec((d, D_FF)), _const_spec((d, D_FF)), _const_spec((D_FF, d)),
        vec(d), _const_spec((d, 2 * KV_DIM)), vec(2 * KV_DIM),
        vec(d), _const_spec((d, d)), vec(d), cast_in,
    ]
    return pl.pallas_call(
        _ffn_kvq_kernel,
        grid=(n // tm,),
        in_specs=in_specs,
        out_specs=[row(d), row(2 * KV_DIM), row(d), cast_out],
        out_shape=[jax.ShapeDtypeStruct((n, d), F32), jax.ShapeDtypeStruct((n, 2 * KV_DIM), F32),
                   jax.ShapeDtypeStruct((n, d), BF16), cast_shape],
        compiler_params=_params("arbitrary"),
        name="ffn_kvq",
    )(h1_p, h1_s, w["gf"], w["wg"], w["wu"], w["wd"], w["gkv"], w["wkv"], w["bkv"], w["gq"], w["wq"], w["bq"],
      cast_w)


def _bias_band_kernel(table_ref, out_ref):
    _, q_len, s_len = out_ref.shape
    qi = lax.broadcasted_iota(I32, (q_len, s_len), 0)
    si = lax.broadcasted_iota(I32, (q_len, s_len), 1)
    dist = qi + WINDOW - si
    max_exact = NUM_BUCKETS // 2
    n = jnp.maximum(dist, 0)
    nf = jnp.maximum(n, max_exact).astype(F32)
    log_ratio = math.log(MAX_DISTANCE / max_exact)
    large = max_exact + (jnp.log(nf / max_exact) / log_ratio * (NUM_BUCKETS - max_exact)).astype(I32)
    large = jnp.minimum(large, NUM_BUCKETS - 1)
    bucket = jnp.where(n < max_exact, n, large)
    for h in range(N_HEADS):
        acc = jnp.zeros((q_len, s_len), F32)
        for bkt in range(NUM_BUCKETS):
            acc = jnp.where(bucket == bkt, table_ref[bkt, h], acc)
        out_ref[h] = acc


def _bias_band(rel_table):
    shape = (N_HEADS, WINDOW, 2 * WINDOW)
    return pl.pallas_call(
        _bias_band_kernel,
        grid=(1,),
        in_specs=[pl.BlockSpec(memory_space=pltpu.SMEM)],
        out_specs=pl.BlockSpec(shape, lambda i: (0, 0, 0)),
        out_shape=jax.ShapeDtypeStruct(shape, F32),
        compiler_params=_params("arbitrary"),
        name="bias_band",
    )(rel_table)


def _softmax_sink_pv(s, sink, v):
    m = jnp.maximum(jnp.max(s, axis=-1, keepdims=True), sink)
    p = jnp.exp(s - m)
    denom = jnp.sum(p, axis=-1, keepdims=True) + jnp.exp(sink - m)
    return _dot(p.astype(BF16), v) / denom


def _attn_prompt_kernel(sink_ref, q_ref, kvp_ref, kvc_ref, bias_ref, cast_a_ref, cast_b_ref,
                        o_ref, cast_a_out_ref, cast_b_out_ref):
    cast_a_out_ref[...] = cast_a_ref[...].astype(BF16)
    cast_b_out_ref[...] = cast_b_ref[...].astype(BF16)
    nblk = pl.program_id(1)
    w = WINDOW
    qi = lax.broadcasted_iota(I32, (w, 2 * w), 0)
    si = lax.broadcasted_iota(I32, (w, 2 * w), 1)
    dist = qi + w - si
    mask = (dist >= 0) & (dist < w) & ((si >= w) | (nblk > 0))
    kv_prev = kvp_ref[...].astype(BF16)
    kv_cur = kvc_ref[...].astype(BF16)
    kv = jnp.concatenate([kv_prev, kv_cur], axis=0)
    q = q_ref[...]
    for j in range(N_KV):
        k = kv[:, j * HEAD_DIM:(j + 1) * HEAD_DIM]
        v = kv[:, KV_DIM + j * HEAD_DIM:KV_DIM + (j + 1) * HEAD_DIM]
        for pair in range(GROUP // 2):
            outs = []
            for sub in range(2):
                h = j * GROUP + 2 * pair + sub
                qh = q[:, h * HEAD_DIM:(h + 1) * HEAD_DIM]
                s = lax.dot_general(qh, k, (((1,), (1,)), ((), ())), preferred_element_type=F32)
                s = jnp.where(mask, s + bias_ref[h], NEG)
                outs.append(_softmax_sink_pv(s, sink_ref[h], v))
            h0 = j * GROUP + 2 * pair
            o_ref[:, h0 * HEAD_DIM:(h0 + 2) * HEAD_DIM] = jnp.concatenate(outs, axis=1).astype(BF16)


def _attn_prompt(q, kv, bias, sinks, batch, seq, cast_a, cast_b):
    nb = seq // WINDOW
    d = q.shape[1]
    step_of = lambda b, n: b * nb + n
    a_in, a_out, a_shape = _side_cast_specs(cast_a, batch * nb, step_of)
    b_in, b_out, b_shape = _side_cast_specs(cast_b, batch * nb, step_of)
    in_specs = [
        pl.BlockSpec(memory_space=pltpu.SMEM),
        pl.BlockSpec((WINDOW, d), lambda b, n: (b * nb + n, 0)),
        pl.BlockSpec((WINDOW, 2 * KV_DIM), lambda b, n: (b * nb + jnp.maximum(n - 1, 0), 0)),
        pl.BlockSpec((WINDOW, 2 * KV_DIM), lambda b, n: (b * nb + n, 0)),
        _const_spec((N_HEADS, WINDOW, 2 * WINDOW)),
        a_in, b_in,
    ]
    return pl.pallas_call(
        _attn_prompt_kernel,
        grid=(batch, nb),
        in_specs=in_specs,
        out_specs=[pl.BlockSpec((WINDOW, d), lambda b, n: (b * nb + n, 0)), a_out, b_out],
        out_shape=[jax.ShapeDtypeStruct((batch * seq, d), BF16), a_shape, b_shape],
        compiler_params=_params("arbitrary", "arbitrary"),
        name="attn_prompt",
    )(sinks, q, kv, kv, bias, cast_a, cast_b)


SAMPLE_GROUP = 8


def _attn_sample_kernel(q_ref, ck_ref, cv_ref, kn_ref, vn_ref, bias_ref, sink_ref, o_ref):
    gb = q_ref.shape[0]
    steps = kn_ref.shape[1]
    rows = steps * GROUP
    pad = (-steps) % SUBLANES
    keys = WINDOW + steps + pad
    ri = lax.broadcasted_iota(I32, (rows, keys), 0)
    si = lax.broadcasted_iota(I32, (rows, keys), 1)
    dist = ri // GROUP + WINDOW - si
    mask = (dist >= 0) & (dist < WINDOW)
    zeros = jnp.zeros((pad, KV_DIM), F32)
    for b in range(gb):
        k_all = jnp.concatenate([ck_ref[b], kn_ref[b], zeros], axis=0).astype(BF16)
        v_all = jnp.concatenate([cv_ref[b], vn_ref[b], zeros], axis=0).astype(BF16)
        for j in range(N_KV):
            k = k_all[:, j * HEAD_DIM:(j + 1) * HEAD_DIM]
            v = v_all[:, j * HEAD_DIM:(j + 1) * HEAD_DIM]
            s = lax.dot_general(q_ref[b, j], k, (((1,), (1,)), ((), ())), preferred_element_type=F32)
            s = jnp.where(mask, s + bias_ref[j], NEG)
            o_ref[b, j] = _softmax_sink_pv(s, sink_ref[j], v).astype(BF16)


def _attn_sample(q4, ck, cv, kn, vn, bias_rows, sink_rows):
    nb, _, rows, hd = q4.shape
    steps = kn.shape[1]
    keys = bias_rows.shape[2]
    gb = SAMPLE_GROUP
    in_specs = [
        pl.BlockSpec((gb, N_KV, rows, hd), lambda i: (i, 0, 0, 0)),
        pl.BlockSpec((gb, WINDOW, KV_DIM), lambda i: (i, 0, 0)),
        pl.BlockSpec((gb, WINDOW, KV_DIM), lambda i: (i, 0, 0)),
        pl.BlockSpec((gb, steps, KV_DIM), lambda i: (i, 0, 0)),
        pl.BlockSpec((gb, steps, KV_DIM), lambda i: (i, 0, 0)),
        pl.BlockSpec((N_KV, rows, keys), lambda i: (0, 0, 0)),
        pl.BlockSpec((N_KV, rows, 1), lambda i: (0, 0, 0)),
    ]
    return pl.pallas_call(
        _attn_sample_kernel,
        grid=(nb // gb,),
        in_specs=in_specs,
        out_specs=pl.BlockSpec((gb, N_KV, rows, hd), lambda i: (i, 0, 0, 0)),
        out_shape=jax.ShapeDtypeStruct(q4.shape, BF16),
        compiler_params=_params("arbitrary"),
        name="attn_sample",
    )(q4, ck, cv, kn, vn, bias_rows, sink_rows)


def _split_bf16(x):
    hi = x.astype(BF16)
    lo = (x - hi.astype(F32)).astype(BF16)
    return hi, lo


def _oproj_router_kernel(op_ref, os_ref, h2_ref, wo_ref, bo_ref, gf_ref, wrh_ref, wrl_ref,
                         h3_ref, route_ref, counts_ref):
    tm = h2_ref.shape[0]
    h3 = h2_ref[...] + _dot(_two_part_tile(op_ref, os_ref), wo_ref[...]) + bo_ref[...]
    h3_ref[...] = h3
    u_hi, u_lo = _split_bf16(_rms(h3, gf_ref[...]))
    logits = _dot(u_hi, wrh_ref[...]) + (_dot(u_lo, wrh_ref[...]) + _dot(u_hi, wrl_ref[...]))

    lane = lax.broadcasted_iota(I32, (tm, LANES), 1)
    lanef = lane.astype(F32)
    logits = jnp.where(lane < N_EXPERTS, logits, -jnp.inf)
    v1 = jnp.max(logits, axis=-1, keepdims=True)
    e1 = jnp.min(jnp.where(logits == v1, lanef, float(LANES)), axis=-1, keepdims=True)
    rest = jnp.where(lanef == e1, -jnp.inf, logits)
    v2 = jnp.max(rest, axis=-1, keepdims=True)
    e2 = jnp.min(jnp.where(rest == v2, lanef, float(LANES)), axis=-1, keepdims=True)
    ex = jnp.exp(v2 - v1)
    w1 = 1.0 / (1.0 + ex)
    w2 = ex / (1.0 + ex)
    pick1 = lanef == e1
    pick2 = lanef == e2
    sel = jnp.where(pick1 | pick2, 1.0, 0.0)

    ri = lax.broadcasted_iota(I32, (tm, tm), 0)
    ci = lax.broadcasted_iota(I32, (tm, tm), 1)
    earlier = jnp.where(ri > ci, 1.0, 0.0).astype(BF16)
    rank = _dot(earlier, sel.astype(BF16))
    cnt = jnp.sum(sel, axis=0, keepdims=True)
    seg = jnp.floor((cnt + (SUBLANES - 1)) * (1.0 / SUBLANES)) * SUBLANES
    ek = lax.broadcasted_iota(I32, (LANES, LANES), 0)
    el = lax.broadcasted_iota(I32, (LANES, LANES), 1)
    lower_experts = jnp.where(ek < el, 1.0, 0.0).astype(BF16)
    seg_start = _dot(jnp.broadcast_to(seg, (SUBLANES, LANES)).astype(BF16), lower_experts)[0:1, :]
    local = rank + seg_start
    lr1 = jnp.sum(jnp.where(pick1, local, 0.0), axis=-1, keepdims=True)
    lr2 = jnp.sum(jnp.where(pick2, local, 0.0), axis=-1, keepdims=True)
    route_ref[...] = jnp.where(lane == 0, lr1, jnp.where(lane == 1, lr2, jnp.where(lane == 2, w1, jnp.where(lane == 3, w2, 0.0))))
    counts_ref[0] = jnp.broadcast_to(cnt, (SUBLANES, LANES))


def _oproj_router(o_p, o_s, h2, w):
    n, d = h2.shape
    tm = ROW_TILE
    row = lambda width: pl.BlockSpec((tm, width), lambda i: (i, 0))
    in_specs = _two_part_specs(o_p.shape[0], o_s.shape[0], d) + [
        row(d), _const_spec((d, d)), _const_spec((1, d)), _const_spec((1, d)),
        _const_spec((d, LANES)), _const_spec((d, LANES))]
    return pl.pallas_call(
        _oproj_router_kernel,
        grid=(n // tm,),
        in_specs=in_specs,
        out_specs=[row(d), row(LANES), pl.BlockSpec((1, SUBLANES, LANES), lambda i: (i, 0, 0))],
        out_shape=[jax.ShapeDtypeStruct((n, d), F32), jax.ShapeDtypeStruct((n, LANES), F32),
                   jax.ShapeDtypeStruct((n // tm, SUBLANES, LANES), F32)],
        compiler_params=_params("arbitrary"),
        name="oproj_router",
    )(o_p, o_s, h2, w["wo"], w["bo"], w["gf"], w["wr_hi"], w["wr_lo"])


LOCAL_ROWS = 1152
XS_WIDTH = D_MODEL + LANES
SEG_TABLE = 3 * N_EXPERTS
SEG_PIECES = tuple(SUBLANES << b for b in reversed(range(7)))


def _segment_copies(src_ref, src_start, dst_ref, dst_start, length, sem, act):
    for piece in SEG_PIECES:
        done = length & (-2 * piece)

        @pl.when((length & piece) != 0)
        def _():
            s = pl.multiple_of(src_start + done, SUBLANES)
            t = pl.multiple_of(dst_start + done, SUBLANES)
            act(pltpu.make_async_copy(src_ref.at[pl.ds(s, piece), :], dst_ref.at[pl.ds(t, piece), :], sem))


def _sort_place_kernel(tbl_ref, tail_ref, nu_ref, h3_ref, route_ref, g_ref, xs_ref, stage, zeros, sems):
    j = pl.program_id(0)
    tm = h3_ref.shape[0]
    xn = _rms(h3_ref[...], g_ref[...]).astype(BF16)
    rt = jnp.transpose(route_ref[...])
    row = lax.broadcasted_iota(I32, (LOCAL_ROWS, tm), 0).astype(F32)
    m1 = row == rt[0:1, :]
    m2 = row == rt[1:2, :]
    onehot = jnp.where(m1 | m2, 1.0, 0.0).astype(BF16)
    slot = lax.rem(j, 2)
    buf = stage.at[slot]
    buf[:, 0:D_MODEL] = _dot(onehot, xn)
    gate = jnp.sum(jnp.where(m1, rt[2:3, :], 0.0) + jnp.where(m2, rt[3:4, :], 0.0), axis=-1, keepdims=True)
    buf[:, D_MODEL:XS_WIDTH] = jnp.broadcast_to(gate, (LOCAL_ROWS, LANES))

    def segments(tile, tile_slot, act):
        for e in range(N_EXPERTS):
            base = tile * SEG_TABLE
            _segment_copies(stage.at[tile_slot], tbl_ref[base + e], xs_ref, tbl_ref[base + 2 * N_EXPERTS + e],
                            tbl_ref[base + N_EXPERTS + e], sems.at[tile_slot], act)

    segments(j, slot, lambda cp: cp.start())

    @pl.when(j > 0)
    def _():
        segments(j - 1, 1 - slot, lambda cp: cp.wait())

    @pl.when(j == pl.num_programs(0) - 1)
    def _():
        segments(j, slot, lambda cp: cp.wait())
        zeros[...] = jnp.zeros_like(zeros)
        sem = sems.at[0]

        def tails(act):
            for e in range(N_EXPERTS):
                _segment_copies(zeros, 0, xs_ref, tail_ref[e], tail_ref[N_EXPERTS + e], sem, act)

        def unused(act):
            def body(i, c):
                t = pl.multiple_of(i * MOE_TILE, MOE_TILE)
                act(pltpu.make_async_copy(zeros, xs_ref.at[pl.ds(t, MOE_TILE), :], sem))
                return c
            lax.fori_loop(nu_ref[0], xs_ref.shape[0] // MOE_TILE, body, 0)

        tails(lambda cp: cp.start())
        unused(lambda cp: cp.start())
        tails(lambda cp: cp.wait())
        unused(lambda cp: cp.wait())


def _sort_place(h3, route, g, tbl, tail, n_used, p_rows):
    n, d = h3.shape
    tm = ROW_TILE
    grid_spec = pltpu.PrefetchScalarGridSpec(
        num_scalar_prefetch=3,
        grid=(n // tm,),
        in_specs=[pl.BlockSpec((tm, d), lambda j, *_: (j, 0)),
                  pl.BlockSpec((tm, LANES), lambda j, *_: (j, 0)),
                  pl.BlockSpec((1, d), lambda j, *_: (0, 0))],
        out_specs=pl.BlockSpec(memory_space=pl.ANY),
        scratch_shapes=[pltpu.VMEM((2, LOCAL_ROWS, XS_WIDTH), F32), pltpu.VMEM((MOE_TILE, XS_WIDTH), F32),
                        pltpu.SemaphoreType.DMA((2,))],
    )
    return pl.pallas_call(
        _sort_place_kernel,
        grid_spec=grid_spec,
        out_shape=jax.ShapeDtypeStruct((p_rows, XS_WIDTH), F32),
        compiler_params=_params("arbitrary"),
        name="moe_sort_place",
    )(tbl, tail, n_used, h3, route, g)


def _moe_kernel(te_ref, nu_ref, xs_ref, wg_ref, wu_ref, wd_ref, y_ref):
    i = pl.program_id(0)
    c = pl.program_id(1)
    used = i < nu_ref[0]

    @pl.when(used)
    def _():
        xn = xs_ref[:, 0:D_MODEL].astype(BF16)
        g = _dot(xn, wg_ref[0])
        up = _dot(xn, wu_ref[0])
        mid = (g * _sigmoid(g) * up).astype(BF16)
        y = _dot(mid, wd_ref[0]) * xs_ref[:, D_MODEL:D_MODEL + 1]

        @pl.when(c == 0)
        def _():
            y_ref[...] = y

        @pl.when(c > 0)
        def _():
            y_ref[...] = y_ref[...] + y

    @pl.when(jnp.logical_not(used) & (c == 0))
    def _():
        y_ref[...] = jnp.zeros_like(y_ref)


def _moe_experts(xs, tile_expert, n_used, wg, wu, wd):
    p_rows = xs.shape[0]
    d = D_MODEL
    tm = MOE_TILE
    n_tiles = p_rows // tm
    ch = D_EXPERT // MOE_CHUNKS
    last = MOE_CHUNKS - 1

    def chunk(i, c, nu):
        return jnp.where(i < nu[0], c, last)

    grid_spec = pltpu.PrefetchScalarGridSpec(
        num_scalar_prefetch=2,
        grid=(n_tiles, MOE_CHUNKS),
        in_specs=[
            pl.BlockSpec((tm, XS_WIDTH), lambda i, c, te, nu: (i, 0)),
            pl.BlockSpec((1, d, ch), lambda i, c, te, nu: (te[i], 0, chunk(i, c, nu))),
            pl.BlockSpec((1, d, ch), lambda i, c, te, nu: (te[i], 0, chunk(i, c, nu))),
            pl.BlockSpec((1, ch, d), lambda i, c, te, nu: (te[i], chunk(i, c, nu), 0)),
        ],
        out_specs=pl.BlockSpec((tm, d), lambda i, c, te, nu: (i, 0)),
    )
    return pl.pallas_call(
        _moe_kernel,
        grid_spec=grid_spec,
        out_shape=jax.ShapeDtypeStruct((p_rows, d), F32),
        compiler_params=_params("arbitrary", "arbitrary"),
        name="moe_experts",
    )(tile_expert, n_used, xs, wg, wu, wd)


def _combine_kernel(tbl_ref, h3_ref, route_ref, ys_ref, g_ref, outp_ref, outs_ref, ybuf, sems):
    j = pl.program_id(0)
    last = pl.num_programs(0) - 1
    tm = h3_ref.shape[0]
    slot = lax.rem(j, 2)

    def segments(tile, tile_slot, act):
        for e in range(N_EXPERTS):
            base = tile * SEG_TABLE
            _segment_copies(ys_ref, tbl_ref[base + 2 * N_EXPERTS + e], ybuf.at[tile_slot], tbl_ref[base + e],
                            tbl_ref[base + N_EXPERTS + e], sems.at[tile_slot], act)

    @pl.when(j == 0)
    def _():
        ybuf[...] = jnp.zeros_like(ybuf)
        segments(j, slot, lambda cp: cp.start())

    @pl.when(j < last)
    def _():
        segments(j + 1, 1 - slot, lambda cp: cp.start())

    segments(j, slot, lambda cp: cp.wait())

    route = route_ref[...]
    col = lax.broadcasted_iota(I32, (tm, LOCAL_ROWS), 1).astype(F32)
    picks = jnp.where((col == route[:, 0:1]) | (col == route[:, 1:2]), 1.0, 0.0).astype(BF16)
    h4 = h3_ref[...] + _dot(picks, ybuf[slot].astype(BF16))
    out = _rms(h4, g_ref[...])

    @pl.when(j < last)
    def _():
        outp_ref[...] = out

    @pl.when(j == last)
    def _():
        outs_ref[...] = out


def _combine_final(h3, route, tbl, ys, g_final, n_prompt):
    n, d = h3.shape
    tm = ROW_TILE
    n_sample = n - n_prompt
    assert n_sample == tm and n_prompt % tm == 0
    last_prompt = n_prompt // tm - 1
    grid_spec = pltpu.PrefetchScalarGridSpec(
        num_scalar_prefetch=1,
        grid=(n // tm,),
        in_specs=[pl.BlockSpec((tm, d), lambda j, *_: (j, 0)),
                  pl.BlockSpec((tm, LANES), lambda j, *_: (j, 0)),
                  pl.BlockSpec(memory_space=pl.ANY),
                  pl.BlockSpec((1, d), lambda j, *_: (0, 0))],
        out_specs=[pl.BlockSpec((tm, d), lambda j, *_: (jnp.minimum(j, last_prompt), 0)),
                   pl.BlockSpec((tm, d), lambda j, *_: (0, 0))],
        scratch_shapes=[pltpu.VMEM((2, LOCAL_ROWS, d), F32), pltpu.SemaphoreType.DMA((2,))],
    )
    return pl.pallas_call(
        _combine_kernel,
        grid_spec=grid_spec,
        out_shape=[jax.ShapeDtypeStruct((n_prompt, d), F32), jax.ShapeDtypeStruct((n_sample, d), F32)],
        compiler_params=_params("arbitrary"),
        name="moe_combine",
    )(tbl, h3, route, ys, g_final)


def _row(v):
    return v.reshape(1, -1).astype(F32)


def _routing_tables(counts):
    n_row_tiles = counts.shape[0]
    cnt = counts[:, 0, :N_EXPERTS].astype(I32)
    seg = ((cnt + SUBLANES - 1) // SUBLANES) * SUBLANES
    local_start = jnp.cumsum(seg, axis=1) - seg
    rows = jnp.sum(seg, axis=0)
    padded = ((rows + MOE_TILE - 1) // MOE_TILE) * MOE_TILE
    ends = jnp.cumsum(padded)
    starts = ends - padded
    sorted_start = starts[None, :] + jnp.cumsum(seg, axis=0) - seg
    tbl = jnp.concatenate([local_start, seg, sorted_start], axis=1).reshape(-1).astype(I32)
    tail = jnp.concatenate([starts + rows, padded - rows]).astype(I32)
    max_rows = n_row_tiles * (TOP_K * ROW_TILE + N_EXPERTS * (SUBLANES - 1)) + N_EXPERTS * (MOE_TILE - SUBLANES)
    n_tiles = -(-max_rows // MOE_TILE)
    tile_start = jnp.arange(n_tiles, dtype=I32) * MOE_TILE
    tile_expert = jnp.minimum(jnp.sum((tile_start[:, None] >= ends[None, :]).astype(I32), axis=1), N_EXPERTS - 1)
    n_used = (ends[-1] // MOE_TILE).reshape(1).astype(I32)
    return tbl, tail, tile_expert.astype(I32), n_used, n_tiles


def kernel(x_prompt, x_sample, state_conv, state_h, cache_k, cache_v, g_mix, g_ffn, g_kv, g_final, a_w_gate, a_b_gate, a_w_in, a_b_in, a_conv_w, a_conv_b, a_w_r, a_b_r, a_w_i, a_b_i, a_lam, a_w_out, a_b_out, w_kv, b_kv, rel_bias, b_w_q, b_b_q, b_sinks, b_w_o, b_b_o, f_w_gate, f_w_up, f_w_down, m_w_router, m_w_gate, m_w_up, m_w_down):
    bp, seq, d = x_prompt.shape
    bs, steps, _ = x_sample.shape
    n_prompt = bp * seq
    n_sample = bs * steps
    n = n_prompt + n_sample
    assert seq % MIX_TILE == 0 and seq % WINDOW == 0 and n_prompt % ROW_TILE == 0
    assert n_sample == ROW_TILE and bs % SAMPLE_GROUP == 0

    mix_w = dict(g=_row(g_mix[0]), wg=a_w_gate[0].astype(BF16), bg=_row(a_b_gate[0]),
                 win=a_w_in[0].astype(BF16), bin=_row(a_b_in[0]), cw=a_conv_w[0], cb=_row(a_conv_b[0]),
                 wr=a_w_r[0].astype(BF16), br=_row(a_b_r[0]), wi=a_w_i[0].astype(BF16), bi=_row(a_b_i[0]),
                 lam=_row(a_lam[0]), wout=a_w_out[0].astype(BF16), bout=_row(a_b_out[0]))

    h1_p, p_conv, p_h = _mixer_prompt(x_prompt, mix_w)
    h1_s, s_conv_tm, s_h = _mixer_sample(jnp.transpose(x_sample, (1, 0, 2)),
                                         jnp.transpose(state_conv[0], (1, 0, 2)), state_h[0], mix_w)
    h1_s = jnp.transpose(h1_s, (1, 0, 2)).reshape(n_sample, d)

    ffn_w = dict(gf=_row(g_ffn[0]), wg=f_w_gate[0].astype(BF16), wu=f_w_up[0].astype(BF16),
                 wd=f_w_down[0].astype(BF16), gkv=_row(g_kv), wkv=w_kv.astype(BF16), bkv=_row(b_kv),
                 gq=_row(g_mix[1]), wq=b_w_q[0].astype(BF16), bq=_row(b_b_q[0]))
    h2, kv, q, moe_wd = _ffn_kvq(h1_p, h1_s, ffn_w, m_w_down[0])

    bias = _bias_band(rel_bias)
    sinks = b_sinks[0].astype(F32)
    o_p, moe_wg, moe_wu = _attn_prompt(q, kv, bias, sinks, bp, seq, m_w_gate[0], m_w_up[0])
    keys = WINDOW + steps + (-steps) % SUBLANES
    q4 = q[n_prompt:].reshape(bs, steps, N_KV, GROUP, HEAD_DIM).transpose(0, 2, 1, 3, 4)
    q4 = q4.reshape(bs, N_KV, steps * GROUP, HEAD_DIM)
    kv_s = kv[n_prompt:].reshape(bs, steps, 2 * KV_DIM)
    k_new, v_new = kv_s[:, :, :KV_DIM], kv_s[:, :, KV_DIM:]
    bias_rows = bias[:, :steps, :keys].reshape(N_KV, GROUP, steps, keys).transpose(0, 2, 1, 3)
    bias_rows = bias_rows.reshape(N_KV, steps * GROUP, keys)
    sink_rows = jnp.broadcast_to(sinks.reshape(N_KV, 1, GROUP), (N_KV, steps, GROUP)).reshape(N_KV, steps * GROUP, 1)
    o4 = _attn_sample(q4, cache_k.reshape(bs, WINDOW, KV_DIM), cache_v.reshape(bs, WINDOW, KV_DIM),
                      k_new, v_new, bias_rows, sink_rows)
    o_s = o4.reshape(bs, N_KV, steps, GROUP, HEAD_DIM).transpose(0, 2, 1, 3, 4).reshape(n_sample, d)

    wr_pad = jnp.zeros((d, LANES), F32).at[:, :N_EXPERTS].set(m_w_router[0])
    wr_hi = wr_pad.astype(BF16)
    wr_lo = (wr_pad - wr_hi.astype(F32)).astype(BF16)
    h3, route, counts = _oproj_router(
        o_p, o_s, h2, dict(wo=b_w_o[0].astype(BF16), bo=_row(b_b_o[0]), gf=_row(g_ffn[1]), wr_hi=wr_hi, wr_lo=wr_lo))

    tbl, tail, tile_expert, n_used, n_tiles = _routing_tables(counts)
    xs = _sort_place(h3, route, _row(g_ffn[1]), tbl, tail, n_used, n_tiles * MOE_TILE)
    ys = _moe_experts(xs, tile_expert, n_used, moe_wg, moe_wu, moe_wd)
    y_p, y_s = _combine_final(h3, route, tbl, ys, _row(g_final), n_prompt)

    y_prompt = y_p.reshape(bp, seq, d)
    y_sample = y_s.reshape(bs, steps, d)
    kv_last = jnp.stack([kv[b * seq + seq - WINDOW:(b + 1) * seq] for b in range(bp)])
    p_k = kv_last[:, :, :KV_DIM].reshape(bp, WINDOW, N_KV, HEAD_DIM)
    p_v = kv_last[:, :, KV_DIM:].reshape(bp, WINDOW, N_KV, HEAD_DIM)
    s_k = jnp.concatenate([cache_k, k_new.reshape(bs, steps, N_KV, HEAD_DIM)], axis=1)[:, -WINDOW:]
    s_v = jnp.concatenate([cache_v, v_new.reshape(bs, steps, N_KV, HEAD_DIM)], axis=1)[:, -WINDOW:]
    return (y_prompt, y_sample, p_conv[None], p_h.reshape(1, bp, LRU_WIDTH), p_k, p_v,
            jnp.transpose(s_conv_tm, (1, 0, 2))[None], s_h[None], s_k, s_v)
```

```python
import math

import jax
import jax.numpy as jnp
from jax import lax
from jax.experimental import pallas as pl
from jax.experimental.pallas import tpu as pltpu

D_MODEL = 1024
LRU_WIDTH = D_MODEL
LRU_BLOCK_W = 256
LRU_BLOCKS = LRU_WIDTH // LRU_BLOCK_W
CONV_W = 4
LRU_C = 8.0
HEAD_DIM = 64
N_HEADS = D_MODEL // HEAD_DIM
N_KV = 2
GROUP = N_HEADS // N_KV
KV_DIM = N_KV * HEAD_DIM
WINDOW = 128
NUM_BUCKETS = 32
MAX_DISTANCE = 128
D_FF = 3 * D_MODEL
N_EXPERTS = 8
TOP_K = 2
D_EXPERT = 7 * D_MODEL // 2
EPS = 1e-6
NEG = -1e30

BF16 = jnp.bfloat16
F32 = jnp.float32
I32 = jnp.int32

SUBLANES = 8
LANES = 128
VMEM_LIMIT_BYTES = 56 * 1024 * 1024

ROW_TILE = 512
MIX_TILE = 256
FF_CHUNK = 1024
MOE_TILE = 512
MOE_CHUNKS = 2


def _params(*semantics):
    return pltpu.CompilerParams(dimension_semantics=semantics, vmem_limit_bytes=VMEM_LIMIT_BYTES)


def _const_spec(shape):
    zeros = (0,) * len(shape)
    return pl.BlockSpec(shape, lambda *_: zeros, pipeline_mode=pl.Buffered(1))


def _dot(a, b):
    return jnp.dot(a, b, preferred_element_type=F32)


def _rms(x, g):
    ms = jnp.mean(x * x, axis=-1, keepdims=True)
    return x * lax.rsqrt(ms + EPS) * g


def _sigmoid(x):
    return 1.0 / (1.0 + jnp.exp(-x))


def _gelu_tanh(x):
    return 0.5 * x * (1.0 + jnp.tanh(0.7978845608028654 * (x + 0.044715 * (x * x * x))))


def _log_sigmoid(x):
    return jnp.minimum(x, 0.0) - jnp.log1p(jnp.exp(-jnp.abs(x)))


def _lru_gates(xc, wr_ref, br, wi_ref, bi, lam):
    xcb = xc.astype(BF16)
    rs, gs = [], []
    for n in range(LRU_BLOCKS):
        xn = xcb[:, n * LRU_BLOCK_W:(n + 1) * LRU_BLOCK_W]
        rs.append(_dot(xn, wr_ref[n]))
        gs.append(_dot(xn, wi_ref[n]))
    r = _sigmoid(jnp.concatenate(rs, axis=1) + br)
    i = _sigmoid(jnp.concatenate(gs, axis=1) + bi)
    log_a = LRU_C * r * _log_sigmoid(lam)
    a = jnp.exp(log_a)
    mult = jnp.sqrt(1.0 - a * a)
    return a, mult * (i * xc)


def _mixer_prompt_kernel(x_ref, g_ref, wg_ref, bg_ref, win_ref, bin_ref, cw_ref, cb_ref,
                         wr_ref, br_ref, wi_ref, bi_ref, lam_ref, wout_ref, bout_ref,
                         h1_ref, conv_ref, hlast_ref, xr_buf, h_carry):
    t = pl.program_id(1)
    tt = x_ref.shape[1]
    pad = SUBLANES

    @pl.when(t == 0)
    def _():
        xr_buf[0:pad, :] = jnp.zeros((pad, LRU_WIDTH), F32)
        h_carry[...] = jnp.zeros_like(h_carry)

    x = x_ref[0]
    u = _rms(x, g_ref[...]).astype(BF16)
    gate = _gelu_tanh(_dot(u, wg_ref[...]) + bg_ref[...])
    xr = _dot(u, win_ref[...]) + bin_ref[...]
    xr_buf[pad:pad + tt, :] = xr
    xc = cb_ref[...] + cw_ref[CONV_W - 1:CONV_W, :] * xr
    for k in range(CONV_W - 1):
        back = CONV_W - 1 - k
        xc = xc + cw_ref[k:k + 1, :] * xr_buf[pad - back:pad - back + tt, :]
    xr_buf[0:pad, :] = xr[tt - pad:tt, :]
    conv_ref[0] = xr[tt - (CONV_W - 1):tt, :]

    a, b = _lru_gates(xc, wr_ref, br_ref[...], wi_ref, bi_ref[...], lam_ref[...])

    groups = tt // SUBLANES
    a3 = a.reshape(groups, SUBLANES, LRU_WIDTH)
    b3 = b.reshape(groups, SUBLANES, LRU_WIDTH)
    row = lax.broadcasted_iota(I32, (1, SUBLANES, LRU_WIDTH), 1)
    step = 1
    while step < SUBLANES:
        keep = row >= step
        a_prev = jnp.where(keep, pltpu.roll(a3, step, axis=1), 1.0)
        b_prev = jnp.where(keep, pltpu.roll(b3, step, axis=1), 0.0)
        b3 = b3 + a3 * b_prev
        a3 = a3 * a_prev
        step *= 2
    h_prev = h_carry[0:1, :]
    hs = []
    for gi in range(groups):
        hg = b3[gi] + a3[gi] * h_prev
        hs.append(hg)
        h_prev = hg[SUBLANES - 1:SUBLANES, :]
    h = jnp.concatenate(hs, axis=0)
    h_carry[0:1, :] = h_prev
    hlast_ref[0] = h_prev

    y = _dot((h * gate).astype(BF16), wout_ref[...]) + bout_ref[...]
    h1_ref[...] = x + y


def _mixer_prompt(x, w):
    b, t, d = x.shape
    tt = MIX_TILE
    nt = t // tt
    vec = lambda n: _const_spec((1, n))
    in_specs = [
        pl.BlockSpec((1, tt, d), lambda bi, ti: (bi, ti, 0)),
        vec(d), _const_spec((d, LRU_WIDTH)), vec(LRU_WIDTH), _const_spec((d, LRU_WIDTH)), vec(LRU_WIDTH),
        _const_spec((CONV_W, LRU_WIDTH)), vec(LRU_WIDTH),
        _const_spec((LRU_BLOCKS, LRU_BLOCK_W, LRU_BLOCK_W)), vec(LRU_WIDTH),
        _const_spec((LRU_BLOCKS, LRU_BLOCK_W, LRU_BLOCK_W)), vec(LRU_WIDTH),
        vec(LRU_WIDTH), _const_spec((LRU_WIDTH, d)), vec(d),
    ]
    out_specs = [
        pl.BlockSpec((tt, d), lambda bi, ti: (bi * nt + ti, 0)),
        pl.BlockSpec((1, CONV_W - 1, LRU_WIDTH), lambda bi, ti: (bi, 0, 0)),
        pl.BlockSpec((1, 1, LRU_WIDTH), lambda bi, ti: (bi, 0, 0)),
    ]
    out_shape = [
        jax.ShapeDtypeStruct((b * t, d), F32),
        jax.ShapeDtypeStruct((b, CONV_W - 1, LRU_WIDTH), F32),
        jax.ShapeDtypeStruct((b, 1, LRU_WIDTH), F32),
    ]
    return pl.pallas_call(
        _mixer_prompt_kernel,
        grid=(b, nt),
        in_specs=in_specs,
        out_specs=out_specs,
        out_shape=out_shape,
        scratch_shapes=[pltpu.VMEM((SUBLANES + tt, LRU_WIDTH), F32), pltpu.VMEM((SUBLANES, LRU_WIDTH), F32)],
        compiler_params=_params("arbitrary", "arbitrary"),
        name="mixer_prompt",
    )(x, w["g"], w["wg"], w["bg"], w["win"], w["bin"], w["cw"], w["cb"], w["wr"], w["br"],
      w["wi"], w["bi"], w["lam"], w["wout"], w["bout"])


def _mixer_sample_kernel(x_ref, cs_ref, h0_ref, g_ref, wg_ref, bg_ref, win_ref, bin_ref, cw_ref, cb_ref,
                         wr_ref, br_ref, wi_ref, bi_ref, lam_ref, wout_ref, bout_ref,
                         h1_ref, conv_ref, hlast_ref):
    steps, nb, d = x_ref.shape
    x = x_ref[...].reshape(steps * nb, d)
    u = _rms(x, g_ref[...]).astype(BF16)
    gate = _gelu_tanh(_dot(u, wg_ref[...]) + bg_ref[...])
    xr = _dot(u, win_ref[...]) + bin_ref[...]
    xpad = [cs_ref[k] for k in range(CONV_W - 1)] + [xr[s * nb:(s + 1) * nb, :] for s in range(steps)]
    xcs = []
    for s in range(steps):
        acc = cb_ref[...] + cw_ref[0:1, :] * xpad[s]
        for k in range(1, CONV_W):
            acc = acc + cw_ref[k:k + 1, :] * xpad[s + k]
        xcs.append(acc)
    for k in range(CONV_W - 1):
        conv_ref[k] = xpad[steps + k]
    xc = jnp.concatenate(xcs, axis=0)
    a, b = _lru_gates(xc, wr_ref, br_ref[...], wi_ref, bi_ref[...], lam_ref[...])
    h = h0_ref[...]
    hs = []
    for s in range(steps):
        h = a[s * nb:(s + 1) * nb, :] * h + b[s * nb:(s + 1) * nb, :]
        hs.append(h)
    hlast_ref[...] = h
    hcat = jnp.concatenate(hs, axis=0)
    y = _dot((hcat * gate).astype(BF16), wout_ref[...]) + bout_ref[...]
    h1_ref[...] = (x + y).reshape(steps, nb, d)


def _mixer_sample(x_tm, cs_tm, h0, w):
    steps, nb, d = x_tm.shape
    full = lambda shape: pl.BlockSpec(shape, lambda i: (0,) * len(shape))
    vec = lambda n: full((1, n))
    in_specs = [
        full((steps, nb, d)), full((CONV_W - 1, nb, LRU_WIDTH)), full((nb, LRU_WIDTH)),
        vec(d), full((d, LRU_WIDTH)), vec(LRU_WIDTH), full((d, LRU_WIDTH)), vec(LRU_WIDTH),
        full((CONV_W, LRU_WIDTH)), vec(LRU_WIDTH),
        full((LRU_BLOCKS, LRU_BLOCK_W, LRU_BLOCK_W)), vec(LRU_WIDTH),
        full((LRU_BLOCKS, LRU_BLOCK_W, LRU_BLOCK_W)), vec(LRU_WIDTH),
        vec(LRU_WIDTH), full((LRU_WIDTH, d)), vec(d),
    ]
    out_specs = [full((steps, nb, d)), full((CONV_W - 1, nb, LRU_WIDTH)), full((nb, LRU_WIDTH))]
    out_shape = [
        jax.ShapeDtypeStruct((steps, nb, d), F32),
        jax.ShapeDtypeStruct((CONV_W - 1, nb, LRU_WIDTH), F32),
        jax.ShapeDtypeStruct((nb, LRU_WIDTH), F32),
    ]
    return pl.pallas_call(
        _mixer_sample_kernel, grid=(1,), in_specs=in_specs, out_specs=out_specs, out_shape=out_shape,
        compiler_params=_params("arbitrary"), name="mixer_sample",
    )(x_tm, cs_tm, h0, w["g"], w["wg"], w["bg"], w["win"], w["bin"], w["cw"], w["cb"], w["wr"], w["br"],
      w["wi"], w["bi"], w["lam"], w["wout"], w["bout"])


def _two_part_specs(n_prompt, n_sample, width):
    assert n_sample == ROW_TILE and n_prompt % ROW_TILE == 0
    last_prompt = n_prompt // ROW_TILE - 1
    return [pl.BlockSpec((ROW_TILE, width), lambda i: (jnp.minimum(i, last_prompt), 0)),
            pl.BlockSpec((ROW_TILE, width), lambda i: (0, 0))]


def _two_part_tile(prompt_ref, sample_ref):
    is_sample = pl.program_id(0) == pl.num_programs(0) - 1
    return jnp.where(is_sample, sample_ref[...], prompt_ref[...])


def _side_cast_specs(w, n_steps, step_of):
    ne, rows, cols = w.shape
    per_expert = n_steps // ne
    blk = rows // per_expert
    assert per_expert * ne == n_steps and blk * per_expert == rows and blk % (2 * SUBLANES) == 0
    index = lambda *ids: (step_of(*ids) // per_expert, step_of(*ids) % per_expert, 0)
    spec = pl.BlockSpec((1, blk, cols), index)
    return spec, spec, jax.ShapeDtypeStruct(w.shape, BF16)


def _ffn_kvq_kernel(h1p_ref, h1s_ref, gf_ref, wg_ref, wu_ref, wd_ref, gkv_ref, wkv_ref, bkv_ref,
                    gq_ref, wq_ref, bq_ref, cast_ref, h2_ref, kv_ref, q_ref, cast_out_ref):
    cast_out_ref[...] = cast_ref[...].astype(BF16)
    h1 = _two_part_tile(h1p_ref, h1s_ref)
    u = _rms(h1, gf_ref[...]).astype(BF16)
    acc = h1
    for c in range(D_FF // FF_CHUNK):
        cols = slice(c * FF_CHUNK, (c + 1) * FF_CHUNK)
        g = _dot(u, wg_ref[:, cols])
        up = _dot(u, wu_ref[:, cols])
        mid = (g * _sigmoid(g) * up).astype(BF16)
        acc = acc + _dot(mid, wd_ref[cols, :])
    h2_ref[...] = acc
    kv_ref[...] = _dot(_rms(acc, gkv_ref[...]).astype(BF16), wkv_ref[...]) + bkv_ref[...]
    q = _dot(_rms(acc, gq_ref[...]).astype(BF16), wq_ref[...]) + bq_ref[...]
    q_ref[...] = (q * (HEAD_DIM ** -0.5)).astype(BF16)


def _ffn_kvq(h1_p, h1_s, w, cast_w):
    d = h1_p.shape[1]
    n = h1_p.shape[0] + h1_s.shape[0]
    tm = ROW_TILE
    row = lambda width: pl.BlockSpec((tm, width), lambda i: (i, 0))
    vec = lambda width: _const_spec((1, width))
    prompt_steps = h1_p.shape[0] // tm
    cast_in, cast_out, cast_shape = _side_cast_specs(cast_w, prompt_steps, lambda i: jnp.minimum(i, prompt_steps - 1))
    in_specs = _two_part_specs(h1_p.shape[0], h1_s.shape[0], d) + [
        vec(d), _const_spec((d, D_FF)), _const_spec((d, D_FF)), _const_spec((D_FF, d)),
        vec(d), _const_spec((d, 2 * KV_DIM)), vec(2 * KV_DIM),
        vec(d), _const_spec((d, d)), vec(d), cast_in,
    ]
    return pl.pallas_call(
        _ffn_kvq_kernel,
        grid=(n // tm,),
        in_specs=in_specs,
        out_specs=[row(d), row(2 * KV_DIM), row(d), cast_out],
        out_shape=[jax.ShapeDtypeStruct((n, d), F32), jax.ShapeDtypeStruct((n, 2 * KV_DIM), F32),
                   jax.ShapeDtypeStruct((n, d), BF16), cast_shape],
        compiler_params=_params("arbitrary"),
        name="ffn_kvq",
    )(h1_p, h1_s, w["gf"], w["wg"], w["wu"], w["wd"], w["gkv"], w["wkv"], w["bkv"], w["gq"], w["wq"], w["bq"],
      cast_w)


def _head_of(j, parity, pair):
    return j * GROUP + 2 * pair + parity


def _bias_band_kernel(table_ref, out_ref):
    q_len, s_len = WINDOW, out_ref.shape[3]
    qi = lax.broadcasted_iota(I32, (q_len, s_len), 0)
    si = lax.broadcasted_iota(I32, (q_len, s_len), 1)
    dist = qi + WINDOW - si
    max_exact = NUM_BUCKETS // 2
    n = jnp.maximum(dist, 0)
    nf = jnp.maximum(n, max_exact).astype(F32)
    log_ratio = math.log(MAX_DISTANCE / max_exact)
    large = max_exact + (jnp.log(nf / max_exact) / log_ratio * (NUM_BUCKETS - max_exact)).astype(I32)
    large = jnp.minimum(large, NUM_BUCKETS - 1)
    bucket = jnp.where(n < max_exact, n, large)
    for j in range(N_KV):
        for parity in range(2):
            for pair in range(GROUP // 2):
                h = _head_of(j, parity, pair)
                acc = jnp.zeros((q_len, s_len), F32)
                for bkt in range(NUM_BUCKETS):
                    acc = jnp.where(bucket == bkt, table_ref[bkt, h], acc)
                out_ref[j, parity, pair * q_len:(pair + 1) * q_len, :] = acc


def _bias_band(rel_table):
    shape = (N_KV, 2, (GROUP // 2) * WINDOW, 2 * WINDOW)
    return pl.pallas_call(
        _bias_band_kernel,
        grid=(1,),
        in_specs=[pl.BlockSpec(memory_space=pltpu.SMEM)],
        out_specs=pl.BlockSpec(shape, lambda i: (0, 0, 0, 0)),
        out_shape=jax.ShapeDtypeStruct(shape, F32),
        compiler_params=_params("arbitrary"),
        name="bias_band",
    )(rel_table)


def _softmax_sink_pv(s, sink, v):
    m = jnp.maximum(jnp.max(s, axis=-1, keepdims=True), sink)
    p = jnp.exp(s - m)
    denom = jnp.sum(p, axis=-1, keepdims=True) + jnp.exp(sink - m)
    return _dot(p.astype(BF16), v) / denom


ATTN_BLOCKS = 4


def _attn_prompt_kernel(sink_ref, q_ref, kvp_ref, kvc_ref, bias_ref, cast_a_ref, cast_b_ref,
                        o_ref, cast_a_out_ref, cast_b_out_ref):
    cast_a_out_ref[...] = cast_a_ref[...].astype(BF16)
    cast_b_out_ref[...] = cast_b_ref[...].astype(BF16)
    w = WINDOW
    pairs = GROUP // 2
    assert 2 * HEAD_DIM == LANES and KV_DIM == LANES
    qi = lax.broadcasted_iota(I32, (pairs * w, 2 * w), 0) % w
    si = lax.broadcasted_iota(I32, (pairs * w, 2 * w), 1)
    dist = qi + w - si
    band = (dist >= 0) & (dist < w)
    band_first = band & ((si >= w) | (pl.program_id(1) > 0))

    kv = jnp.concatenate([kvp_ref[...], kvc_ref[...]], axis=0)
    low = lax.broadcasted_iota(I32, (kv.shape[0], LANES), 1) < HEAD_DIM

    def halves(x):
        swapped = pltpu.roll(x, HEAD_DIM, axis=1)
        return (((jnp.where(low, x, 0.0)).astype(BF16), (jnp.where(low, 0.0, swapped)).astype(BF16)),
                ((jnp.where(low, swapped, 0.0)).astype(BF16), (jnp.where(low, 0.0, x)).astype(BF16)))

    k_ops = halves(kv[:, 0:KV_DIM])
    v_ops = halves(kv[:, KV_DIM:2 * KV_DIM])

    for i in range(ATTN_BLOCKS):
        q_rows = slice(i * w, (i + 1) * w)
        k_rows = slice(i * w, (i + 2) * w)
        mask = band_first if i == 0 else band
        for j in range(N_KV):
            q4 = jnp.concatenate([q_ref[q_rows, (j * pairs + b) * LANES:(j * pairs + b + 1) * LANES]
                                  for b in range(pairs)], axis=0)
            acc = None
            for parity in range(2):
                s = lax.dot_general(q4, k_ops[j][parity][k_rows], (((1,), (1,)), ((), ())),
                                    preferred_element_type=F32)
                s = jnp.where(mask, s + bias_ref[j, parity], NEG)
                sink = jnp.concatenate([jnp.full((w, 1), sink_ref[_head_of(j, parity, b)], F32)
                                        for b in range(pairs)], axis=0)
                o = _softmax_sink_pv(s, sink, v_ops[j][parity][k_rows])
                acc = o if acc is None else acc + o
            for b in range(pairs):
                o_ref[q_rows, (j * pairs + b) * LANES:(j * pairs + b + 1) * LANES] = acc[b * w:(b + 1) * w].astype(BF16)


def _attn_prompt(q, kv, bias, sinks, batch, seq, cast_a, cast_b):
    nb = seq // WINDOW
    assert nb % ATTN_BLOCKS == 0
    ns = nb // ATTN_BLOCKS
    rows = ATTN_BLOCKS * WINDOW
    d = q.shape[1]
    step_of = lambda b, n: b * ns + n
    a_in, a_out, a_shape = _side_cast_specs(cast_a, batch * ns, step_of)
    b_in, b_out, b_shape = _side_cast_specs(cast_b, batch * ns, step_of)
    in_specs = [
        pl.BlockSpec(memory_space=pltpu.SMEM),
        pl.BlockSpec((rows, d), lambda b, n: (b * ns + n, 0)),
        pl.BlockSpec((WINDOW, 2 * KV_DIM), lambda b, n: (b * nb + jnp.maximum(n * ATTN_BLOCKS - 1, 0), 0)),
        pl.BlockSpec((rows, 2 * KV_DIM), lambda b, n: (b * ns + n, 0)),
        _const_spec(bias.shape),
        a_in, b_in,
    ]
    return pl.pallas_call(
        _attn_prompt_kernel,
        grid=(batch, ns),
        in_specs=in_specs,
        out_specs=[pl.BlockSpec((rows, d), lambda b, n: (b * ns + n, 0)), a_out, b_out],
        out_shape=[jax.ShapeDtypeStruct((batch * seq, d), BF16), a_shape, b_shape],
        compiler_params=_params("arbitrary", "arbitrary"),
        name="attn_prompt",
    )(sinks, q, kv, kv, bias, cast_a, cast_b)


SAMPLE_GROUP = 8


def _attn_sample_kernel(q_ref, ck_ref, cv_ref, kn_ref, vn_ref, bias_ref, sink_ref, o_ref):
    gb = q_ref.shape[0]
    steps = kn_ref.shape[1]
    rows = steps * GROUP
    pad = (-steps) % SUBLANES
    keys = WINDOW + steps + pad
    ri = lax.broadcasted_iota(I32, (rows, keys), 0)
    si = lax.broadcasted_iota(I32, (rows, keys), 1)
    dist = ri // GROUP + WINDOW - si
    mask = (dist >= 0) & (dist < WINDOW)
    zeros = jnp.zeros((pad, KV_DIM), F32)

    def scores(b, j):
        k_all = jnp.concatenate([ck_ref[b], kn_ref[b], zeros], axis=0).astype(BF16)
        k = k_all[:, j * HEAD_DIM:(j + 1) * HEAD_DIM]
        s = lax.dot_general(q_ref[b, j], k, (((1,), (1,)), ((), ())), preferred_element_type=F32)
        return jnp.where(mask, s + bias_ref[j], NEG)

    tasks = [(b, j) for b in range(gb) for j in range(N_KV)]
    all_scores = [scores(b, j) for b, j in tasks]
    for (b, j), s in zip(tasks, all_scores):
        v_all = jnp.concatenate([cv_ref[b], vn_ref[b], zeros], axis=0).astype(BF16)
        v = v_all[:, j * HEAD_DIM:(j + 1) * HEAD_DIM]
        o_ref[b, j] = _softmax_sink_pv(s, sink_ref[j], v).astype(BF16)


def _attn_sample(q4, ck, cv, kn, vn, bias_rows, sink_rows):
    nb, _, rows, hd = q4.shape
    steps = kn.shape[1]
    keys = bias_rows.shape[2]
    gb = SAMPLE_GROUP
    in_specs = [
        pl.BlockSpec((gb, N_KV, rows, hd), lambda i: (i, 0, 0, 0)),
        pl.BlockSpec((gb, WINDOW, KV_DIM), lambda i: (i, 0, 0)),
        pl.BlockSpec((gb, WINDOW, KV_DIM), lambda i: (i, 0, 0)),
        pl.BlockSpec((gb, steps, KV_DIM), lambda i: (i, 0, 0)),
        pl.BlockSpec((gb, steps, KV_DIM), lambda i: (i, 0, 0)),
        pl.BlockSpec((N_KV, rows, keys), lambda i: (0, 0, 0)),
        pl.BlockSpec((N_KV, rows, 1), lambda i: (0, 0, 0)),
    ]
    return pl.pallas_call(
        _attn_sample_kernel,
        grid=(nb // gb,),
        in_specs=in_specs,
        out_specs=pl.BlockSpec((gb, N_KV, rows, hd), lambda i: (i, 0, 0, 0)),
        out_shape=jax.ShapeDtypeStruct(q4.shape, BF16),
        compiler_params=_params("arbitrary"),
        name="attn_sample",
    )(q4, ck, cv, kn, vn, bias_rows, sink_rows)


def _split_bf16(x):
    hi = x.astype(BF16)
    lo = (x - hi.astype(F32)).astype(BF16)
    return hi, lo


ROUTER_PARTS = 2


def _oproj_router_kernel(op_ref, os_ref, h2_ref, wo_ref, bo_ref, gf_ref, wrh_ref, wrl_ref,
                         h3_ref, route_ref, counts_ref):
    tm = h2_ref.shape[0]
    parts = ROUTER_PARTS
    pr = tm // parts
    o = _two_part_tile(op_ref, os_ref)
    lane = lax.broadcasted_iota(I32, (pr, LANES), 1)
    lanef = lane.astype(F32)
    ri = lax.broadcasted_iota(I32, (pr, pr), 0)
    ci = lax.broadcasted_iota(I32, (pr, pr), 1)
    earlier = jnp.where(ri > ci, 1.0, 0.0).astype(BF16)

    picks, ranks, gates = [], [], []
    cnt = jnp.zeros((1, LANES), F32)
    for p in range(parts):
        rows = slice(p * pr, (p + 1) * pr)
        h3 = h2_ref[rows, :] + _dot(o[rows, :], wo_ref[...]) + bo_ref[...]
        h3_ref[rows, :] = h3
        u_hi, u_lo = _split_bf16(_rms(h3, gf_ref[...]))
        logits = _dot(u_hi, wrh_ref[...]) + (_dot(u_lo, wrh_ref[...]) + _dot(u_hi, wrl_ref[...]))
        logits = jnp.where(lane < N_EXPERTS, logits, -jnp.inf)
        v1 = jnp.max(logits, axis=-1, keepdims=True)
        e1 = jnp.min(jnp.where(logits == v1, lanef, float(LANES)), axis=-1, keepdims=True)
        rest = jnp.where(lanef == e1, -jnp.inf, logits)
        v2 = jnp.max(rest, axis=-1, keepdims=True)
        e2 = jnp.min(jnp.where(rest == v2, lanef, float(LANES)), axis=-1, keepdims=True)
        ex = jnp.exp(v2 - v1)
        gates.append((1.0 / (1.0 + ex), ex / (1.0 + ex)))
        pick1 = lanef == e1
        pick2 = lanef == e2
        picks.append((pick1, pick2))
        sel = jnp.where(pick1 | pick2, 1.0, 0.0)
        ranks.append(_dot(earlier, sel.astype(BF16)) + cnt)
        cnt = cnt + jnp.sum(sel, axis=0, keepdims=True)

    seg = jnp.floor((cnt + (SUBLANES - 1)) * (1.0 / SUBLANES)) * SUBLANES
    ek = lax.broadcasted_iota(I32, (LANES, LANES), 0)
    el = lax.broadcasted_iota(I32, (LANES, LANES), 1)
    lower_experts = jnp.where(ek < el, 1.0, 0.0).astype(BF16)
    seg_start = _dot(jnp.broadcast_to(seg, (SUBLANES, LANES)).astype(BF16), lower_experts)[0:1, :]
    for p in range(parts):
        local = ranks[p] + seg_start
        lr1 = jnp.sum(jnp.where(picks[p][0], local, 0.0), axis=-1, keepdims=True)
        lr2 = jnp.sum(jnp.where(picks[p][1], local, 0.0), axis=-1, keepdims=True)
        w1, w2 = gates[p]
        route_ref[p * pr:(p + 1) * pr, :] = jnp.where(
            lane == 0, lr1, jnp.where(lane == 1, lr2, jnp.where(lane == 2, w1, jnp.where(lane == 3, w2, 0.0))))
    counts_ref[0] = jnp.broadcast_to(cnt, (SUBLANES, LANES))


def _oproj_router(o_p, o_s, h2, w):
    n, d = h2.shape
    tm = ROW_TILE
    row = lambda width: pl.BlockSpec((tm, width), lambda i: (i, 0))
    in_specs = _two_part_specs(o_p.shape[0], o_s.shape[0], d) + [
        row(d), _const_spec((d, d)), _const_spec((1, d)), _const_spec((1, d)),
        _const_spec((d, LANES)), _const_spec((d, LANES))]
    return pl.pallas_call(
        _oproj_router_kernel,
        grid=(n // tm,),
        in_specs=in_specs,
        out_specs=[row(d), row(LANES), pl.BlockSpec((1, SUBLANES, LANES), lambda i: (i, 0, 0))],
        out_shape=[jax.ShapeDtypeStruct((n, d), F32), jax.ShapeDtypeStruct((n, LANES), F32),
                   jax.ShapeDtypeStruct((n // tm, SUBLANES, LANES), F32)],
        compiler_params=_params("arbitrary"),
        name="oproj_router",
    )(o_p, o_s, h2, w["wo"], w["bo"], w["gf"], w["wr_hi"], w["wr_lo"])


LOCAL_ROWS = 1152
XS_WIDTH = D_MODEL + LANES
SEG_TABLE = 3 * N_EXPERTS
SEG_PIECES = tuple(SUBLANES << b for b in reversed(range(7)))


def _segment_copies(src_ref, src_start, dst_ref, dst_start, length, sem, act):
    for piece in SEG_PIECES:
        done = length & (-2 * piece)

        @pl.when((length & piece) != 0)
        def _():
            s = pl.multiple_of(src_start + done, SUBLANES)
            t = pl.multiple_of(dst_start + done, SUBLANES)
            act(pltpu.make_async_copy(src_ref.at[pl.ds(s, piece), :], dst_ref.at[pl.ds(t, piece), :], sem))


def _sort_place_kernel(tbl_ref, tail_ref, nu_ref, h3_ref, route_ref, g_ref, xs_ref, stage, zeros, sems):
    j = pl.program_id(0)
    tm = h3_ref.shape[0]
    xn = _rms(h3_ref[...], g_ref[...]).astype(BF16)
    rt = jnp.transpose(route_ref[...])
    row = lax.broadcasted_iota(I32, (LOCAL_ROWS, tm), 0).astype(F32)
    m1 = row == rt[0:1, :]
    m2 = row == rt[1:2, :]
    onehot = jnp.where(m1 | m2, 1.0, 0.0).astype(BF16)
    slot = lax.rem(j, 2)
    buf = stage.at[slot]
    buf[:, 0:D_MODEL] = _dot(onehot, xn)
    gate = jnp.sum(jnp.where(m1, rt[2:3, :], 0.0) + jnp.where(m2, rt[3:4, :], 0.0), axis=-1, keepdims=True)
    buf[:, D_MODEL:XS_WIDTH] = jnp.broadcast_to(gate, (LOCAL_ROWS, LANES))

    def segments(tile, tile_slot, act):
        for e in range(N_EXPERTS):
            base = tile * SEG_TABLE
            _segment_copies(stage.at[tile_slot], tbl_ref[base + e], xs_ref, tbl_ref[base + 2 * N_EXPERTS + e],
                            tbl_ref[base + N_EXPERTS + e], sems.at[tile_slot], act)

    segments(j, slot, lambda cp: cp.start())

    @pl.when(j > 0)
    def _():
        segments(j - 1, 1 - slot, lambda cp: cp.wait())

    @pl.when(j == pl.num_programs(0) - 1)
    def _():
        segments(j, slot, lambda cp: cp.wait())
        zeros[...] = jnp.zeros_like(zeros)
        sem = sems.at[0]

        def tails(act):
            for e in range(N_EXPERTS):
                _segment_copies(zeros, 0, xs_ref, tail_ref[e], tail_ref[N_EXPERTS + e], sem, act)

        def unused(act):
            def body(i, c):
                t = pl.multiple_of(i * MOE_TILE, MOE_TILE)
                act(pltpu.make_async_copy(zeros, xs_ref.at[pl.ds(t, MOE_TILE), :], sem))
                return c
            lax.fori_loop(nu_ref[0], xs_ref.shape[0] // MOE_TILE, body, 0)

        tails(lambda cp: cp.start())
        unused(lambda cp: cp.start())
        tails(lambda cp: cp.wait())
        unused(lambda cp: cp.wait())


def _sort_place(h3, route, g, tbl, tail, n_used, p_rows):
    n, d = h3.shape
    tm = ROW_TILE
    grid_spec = pltpu.PrefetchScalarGridSpec(
        num_scalar_prefetch=3,
        grid=(n // tm,),
        in_specs=[pl.BlockSpec((tm, d), lambda j, *_: (j, 0)),
                  pl.BlockSpec((tm, LANES), lambda j, *_: (j, 0)),
                  pl.BlockSpec((1, d), lambda j, *_: (0, 0))],
        out_specs=pl.BlockSpec(memory_space=pl.ANY),
        scratch_shapes=[pltpu.VMEM((2, LOCAL_ROWS, XS_WIDTH), F32), pltpu.VMEM((MOE_TILE, XS_WIDTH), F32),
                        pltpu.SemaphoreType.DMA((2,))],
    )
    return pl.pallas_call(
        _sort_place_kernel,
        grid_spec=grid_spec,
        out_shape=jax.ShapeDtypeStruct((p_rows, XS_WIDTH), F32),
        compiler_params=_params("arbitrary"),
        name="moe_sort_place",
    )(tbl, tail, n_used, h3, route, g)


def _moe_kernel(te_ref, nu_ref, xs_ref, wg_ref, wu_ref, wd_ref, y_ref):
    i = pl.program_id(0)
    c = pl.program_id(1)
    used = i < nu_ref[0]

    @pl.when(used)
    def _():
        xn = xs_ref[:, 0:D_MODEL].astype(BF16)
        g = _dot(xn, wg_ref[0])
        up = _dot(xn, wu_ref[0])
        mid = (g * _sigmoid(g) * up).astype(BF16)
        y = _dot(mid, wd_ref[0]) * xs_ref[:, D_MODEL:D_MODEL + 1]

        @pl.when(c == 0)
        def _():
            y_ref[...] = y

        @pl.when(c > 0)
        def _():
            y_ref[...] = y_ref[...] + y

    @pl.when(jnp.logical_not(used) & (c == 0))
    def _():
        y_ref[...] = jnp.zeros_like(y_ref)


def _moe_experts(xs, tile_expert, n_used, wg, wu, wd):
    p_rows = xs.shape[0]
    d = D_MODEL
    tm = MOE_TILE
    n_tiles = p_rows // tm
    ch = D_EXPERT // MOE_CHUNKS
    last = MOE_CHUNKS - 1

    def chunk(i, c, nu):
        return jnp.where(i < nu[0], c, last)

    grid_spec = pltpu.PrefetchScalarGridSpec(
        num_scalar_prefetch=2,
        grid=(n_tiles, MOE_CHUNKS),
        in_specs=[
            pl.BlockSpec((tm, XS_WIDTH), lambda i, c, te, nu: (i, 0)),
            pl.BlockSpec((1, d, ch), lambda i, c, te, nu: (te[i], 0, chunk(i, c, nu))),
            pl.BlockSpec((1, d, ch), lambda i, c, te, nu: (te[i], 0, chunk(i, c, nu))),
            pl.BlockSpec((1, ch, d), lambda i, c, te, nu: (te[i], chunk(i, c, nu), 0)),
        ],
        out_specs=pl.BlockSpec((tm, d), lambda i, c, te, nu: (i, 0)),
    )
    return pl.pallas_call(
        _moe_kernel,
        grid_spec=grid_spec,
        out_shape=jax.ShapeDtypeStruct((p_rows, d), F32),
        compiler_params=_params("arbitrary", "arbitrary"),
        name="moe_experts",
    )(tile_expert, n_used, xs, wg, wu, wd)


def _combine_kernel(tbl_ref, h3_ref, route_ref, ys_ref, g_ref, outp_ref, outs_ref, ybuf, sems):
    j = pl.program_id(0)
    last = pl.num_programs(0) - 1
    tm = h3_ref.shape[0]
    slot = lax.rem(j, 2)

    def segments(tile, tile_slot, act):
        for e in range(N_EXPERTS):
            base = tile * SEG_TABLE
            _segment_copies(ys_ref, tbl_ref[base + 2 * N_EXPERTS + e], ybuf.at[tile_slot], tbl_ref[base + e],
                            tbl_ref[base + N_EXPERTS + e], sems.at[tile_slot], act)

    @pl.when(j == 0)
    def _():
        ybuf[...] = jnp.zeros_like(ybuf)
        segments(j, slot, lambda cp: cp.start())

    @pl.when(j < last)
    def _():
        segments(j + 1, 1 - slot, lambda cp: cp.start())

    segments(j, slot, lambda cp: cp.wait())

    route = route_ref[...]
    col = lax.broadcasted_iota(I32, (tm, LOCAL_ROWS), 1).astype(F32)
    picks = jnp.where((col == route[:, 0:1]) | (col == route[:, 1:2]), 1.0, 0.0).astype(BF16)
    h4 = h3_ref[...] + _dot(picks, ybuf[slot].astype(BF16))
    out = _rms(h4, g_ref[...])

    @pl.when(j < last)
    def _():
        outp_ref[...] = out

    @pl.when(j == last)
    def _():
        outs_ref[...] = out


def _combine_final(h3, route, tbl, ys, g_final, n_prompt):
    n, d = h3.shape
    tm = ROW_TILE
    n_sample = n - n_prompt
    assert n_sample == tm and n_prompt % tm == 0
    last_prompt = n_prompt // tm - 1
    grid_spec = pltpu.PrefetchScalarGridSpec(
        num_scalar_prefetch=1,
        grid=(n // tm,),
        in_specs=[pl.BlockSpec((tm, d), lambda j, *_: (j, 0)),
                  pl.BlockSpec((tm, LANES), lambda j, *_: (j, 0)),
                  pl.BlockSpec(memory_space=pl.ANY),
                  pl.BlockSpec((1, d), lambda j, *_: (0, 0))],
        out_specs=[pl.BlockSpec((tm, d), lambda j, *_: (jnp.minimum(j, last_prompt), 0)),
                   pl.BlockSpec((tm, d), lambda j, *_: (0, 0))],
        scratch_shapes=[pltpu.VMEM((2, LOCAL_ROWS, d), F32), pltpu.SemaphoreType.DMA((2,))],
    )
    return pl.pallas_call(
        _combine_kernel,
        grid_spec=grid_spec,
        out_shape=[jax.ShapeDtypeStruct((n_prompt, d), F32), jax.ShapeDtypeStruct((n_sample, d), F32)],
        compiler_params=_params("arbitrary"),
        name="moe_combine",
    )(tbl, h3, route, ys, g_final)


def _row(v):
    return v.reshape(1, -1).astype(F32)


def _routing_tables(counts):
    n_row_tiles = counts.shape[0]
    cnt = counts[:, 0, :N_EXPERTS].astype(I32)
    seg = ((cnt + SUBLANES - 1) // SUBLANES) * SUBLANES
    local_start = jnp.cumsum(seg, axis=1) - seg
    rows = jnp.sum(seg, axis=0)
    padded = ((rows + MOE_TILE - 1) // MOE_TILE) * MOE_TILE
    ends = jnp.cumsum(padded)
    starts = ends - padded
    sorted_start = starts[None, :] + jnp.cumsum(seg, axis=0) - seg
    tbl = jnp.concatenate([local_start, seg, sorted_start], axis=1).reshape(-1).astype(I32)
    tail = jnp.concatenate([starts + rows, padded - rows]).astype(I32)
    max_rows = n_row_tiles * (TOP_K * ROW_TILE + N_EXPERTS * (SUBLANES - 1)) + N_EXPERTS * (MOE_TILE - SUBLANES)
    n_tiles = -(-max_rows // MOE_TILE)
    tile_start = jnp.arange(n_tiles, dtype=I32) * MOE_TILE
    tile_expert = jnp.minimum(jnp.sum((tile_start[:, None] >= ends[None, :]).astype(I32), axis=1), N_EXPERTS - 1)
    n_used = (ends[-1] // MOE_TILE).reshape(1).astype(I32)
    return tbl, tail, tile_expert.astype(I32), n_used, n_tiles


def kernel(x_prompt, x_sample, state_conv, state_h, cache_k, cache_v, g_mix, g_ffn, g_kv, g_final, a_w_gate, a_b_gate, a_w_in, a_b_in, a_conv_w, a_conv_b, a_w_r, a_b_r, a_w_i, a_b_i, a_lam, a_w_out, a_b_out, w_kv, b_kv, rel_bias, b_w_q, b_b_q, b_sinks, b_w_o, b_b_o, f_w_gate, f_w_up, f_w_down, m_w_router, m_w_gate, m_w_up, m_w_down):
    bp, seq, d = x_prompt.shape
    bs, steps, _ = x_sample.shape
    n_prompt = bp * seq
    n_sample = bs * steps
    n = n_prompt + n_sample
    assert seq % MIX_TILE == 0 and seq % WINDOW == 0 and n_prompt % ROW_TILE == 0
    assert n_sample == ROW_TILE and bs % SAMPLE_GROUP == 0

    mix_w = dict(g=_row(g_mix[0]), wg=a_w_gate[0].astype(BF16), bg=_row(a_b_gate[0]),
                 win=a_w_in[0].astype(BF16), bin=_row(a_b_in[0]), cw=a_conv_w[0], cb=_row(a_conv_b[0]),
                 wr=a_w_r[0].astype(BF16), br=_row(a_b_r[0]), wi=a_w_i[0].astype(BF16), bi=_row(a_b_i[0]),
                 lam=_row(a_lam[0]), wout=a_w_out[0].astype(BF16), bout=_row(a_b_out[0]))

    h1_p, p_conv, p_h = _mixer_prompt(x_prompt, mix_w)
    h1_s, s_conv_tm, s_h = _mixer_sample(jnp.transpose(x_sample, (1, 0, 2)),
                                         jnp.transpose(state_conv[0], (1, 0, 2)), state_h[0], mix_w)
    h1_s = jnp.transpose(h1_s, (1, 0, 2)).reshape(n_sample, d)

    ffn_w = dict(gf=_row(g_ffn[0]), wg=f_w_gate[0].astype(BF16), wu=f_w_up[0].astype(BF16),
                 wd=f_w_down[0].astype(BF16), gkv=_row(g_kv), wkv=w_kv.astype(BF16), bkv=_row(b_kv),
                 gq=_row(g_mix[1]), wq=b_w_q[0].astype(BF16), bq=_row(b_b_q[0]))
    h2, kv, q, moe_wd = _ffn_kvq(h1_p, h1_s, ffn_w, m_w_down[0])

    bias = _bias_band(rel_bias)
    sinks = b_sinks[0].astype(F32)
    o_p, moe_wg, moe_wu = _attn_prompt(q, kv, bias, sinks, bp, seq, m_w_gate[0], m_w_up[0])
    keys = WINDOW + steps + (-steps) % SUBLANES
    q4 = q[n_prompt:].reshape(bs, steps, N_KV, GROUP, HEAD_DIM).transpose(0, 2, 1, 3, 4)
    q4 = q4.reshape(bs, N_KV, steps * GROUP, HEAD_DIM)
    kv_s = kv[n_prompt:].reshape(bs, steps, 2 * KV_DIM)
    k_new, v_new = kv_s[:, :, :KV_DIM], kv_s[:, :, KV_DIM:]
    bias_rows = bias.reshape(N_KV, 2, GROUP // 2, WINDOW, 2 * WINDOW)[:, :, :, :steps, :keys]
    bias_rows = bias_rows.transpose(0, 3, 2, 1, 4).reshape(N_KV, steps * GROUP, keys)
    sink_rows = jnp.broadcast_to(sinks.reshape(N_KV, 1, GROUP), (N_KV, steps, GROUP)).reshape(N_KV, steps * GROUP, 1)
    o4 = _attn_sample(q4, cache_k.reshape(bs, WINDOW, KV_DIM), cache_v.reshape(bs, WINDOW, KV_DIM),
                      k_new, v_new, bias_rows, sink_rows)
    o_s = o4.reshape(bs, N_KV, steps, GROUP, HEAD_DIM).transpose(0, 2, 1, 3, 4).reshape(n_sample, d)

    wr_pad = jnp.zeros((d, LANES), F32).at[:, :N_EXPERTS].set(m_w_router[0])
    wr_hi = wr_pad.astype(BF16)
    wr_lo = (wr_pad - wr_hi.astype(F32)).astype(BF16)
    h3, route, counts = _oproj_router(
        o_p, o_s, h2, dict(wo=b_w_o[0].astype(BF16), bo=_row(b_b_o[0]), gf=_row(g_ffn[1]), wr_hi=wr_hi, wr_lo=wr_lo))

    tbl, tail, tile_expert, n_used, n_tiles = _routing_tables(counts)
    xs = _sort_place(h3, route, _row(g_ffn[1]), tbl, tail, n_used, n_tiles * MOE_TILE)
    ys = _moe_experts(xs, tile_expert, n_used, moe_wg, moe_wu, moe_wd)
    y_p, y_s = _combine_final(h3, route, tbl, ys, _row(g_final), n_prompt)

    y_prompt = y_p.reshape(bp, seq, d)
    y_sample = y_s.reshape(bs, steps, d)
    kv_last = jnp.stack([kv[b * seq + seq - WINDOW:(b + 1) * seq] for b in range(bp)])
    p_k = kv_last[:, :, :KV_DIM].reshape(bp, WINDOW, N_KV, HEAD_DIM)
    p_v = kv_last[:, :, KV_DIM:].reshape(bp, WINDOW, N_KV, HEAD_DIM)
    s_k = jnp.concatenate([cache_k, k_new.reshape(bs, steps, N_KV, HEAD_DIM)], axis=1)[:, -WINDOW:]
    s_v = jnp.concatenate([cache_v, v_new.reshape(bs, steps, N_KV, HEAD_DIM)], axis=1)[:, -WINDOW:]
    return (y_prompt, y_sample, p_conv[None], p_h.reshape(1, bp, LRU_WIDTH), p_k, p_v,
            jnp.transpose(s_conv_tm, (1, 0, 2))[None], s_h[None], s_k, s_v)
```

```python
import functools
import math

import jax
import jax.numpy as jnp
from jax import lax
from jax.experimental import pallas as pl
from jax.experimental.pallas import tpu as pltpu

D_MODEL = 1024
LRU_WIDTH = D_MODEL
LRU_BLOCK_W = 256
LRU_BLOCKS = LRU_WIDTH // LRU_BLOCK_W
CONV_W = 4
LRU_C = 8.0
HEAD_DIM = 64
N_HEADS = D_MODEL // HEAD_DIM
N_KV = 2
GROUP = N_HEADS // N_KV
KV_DIM = N_KV * HEAD_DIM
WINDOW = 128
NUM_BUCKETS = 32
MAX_DISTANCE = 128
D_FF = 3 * D_MODEL
N_EXPERTS = 8
TOP_K = 2
D_EXPERT = 7 * D_MODEL // 2
EPS = 1e-6
NEG = -1e30

BF16 = jnp.bfloat16
F32 = jnp.float32
I32 = jnp.int32

SUBLANES = 8
LANES = 128
VMEM_LIMIT_BYTES = 56 * 1024 * 1024

ROW_TILE = 512
MIX_TILE = 256
FF_CHUNK = 1024
MOE_TILE = 512
MOE_CHUNKS = 2


def _params(*semantics):
    return pltpu.CompilerParams(dimension_semantics=semantics, vmem_limit_bytes=VMEM_LIMIT_BYTES)


def _const_spec(shape):
    zeros = (0,) * len(shape)
    return pl.BlockSpec(shape, lambda *_: zeros, pipeline_mode=pl.Buffered(1))


def _dot(a, b):
    return jnp.dot(a, b, preferred_element_type=F32)


def _rms(x, g):
    ms = jnp.mean(x * x, axis=-1, keepdims=True)
    return x * lax.rsqrt(ms + EPS) * g


def _sigmoid(x):
    return 1.0 / (1.0 + jnp.exp(-x))


def _gelu_tanh(x):
    return 0.5 * x * (1.0 + jnp.tanh(0.7978845608028654 * (x + 0.044715 * (x * x * x))))


def _log_sigmoid(x):
    return jnp.minimum(x, 0.0) - jnp.log1p(jnp.exp(-jnp.abs(x)))


def _lru_gates(xc, wr_ref, br, wi_ref, bi, lam):
    xcb = xc.astype(BF16)
    rs, gs = [], []
    for n in range(LRU_BLOCKS):
        xn = xcb[:, n * LRU_BLOCK_W:(n + 1) * LRU_BLOCK_W]
        rs.append(_dot(xn, wr_ref[n]))
        gs.append(_dot(xn, wi_ref[n]))
    r = _sigmoid(jnp.concatenate(rs, axis=1) + br)
    i = _sigmoid(jnp.concatenate(gs, axis=1) + bi)
    log_a = LRU_C * r * _log_sigmoid(lam)
    a = jnp.exp(log_a)
    mult = jnp.sqrt(1.0 - a * a)
    return a, mult * (i * xc)


def _interleave(order, *stage_generators):
    results = [None] * len(stage_generators)
    finished = set()

    def advance(idx):
        if idx in finished:
            return
        try:
            next(stage_generators[idx])
        except StopIteration as done:
            results[idx] = done.value
            finished.add(idx)

    for idx in order:
        advance(idx)
    while len(finished) < len(stage_generators):
        for idx in range(len(stage_generators)):
            advance(idx)
    return results


def _mixer_stages(x, g_ref, wg_ref, bg_ref, win_ref, bin_ref, cw_ref, cb_ref,
                  wr_ref, br_ref, wi_ref, bi_ref, lam_ref, wout_ref, bout_ref, xr_buf, h_carry):
    tt = x.shape[0]
    pad = SUBLANES
    u = _rms(x, g_ref[...]).astype(BF16)
    gate = _gelu_tanh(_dot(u, wg_ref[...]) + bg_ref[...])
    xr = _dot(u, win_ref[...]) + bin_ref[...]
    yield
    xr_buf[pad:pad + tt, :] = xr
    xc = cb_ref[...] + cw_ref[CONV_W - 1:CONV_W, :] * xr
    for k in range(CONV_W - 1):
        back = CONV_W - 1 - k
        xc = xc + cw_ref[k:k + 1, :] * xr_buf[pad - back:pad - back + tt, :]
    xr_buf[0:pad, :] = xr[tt - pad:tt, :]
    yield

    a, b = _lru_gates(xc, wr_ref, br_ref[...], wi_ref, bi_ref[...], lam_ref[...])
    yield

    groups = tt // SUBLANES
    a3 = a.reshape(groups, SUBLANES, LRU_WIDTH)
    b3 = b.reshape(groups, SUBLANES, LRU_WIDTH)
    row = lax.broadcasted_iota(I32, (1, SUBLANES, LRU_WIDTH), 1)
    step = 1
    while step < SUBLANES:
        keep = row >= step
        a_prev = jnp.where(keep, pltpu.roll(a3, step, axis=1), 1.0)
        b_prev = jnp.where(keep, pltpu.roll(b3, step, axis=1), 0.0)
        b3 = b3 + a3 * b_prev
        a3 = a3 * a_prev
        step *= 2
    yield
    h_prev = h_carry[0:1, :]
    hs = []
    for gi in range(groups):
        hg = b3[gi] + a3[gi] * h_prev
        hs.append(hg)
        h_prev = hg[SUBLANES - 1:SUBLANES, :]
        if gi + 1 == groups // 2:
            yield
    h = jnp.concatenate(hs, axis=0)
    h_carry[0:1, :] = h_prev
    yield
    y = _dot((h * gate).astype(BF16), wout_ref[...]) + bout_ref[...]
    return x + y, xr[tt - (CONV_W - 1):tt, :], h_prev


LAYER0_ORDER = (0, 1, 0, 0, 1, 0, 0, 1, 0, 0)
N_MIX_REFS = 14
N_FFN_REFS = 10


def _ffn_stages(h1, gf_ref, wg_ref, wu_ref, wd_ref, gkv_ref, wkv_ref, bkv_ref, gq_ref, wq_ref, bq_ref):
    u = _rms(h1, gf_ref[...]).astype(BF16)
    acc = h1
    n_chunks = D_FF // FF_CHUNK
    for c in range(n_chunks):
        cols = slice(c * FF_CHUNK, (c + 1) * FF_CHUNK)
        g = _dot(u, wg_ref[:, cols])
        up = _dot(u, wu_ref[:, cols])
        mid = (g * _sigmoid(g) * up).astype(BF16)
        acc = acc + _dot(mid, wd_ref[cols, :])
        if c + 1 < n_chunks:
            yield
    kv = _dot(_rms(acc, gkv_ref[...]).astype(BF16), wkv_ref[...]) + bkv_ref[...]
    q = _dot(_rms(acc, gq_ref[...]).astype(BF16), wq_ref[...]) + bq_ref[...]
    return acc, kv, (q * (HEAD_DIM ** -0.5)).astype(BF16)


def _layer0_prompt_kernel(*refs, n_tiles, tiles_per_seq):
    x_ref = refs[0]
    mix_refs = refs[1:1 + N_MIX_REFS]
    ffn_refs = refs[1 + N_MIX_REFS:1 + N_MIX_REFS + N_FFN_REFS]
    h2_ref, kv_ref, q_ref, conv_ref, hlast_ref, xr_buf, h_carry, h1_buf = refs[1 + N_MIX_REFS + N_FFN_REFS:]
    s = pl.program_id(0)
    slot = lax.rem(s, 2)

    @pl.when(s == 0)
    def _():
        h1_buf[...] = jnp.zeros_like(h1_buf)

    @pl.when(lax.rem(s, tiles_per_seq) == 0)
    def _():
        xr_buf[0:SUBLANES, :] = jnp.zeros((SUBLANES, LRU_WIDTH), F32)
        h_carry[...] = jnp.zeros_like(h_carry)

    (h1, conv_tail, h_last), (h2, kv, q) = _interleave(
        LAYER0_ORDER,
        _mixer_stages(x_ref[...], *mix_refs, xr_buf, h_carry), _ffn_stages(h1_buf[1 - slot], *ffn_refs))
    h2_ref[...] = h2
    kv_ref[...] = kv
    q_ref[...] = q
    h1_buf[slot] = h1

    @pl.when(s < n_tiles)
    def _():
        b = s // tiles_per_seq
        conv_ref[b] = conv_tail
        hlast_ref[b] = h_last


def _layer0_prompt(x, mix_w, ffn_w):
    b, t, d = x.shape
    tt = MIX_TILE
    n_tiles = (b * t) // tt
    vec = lambda n: _const_spec((1, n))
    mix_specs = [
        vec(d), _const_spec((d, LRU_WIDTH)), vec(LRU_WIDTH), _const_spec((d, LRU_WIDTH)), vec(LRU_WIDTH),
        _const_spec((CONV_W, LRU_WIDTH)), vec(LRU_WIDTH),
        _const_spec((LRU_BLOCKS, LRU_BLOCK_W, LRU_BLOCK_W)), vec(LRU_WIDTH),
        _const_spec((LRU_BLOCKS, LRU_BLOCK_W, LRU_BLOCK_W)), vec(LRU_WIDTH),
        vec(LRU_WIDTH), _const_spec((LRU_WIDTH, d)), vec(d),
    ]
    ffn_specs = [
        vec(d), _const_spec((d, D_FF)), _const_spec((d, D_FF)), _const_spec((D_FF, d)),
        vec(d), _const_spec((d, 2 * KV_DIM)), vec(2 * KV_DIM), vec(d), _const_spec((d, d)), vec(d),
    ]
    assert len(mix_specs) == N_MIX_REFS and len(ffn_specs) == N_FFN_REFS
    prev = lambda s: (jnp.maximum(s - 1, 0), 0)
    whole = lambda shape: pl.BlockSpec(shape, lambda s: (0,) * len(shape))
    return pl.pallas_call(
        functools.partial(_layer0_prompt_kernel, n_tiles=n_tiles, tiles_per_seq=t // tt),
        grid=(n_tiles + 1,),
        in_specs=[pl.BlockSpec((tt, d), lambda s: (jnp.minimum(s, n_tiles - 1), 0))] + mix_specs + ffn_specs,
        out_specs=[pl.BlockSpec((tt, d), prev), pl.BlockSpec((tt, 2 * KV_DIM), prev), pl.BlockSpec((tt, d), prev),
                   whole((b, CONV_W - 1, LRU_WIDTH)), whole((b, 1, LRU_WIDTH))],
        out_shape=[jax.ShapeDtypeStruct((b * t, d), F32), jax.ShapeDtypeStruct((b * t, 2 * KV_DIM), F32),
                   jax.ShapeDtypeStruct((b * t, d), BF16),
                   jax.ShapeDtypeStruct((b, CONV_W - 1, LRU_WIDTH), F32),
                   jax.ShapeDtypeStruct((b, 1, LRU_WIDTH), F32)],
        scratch_shapes=[pltpu.VMEM((SUBLANES + tt, LRU_WIDTH), F32), pltpu.VMEM((SUBLANES, LRU_WIDTH), F32),
                        pltpu.VMEM((2, tt, d), F32)],
        compiler_params=_params("arbitrary"),
        name="layer0_prompt",
    )(x.reshape(b * t, d), *[mix_w[k] for k in MIX_KEYS], *[ffn_w[k] for k in FFN_KEYS])


MIX_KEYS = ("g", "wg", "bg", "win", "bin", "cw", "cb", "wr", "br", "wi", "bi", "lam", "wout", "bout")
FFN_KEYS = ("gf", "wg", "wu", "wd", "gkv", "wkv", "bkv", "gq", "wq", "bq")


def _mixer_sample_kernel(x_ref, cs_ref, h0_ref, g_ref, wg_ref, bg_ref, win_ref, bin_ref, cw_ref, cb_ref,
                         wr_ref, br_ref, wi_ref, bi_ref, lam_ref, wout_ref, bout_ref,
                         h1_ref, conv_ref, hlast_ref):
    steps, nb, d = x_ref.shape
    x = x_ref[...].reshape(steps * nb, d)
    u = _rms(x, g_ref[...]).astype(BF16)
    gate = _gelu_tanh(_dot(u, wg_ref[...]) + bg_ref[...])
    xr = _dot(u, win_ref[...]) + bin_ref[...]
    xpad = [cs_ref[k] for k in range(CONV_W - 1)] + [xr[s * nb:(s + 1) * nb, :] for s in range(steps)]
    xcs = []
    for s in range(steps):
        acc = cb_ref[...] + cw_ref[0:1, :] * xpad[s]
        for k in range(1, CONV_W):
            acc = acc + cw_ref[k:k + 1, :] * xpad[s + k]
        xcs.append(acc)
    for k in range(CONV_W - 1):
        conv_ref[k] = xpad[steps + k]
    xc = jnp.concatenate(xcs, axis=0)
    a, b = _lru_gates(xc, wr_ref, br_ref[...], wi_ref, bi_ref[...], lam_ref[...])
    h = h0_ref[...]
    hs = []
    for s in range(steps):
        h = a[s * nb:(s + 1) * nb, :] * h + b[s * nb:(s + 1) * nb, :]
        hs.append(h)
    hlast_ref[...] = h
    hcat = jnp.concatenate(hs, axis=0)
    y = _dot((hcat * gate).astype(BF16), wout_ref[...]) + bout_ref[...]
    h1_ref[...] = (x + y).reshape(steps, nb, d)


def _mixer_sample(x_tm, cs_tm, h0, w):
    steps, nb, d = x_tm.shape
    full = lambda shape: pl.BlockSpec(shape, lambda i: (0,) * len(shape))
    vec = lambda n: full((1, n))
    in_specs = [
        full((steps, nb, d)), full((CONV_W - 1, nb, LRU_WIDTH)), full((nb, LRU_WIDTH)),
        vec(d), full((d, LRU_WIDTH)), vec(LRU_WIDTH), full((d, LRU_WIDTH)), vec(LRU_WIDTH),
        full((CONV_W, LRU_WIDTH)), vec(LRU_WIDTH),
        full((LRU_BLOCKS, LRU_BLOCK_W, LRU_BLOCK_W)), vec(LRU_WIDTH),
        full((LRU_BLOCKS, LRU_BLOCK_W, LRU_BLOCK_W)), vec(LRU_WIDTH),
        vec(LRU_WIDTH), full((LRU_WIDTH, d)), vec(d),
    ]
    out_specs = [full((steps, nb, d)), full((CONV_W - 1, nb, LRU_WIDTH)), full((nb, LRU_WIDTH))]
    out_shape = [
        jax.ShapeDtypeStruct((steps, nb, d), F32),
        jax.ShapeDtypeStruct((CONV_W - 1, nb, LRU_WIDTH), F32),
        jax.ShapeDtypeStruct((nb, LRU_WIDTH), F32),
    ]
    return pl.pallas_call(
        _mixer_sample_kernel, grid=(1,), in_specs=in_specs, out_specs=out_specs, out_shape=out_shape,
        compiler_params=_params("arbitrary"), name="mixer_sample",
    )(x_tm, cs_tm, h0, w["g"], w["wg"], w["bg"], w["win"], w["bin"], w["cw"], w["cb"], w["wr"], w["br"],
      w["wi"], w["bi"], w["lam"], w["wout"], w["bout"])


def _two_part_specs(n_prompt, n_sample, width):
    assert n_sample == ROW_TILE and n_prompt % ROW_TILE == 0
    last_prompt = n_prompt // ROW_TILE - 1
    return [pl.BlockSpec((ROW_TILE, width), lambda i: (jnp.minimum(i, last_prompt), 0)),
            pl.BlockSpec((ROW_TILE, width), lambda i: (0, 0))]


def _two_part_tile(prompt_ref, sample_ref):
    is_sample = pl.program_id(0) == pl.num_programs(0) - 1
    return jnp.where(is_sample, sample_ref[...], prompt_ref[...])


def _side_cast_specs(w, n_steps, step_of):
    ne, rows, cols = w.shape
    per_expert = n_steps // ne
    blk = rows // per_expert
    assert per_expert * ne == n_steps and blk * per_expert == rows and blk % (2 * SUBLANES) == 0
    index = lambda *ids: (step_of(*ids) // per_expert, step_of(*ids) % per_expert, 0)
    spec = pl.BlockSpec((1, blk, cols), index)
    return spec, spec, jax.ShapeDtypeStruct(w.shape, BF16)


def _ffn_rows_kernel(h1_ref, *refs):
    h2_ref, kv_ref, q_ref = refs[N_FFN_REFS:]
    (h2_ref[...], kv_ref[...], q_ref[...]), = _interleave((), _ffn_stages(h1_ref[...], *refs[:N_FFN_REFS]))


def _ffn_rows(h1, w):
    n, d = h1.shape
    tm = ROW_TILE
    row = lambda width: pl.BlockSpec((tm, width), lambda i: (i, 0))
    vec = lambda width: _const_spec((1, width))
    in_specs = [
        row(d), vec(d), _const_spec((d, D_FF)), _const_spec((d, D_FF)), _const_spec((D_FF, d)),
        vec(d), _const_spec((d, 2 * KV_DIM)), vec(2 * KV_DIM), vec(d), _const_spec((d, d)), vec(d),
    ]
    return pl.pallas_call(
        _ffn_rows_kernel,
        grid=(n // tm,),
        in_specs=in_specs,
        out_specs=[row(d), row(2 * KV_DIM), row(d)],
        out_shape=[jax.ShapeDtypeStruct((n, d), F32), jax.ShapeDtypeStruct((n, 2 * KV_DIM), F32),
                   jax.ShapeDtypeStruct((n, d), BF16)],
        compiler_params=_params("arbitrary"),
        name="ffn_rows",
    )(h1, *[w[k] for k in FFN_KEYS])


def _head_of(j, parity, pair):
    return j * GROUP + 2 * pair + parity


def _bias_band_kernel(table_ref, out_ref):
    q_len, s_len = WINDOW, out_ref.shape[3]
    qi = lax.broadcasted_iota(I32, (q_len, s_len), 0)
    si = lax.broadcasted_iota(I32, (q_len, s_len), 1)
    dist = qi + WINDOW - si
    max_exact = NUM_BUCKETS // 2
    n = jnp.maximum(dist, 0)
    large = jnp.full((q_len, s_len), max_exact, I32)
    for step in range(1, NUM_BUCKETS - max_exact):
        threshold = math.ceil(max_exact * (MAX_DISTANCE / max_exact) ** (step / (NUM_BUCKETS - max_exact)))
        large = large + jnp.where(n >= threshold, 1, 0)
    bucket = jnp.where(n < max_exact, n, large)
    for j in range(N_KV):
        for parity in range(2):
            for pair in range(GROUP // 2):
                h = _head_of(j, parity, pair)
                acc = jnp.zeros((q_len, s_len), F32)
                for bkt in range(NUM_BUCKETS):
                    acc = jnp.where(bucket == bkt, table_ref[bkt, h], acc)
                out_ref[j, parity, pair * q_len:(pair + 1) * q_len, :] = acc


def _bias_band(rel_table):
    shape = (N_KV, 2, (GROUP // 2) * WINDOW, 2 * WINDOW)
    return pl.pallas_call(
        _bias_band_kernel,
        grid=(1,),
        in_specs=[pl.BlockSpec(memory_space=pltpu.SMEM)],
        out_specs=pl.BlockSpec(shape, lambda i: (0, 0, 0, 0)),
        out_shape=jax.ShapeDtypeStruct(shape, F32),
        compiler_params=_params("arbitrary"),
        name="bias_band",
    )(rel_table)


def _softmax_sink_pv(s, sink, v):
    m = jnp.maximum(jnp.max(s, axis=-1, keepdims=True), sink)
    p = jnp.exp(s - m)
    denom = jnp.sum(p, axis=-1, keepdims=True) + jnp.exp(sink - m)
    return _dot(p.astype(BF16), v) / denom


ATTN_BLOCKS = 4


def _attn_prompt_kernel(sink_ref, q_ref, kvp_ref, kvc_ref, bias_ref, cast_a_ref, cast_b_ref,
                        o_ref, cast_a_out_ref, cast_b_out_ref):
    cast_a_out_ref[...] = cast_a_ref[...].astype(BF16)
    cast_b_out_ref[...] = cast_b_ref[...].astype(BF16)
    w = WINDOW
    pairs = GROUP // 2
    assert 2 * HEAD_DIM == LANES and KV_DIM == LANES
    qi = lax.broadcasted_iota(I32, (pairs * w, 2 * w), 0) % w
    si = lax.broadcasted_iota(I32, (pairs * w, 2 * w), 1)
    dist = qi + w - si
    band = (dist >= 0) & (dist < w)
    band_first = band & ((si >= w) | (pl.program_id(1) > 0))

    kv = jnp.concatenate([kvp_ref[...], kvc_ref[...]], axis=0)
    low = lax.broadcasted_iota(I32, (kv.shape[0], LANES), 1) < HEAD_DIM

    def halves(x):
        swapped = pltpu.roll(x, HEAD_DIM, axis=1)
        return (((jnp.where(low, x, 0.0)).astype(BF16), (jnp.where(low, 0.0, swapped)).astype(BF16)),
                ((jnp.where(low, swapped, 0.0)).astype(BF16), (jnp.where(low, 0.0, x)).astype(BF16)))

    k_ops = halves(kv[:, 0:KV_DIM])
    v_ops = halves(kv[:, KV_DIM:2 * KV_DIM])

    for i in range(ATTN_BLOCKS):
        q_rows = slice(i * w, (i + 1) * w)
        k_rows = slice(i * w, (i + 2) * w)
        mask = band_first if i == 0 else band
        for j in range(N_KV):
            q4 = jnp.concatenate([q_ref[q_rows, (j * pairs + b) * LANES:(j * pairs + b + 1) * LANES]
                                  for b in range(pairs)], axis=0)
            acc = None
            for parity in range(2):
                s = lax.dot_general(q4, k_ops[j][parity][k_rows], (((1,), (1,)), ((), ())),
                                    preferred_element_type=F32)
                s = jnp.where(mask, s + bias_ref[j, parity], NEG)
                sink = jnp.concatenate([jnp.full((w, 1), sink_ref[_head_of(j, parity, b)], F32)
                                        for b in range(pairs)], axis=0)
                o = _softmax_sink_pv(s, sink, v_ops[j][parity][k_rows])
                acc = o if acc is None else acc + o
            for b in range(pairs):
                o_ref[q_rows, (j * pairs + b) * LANES:(j * pairs + b + 1) * LANES] = acc[b * w:(b + 1) * w].astype(BF16)


def _attn_prompt(q, kv, bias, sinks, batch, seq, cast_a, cast_b):
    nb = seq // WINDOW
    assert nb % ATTN_BLOCKS == 0
    ns = nb // ATTN_BLOCKS
    rows = ATTN_BLOCKS * WINDOW
    d = q.shape[1]
    step_of = lambda b, n: b * ns + n
    a_in, a_out, a_shape = _side_cast_specs(cast_a, batch * ns, step_of)
    b_in, b_out, b_shape = _side_cast_specs(cast_b, batch * ns, step_of)
    in_specs = [
        pl.BlockSpec(memory_space=pltpu.SMEM),
        pl.BlockSpec((rows, d), lambda b, n: (b * ns + n, 0)),
        pl.BlockSpec((WINDOW, 2 * KV_DIM), lambda b, n: (b * nb + jnp.maximum(n * ATTN_BLOCKS - 1, 0), 0)),
        pl.BlockSpec((rows, 2 * KV_DIM), lambda b, n: (b * ns + n, 0)),
        _const_spec(bias.shape),
        a_in, b_in,
    ]
    return pl.pallas_call(
        _attn_prompt_kernel,
        grid=(batch, ns),
        in_specs=in_specs,
        out_specs=[pl.BlockSpec((rows, d), lambda b, n: (b * ns + n, 0)), a_out, b_out],
        out_shape=[jax.ShapeDtypeStruct((batch * seq, d), BF16), a_shape, b_shape],
        compiler_params=_params("arbitrary", "arbitrary"),
        name="attn_prompt",
    )(sinks, q, kv, kv, bias, cast_a, cast_b)


SAMPLE_GROUP = 8


def _attn_sample_kernel(q_ref, ck_ref, cv_ref, kn_ref, vn_ref, bias_ref, sink_ref, o_ref):
    gb = q_ref.shape[0]
    steps = kn_ref.shape[1]
    rows = steps * GROUP
    pad = (-steps) % SUBLANES
    keys = WINDOW + steps + pad
    ri = lax.broadcasted_iota(I32, (rows, keys), 0)
    si = lax.broadcasted_iota(I32, (rows, keys), 1)
    dist = ri // GROUP + WINDOW - si
    mask = (dist >= 0) & (dist < WINDOW)
    zeros = jnp.zeros((pad, KV_DIM), F32)

    def scores(b, j):
        k_all = jnp.concatenate([ck_ref[b], kn_ref[b], zeros], axis=0).astype(BF16)
        k = k_all[:, j * HEAD_DIM:(j + 1) * HEAD_DIM]
        s = lax.dot_general(q_ref[b, j], k, (((1,), (1,)), ((), ())), preferred_element_type=F32)
        return jnp.where(mask, s + bias_ref[j], NEG)

    tasks = [(b, j) for b in range(gb) for j in range(N_KV)]
    all_scores = [scores(b, j) for b, j in tasks]
    for (b, j), s in zip(tasks, all_scores):
        v_all = jnp.concatenate([cv_ref[b], vn_ref[b], zeros], axis=0).astype(BF16)
        v = v_all[:, j * HEAD_DIM:(j + 1) * HEAD_DIM]
        o_ref[b, j] = _softmax_sink_pv(s, sink_ref[j], v).astype(BF16)


def _attn_sample(q4, ck, cv, kn, vn, bias_rows, sink_rows):
    nb, _, rows, hd = q4.shape
    steps = kn.shape[1]
    keys = bias_rows.shape[2]
    gb = SAMPLE_GROUP
    in_specs = [
        pl.BlockSpec((gb, N_KV, rows, hd), lambda i: (i, 0, 0, 0)),
        pl.BlockSpec((gb, WINDOW, KV_DIM), lambda i: (i, 0, 0)),
        pl.BlockSpec((gb, WINDOW, KV_DIM), lambda i: (i, 0, 0)),
        pl.BlockSpec((gb, steps, KV_DIM), lambda i: (i, 0, 0)),
        pl.BlockSpec((gb, steps, KV_DIM), lambda i: (i, 0, 0)),
        pl.BlockSpec((N_KV, rows, keys), lambda i: (0, 0, 0)),
        pl.BlockSpec((N_KV, rows, 1), lambda i: (0, 0, 0)),
    ]
    return pl.pallas_call(
        _attn_sample_kernel,
        grid=(nb // gb,),
        in_specs=in_specs,
        out_specs=pl.BlockSpec((gb, N_KV, rows, hd), lambda i: (i, 0, 0, 0)),
        out_shape=jax.ShapeDtypeStruct(q4.shape, BF16),
        compiler_params=_params("arbitrary"),
        name="attn_sample",
    )(q4, ck, cv, kn, vn, bias_rows, sink_rows)


def _split_bf16(x):
    hi = x.astype(BF16)
    lo = (x - hi.astype(F32)).astype(BF16)
    return hi, lo


ROUTER_PARTS = 2


def _oproj_router_kernel(op_ref, os_ref, h2p_ref, h2s_ref, wo_ref, bo_ref, gf_ref, wrh_ref, wrl_ref, cast_ref,
                         h3_ref, route_ref, counts_ref, cast_out_ref):
    cast_out_ref[...] = cast_ref[...].astype(BF16)
    tm = h3_ref.shape[0]
    parts = ROUTER_PARTS
    pr = tm // parts
    o = _two_part_tile(op_ref, os_ref)
    h2 = _two_part_tile(h2p_ref, h2s_ref)
    lane = lax.broadcasted_iota(I32, (pr, LANES), 1)
    lanef = lane.astype(F32)
    ri = lax.broadcasted_iota(I32, (pr, pr), 0)
    ci = lax.broadcasted_iota(I32, (pr, pr), 1)
    earlier = jnp.where(ri > ci, 1.0, 0.0).astype(BF16)

    picks, ranks, gates = [], [], []
    cnt = jnp.zeros((1, LANES), F32)
    for p in range(parts):
        rows = slice(p * pr, (p + 1) * pr)
        h3 = h2[rows, :] + _dot(o[rows, :], wo_ref[...]) + bo_ref[...]
        h3_ref[rows, :] = h3
        u_hi, u_lo = _split_bf16(_rms(h3, gf_ref[...]))
        logits = _dot(u_hi, wrh_ref[...]) + (_dot(u_lo, wrh_ref[...]) + _dot(u_hi, wrl_ref[...]))
        logits = jnp.where(lane < N_EXPERTS, logits, -jnp.inf)
        v1 = jnp.max(logits, axis=-1, keepdims=True)
        e1 = jnp.min(jnp.where(logits == v1, lanef, float(LANES)), axis=-1, keepdims=True)
        rest = jnp.where(lanef == e1, -jnp.inf, logits)
        v2 = jnp.max(rest, axis=-1, keepdims=True)
        e2 = jnp.min(jnp.where(rest == v2, lanef, float(LANES)), axis=-1, keepdims=True)
        ex = jnp.exp(v2 - v1)
        gates.append((1.0 / (1.0 + ex), ex / (1.0 + ex)))
        pick1 = lanef == e1
        pick2 = lanef == e2
        picks.append((pick1, pick2))
        sel = jnp.where(pick1 | pick2, 1.0, 0.0)
        ranks.append(_dot(earlier, sel.astype(BF16)) + cnt)
        cnt = cnt + jnp.sum(sel, axis=0, keepdims=True)

    seg = jnp.floor((cnt + (SUBLANES - 1)) * (1.0 / SUBLANES)) * SUBLANES
    ek = lax.broadcasted_iota(I32, (LANES, LANES), 0)
    el = lax.broadcasted_iota(I32, (LANES, LANES), 1)
    lower_experts = jnp.where(ek < el, 1.0, 0.0).astype(BF16)
    seg_start = _dot(jnp.broadcast_to(seg, (SUBLANES, LANES)).astype(BF16), lower_experts)[0:1, :]
    for p in range(parts):
        local = ranks[p] + seg_start
        lr1 = jnp.sum(jnp.where(picks[p][0], local, 0.0), axis=-1, keepdims=True)
        lr2 = jnp.sum(jnp.where(picks[p][1], local, 0.0), axis=-1, keepdims=True)
        w1, w2 = gates[p]
        route_ref[p * pr:(p + 1) * pr, :] = jnp.where(
            lane == 0, lr1, jnp.where(lane == 1, lr2, jnp.where(lane == 2, w1, jnp.where(lane == 3, w2, 0.0))))
    counts_ref[0] = jnp.broadcast_to(cnt, (SUBLANES, LANES))


def _oproj_router(o_p, o_s, h2_p, h2_s, w, cast_w):
    d = h2_p.shape[1]
    n = h2_p.shape[0] + h2_s.shape[0]
    tm = ROW_TILE
    row = lambda width: pl.BlockSpec((tm, width), lambda i: (i, 0))
    prompt_steps = h2_p.shape[0] // tm
    cast_in, cast_out, cast_shape = _side_cast_specs(cast_w, prompt_steps, lambda i: jnp.minimum(i, prompt_steps - 1))
    in_specs = (_two_part_specs(o_p.shape[0], o_s.shape[0], d) + _two_part_specs(h2_p.shape[0], h2_s.shape[0], d) + [
        _const_spec((d, d)), _const_spec((1, d)), _const_spec((1, d)),
        _const_spec((d, LANES)), _const_spec((d, LANES)), cast_in])
    return pl.pallas_call(
        _oproj_router_kernel,
        grid=(n // tm,),
        in_specs=in_specs,
        out_specs=[row(d), row(LANES), pl.BlockSpec((1, SUBLANES, LANES), lambda i: (i, 0, 0)), cast_out],
        out_shape=[jax.ShapeDtypeStruct((n, d), F32), jax.ShapeDtypeStruct((n, LANES), F32),
                   jax.ShapeDtypeStruct((n // tm, SUBLANES, LANES), F32), cast_shape],
        compiler_params=_params("arbitrary"),
        name="oproj_router",
    )(o_p, o_s, h2_p, h2_s, w["wo"], w["bo"], w["gf"], w["wr_hi"], w["wr_lo"], cast_w)


LOCAL_ROWS = 1152
XS_WIDTH = D_MODEL + LANES
SEG_TABLE = 3 * N_EXPERTS
SEG_PIECES = tuple(SUBLANES << b for b in reversed(range(7)))


def _segment_copies(src_ref, src_start, dst_ref, dst_start, length, sem, act):
    for piece in SEG_PIECES:
        done = lax.div(length, 2 * piece) * (2 * piece)

        @pl.when(lax.rem(lax.div(length, piece), 2) != 0)
        def _():
            s = pl.multiple_of(src_start + done, SUBLANES)
            t = pl.multiple_of(dst_start + done, SUBLANES)
            act(pltpu.make_async_copy(src_ref.at[pl.ds(s, piece), :], dst_ref.at[pl.ds(t, piece), :], sem))


def _sort_place_kernel(tbl_ref, tail_ref, nu_ref, h3_ref, route_ref, g_ref, xs_ref, stage, zeros, sems):
    j = pl.program_id(0)
    tm = h3_ref.shape[0]
    xn = _rms(h3_ref[...], g_ref[...]).astype(BF16)
    rt = jnp.transpose(route_ref[...])
    row = lax.broadcasted_iota(I32, (LOCAL_ROWS, tm), 0).astype(F32)
    m1 = row == rt[0:1, :]
    m2 = row == rt[1:2, :]
    onehot = jnp.where(m1 | m2, 1.0, 0.0).astype(BF16)
    slot = lax.rem(j, 2)
    buf = stage.at[slot]
    buf[:, 0:D_MODEL] = _dot(onehot, xn)
    gate = jnp.sum(jnp.where(m1, rt[2:3, :], 0.0) + jnp.where(m2, rt[3:4, :], 0.0), axis=-1, keepdims=True)
    buf[:, D_MODEL:XS_WIDTH] = jnp.broadcast_to(gate, (LOCAL_ROWS, LANES))

    def segments(tile, tile_slot, act):
        for e in range(N_EXPERTS):
            base = tile * SEG_TABLE
            _segment_copies(stage.at[tile_slot], tbl_ref[base + e], xs_ref, tbl_ref[base + 2 * N_EXPERTS + e],
                            tbl_ref[base + N_EXPERTS + e], sems.at[tile_slot], act)

    segments(j, slot, lambda cp: cp.start())

    @pl.when(j > 0)
    def _():
        segments(j - 1, 1 - slot, lambda cp: cp.wait())

    @pl.when(j == pl.num_programs(0) - 1)
    def _():
        segments(j, slot, lambda cp: cp.wait())
        zeros[...] = jnp.zeros_like(zeros)
        sem = sems.at[0]

        def tails(act):
            for e in range(N_EXPERTS):
                _segment_copies(zeros, 0, xs_ref, tail_ref[e], tail_ref[N_EXPERTS + e], sem, act)

        def unused(act):
            def body(i, c):
                t = pl.multiple_of(i * MOE_TILE, MOE_TILE)
                act(pltpu.make_async_copy(zeros, xs_ref.at[pl.ds(t, MOE_TILE), :], sem))
                return c
            lax.fori_loop(nu_ref[0], xs_ref.shape[0] // MOE_TILE, body, 0)

        tails(lambda cp: cp.start())
        unused(lambda cp: cp.start())
        tails(lambda cp: cp.wait())
        unused(lambda cp: cp.wait())


def _sort_place(h3, route, g, tbl, tail, n_used, p_rows):
    n, d = h3.shape
    tm = ROW_TILE
    grid_spec = pltpu.PrefetchScalarGridSpec(
        num_scalar_prefetch=3,
        grid=(n // tm,),
        in_specs=[pl.BlockSpec((tm, d), lambda j, *_: (j, 0)),
                  pl.BlockSpec((tm, LANES), lambda j, *_: (j, 0)),
                  pl.BlockSpec((1, d), lambda j, *_: (0, 0))],
        out_specs=pl.BlockSpec(memory_space=pl.ANY),
        scratch_shapes=[pltpu.VMEM((2, LOCAL_ROWS, XS_WIDTH), F32), pltpu.VMEM((MOE_TILE, XS_WIDTH), F32),
                        pltpu.SemaphoreType.DMA((2,))],
    )
    return pl.pallas_call(
        _sort_place_kernel,
        grid_spec=grid_spec,
        out_shape=jax.ShapeDtypeStruct((p_rows, XS_WIDTH), F32),
        compiler_params=_params("arbitrary"),
        name="moe_sort_place",
    )(tbl, tail, n_used, h3, route, g)


def _moe_kernel(te_ref, nu_ref, xs_ref, wg_ref, wu_ref, wd_ref, y_ref):
    i = pl.program_id(0)
    c = pl.program_id(1)
    used = i < nu_ref[0]

    @pl.when(used)
    def _():
        xn = xs_ref[:, 0:D_MODEL].astype(BF16)
        g = _dot(xn, wg_ref[0])
        up = _dot(xn, wu_ref[0])
        mid = (g * _sigmoid(g) * up).astype(BF16)
        y = _dot(mid, wd_ref[0]) * xs_ref[:, D_MODEL:D_MODEL + 1]

        @pl.when(c == 0)
        def _():
            y_ref[...] = y

        @pl.when(c > 0)
        def _():
            y_ref[...] = y_ref[...] + y

    @pl.when(jnp.logical_not(used) & (c == 0))
    def _():
        y_ref[...] = jnp.zeros_like(y_ref)


def _moe_experts(xs, tile_expert, n_used, wg, wu, wd):
    p_rows = xs.shape[0]
    d = D_MODEL
    tm = MOE_TILE
    n_tiles = p_rows // tm
    ch = D_EXPERT // MOE_CHUNKS
    last = MOE_CHUNKS - 1

    def chunk(i, c, nu):
        return jnp.where(i < nu[0], c, last)

    grid_spec = pltpu.PrefetchScalarGridSpec(
        num_scalar_prefetch=2,
        grid=(n_tiles, MOE_CHUNKS),
        in_specs=[
            pl.BlockSpec((tm, XS_WIDTH), lambda i, c, te, nu: (i, 0)),
            pl.BlockSpec((1, d, ch), lambda i, c, te, nu: (te[i], 0, chunk(i, c, nu))),
            pl.BlockSpec((1, d, ch), lambda i, c, te, nu: (te[i], 0, chunk(i, c, nu))),
            pl.BlockSpec((1, ch, d), lambda i, c, te, nu: (te[i], chunk(i, c, nu), 0)),
        ],
        out_specs=pl.BlockSpec((tm, d), lambda i, c, te, nu: (i, 0)),
    )
    return pl.pallas_call(
        _moe_kernel,
        grid_spec=grid_spec,
        out_shape=jax.ShapeDtypeStruct((p_rows, d), F32),
        compiler_params=_params("arbitrary", "arbitrary"),
        name="moe_experts",
    )(tile_expert, n_used, xs, wg, wu, wd)


def _combine_kernel(tbl_ref, h3_ref, route_ref, ys_ref, g_ref, outp_ref, outs_ref, ybuf, sems):
    j = pl.program_id(0)
    last = pl.num_programs(0) - 1
    tm = h3_ref.shape[0]
    slot = lax.rem(j, 2)

    def segments(tile, tile_slot, act):
        for e in range(N_EXPERTS):
            base = tile * SEG_TABLE
            _segment_copies(ys_ref, tbl_ref[base + 2 * N_EXPERTS + e], ybuf.at[tile_slot], tbl_ref[base + e],
                            tbl_ref[base + N_EXPERTS + e], sems.at[tile_slot], act)

    @pl.when(j == 0)
    def _():
        ybuf[...] = jnp.zeros_like(ybuf)
        segments(j, slot, lambda cp: cp.start())

    @pl.when(j < last)
    def _():
        segments(j + 1, 1 - slot, lambda cp: cp.start())

    segments(j, slot, lambda cp: cp.wait())

    route = route_ref[...]
    col = lax.broadcasted_iota(I32, (tm, LOCAL_ROWS), 1).astype(F32)
    picks = jnp.where((col == route[:, 0:1]) | (col == route[:, 1:2]), 1.0, 0.0).astype(BF16)
    h4 = h3_ref[...] + _dot(picks, ybuf[slot].astype(BF16))
    out = _rms(h4, g_ref[...])

    @pl.when(j < last)
    def _():
        outp_ref[...] = out

    @pl.when(j == last)
    def _():
        outs_ref[...] = out


def _combine_final(h3, route, tbl, ys, g_final, n_prompt):
    n, d = h3.shape
    tm = ROW_TILE
    n_sample = n - n_prompt
    assert n_sample == tm and n_prompt % tm == 0
    last_prompt = n_prompt // tm - 1
    grid_spec = pltpu.PrefetchScalarGridSpec(
        num_scalar_prefetch=1,
        grid=(n // tm,),
        in_specs=[pl.BlockSpec((tm, d), lambda j, *_: (j, 0)),
                  pl.BlockSpec((tm, LANES), lambda j, *_: (j, 0)),
                  pl.BlockSpec(memory_space=pl.ANY),
                  pl.BlockSpec((1, d), lambda j, *_: (0, 0))],
        out_specs=[pl.BlockSpec((tm, d), lambda j, *_: (jnp.minimum(j, last_prompt), 0)),
                   pl.BlockSpec((tm, d), lambda j, *_: (0, 0))],
        scratch_shapes=[pltpu.VMEM((2, LOCAL_ROWS, d), F32), pltpu.SemaphoreType.DMA((2,))],
    )
    return pl.pallas_call(
        _combine_kernel,
        grid_spec=grid_spec,
        out_shape=[jax.ShapeDtypeStruct((n_prompt, d), F32), jax.ShapeDtypeStruct((n_sample, d), F32)],
        compiler_params=_params("arbitrary"),
        name="moe_combine",
    )(tbl, h3, route, ys, g_final)


def _row(v):
    return v.reshape(1, -1).astype(F32)


def _routing_tables(counts):
    n_row_tiles = counts.shape[0]
    cnt = counts[:, 0, :N_EXPERTS].astype(I32)
    seg = ((cnt + SUBLANES - 1) // SUBLANES) * SUBLANES
    local_start = jnp.cumsum(seg, axis=1) - seg
    rows = jnp.sum(seg, axis=0)
    padded = ((rows + MOE_TILE - 1) // MOE_TILE) * MOE_TILE
    ends = jnp.cumsum(padded)
    starts = ends - padded
    sorted_start = starts[None, :] + jnp.cumsum(seg, axis=0) - seg
    tbl = jnp.concatenate([local_start, seg, sorted_start], axis=1).reshape(-1).astype(I32)
    tail = jnp.concatenate([starts + rows, padded - rows]).astype(I32)
    max_rows = n_row_tiles * (TOP_K * ROW_TILE + N_EXPERTS * (SUBLANES - 1)) + N_EXPERTS * (MOE_TILE - SUBLANES)
    n_tiles = -(-max_rows // MOE_TILE)
    tile_start = jnp.arange(n_tiles, dtype=I32) * MOE_TILE
    tile_expert = jnp.minimum(jnp.sum((tile_start[:, None] >= ends[None, :]).astype(I32), axis=1), N_EXPERTS - 1)
    n_used = (ends[-1] // MOE_TILE).reshape(1).astype(I32)
    return tbl, tail, tile_expert.astype(I32), n_used, n_tiles


def kernel(x_prompt, x_sample, state_conv, state_h, cache_k, cache_v, g_mix, g_ffn, g_kv, g_final, a_w_gate, a_b_gate, a_w_in, a_b_in, a_conv_w, a_conv_b, a_w_r, a_b_r, a_w_i, a_b_i, a_lam, a_w_out, a_b_out, w_kv, b_kv, rel_bias, b_w_q, b_b_q, b_sinks, b_w_o, b_b_o, f_w_gate, f_w_up, f_w_down, m_w_router, m_w_gate, m_w_up, m_w_down):
    bp, seq, d = x_prompt.shape
    bs, steps, _ = x_sample.shape
    n_prompt = bp * seq
    n_sample = bs * steps
    n = n_prompt + n_sample
    assert seq % MIX_TILE == 0 and seq % WINDOW == 0 and n_prompt % ROW_TILE == 0
    assert n_sample == ROW_TILE and bs % SAMPLE_GROUP == 0

    mix_w = dict(g=_row(g_mix[0]), wg=a_w_gate[0].astype(BF16), bg=_row(a_b_gate[0]),
                 win=a_w_in[0].astype(BF16), bin=_row(a_b_in[0]), cw=a_conv_w[0], cb=_row(a_conv_b[0]),
                 wr=a_w_r[0].astype(BF16), br=_row(a_b_r[0]), wi=a_w_i[0].astype(BF16), bi=_row(a_b_i[0]),
                 lam=_row(a_lam[0]), wout=a_w_out[0].astype(BF16), bout=_row(a_b_out[0]))

    ffn_w = dict(gf=_row(g_ffn[0]), wg=f_w_gate[0].astype(BF16), wu=f_w_up[0].astype(BF16),
                 wd=f_w_down[0].astype(BF16), gkv=_row(g_kv), wkv=w_kv.astype(BF16), bkv=_row(b_kv),
                 gq=_row(g_mix[1]), wq=b_w_q[0].astype(BF16), bq=_row(b_b_q[0]))

    h2_p, kv_p, q_p, p_conv, p_h = _layer0_prompt(x_prompt, mix_w, ffn_w)
    h1_s, s_conv_tm, s_h = _mixer_sample(jnp.transpose(x_sample, (1, 0, 2)),
                                         jnp.transpose(state_conv[0], (1, 0, 2)), state_h[0], mix_w)
    h2_s, kv_s, q_s = _ffn_rows(jnp.transpose(h1_s, (1, 0, 2)).reshape(n_sample, d), ffn_w)

    bias = _bias_band(rel_bias)
    sinks = b_sinks[0].astype(F32)
    o_p, moe_wg, moe_wu = _attn_prompt(q_p, kv_p, bias, sinks, bp, seq, m_w_gate[0], m_w_up[0])
    keys = WINDOW + steps + (-steps) % SUBLANES
    q4 = q_s.reshape(bs, steps, N_KV, GROUP, HEAD_DIM).transpose(0, 2, 1, 3, 4)
    q4 = q4.reshape(bs, N_KV, steps * GROUP, HEAD_DIM)
    kv_s = kv_s.reshape(bs, steps, 2 * KV_DIM)
    k_new, v_new = kv_s[:, :, :KV_DIM], kv_s[:, :, KV_DIM:]
    bias_rows = bias.reshape(N_KV, 2, GROUP // 2, WINDOW, 2 * WINDOW)[:, :, :, :steps, :keys]
    bias_rows = bias_rows.transpose(0, 3, 2, 1, 4).reshape(N_KV, steps * GROUP, keys)
    sink_rows = jnp.broadcast_to(sinks.reshape(N_KV, 1, GROUP), (N_KV, steps, GROUP)).reshape(N_KV, steps * GROUP, 1)
    o4 = _attn_sample(q4, cache_k.reshape(bs, WINDOW, KV_DIM), cache_v.reshape(bs, WINDOW, KV_DIM),
                      k_new, v_new, bias_rows, sink_rows)
    o_s = o4.reshape(bs, N_KV, steps, GROUP, HEAD_DIM).transpose(0, 2, 1, 3, 4).reshape(n_sample, d)

    wr_pad = jnp.zeros((d, LANES), F32).at[:, :N_EXPERTS].set(m_w_router[0])
    wr_hi = wr_pad.astype(BF16)
    wr_lo = (wr_pad - wr_hi.astype(F32)).astype(BF16)
    h3, route, counts, moe_wd = _oproj_router(
        o_p, o_s, h2_p, h2_s,
        dict(wo=b_w_o[0].astype(BF16), bo=_row(b_b_o[0]), gf=_row(g_ffn[1]), wr_hi=wr_hi, wr_lo=wr_lo), m_w_down[0])

    tbl, tail, tile_expert, n_used, n_tiles = _routing_tables(counts)
    xs = _sort_place(h3, route, _row(g_ffn[1]), tbl, tail, n_used, n_tiles * MOE_TILE)
    ys = _moe_experts(xs, tile_expert, n_used, moe_wg, moe_wu, moe_wd)
    y_p, y_s = _combine_final(h3, route, tbl, ys, _row(g_final), n_prompt)

    y_prompt = y_p.reshape(bp, seq, d)
    y_sample = y_s.reshape(bs, steps, d)
    kv_last = jnp.stack([kv_p[b * seq + seq - WINDOW:(b + 1) * seq] for b in range(bp)])
    p_k = kv_last[:, :, :KV_DIM].reshape(bp, WINDOW, N_KV, HEAD_DIM)
    p_v = kv_last[:, :, KV_DIM:].reshape(bp, WINDOW, N_KV, HEAD_DIM)
    s_k = jnp.concatenate([cache_k, k_new.reshape(bs, steps, N_KV, HEAD_DIM)], axis=1)[:, -WINDOW:]
    s_v = jnp.concatenate([cache_v, v_new.reshape(bs, steps, N_KV, HEAD_DIM)], axis=1)[:, -WINDOW:]
    return (y_prompt, y_sample, p_conv[None], p_h.reshape(1, bp, LRU_WIDTH), p_k, p_v,
            jnp.transpose(s_conv_tm, (1, 0, 2))[None], s_h[None], s_k, s_v)
```

```python
import functools
import math

import jax
import jax.numpy as jnp
from jax import lax
from jax.experimental import pallas as pl
from jax.experimental.pallas import tpu as pltpu

D_MODEL = 1024
LRU_WIDTH = D_MODEL
LRU_BLOCK_W = 256
LRU_BLOCKS = LRU_WIDTH // LRU_BLOCK_W
CONV_W = 4
LRU_C = 8.0
HEAD_DIM = 64
N_HEADS = D_MODEL // HEAD_DIM
N_KV = 2
GROUP = N_HEADS // N_KV
KV_DIM = N_KV * HEAD_DIM
WINDOW = 128
NUM_BUCKETS = 32
MAX_DISTANCE = 128
D_FF = 3 * D_MODEL
N_EXPERTS = 8
TOP_K = 2
D_EXPERT = 7 * D_MODEL // 2
EPS = 1e-6
NEG = -1e30

BF16 = jnp.bfloat16
F32 = jnp.float32
I32 = jnp.int32

SUBLANES = 8
LANES = 128
VMEM_LIMIT_BYTES = 56 * 1024 * 1024

ROW_TILE = 512
MIX_TILE = 256
FF_CHUNK = 1024
MOE_TILE = 512
MOE_CHUNKS = 2


def _params(*semantics):
    return pltpu.CompilerParams(dimension_semantics=semantics, vmem_limit_bytes=VMEM_LIMIT_BYTES)


def _const_spec(shape):
    zeros = (0,) * len(shape)
    return pl.BlockSpec(shape, lambda *_: zeros, pipeline_mode=pl.Buffered(1))


def _dot(a, b):
    return jnp.dot(a, b, preferred_element_type=F32)


def _rms(x, g):
    ms = jnp.mean(x * x, axis=-1, keepdims=True)
    return x * lax.rsqrt(ms + EPS) * g


def _sigmoid(x):
    return 1.0 / (1.0 + jnp.exp(-x))


def _gelu_tanh(x):
    return 0.5 * x * (1.0 + jnp.tanh(0.7978845608028654 * (x + 0.044715 * (x * x * x))))


def _log_sigmoid(x):
    return jnp.minimum(x, 0.0) - jnp.log1p(jnp.exp(-jnp.abs(x)))


def _lru_gates(xc, wr_ref, br, wi_ref, bi, lam):
    xcb = xc.astype(BF16)
    rs, gs = [], []
    for n in range(LRU_BLOCKS):
        xn = xcb[:, n * LRU_BLOCK_W:(n + 1) * LRU_BLOCK_W]
        rs.append(_dot(xn, wr_ref[n]))
        gs.append(_dot(xn, wi_ref[n]))
    r = _sigmoid(jnp.concatenate(rs, axis=1) + br)
    i = _sigmoid(jnp.concatenate(gs, axis=1) + bi)
    log_a = LRU_C * r * _log_sigmoid(lam)
    a = jnp.exp(log_a)
    mult = jnp.sqrt(1.0 - a * a)
    return a, mult * (i * xc)


def _interleave(order, *stage_generators):
    results = [None] * len(stage_generators)
    finished = set()

    def advance(idx):
        if idx in finished:
            return
        try:
            next(stage_generators[idx])
        except StopIteration as done:
            results[idx] = done.value
            finished.add(idx)

    for idx in order:
        advance(idx)
    while len(finished) < len(stage_generators):
        for idx in range(len(stage_generators)):
            advance(idx)
    return results


def _mixer_stages(x, g_ref, wg_ref, bg_ref, win_ref, bin_ref, cw_ref, cb_ref,
                  wr_ref, br_ref, wi_ref, bi_ref, lam_ref, wout_ref, bout_ref, xr_buf, h_carry):
    tt = x.shape[0]
    pad = SUBLANES
    u = _rms(x, g_ref[...]).astype(BF16)
    gate = _gelu_tanh(_dot(u, wg_ref[...]) + bg_ref[...])
    xr = _dot(u, win_ref[...]) + bin_ref[...]
    yield
    xr_buf[pad:pad + tt, :] = xr
    xc = cb_ref[...] + cw_ref[CONV_W - 1:CONV_W, :] * xr
    for k in range(CONV_W - 1):
        back = CONV_W - 1 - k
        xc = xc + cw_ref[k:k + 1, :] * xr_buf[pad - back:pad - back + tt, :]
    xr_buf[0:pad, :] = xr[tt - pad:tt, :]
    yield

    a, b = _lru_gates(xc, wr_ref, br_ref[...], wi_ref, bi_ref[...], lam_ref[...])
    yield

    groups = tt // SUBLANES
    a3 = a.reshape(groups, SUBLANES, LRU_WIDTH)
    b3 = b.reshape(groups, SUBLANES, LRU_WIDTH)
    row = lax.broadcasted_iota(I32, (1, SUBLANES, LRU_WIDTH), 1)
    step = 1
    while step < SUBLANES:
        keep = row >= step
        a_prev = jnp.where(keep, pltpu.roll(a3, step, axis=1), 1.0)
        b_prev = jnp.where(keep, pltpu.roll(b3, step, axis=1), 0.0)
        b3 = b3 + a3 * b_prev
        a3 = a3 * a_prev
        step *= 2
    yield
    h_prev = h_carry[0:1, :]
    hs = []
    for gi in range(groups):
        hg = b3[gi] + a3[gi] * h_prev
        hs.append(hg)
        h_prev = hg[SUBLANES - 1:SUBLANES, :]
        if gi + 1 == groups // 2:
            yield
    h = jnp.concatenate(hs, axis=0)
    h_carry[0:1, :] = h_prev
    yield
    y = _dot((h * gate).astype(BF16), wout_ref[...]) + bout_ref[...]
    return x + y, xr[tt - (CONV_W - 1):tt, :], h_prev


LAYER0_ORDER = (0, 1, 0, 0, 1, 0, 0, 1, 0, 0)
N_MIX_REFS = 14
N_FFN_REFS = 10


def _ffn_stages(h1, gf_ref, wg_ref, wu_ref, wd_ref, gkv_ref, wkv_ref, bkv_ref, gq_ref, wq_ref, bq_ref):
    u = _rms(h1, gf_ref[...]).astype(BF16)
    acc = h1
    n_chunks = D_FF // FF_CHUNK
    for c in range(n_chunks):
        cols = slice(c * FF_CHUNK, (c + 1) * FF_CHUNK)
        g = _dot(u, wg_ref[:, cols])
        up = _dot(u, wu_ref[:, cols])
        mid = (g * _sigmoid(g) * up).astype(BF16)
        acc = acc + _dot(mid, wd_ref[cols, :])
        if c + 1 < n_chunks:
            yield
    kv = _dot(_rms(acc, gkv_ref[...]).astype(BF16), wkv_ref[...]) + bkv_ref[...]
    q = _dot(_rms(acc, gq_ref[...]).astype(BF16), wq_ref[...]) + bq_ref[...]
    return acc, kv, (q * (HEAD_DIM ** -0.5)).astype(BF16)


def _layer0_prompt_kernel(*refs, n_tiles, tiles_per_seq):
    x_ref = refs[0]
    mix_refs = refs[1:1 + N_MIX_REFS]
    ffn_refs = refs[1 + N_MIX_REFS:1 + N_MIX_REFS + N_FFN_REFS]
    h2_ref, kv_ref, q_ref, conv_ref, hlast_ref, xr_buf, h_carry, h1_buf = refs[1 + N_MIX_REFS + N_FFN_REFS:]
    s = pl.program_id(0)
    slot = lax.rem(s, 2)

    @pl.when(s == 0)
    def _():
        h1_buf[...] = jnp.zeros_like(h1_buf)

    @pl.when(lax.rem(s, tiles_per_seq) == 0)
    def _():
        xr_buf[0:SUBLANES, :] = jnp.zeros((SUBLANES, LRU_WIDTH), F32)
        h_carry[...] = jnp.zeros_like(h_carry)

    (h1, conv_tail, h_last), (h2, kv, q) = _interleave(
        LAYER0_ORDER,
        _mixer_stages(x_ref[...], *mix_refs, xr_buf, h_carry), _ffn_stages(h1_buf[1 - slot], *ffn_refs))
    h2_ref[...] = h2
    kv_ref[...] = kv
    q_ref[...] = q
    h1_buf[slot] = h1

    @pl.when(s < n_tiles)
    def _():
        b = s // tiles_per_seq
        conv_ref[b] = conv_tail
        hlast_ref[b] = h_last


def _layer0_prompt(x, mix_w, ffn_w):
    b, t, d = x.shape
    tt = MIX_TILE
    n_tiles = (b * t) // tt
    vec = lambda n: _const_spec((1, n))
    mix_specs = [
        vec(d), _const_spec((d, LRU_WIDTH)), vec(LRU_WIDTH), _const_spec((d, LRU_WIDTH)), vec(LRU_WIDTH),
        _const_spec((CONV_W, LRU_WIDTH)), vec(LRU_WIDTH),
        _const_spec((LRU_BLOCKS, LRU_BLOCK_W, LRU_BLOCK_W)), vec(LRU_WIDTH),
        _const_spec((LRU_BLOCKS, LRU_BLOCK_W, LRU_BLOCK_W)), vec(LRU_WIDTH),
        vec(LRU_WIDTH), _const_spec((LRU_WIDTH, d)), vec(d),
    ]
    ffn_specs = [
        vec(d), _const_spec((d, D_FF)), _const_spec((d, D_FF)), _const_spec((D_FF, d)),
        vec(d), _const_spec((d, 2 * KV_DIM)), vec(2 * KV_DIM), vec(d), _const_spec((d, d)), vec(d),
    ]
    assert len(mix_specs) == N_MIX_REFS and len(ffn_specs) == N_FFN_REFS
    prev = lambda s: (jnp.maximum(s - 1, 0), 0)
    whole = lambda shape: pl.BlockSpec(shape, lambda s: (0,) * len(shape))
    return pl.pallas_call(
        functools.partial(_layer0_prompt_kernel, n_tiles=n_tiles, tiles_per_seq=t // tt),
        grid=(n_tiles + 1,),
        in_specs=[pl.BlockSpec((tt, d), lambda s: (jnp.minimum(s, n_tiles - 1), 0))] + mix_specs + ffn_specs,
        out_specs=[pl.BlockSpec((tt, d), prev), pl.BlockSpec((tt, 2 * KV_DIM), prev), pl.BlockSpec((tt, d), prev),
                   whole((b, CONV_W - 1, LRU_WIDTH)), whole((b, 1, LRU_WIDTH))],
        out_shape=[jax.ShapeDtypeStruct((b * t, d), F32), jax.ShapeDtypeStruct((b * t, 2 * KV_DIM), F32),
                   jax.ShapeDtypeStruct((b * t, d), BF16),
                   jax.ShapeDtypeStruct((b, CONV_W - 1, LRU_WIDTH), F32),
                   jax.ShapeDtypeStruct((b, 1, LRU_WIDTH), F32)],
        scratch_shapes=[pltpu.VMEM((SUBLANES + tt, LRU_WIDTH), F32), pltpu.VMEM((SUBLANES, LRU_WIDTH), F32),
                        pltpu.VMEM((2, tt, d), F32)],
        compiler_params=_params("arbitrary"),
        name="layer0_prompt",
    )(x.reshape(b * t, d), *[mix_w[k] for k in MIX_KEYS], *[ffn_w[k] for k in FFN_KEYS])


MIX_KEYS = ("g", "wg", "bg", "win", "bin", "cw", "cb", "wr", "br", "wi", "bi", "lam", "wout", "bout")
FFN_KEYS = ("gf", "wg", "wu", "wd", "gkv", "wkv", "bkv", "gq", "wq", "bq")


def _mixer_sample_kernel(x_ref, cs_ref, h0_ref, g_ref, wg_ref, bg_ref, win_ref, bin_ref, cw_ref, cb_ref,
                         wr_ref, br_ref, wi_ref, bi_ref, lam_ref, wout_ref, bout_ref,
                         h1_ref, conv_ref, hlast_ref):
    steps, nb, d = x_ref.shape
    x = x_ref[...].reshape(steps * nb, d)
    u = _rms(x, g_ref[...]).astype(BF16)
    gate = _gelu_tanh(_dot(u, wg_ref[...]) + bg_ref[...])
    xr = _dot(u, win_ref[...]) + bin_ref[...]
    xpad = [cs_ref[k] for k in range(CONV_W - 1)] + [xr[s * nb:(s + 1) * nb, :] for s in range(steps)]
    xcs = []
    for s in range(steps):
        acc = cb_ref[...] + cw_ref[0:1, :] * xpad[s]
        for k in range(1, CONV_W):
            acc = acc + cw_ref[k:k + 1, :] * xpad[s + k]
        xcs.append(acc)
    for k in range(CONV_W - 1):
        conv_ref[k] = xpad[steps + k]
    xc = jnp.concatenate(xcs, axis=0)
    a, b = _lru_gates(xc, wr_ref, br_ref[...], wi_ref, bi_ref[...], lam_ref[...])
    h = h0_ref[...]
    hs = []
    for s in range(steps):
        h = a[s * nb:(s + 1) * nb, :] * h + b[s * nb:(s + 1) * nb, :]
        hs.append(h)
    hlast_ref[...] = h
    hcat = jnp.concatenate(hs, axis=0)
    y = _dot((hcat * gate).astype(BF16), wout_ref[...]) + bout_ref[...]
    h1_ref[...] = (x + y).reshape(steps, nb, d)


def _mixer_sample(x_tm, cs_tm, h0, w):
    steps, nb, d = x_tm.shape
    full = lambda shape: pl.BlockSpec(shape, lambda i: (0,) * len(shape))
    vec = lambda n: full((1, n))
    in_specs = [
        full((steps, nb, d)), full((CONV_W - 1, nb, LRU_WIDTH)), full((nb, LRU_WIDTH)),
        vec(d), full((d, LRU_WIDTH)), vec(LRU_WIDTH), full((d, LRU_WIDTH)), vec(LRU_WIDTH),
        full((CONV_W, LRU_WIDTH)), vec(LRU_WIDTH),
        full((LRU_BLOCKS, LRU_BLOCK_W, LRU_BLOCK_W)), vec(LRU_WIDTH),
        full((LRU_BLOCKS, LRU_BLOCK_W, LRU_BLOCK_W)), vec(LRU_WIDTH),
        vec(LRU_WIDTH), full((LRU_WIDTH, d)), vec(d),
    ]
    out_specs = [full((steps, nb, d)), full((CONV_W - 1, nb, LRU_WIDTH)), full((nb, LRU_WIDTH))]
    out_shape = [
        jax.ShapeDtypeStruct((steps, nb, d), F32),
        jax.ShapeDtypeStruct((CONV_W - 1, nb, LRU_WIDTH), F32),
        jax.ShapeDtypeStruct((nb, LRU_WIDTH), F32),
    ]
    return pl.pallas_call(
        _mixer_sample_kernel, grid=(1,), in_specs=in_specs, out_specs=out_specs, out_shape=out_shape,
        compiler_params=_params("arbitrary"), name="mixer_sample",
    )(x_tm, cs_tm, h0, w["g"], w["wg"], w["bg"], w["win"], w["bin"], w["cw"], w["cb"], w["wr"], w["br"],
      w["wi"], w["bi"], w["lam"], w["wout"], w["bout"])


def _two_part_specs(n_prompt, n_sample, width):
    assert n_sample == ROW_TILE and n_prompt % ROW_TILE == 0
    last_prompt = n_prompt // ROW_TILE - 1
    return [pl.BlockSpec((ROW_TILE, width), lambda i: (jnp.minimum(i, last_prompt), 0)),
            pl.BlockSpec((ROW_TILE, width), lambda i: (0, 0))]


def _two_part_tile(prompt_ref, sample_ref):
    is_sample = pl.program_id(0) == pl.num_programs(0) - 1
    return jnp.where(is_sample, sample_ref[...], prompt_ref[...])


def _side_cast_specs(w, n_steps, step_of):
    ne, rows, cols = w.shape
    per_expert = n_steps // ne
    blk = rows // per_expert
    assert per_expert * ne == n_steps and blk * per_expert == rows and blk % (2 * SUBLANES) == 0
    index = lambda *ids: (step_of(*ids) // per_expert, step_of(*ids) % per_expert, 0)
    spec = pl.BlockSpec((1, blk, cols), index)
    return spec, spec, jax.ShapeDtypeStruct(w.shape, BF16)


def _ffn_rows_kernel(h1_ref, *refs):
    h2_ref, kv_ref, q_ref = refs[N_FFN_REFS:]
    (h2_ref[...], kv_ref[...], q_ref[...]), = _interleave((), _ffn_stages(h1_ref[...], *refs[:N_FFN_REFS]))


def _ffn_rows(h1, w):
    n, d = h1.shape
    tm = ROW_TILE
    row = lambda width: pl.BlockSpec((tm, width), lambda i: (i, 0))
    vec = lambda width: _const_spec((1, width))
    in_specs = [
        row(d), vec(d), _const_spec((d, D_FF)), _const_spec((d, D_FF)), _const_spec((D_FF, d)),
        vec(d), _const_spec((d, 2 * KV_DIM)), vec(2 * KV_DIM), vec(d), _const_spec((d, d)), vec(d),
    ]
    return pl.pallas_call(
        _ffn_rows_kernel,
        grid=(n // tm,),
        in_specs=in_specs,
        out_specs=[row(d), row(2 * KV_DIM), row(d)],
        out_shape=[jax.ShapeDtypeStruct((n, d), F32), jax.ShapeDtypeStruct((n, 2 * KV_DIM), F32),
                   jax.ShapeDtypeStruct((n, d), BF16)],
        compiler_params=_params("arbitrary"),
        name="ffn_rows",
    )(h1, *[w[k] for k in FFN_KEYS])


def _head_of(j, parity, pair):
    return j * GROUP + 2 * pair + parity


def _bias_band_kernel(table_ref, out_ref):
    q_len, s_len = WINDOW, out_ref.shape[3]
    qi = lax.broadcasted_iota(I32, (q_len, s_len), 0)
    si = lax.broadcasted_iota(I32, (q_len, s_len), 1)
    dist = qi + WINDOW - si
    max_exact = NUM_BUCKETS // 2
    n = jnp.maximum(dist, 0)
    large = jnp.full((q_len, s_len), max_exact, I32)
    for step in range(1, NUM_BUCKETS - max_exact):
        threshold = math.ceil(max_exact * (MAX_DISTANCE / max_exact) ** (step / (NUM_BUCKETS - max_exact)))
        large = large + jnp.where(n >= threshold, 1, 0)
    bucket = jnp.where(n < max_exact, n, large)
    for j in range(N_KV):
        for parity in range(2):
            for pair in range(GROUP // 2):
                h = _head_of(j, parity, pair)
                acc = jnp.zeros((q_len, s_len), F32)
                for bkt in range(NUM_BUCKETS):
                    acc = jnp.where(bucket == bkt, table_ref[bkt, h], acc)
                out_ref[j, parity, pair * q_len:(pair + 1) * q_len, :] = acc


def _bias_band(rel_table):
    shape = (N_KV, 2, (GROUP // 2) * WINDOW, 2 * WINDOW)
    return pl.pallas_call(
        _bias_band_kernel,
        grid=(1,),
        in_specs=[pl.BlockSpec(memory_space=pltpu.SMEM)],
        out_specs=pl.BlockSpec(shape, lambda i: (0, 0, 0, 0)),
        out_shape=jax.ShapeDtypeStruct(shape, F32),
        compiler_params=_params("arbitrary"),
        name="bias_band",
    )(rel_table)


def _softmax_sink_pv(s, sink, v):
    m = jnp.maximum(jnp.max(s, axis=-1, keepdims=True), sink)
    p = jnp.exp(s - m)
    denom = jnp.sum(p, axis=-1, keepdims=True) + jnp.exp(sink - m)
    return _dot(p.astype(BF16), v) / denom


ATTN_BLOCKS = 4


def _attn_prompt_kernel(sink_ref, q_ref, kvp_ref, kvc_ref, bias_ref, *refs):
    n_casts = (len(refs) - 1) // 2
    o_ref = refs[n_casts]
    for src_ref, dst_ref in zip(refs[:n_casts], refs[n_casts + 1:]):
        dst_ref[...] = src_ref[...].astype(BF16)
    w = WINDOW
    pairs = GROUP // 2
    assert 2 * HEAD_DIM == LANES and KV_DIM == LANES
    qi = lax.broadcasted_iota(I32, (pairs * w, 2 * w), 0) % w
    si = lax.broadcasted_iota(I32, (pairs * w, 2 * w), 1)
    dist = qi + w - si
    band = (dist >= 0) & (dist < w)
    band_first = band & ((si >= w) | (pl.program_id(1) > 0))

    kv = jnp.concatenate([kvp_ref[...], kvc_ref[...]], axis=0)
    low = lax.broadcasted_iota(I32, (kv.shape[0], LANES), 1) < HEAD_DIM

    def halves(x):
        swapped = pltpu.roll(x, HEAD_DIM, axis=1)
        return (((jnp.where(low, x, 0.0)).astype(BF16), (jnp.where(low, 0.0, swapped)).astype(BF16)),
                ((jnp.where(low, swapped, 0.0)).astype(BF16), (jnp.where(low, 0.0, x)).astype(BF16)))

    k_ops = halves(kv[:, 0:KV_DIM])
    v_ops = halves(kv[:, KV_DIM:2 * KV_DIM])

    for i in range(ATTN_BLOCKS):
        q_rows = slice(i * w, (i + 1) * w)
        k_rows = slice(i * w, (i + 2) * w)
        mask = band_first if i == 0 else band
        for j in range(N_KV):
            q4 = jnp.concatenate([q_ref[q_rows, (j * pairs + b) * LANES:(j * pairs + b + 1) * LANES]
                                  for b in range(pairs)], axis=0)
            acc = None
            for parity in range(2):
                s = lax.dot_general(q4, k_ops[j][parity][k_rows], (((1,), (1,)), ((), ())),
                                    preferred_element_type=F32)
                s = jnp.where(mask, s + bias_ref[j, parity], NEG)
                sink = jnp.concatenate([jnp.full((w, 1), sink_ref[_head_of(j, parity, b)], F32)
                                        for b in range(pairs)], axis=0)
                o = _softmax_sink_pv(s, sink, v_ops[j][parity][k_rows])
                acc = o if acc is None else acc + o
            for b in range(pairs):
                o_ref[q_rows, (j * pairs + b) * LANES:(j * pairs + b + 1) * LANES] = acc[b * w:(b + 1) * w].astype(BF16)


def _attn_prompt(q, kv, bias, sinks, batch, seq, cast_ws):
    nb = seq // WINDOW
    assert nb % ATTN_BLOCKS == 0
    ns = nb // ATTN_BLOCKS
    rows = ATTN_BLOCKS * WINDOW
    d = q.shape[1]
    casts = [_side_cast_specs(w, batch * ns, lambda b, n: b * ns + n) for w in cast_ws]
    in_specs = [
        pl.BlockSpec(memory_space=pltpu.SMEM),
        pl.BlockSpec((rows, d), lambda b, n: (b * ns + n, 0)),
        pl.BlockSpec((WINDOW, 2 * KV_DIM), lambda b, n: (b * nb + jnp.maximum(n * ATTN_BLOCKS - 1, 0), 0)),
        pl.BlockSpec((rows, 2 * KV_DIM), lambda b, n: (b * ns + n, 0)),
        _const_spec(bias.shape),
    ] + [c[0] for c in casts]
    return pl.pallas_call(
        _attn_prompt_kernel,
        grid=(batch, ns),
        in_specs=in_specs,
        out_specs=[pl.BlockSpec((rows, d), lambda b, n: (b * ns + n, 0))] + [c[1] for c in casts],
        out_shape=[jax.ShapeDtypeStruct((batch * seq, d), BF16)] + [c[2] for c in casts],
        compiler_params=_params("arbitrary", "arbitrary"),
        name="attn_prompt",
    )(sinks, q, kv, kv, bias, *cast_ws)


SAMPLE_GROUP = 8


def _attn_sample_kernel(q_ref, ckt_ref, cvt_ref, new_ref, newt_ref, bias_ref, sink_ref, o_ref, skt_ref, svt_ref):
    gb = q_ref.shape[0]
    steps = new_ref.shape[1]
    rows = steps * GROUP
    pad = (-steps) % SUBLANES
    ri = lax.broadcasted_iota(I32, (rows, WINDOW + steps + pad), 0)
    si = lax.broadcasted_iota(I32, (rows, WINDOW + steps + pad), 1)
    dist = ri // GROUP + WINDOW - si
    mask = (dist >= 0) & (dist < WINDOW)
    mask_c, mask_n = mask[:, :WINDOW], mask[:, WINDOW:]
    zeros = jnp.zeros((pad, 2 * KV_DIM), F32)

    def scores(b, j):
        new = jnp.concatenate([new_ref[b], zeros], axis=0).astype(BF16)
        q = q_ref[b, j]
        s_c = _dot(q, ckt_ref[b, j].astype(BF16))
        s_n = lax.dot_general(q, new[:, j * HEAD_DIM:(j + 1) * HEAD_DIM], (((1,), (1,)), ((), ())),
                              preferred_element_type=F32)
        bias = bias_ref[j]
        return (jnp.where(mask_c, s_c + bias[:, :WINDOW], NEG), jnp.where(mask_n, s_n + bias[:, WINDOW:], NEG), new)

    tasks = [(b, j) for b in range(gb) for j in range(N_KV)]
    all_scores = [scores(b, j) for b, j in tasks]
    for (b, j), (s_c, s_n, new) in zip(tasks, all_scores):
        sink = sink_ref[j]
        m = jnp.maximum(jnp.maximum(jnp.max(s_c, axis=-1, keepdims=True), jnp.max(s_n, axis=-1, keepdims=True)), sink)
        p_c = jnp.exp(s_c - m)
        p_n = jnp.exp(s_n - m)
        denom = jnp.sum(p_c, axis=-1, keepdims=True) + jnp.sum(p_n, axis=-1, keepdims=True) + jnp.exp(sink - m)
        pv = lax.dot_general(p_c.astype(BF16), cvt_ref[b, j].astype(BF16), (((1,), (1,)), ((), ())),
                             preferred_element_type=F32)
        pv = pv + _dot(p_n.astype(BF16), new[:, KV_DIM + j * HEAD_DIM:KV_DIM + (j + 1) * HEAD_DIM])
        o_ref[b, j] = (pv / denom).astype(BF16)

    lane = lax.broadcasted_iota(I32, (HEAD_DIM, WINDOW), 1)
    for b in range(gb):
        newt = jnp.concatenate([newt_ref[b], jnp.zeros((2 * KV_DIM, WINDOW - steps), F32)], axis=1)
        for j in range(N_KV):
            for src_ref, dst_ref, base in ((ckt_ref, skt_ref, 0), (cvt_ref, svt_ref, KV_DIM)):
                fresh = newt[base + j * HEAD_DIM:base + (j + 1) * HEAD_DIM, :]
                dst_ref[b, j] = pltpu.roll(jnp.where(lane < steps, fresh, src_ref[b, j]), WINDOW - steps, axis=1)


def _attn_sample(q4, ckt, cvt, new, newt, bias_rows, sink_rows):
    nb, _, rows, hd = q4.shape
    steps = new.shape[1]
    keys = bias_rows.shape[2]
    gb = SAMPLE_GROUP
    cache_spec = pl.BlockSpec((gb, N_KV, hd, WINDOW), lambda i: (i, 0, 0, 0))
    in_specs = [
        pl.BlockSpec((gb, N_KV, rows, hd), lambda i: (i, 0, 0, 0)),
        cache_spec, cache_spec,
        pl.BlockSpec((gb, steps, 2 * KV_DIM), lambda i: (i, 0, 0)),
        pl.BlockSpec((gb, 2 * KV_DIM, steps), lambda i: (i, 0, 0)),
        pl.BlockSpec((N_KV, rows, keys), lambda i: (0, 0, 0)),
        pl.BlockSpec((N_KV, rows, 1), lambda i: (0, 0, 0)),
    ]
    return pl.pallas_call(
        _attn_sample_kernel,
        grid=(nb // gb,),
        in_specs=in_specs,
        out_specs=[pl.BlockSpec((gb, N_KV, rows, hd), lambda i: (i, 0, 0, 0)), cache_spec, cache_spec],
        out_shape=[jax.ShapeDtypeStruct(q4.shape, BF16), jax.ShapeDtypeStruct(ckt.shape, F32),
                   jax.ShapeDtypeStruct(cvt.shape, F32)],
        compiler_params=_params("arbitrary"),
        name="attn_sample",
    )(q4, ckt, cvt, new, newt, bias_rows, sink_rows)


def _split_bf16(x):
    hi = x.astype(BF16)
    lo = (x - hi.astype(F32)).astype(BF16)
    return hi, lo


ROUTER_PARTS = 2


def _oproj_router_kernel(op_ref, os_ref, h2p_ref, h2s_ref, wo_ref, bo_ref, gf_ref, wrh_ref, wrl_ref,
                         h3_ref, xn_ref, route_ref, counts_ref):
    tm = h3_ref.shape[0]
    parts = ROUTER_PARTS
    pr = tm // parts
    o = _two_part_tile(op_ref, os_ref)
    h2 = _two_part_tile(h2p_ref, h2s_ref)
    lane = lax.broadcasted_iota(I32, (pr, LANES), 1)
    lanef = lane.astype(F32)
    ri = lax.broadcasted_iota(I32, (pr, pr), 0)
    ci = lax.broadcasted_iota(I32, (pr, pr), 1)
    earlier = jnp.where(ri > ci, 1.0, 0.0).astype(BF16)

    picks, ranks, gates = [], [], []
    cnt = jnp.zeros((1, LANES), F32)
    for p in range(parts):
        rows = slice(p * pr, (p + 1) * pr)
        h3 = h2[rows, :] + _dot(o[rows, :], wo_ref[...]) + bo_ref[...]
        h3_ref[rows, :] = h3
        u_hi, u_lo = _split_bf16(_rms(h3, gf_ref[...]))
        xn_ref[rows, :] = u_hi
        logits =_dot(u_hi, wrh_ref[...]) + (_dot(u_lo, wrh_ref[...]) + _dot(u_hi, wrl_ref[...]))
        logits = jnp.where(lane < N_EXPERTS, logits, -jnp.inf)
        v1 = jnp.max(logits, axis=-1, keepdims=True)
        e1 = jnp.min(jnp.where(logits == v1, lanef, float(LANES)), axis=-1, keepdims=True)
        rest = jnp.where(lanef == e1, -jnp.inf, logits)
        v2 = jnp.max(rest, axis=-1, keepdims=True)
        e2 = jnp.min(jnp.where(rest == v2, lanef, float(LANES)), axis=-1, keepdims=True)
        ex = jnp.exp(v2 - v1)
        gates.append((1.0 / (1.0 + ex), ex / (1.0 + ex)))
        pick1 = lanef == e1
        pick2 = lanef == e2
        picks.append((pick1, pick2))
        sel = jnp.where(pick1 | pick2, 1.0, 0.0)
        ranks.append(_dot(earlier, sel.astype(BF16)) + cnt)
        cnt = cnt + jnp.sum(sel, axis=0, keepdims=True)

    seg = jnp.floor((cnt + (SUBLANES - 1)) * (1.0 / SUBLANES)) * SUBLANES
    ek = lax.broadcasted_iota(I32, (LANES, LANES), 0)
    el = lax.broadcasted_iota(I32, (LANES, LANES), 1)
    lower_experts = jnp.where(ek < el, 1.0, 0.0).astype(BF16)
    seg_start = _dot(jnp.broadcast_to(seg, (SUBLANES, LANES)).astype(BF16), lower_experts)[0:1, :]
    for p in range(parts):
        local = ranks[p] + seg_start
        lr1 = jnp.sum(jnp.where(picks[p][0], local, 0.0), axis=-1, keepdims=True)
        lr2 = jnp.sum(jnp.where(picks[p][1], local, 0.0), axis=-1, keepdims=True)
        w1, w2 = gates[p]
        route_ref[p * pr:(p + 1) * pr, :] = jnp.where(
            lane == 0, lr1, jnp.where(lane == 1, lr2, jnp.where(lane == 2, w1, jnp.where(lane == 3, w2, 0.0))))
    counts_ref[0] = jnp.broadcast_to(cnt, (SUBLANES, LANES))


def _oproj_router(o_p, o_s, h2_p, h2_s, w):
    d = h2_p.shape[1]
    n = h2_p.shape[0] + h2_s.shape[0]
    tm = ROW_TILE
    row = lambda width: pl.BlockSpec((tm, width), lambda i: (i, 0))
    in_specs = (_two_part_specs(o_p.shape[0], o_s.shape[0], d) + _two_part_specs(h2_p.shape[0], h2_s.shape[0], d) + [
        _const_spec((d, d)), _const_spec((1, d)), _const_spec((1, d)),
        _const_spec((d, LANES)), _const_spec((d, LANES))])
    return pl.pallas_call(
        _oproj_router_kernel,
        grid=(n // tm,),
        in_specs=in_specs,
        out_specs=[row(d), row(d), row(LANES), pl.BlockSpec((1, SUBLANES, LANES), lambda i: (i, 0, 0))],
        out_shape=[jax.ShapeDtypeStruct((n, d), F32), jax.ShapeDtypeStruct((n, d), BF16),
                   jax.ShapeDtypeStruct((n, LANES), F32), jax.ShapeDtypeStruct((n // tm, SUBLANES, LANES), F32)],
        compiler_params=_params("arbitrary"),
        name="oproj_router",
    )(o_p, o_s, h2_p, h2_s, w["wo"], w["bo"], w["gf"], w["wr_hi"], w["wr_lo"])


LOCAL_ROWS = 1152
XS_WIDTH = D_MODEL + LANES
SEG_TABLE = 3 * N_EXPERTS
SEG_PIECES = tuple(SUBLANES << b for b in reversed(range(7)))


def _segment_copies(src_ref, src_start, dst_ref, dst_start, length, sem, act):
    for piece in SEG_PIECES:
        done = lax.div(length, 2 * piece) * (2 * piece)

        @pl.when(lax.rem(lax.div(length, piece), 2) != 0)
        def _():
            s = pl.multiple_of(src_start + done, SUBLANES)
            t = pl.multiple_of(dst_start + done, SUBLANES)
            act(pltpu.make_async_copy(src_ref.at[pl.ds(s, piece), :], dst_ref.at[pl.ds(t, piece), :], sem))


def _sort_place_kernel(tbl_ref, tail_ref, nu_ref, xn_ref, route_ref, xs_ref, stage, zeros, sems):
    j = pl.program_id(0)
    tm = xn_ref.shape[0]
    xn = xn_ref[...]
    rt = jnp.transpose(route_ref[...])
    row = lax.broadcasted_iota(I32, (LOCAL_ROWS, tm), 0).astype(F32)
    m1 = row == rt[0:1, :]
    m2 = row == rt[1:2, :]
    onehot = jnp.where(m1 | m2, 1.0, 0.0).astype(BF16)
    slot = lax.rem(j, 2)
    buf = stage.at[slot]
    buf[:, 0:D_MODEL] = _dot(onehot, xn)
    gate = jnp.sum(jnp.where(m1, rt[2:3, :], 0.0) + jnp.where(m2, rt[3:4, :], 0.0), axis=-1, keepdims=True)
    buf[:, D_MODEL:XS_WIDTH] = jnp.broadcast_to(gate, (LOCAL_ROWS, LANES))

    def segments(tile, tile_slot, act):
        for e in range(N_EXPERTS):
            base = tile * SEG_TABLE
            _segment_copies(stage.at[tile_slot], tbl_ref[base + e], xs_ref, tbl_ref[base + 2 * N_EXPERTS + e],
                            tbl_ref[base + N_EXPERTS + e], sems.at[tile_slot], act)

    segments(j, slot, lambda cp: cp.start())

    @pl.when(j > 0)
    def _():
        segments(j - 1, 1 - slot, lambda cp: cp.wait())

    @pl.when(j == pl.num_programs(0) - 1)
    def _():
        segments(j, slot, lambda cp: cp.wait())
        zeros[...] = jnp.zeros_like(zeros)
        sem = sems.at[0]

        def tails(act):
            for e in range(N_EXPERTS):
                _segment_copies(zeros, 0, xs_ref, tail_ref[e], tail_ref[N_EXPERTS + e], sem, act)

        def unused(act):
            def body(i, c):
                t = pl.multiple_of(i * MOE_TILE, MOE_TILE)
                act(pltpu.make_async_copy(zeros, xs_ref.at[pl.ds(t, MOE_TILE), :], sem))
                return c
            lax.fori_loop(nu_ref[0], xs_ref.shape[0] // MOE_TILE, body, 0)

        tails(lambda cp: cp.start())
        unused(lambda cp: cp.start())
        tails(lambda cp: cp.wait())
        unused(lambda cp: cp.wait())


def _sort_place(xn, route, tbl, tail, n_used, p_rows):
    n, d = xn.shape
    tm = ROW_TILE
    grid_spec = pltpu.PrefetchScalarGridSpec(
        num_scalar_prefetch=3,
        grid=(n // tm,),
        in_specs=[pl.BlockSpec((tm, d), lambda j, *_: (j, 0)),
                  pl.BlockSpec((tm, LANES), lambda j, *_: (j, 0))],
        out_specs=pl.BlockSpec(memory_space=pl.ANY),
        scratch_shapes=[pltpu.VMEM((2, LOCAL_ROWS, XS_WIDTH), F32), pltpu.VMEM((MOE_TILE, XS_WIDTH), F32),
                        pltpu.SemaphoreType.DMA((2,))],
    )
    return pl.pallas_call(
        _sort_place_kernel,
        grid_spec=grid_spec,
        out_shape=jax.ShapeDtypeStruct((p_rows, XS_WIDTH), F32),
        compiler_params=_params("arbitrary"),
        name="moe_sort_place",
    )(tbl, tail, n_used, xn, route)


def _moe_kernel(te_ref, nu_ref, xs_ref, wg_ref, wu_ref, wd_ref, y_ref):
    i = pl.program_id(0)
    c = pl.program_id(1)
    used = i < nu_ref[0]

    @pl.when(used)
    def _():
        xn = xs_ref[:, 0:D_MODEL].astype(BF16)
        g = _dot(xn, wg_ref[0])
        up = _dot(xn, wu_ref[0])
        mid = (g * _sigmoid(g) * up).astype(BF16)
        y = _dot(mid, wd_ref[0]) * xs_ref[:, D_MODEL:D_MODEL + 1]

        @pl.when(c == 0)
        def _():
            y_ref[...] = y

        @pl.when(c > 0)
        def _():
            y_ref[...] = y_ref[...] + y

    @pl.when(jnp.logical_not(used) & (c == 0))
    def _():
        y_ref[...] = jnp.zeros_like(y_ref)


def _moe_experts(xs, tile_expert, n_used, wg, wu, wd):
    p_rows = xs.shape[0]
    d = D_MODEL
    tm = MOE_TILE
    n_tiles = p_rows // tm
    ch = D_EXPERT // MOE_CHUNKS
    last = MOE_CHUNKS - 1

    def chunk(i, c, nu):
        return jnp.where(i < nu[0], c, last)

    grid_spec = pltpu.PrefetchScalarGridSpec(
        num_scalar_prefetch=2,
        grid=(n_tiles, MOE_CHUNKS),
        in_specs=[
            pl.BlockSpec((tm, XS_WIDTH), lambda i, c, te, nu: (i, 0)),
            pl.BlockSpec((1, d, ch), lambda i, c, te, nu: (te[i], 0, chunk(i, c, nu))),
            pl.BlockSpec((1, d, ch), lambda i, c, te, nu: (te[i], 0, chunk(i, c, nu))),
            pl.BlockSpec((1, ch, d), lambda i, c, te, nu: (te[i], chunk(i, c, nu), 0)),
        ],
        out_specs=pl.BlockSpec((tm, d), lambda i, c, te, nu: (i, 0)),
    )
    return pl.pallas_call(
        _moe_kernel,
        grid_spec=grid_spec,
        out_shape=jax.ShapeDtypeStruct((p_rows, d), F32),
        compiler_params=_params("arbitrary", "arbitrary"),
        name="moe_experts",
    )(tile_expert, n_used, xs, wg, wu, wd)


def _combine_kernel(tbl_ref, h3_ref, route_ref, ys_ref, g_ref, outp_ref, outs_ref, ybuf, sems):
    j = pl.program_id(0)
    last = pl.num_programs(0) - 1
    tm = h3_ref.shape[0]
    slot = lax.rem(j, 2)

    def segments(tile, tile_slot, act):
        for e in range(N_EXPERTS):
            base = tile * SEG_TABLE
            _segment_copies(ys_ref, tbl_ref[base + 2 * N_EXPERTS + e], ybuf.at[tile_slot], tbl_ref[base + e],
                            tbl_ref[base + N_EXPERTS + e], sems.at[tile_slot], act)

    @pl.when(j == 0)
    def _():
        ybuf[...] = jnp.zeros_like(ybuf)
        segments(j, slot, lambda cp: cp.start())

    @pl.when(j < last)
    def _():
        segments(j + 1, 1 - slot, lambda cp: cp.start())

    segments(j, slot, lambda cp: cp.wait())

    route = route_ref[...]
    col = lax.broadcasted_iota(I32, (tm, LOCAL_ROWS), 1).astype(F32)
    picks = jnp.where((col == route[:, 0:1]) | (col == route[:, 1:2]), 1.0, 0.0).astype(BF16)
    h4 = h3_ref[...] + _dot(picks, ybuf[slot].astype(BF16))
    out = _rms(h4, g_ref[...])

    @pl.when(j < last)
    def _():
        outp_ref[...] = out

    @pl.when(j == last)
    def _():
        outs_ref[...] = out


def _combine_final(h3, route, tbl, ys, g_final, n_prompt):
    n, d = h3.shape
    tm = ROW_TILE
    n_sample = n - n_prompt
    assert n_sample == tm and n_prompt % tm == 0
    last_prompt = n_prompt // tm - 1
    grid_spec = pltpu.PrefetchScalarGridSpec(
        num_scalar_prefetch=1,
        grid=(n // tm,),
        in_specs=[pl.BlockSpec((tm, d), lambda j, *_: (j, 0)),
                  pl.BlockSpec((tm, LANES), lambda j, *_: (j, 0)),
                  pl.BlockSpec(memory_space=pl.ANY),
                  pl.BlockSpec((1, d), lambda j, *_: (0, 0))],
        out_specs=[pl.BlockSpec((tm, d), lambda j, *_: (jnp.minimum(j, last_prompt), 0)),
                   pl.BlockSpec((tm, d), lambda j, *_: (0, 0))],
        scratch_shapes=[pltpu.VMEM((2, LOCAL_ROWS, d), F32), pltpu.SemaphoreType.DMA((2,))],
    )
    return pl.pallas_call(
        _combine_kernel,
        grid_spec=grid_spec,
        out_shape=[jax.ShapeDtypeStruct((n_prompt, d), F32), jax.ShapeDtypeStruct((n_sample, d), F32)],
        compiler_params=_params("arbitrary"),
        name="moe_combine",
    )(tbl, h3, route, ys, g_final)


def _row(v):
    return v.reshape(1, -1).astype(F32)


def _routing_tables(counts):
    n_row_tiles = counts.shape[0]
    cnt = counts[:, 0, :N_EXPERTS].astype(I32)
    seg = ((cnt + SUBLANES - 1) // SUBLANES) * SUBLANES
    local_start = jnp.cumsum(seg, axis=1) - seg
    rows = jnp.sum(seg, axis=0)
    padded = ((rows + MOE_TILE - 1) // MOE_TILE) * MOE_TILE
    ends = jnp.cumsum(padded)
    starts = ends - padded
    sorted_start = starts[None, :] + jnp.cumsum(seg, axis=0) - seg
    tbl = jnp.concatenate([local_start, seg, sorted_start], axis=1).reshape(-1).astype(I32)
    tail = jnp.concatenate([starts + rows, padded - rows]).astype(I32)
    max_rows = n_row_tiles * (TOP_K * ROW_TILE + N_EXPERTS * (SUBLANES - 1)) + N_EXPERTS * (MOE_TILE - SUBLANES)
    n_tiles = -(-max_rows // MOE_TILE)
    tile_start = jnp.arange(n_tiles, dtype=I32) * MOE_TILE
    tile_expert = jnp.minimum(jnp.sum((tile_start[:, None] >= ends[None, :]).astype(I32), axis=1), N_EXPERTS - 1)
    n_used = (ends[-1] // MOE_TILE).reshape(1).astype(I32)
    return tbl, tail, tile_expert.astype(I32), n_used, n_tiles


def kernel(x_prompt, x_sample, state_conv, state_h, cache_k, cache_v, g_mix, g_ffn, g_kv, g_final, a_w_gate, a_b_gate, a_w_in, a_b_in, a_conv_w, a_conv_b, a_w_r, a_b_r, a_w_i, a_b_i, a_lam, a_w_out, a_b_out, w_kv, b_kv, rel_bias, b_w_q, b_b_q, b_sinks, b_w_o, b_b_o, f_w_gate, f_w_up, f_w_down, m_w_router, m_w_gate, m_w_up, m_w_down):
    bp, seq, d = x_prompt.shape
    bs, steps, _ = x_sample.shape
    n_prompt = bp * seq
    n_sample = bs * steps
    n = n_prompt + n_sample
    assert seq % MIX_TILE == 0 and seq % WINDOW == 0 and n_prompt % ROW_TILE == 0
    assert n_sample == ROW_TILE and bs % SAMPLE_GROUP == 0

    mix_w = dict(g=_row(g_mix[0]), wg=a_w_gate[0].astype(BF16), bg=_row(a_b_gate[0]),
                 win=a_w_in[0].astype(BF16), bin=_row(a_b_in[0]), cw=a_conv_w[0], cb=_row(a_conv_b[0]),
                 wr=a_w_r[0].astype(BF16), br=_row(a_b_r[0]), wi=a_w_i[0].astype(BF16), bi=_row(a_b_i[0]),
                 lam=_row(a_lam[0]), wout=a_w_out[0].astype(BF16), bout=_row(a_b_out[0]))

    ffn_w = dict(gf=_row(g_ffn[0]), wg=f_w_gate[0].astype(BF16), wu=f_w_up[0].astype(BF16),
                 wd=f_w_down[0].astype(BF16), gkv=_row(g_kv), wkv=w_kv.astype(BF16), bkv=_row(b_kv),
                 gq=_row(g_mix[1]), wq=b_w_q[0].astype(BF16), bq=_row(b_b_q[0]))

    h2_p, kv_p, q_p, p_conv, p_h = _layer0_prompt(x_prompt, mix_w, ffn_w)
    h1_s, s_conv_tm, s_h = _mixer_sample(jnp.transpose(x_sample, (1, 0, 2)),
                                         jnp.transpose(state_conv[0], (1, 0, 2)), state_h[0], mix_w)
    h2_s, kv_s, q_s = _ffn_rows(jnp.transpose(h1_s, (1, 0, 2)).reshape(n_sample, d), ffn_w)

    bias = _bias_band(rel_bias)
    sinks = b_sinks[0].astype(F32)
    o_p, moe_wg, moe_wu, moe_wd = _attn_prompt(q_p, kv_p, bias, sinks, bp, seq,
                                               [m_w_gate[0], m_w_up[0], m_w_down[0]])
    keys = WINDOW + steps + (-steps) % SUBLANES
    q4 = q_s.reshape(bs, steps, N_KV, GROUP, HEAD_DIM).transpose(0, 2, 1, 3, 4)
    q4 = q4.reshape(bs, N_KV, steps * GROUP, HEAD_DIM)
    kv_new = kv_s.reshape(bs, steps, 2 * KV_DIM)
    bias_rows = bias.reshape(N_KV, 2, GROUP // 2, WINDOW, 2 * WINDOW)[:, :, :, :steps, :keys]
    bias_rows = bias_rows.transpose(0, 3, 2, 1, 4).reshape(N_KV, steps * GROUP, keys)
    sink_rows = jnp.broadcast_to(sinks.reshape(N_KV, 1, GROUP), (N_KV, steps, GROUP)).reshape(N_KV, steps * GROUP, 1)
    o4, s_kt, s_vt = _attn_sample(q4, jnp.transpose(cache_k, (0, 2, 3, 1)), jnp.transpose(cache_v, (0, 2, 3, 1)),
                                  kv_new, jnp.transpose(kv_new, (0, 2, 1)), bias_rows, sink_rows)
    o_s = o4.reshape(bs, N_KV, steps, GROUP, HEAD_DIM).transpose(0, 2, 1, 3, 4).reshape(n_sample, d)

    wr_pad = jnp.zeros((d, LANES), F32).at[:, :N_EXPERTS].set(m_w_router[0])
    wr_hi = wr_pad.astype(BF16)
    wr_lo = (wr_pad - wr_hi.astype(F32)).astype(BF16)
    h3, xn, route, counts = _oproj_router(
        o_p, o_s, h2_p, h2_s,
        dict(wo=b_w_o[0].astype(BF16), bo=_row(b_b_o[0]), gf=_row(g_ffn[1]), wr_hi=wr_hi, wr_lo=wr_lo))

    tbl, tail, tile_expert, n_used, n_tiles = _routing_tables(counts)
    xs = _sort_place(xn, route, tbl, tail, n_used, n_tiles * MOE_TILE)
    ys = _moe_experts(xs, tile_expert, n_used, moe_wg, moe_wu, moe_wd)
    y_p, y_s = _combine_final(h3, route, tbl, ys, _row(g_final), n_prompt)

    y_prompt = y_p.reshape(bp, seq, d)
    y_sample = y_s.reshape(bs, steps, d)
    kv_last = jnp.stack([kv_p[b * seq + seq - WINDOW:(b + 1) * seq] for b in range(bp)])
    p_k = kv_last[:, :, :KV_DIM].reshape(bp, WINDOW, N_KV, HEAD_DIM)
    p_v = kv_last[:, :, KV_DIM:].reshape(bp, WINDOW, N_KV, HEAD_DIM)
    s_k = jnp.transpose(s_kt, (0, 3, 1, 2))
    s_v = jnp.transpose(s_vt, (0, 3, 1, 2))
    return (y_prompt, y_sample, p_conv[None], p_h.reshape(1, bp, LRU_WIDTH), p_k, p_v,
            jnp.transpose(s_conv_tm, (1, 0, 2))[None], s_h[None], s_k, s_v)
```

```python
import functools
import math

import jax
import jax.numpy as jnp
from jax import lax
from jax.experimental import pallas as pl
from jax.experimental.pallas import tpu as pltpu

D_MODEL = 1024
LRU_WIDTH = D_MODEL
LRU_BLOCK_W = 256
LRU_BLOCKS = LRU_WIDTH // LRU_BLOCK_W
CONV_W = 4
LRU_C = 8.0
HEAD_DIM = 64
N_HEADS = D_MODEL // HEAD_DIM
N_KV = 2
GROUP = N_HEADS // N_KV
KV_DIM = N_KV * HEAD_DIM
WINDOW = 128
NUM_BUCKETS = 32
MAX_DISTANCE = 128
D_FF = 3 * D_MODEL
N_EXPERTS = 8
TOP_K = 2
D_EXPERT = 7 * D_MODEL // 2
EPS = 1e-6
NEG = -1e30

BF16 = jnp.bfloat16
F32 = jnp.float32
I32 = jnp.int32

SUBLANES = 8
LANES = 128
VMEM_LIMIT_BYTES = 56 * 1024 * 1024

ROW_TILE = 512
MIX_TILE = 256
FF_CHUNK = 1024
MOE_TILE = 512
MOE_CHUNKS = 2


def _params(*semantics):
    return pltpu.CompilerParams(dimension_semantics=semantics, vmem_limit_bytes=VMEM_LIMIT_BYTES)


def _const_spec(shape):
    zeros = (0,) * len(shape)
    return pl.BlockSpec(shape, lambda *_: zeros, pipeline_mode=pl.Buffered(1))


def _dot(a, b):
    return jnp.dot(a, b, preferred_element_type=F32)


def _rms(x, g):
    ms = jnp.mean(x * x, axis=-1, keepdims=True)
    return x * lax.rsqrt(ms + EPS) * g


def _sigmoid(x):
    return 1.0 / (1.0 + jnp.exp(-x))


def _gelu_tanh(x):
    return 0.5 * x * (1.0 + jnp.tanh(0.7978845608028654 * (x + 0.044715 * (x * x * x))))


def _log_sigmoid(x):
    return jnp.minimum(x, 0.0) - jnp.log1p(jnp.exp(-jnp.abs(x)))


def _lru_gates(xc, wr_ref, br, wi_ref, bi, lam):
    xcb = xc.astype(BF16)
    rs, gs = [], []
    for n in range(LRU_BLOCKS):
        xn = xcb[:, n * LRU_BLOCK_W:(n + 1) * LRU_BLOCK_W]
        rs.append(_dot(xn, wr_ref[n]))
        gs.append(_dot(xn, wi_ref[n]))
    r = _sigmoid(jnp.concatenate(rs, axis=1) + br)
    i = _sigmoid(jnp.concatenate(gs, axis=1) + bi)
    log_a = LRU_C * r * _log_sigmoid(lam)
    a = jnp.exp(log_a)
    mult = jnp.sqrt(1.0 - a * a)
    return a, mult * (i * xc)


def _interleave(order, *stage_generators):
    results = [None] * len(stage_generators)
    finished = set()

    def advance(idx):
        if idx in finished:
            return
        try:
            next(stage_generators[idx])
        except StopIteration as done:
            results[idx] = done.value
            finished.add(idx)

    for idx in order:
        advance(idx)
    while len(finished) < len(stage_generators):
        for idx in range(len(stage_generators)):
            advance(idx)
    return results


def _mixer_stages(x, g_ref, wg_ref, bg_ref, win_ref, bin_ref, cw_ref, cb_ref,
                  wr_ref, br_ref, wi_ref, bi_ref, lam_ref, wout_ref, bout_ref, xr_buf, h_carry):
    tt = x.shape[0]
    pad = SUBLANES
    u = _rms(x, g_ref[...]).astype(BF16)
    gate = _gelu_tanh(_dot(u, wg_ref[...]) + bg_ref[...])
    xr = _dot(u, win_ref[...]) + bin_ref[...]
    yield
    xr_buf[pad:pad + tt, :] = xr
    xc = cb_ref[...] + cw_ref[CONV_W - 1:CONV_W, :] * xr
    for k in range(CONV_W - 1):
        back = CONV_W - 1 - k
        xc = xc + cw_ref[k:k + 1, :] * xr_buf[pad - back:pad - back + tt, :]
    xr_buf[0:pad, :] = xr[tt - pad:tt, :]
    yield

    a, b = _lru_gates(xc, wr_ref, br_ref[...], wi_ref, bi_ref[...], lam_ref[...])
    yield

    groups = tt // SUBLANES
    a3 = a.reshape(groups, SUBLANES, LRU_WIDTH)
    b3 = b.reshape(groups, SUBLANES, LRU_WIDTH)
    row = lax.broadcasted_iota(I32, (1, SUBLANES, LRU_WIDTH), 1)
    step = 1
    while step < SUBLANES:
        keep = row >= step
        a_prev = jnp.where(keep, pltpu.roll(a3, step, axis=1), 1.0)
        b_prev = jnp.where(keep, pltpu.roll(b3, step, axis=1), 0.0)
        b3 = b3 + a3 * b_prev
        a3 = a3 * a_prev
        step *= 2
    yield
    h_prev = h_carry[0:1, :]
    hs = []
    for gi in range(groups):
        hg = b3[gi] + a3[gi] * h_prev
        hs.append(hg)
        h_prev = hg[SUBLANES - 1:SUBLANES, :]
        if gi + 1 == groups // 2:
            yield
    h = jnp.concatenate(hs, axis=0)
    h_carry[0:1, :] = h_prev
    yield
    y = _dot((h * gate).astype(BF16), wout_ref[...]) + bout_ref[...]
    return x + y, xr[tt - (CONV_W - 1):tt, :], h_prev


LAYER0_ORDER = (0, 1, 0, 0, 1, 0, 0, 1, 0, 0)
N_MIX_REFS = 14
N_FFN_REFS = 10


def _ffn_stages(h1, gf_ref, wg_ref, wu_ref, wd_ref, gkv_ref, wkv_ref, bkv_ref, gq_ref, wq_ref, bq_ref):
    u = _rms(h1, gf_ref[...]).astype(BF16)
    acc = h1
    n_chunks = D_FF // FF_CHUNK
    for c in range(n_chunks):
        cols = slice(c * FF_CHUNK, (c + 1) * FF_CHUNK)
        g = _dot(u, wg_ref[:, cols])
        up = _dot(u, wu_ref[:, cols])
        mid = (g * _sigmoid(g) * up).astype(BF16)
        acc = acc + _dot(mid, wd_ref[cols, :])
        if c + 1 < n_chunks:
            yield
    kv = _dot(_rms(acc, gkv_ref[...]).astype(BF16), wkv_ref[...]) + bkv_ref[...]
    q = _dot(_rms(acc, gq_ref[...]).astype(BF16), wq_ref[...]) + bq_ref[...]
    return acc, kv, (q * (HEAD_DIM ** -0.5)).astype(BF16)


def _layer0_prompt_kernel(*refs, n_tiles, tiles_per_seq):
    x_ref = refs[0]
    mix_refs = refs[1:1 + N_MIX_REFS]
    ffn_refs = refs[1 + N_MIX_REFS:1 + N_MIX_REFS + N_FFN_REFS]
    h2_ref, kv_ref, q_ref, conv_ref, hlast_ref, xr_buf, h_carry, h1_buf = refs[1 + N_MIX_REFS + N_FFN_REFS:]
    s = pl.program_id(0)
    slot = lax.rem(s, 2)

    @pl.when(s == 0)
    def _():
        h1_buf[...] = jnp.zeros_like(h1_buf)

    @pl.when(lax.rem(s, tiles_per_seq) == 0)
    def _():
        xr_buf[0:SUBLANES, :] = jnp.zeros((SUBLANES, LRU_WIDTH), F32)
        h_carry[...] = jnp.zeros_like(h_carry)

    (h1, conv_tail, h_last), (h2, kv, q) = _interleave(
        LAYER0_ORDER,
        _mixer_stages(x_ref[...], *mix_refs, xr_buf, h_carry), _ffn_stages(h1_buf[1 - slot], *ffn_refs))
    h2_ref[...] = h2
    kv_ref[...] = kv
    q_ref[...] = q
    h1_buf[slot] = h1

    @pl.when(s < n_tiles)
    def _():
        b = s // tiles_per_seq
        conv_ref[b] = conv_tail
        hlast_ref[b] = h_last


def _layer0_prompt(x, mix_w, ffn_w):
    b, t, d = x.shape
    tt = MIX_TILE
    n_tiles = (b * t) // tt
    vec = lambda n: _const_spec((1, n))
    mix_specs = [
        vec(d), _const_spec((d, LRU_WIDTH)), vec(LRU_WIDTH), _const_spec((d, LRU_WIDTH)), vec(LRU_WIDTH),
        _const_spec((CONV_W, LRU_WIDTH)), vec(LRU_WIDTH),
        _const_spec((LRU_BLOCKS, LRU_BLOCK_W, LRU_BLOCK_W)), vec(LRU_WIDTH),
        _const_spec((LRU_BLOCKS, LRU_BLOCK_W, LRU_BLOCK_W)), vec(LRU_WIDTH),
        vec(LRU_WIDTH), _const_spec((LRU_WIDTH, d)), vec(d),
    ]
    ffn_specs = [
        vec(d), _const_spec((d, D_FF)), _const_spec((d, D_FF)), _const_spec((D_FF, d)),
        vec(d), _const_spec((d, 2 * KV_DIM)), vec(2 * KV_DIM), vec(d), _const_spec((d, d)), vec(d),
    ]
    assert len(mix_specs) == N_MIX_REFS and len(ffn_specs) == N_FFN_REFS
    prev = lambda s: (jnp.maximum(s - 1, 0), 0)
    whole = lambda shape: pl.BlockSpec(shape, lambda s: (0,) * len(shape))
    return pl.pallas_call(
        functools.partial(_layer0_prompt_kernel, n_tiles=n_tiles, tiles_per_seq=t // tt),
        grid=(n_tiles + 1,),
        in_specs=[pl.BlockSpec((tt, d), lambda s: (jnp.minimum(s, n_tiles - 1), 0))] + mix_specs + ffn_specs,
        out_specs=[pl.BlockSpec((tt, d), prev), pl.BlockSpec((tt, 2 * KV_DIM), prev), pl.BlockSpec((tt, d), prev),
                   whole((b, CONV_W - 1, LRU_WIDTH)), whole((b, 1, LRU_WIDTH))],
        out_shape=[jax.ShapeDtypeStruct((b * t, d), F32), jax.ShapeDtypeStruct((b * t, 2 * KV_DIM), F32),
                   jax.ShapeDtypeStruct((b * t, d), BF16),
                   jax.ShapeDtypeStruct((b, CONV_W - 1, LRU_WIDTH), F32),
                   jax.ShapeDtypeStruct((b, 1, LRU_WIDTH), F32)],
        scratch_shapes=[pltpu.VMEM((SUBLANES + tt, LRU_WIDTH), F32), pltpu.VMEM((SUBLANES, LRU_WIDTH), F32),
                        pltpu.VMEM((2, tt, d), F32)],
        compiler_params=_params("arbitrary"),
        name="layer0_prompt",
    )(x.reshape(b * t, d), *[mix_w[k] for k in MIX_KEYS], *[ffn_w[k] for k in FFN_KEYS])


MIX_KEYS = ("g", "wg", "bg", "win", "bin", "cw", "cb", "wr", "br", "wi", "bi", "lam", "wout", "bout")
FFN_KEYS = ("gf", "wg", "wu", "wd", "gkv", "wkv", "bkv", "gq", "wq", "bq")


def _mixer_sample_kernel(x_ref, cs_ref, h0_ref, g_ref, wg_ref, bg_ref, win_ref, bin_ref, cw_ref, cb_ref,
                         wr_ref, br_ref, wi_ref, bi_ref, lam_ref, wout_ref, bout_ref,
                         h1_ref, conv_ref, hlast_ref):
    steps, nb, d = x_ref.shape
    x = x_ref[...].reshape(steps * nb, d)
    u = _rms(x, g_ref[...]).astype(BF16)
    gate = _gelu_tanh(_dot(u, wg_ref[...]) + bg_ref[...])
    xr = _dot(u, win_ref[...]) + bin_ref[...]
    xpad = [cs_ref[k] for k in range(CONV_W - 1)] + [xr[s * nb:(s + 1) * nb, :] for s in range(steps)]
    xcs = []
    for s in range(steps):
        acc = cb_ref[...] + cw_ref[0:1, :] * xpad[s]
        for k in range(1, CONV_W):
            acc = acc + cw_ref[k:k + 1, :] * xpad[s + k]
        xcs.append(acc)
    for k in range(CONV_W - 1):
        conv_ref[k] = xpad[steps + k]
    xc = jnp.concatenate(xcs, axis=0)
    a, b = _lru_gates(xc, wr_ref, br_ref[...], wi_ref, bi_ref[...], lam_ref[...])
    h = h0_ref[...]
    hs = []
    for s in range(steps):
        h = a[s * nb:(s + 1) * nb, :] * h + b[s * nb:(s + 1) * nb, :]
        hs.append(h)
    hlast_ref[...] = h
    hcat = jnp.concatenate(hs, axis=0)
    y = _dot((hcat * gate).astype(BF16), wout_ref[...]) + bout_ref[...]
    h1_ref[...] = (x + y).reshape(steps, nb, d)


def _mixer_sample(x_tm, cs_tm, h0, w):
    steps, nb, d = x_tm.shape
    full = lambda shape: pl.BlockSpec(shape, lambda i: (0,) * len(shape))
    vec = lambda n: full((1, n))
    in_specs = [
        full((steps, nb, d)), full((CONV_W - 1, nb, LRU_WIDTH)), full((nb, LRU_WIDTH)),
        vec(d), full((d, LRU_WIDTH)), vec(LRU_WIDTH), full((d, LRU_WIDTH)), vec(LRU_WIDTH),
        full((CONV_W, LRU_WIDTH)), vec(LRU_WIDTH),
        full((LRU_BLOCKS, LRU_BLOCK_W, LRU_BLOCK_W)), vec(LRU_WIDTH),
        full((LRU_BLOCKS, LRU_BLOCK_W, LRU_BLOCK_W)), vec(LRU_WIDTH),
        vec(LRU_WIDTH), full((LRU_WIDTH, d)), vec(d),
    ]
    out_specs = [full((steps, nb, d)), full((CONV_W - 1, nb, LRU_WIDTH)), full((nb, LRU_WIDTH))]
    out_shape = [
        jax.ShapeDtypeStruct((steps, nb, d), F32),
        jax.ShapeDtypeStruct((CONV_W - 1, nb, LRU_WIDTH), F32),
        jax.ShapeDtypeStruct((nb, LRU_WIDTH), F32),
    ]
    return pl.pallas_call(
        _mixer_sample_kernel, grid=(1,), in_specs=in_specs, out_specs=out_specs, out_shape=out_shape,
        compiler_params=_params("arbitrary"), name="mixer_sample",
    )(x_tm, cs_tm, h0, w["g"], w["wg"], w["bg"], w["win"], w["bin"], w["cw"], w["cb"], w["wr"], w["br"],
      w["wi"], w["bi"], w["lam"], w["wout"], w["bout"])


def _two_part_specs(n_prompt, n_sample, width):
    assert n_sample == ROW_TILE and n_prompt % ROW_TILE == 0
    last_prompt = n_prompt // ROW_TILE - 1
    return [pl.BlockSpec((ROW_TILE, width), lambda i: (jnp.minimum(i, last_prompt), 0)),
            pl.BlockSpec((ROW_TILE, width), lambda i: (0, 0))]


def _two_part_tile(prompt_ref, sample_ref):
    is_sample = pl.program_id(0) == pl.num_programs(0) - 1
    return jnp.where(is_sample, sample_ref[...], prompt_ref[...])


def _side_cast_specs(w, n_steps, step_of):
    ne, rows, cols = w.shape
    per_expert = n_steps // ne
    blk = rows // per_expert
    assert per_expert * ne == n_steps and blk * per_expert == rows and blk % (2 * SUBLANES) == 0
    index = lambda *ids: (step_of(*ids) // per_expert, step_of(*ids) % per_expert, 0)
    spec = pl.BlockSpec((1, blk, cols), index)
    return spec, spec, jax.ShapeDtypeStruct(w.shape, BF16)


def _ffn_rows_kernel(h1_ref, *refs):
    h2_ref, kv_ref, q_ref = refs[N_FFN_REFS:]
    (h2_ref[...], kv_ref[...], q_ref[...]), = _interleave((), _ffn_stages(h1_ref[...], *refs[:N_FFN_REFS]))


def _ffn_rows(h1, w):
    n, d = h1.shape
    tm = ROW_TILE
    row = lambda width: pl.BlockSpec((tm, width), lambda i: (i, 0))
    vec = lambda width: _const_spec((1, width))
    in_specs = [
        row(d), vec(d), _const_spec((d, D_FF)), _const_spec((d, D_FF)), _const_spec((D_FF, d)),
        vec(d), _const_spec((d, 2 * KV_DIM)), vec(2 * KV_DIM), vec(d), _const_spec((d, d)), vec(d),
    ]
    return pl.pallas_call(
        _ffn_rows_kernel,
        grid=(n // tm,),
        in_specs=in_specs,
        out_specs=[row(d), row(2 * KV_DIM), row(d)],
        out_shape=[jax.ShapeDtypeStruct((n, d), F32), jax.ShapeDtypeStruct((n, 2 * KV_DIM), F32),
                   jax.ShapeDtypeStruct((n, d), BF16)],
        compiler_params=_params("arbitrary"),
        name="ffn_rows",
    )(h1, *[w[k] for k in FFN_KEYS])


def _head_of(j, parity, pair):
    return j * GROUP + 2 * pair + parity


def _distance_bias(dist, table_ref, head):
    max_exact = NUM_BUCKETS // 2
    n = jnp.maximum(dist, 0)
    large = jnp.full(dist.shape, max_exact, I32)
    for step in range(1, NUM_BUCKETS - max_exact):
        threshold = math.ceil(max_exact * (MAX_DISTANCE / max_exact) ** (step / (NUM_BUCKETS - max_exact)))
        large = large + jnp.where(n >= threshold, 1, 0)
    bucket = jnp.where(n < max_exact, n, large)
    acc = jnp.zeros(dist.shape, F32)
    for bkt in range(NUM_BUCKETS):
        acc = jnp.where(bucket == bkt, table_ref[bkt, head], acc)
    return acc


def _bias_band_kernel(table_ref, folded_ref, rows_ref):
    w = WINDOW
    qi = lax.broadcasted_iota(I32, (w, w), 0)
    ci = lax.broadcasted_iota(I32, (w, w), 1)
    folded_dist = jnp.where(ci > qi, qi + w - ci, qi - ci)
    for j in range(N_KV):
        for parity in range(2):
            for pair in range(GROUP // 2):
                folded_ref[j, parity, pair * w:(pair + 1) * w, :] = _distance_bias(
                    folded_dist, table_ref, _head_of(j, parity, pair))
    n_rows, n_keys = rows_ref.shape[1], rows_ref.shape[2]
    ri = lax.broadcasted_iota(I32, (GROUP, n_keys), 0)
    si = lax.broadcasted_iota(I32, (GROUP, n_keys), 1)
    for j in range(N_KV):
        for t in range(n_rows // GROUP):
            acc = jnp.zeros((GROUP, n_keys), F32)
            for g in range(GROUP):
                acc = jnp.where(ri == g, _distance_bias(t + w - si, table_ref, j * GROUP + g), acc)
            rows_ref[j, t * GROUP:(t + 1) * GROUP, :] = acc


def _bias_band(rel_table, steps):
    folded = (N_KV, 2, (GROUP // 2) * WINDOW, WINDOW)
    rows = (N_KV, steps * GROUP, WINDOW + steps + (-steps) % SUBLANES)
    return pl.pallas_call(
        _bias_band_kernel,
        grid=(1,),
        in_specs=[pl.BlockSpec(memory_space=pltpu.SMEM)],
        out_specs=[pl.BlockSpec(folded, lambda i: (0, 0, 0, 0)), pl.BlockSpec(rows, lambda i: (0, 0, 0))],
        out_shape=[jax.ShapeDtypeStruct(folded, F32), jax.ShapeDtypeStruct(rows, F32)],
        compiler_params=_params("arbitrary"),
        name="bias_band",
    )(rel_table)


def _softmax_sink_pv(s, sink, v):
    m = jnp.maximum(jnp.max(s, axis=-1, keepdims=True), sink)
    p = jnp.exp(s - m)
    denom = jnp.sum(p, axis=-1, keepdims=True) + jnp.exp(sink - m)
    return _dot(p.astype(BF16), v) / denom


ATTN_BLOCKS = 4


def _attn_prompt_kernel(sink_ref, q_ref, kvp_ref, kvc_ref, bias_ref, *refs):
    n_casts = (len(refs) - 1) // 2
    o_ref = refs[n_casts]
    for src_ref, dst_ref in zip(refs[:n_casts], refs[n_casts + 1:]):
        dst_ref[...] = src_ref[...].astype(BF16)
    w = WINDOW
    pairs = GROUP // 2
    assert 2 * HEAD_DIM == LANES and KV_DIM == LANES
    qi = lax.broadcasted_iota(I32, (pairs * w, w), 0) % w
    ci = lax.broadcasted_iota(I32, (pairs * w, w), 1)
    from_prev = ci > qi
    has_prev = pl.program_id(1) > 0

    kv = jnp.concatenate([kvp_ref[...], kvc_ref[...]], axis=0)
    low = lax.broadcasted_iota(I32, (kv.shape[0], LANES), 1) < HEAD_DIM

    def halves(x):
        swapped = pltpu.roll(x, HEAD_DIM, axis=1)
        return (((jnp.where(low, x, 0.0)).astype(BF16), (jnp.where(low, 0.0, swapped)).astype(BF16)),
                ((jnp.where(low, swapped, 0.0)).astype(BF16), (jnp.where(low, 0.0, x)).astype(BF16)))

    k_ops = halves(kv[:, 0:KV_DIM])
    v_ops = halves(kv[:, KV_DIM:2 * KV_DIM])

    sinks = [[jnp.concatenate([jnp.full((w, 1), sink_ref[_head_of(j, parity, b)], F32) for b in range(pairs)], axis=0)
              for parity in range(2)] for j in range(N_KV)]
    nt = (((1,), (1,)), ((), ()))
    for i in range(ATTN_BLOCKS):
        q_rows = slice(i * w, (i + 1) * w)
        prev_rows = slice(i * w, (i + 1) * w)
        own_rows = slice((i + 1) * w, (i + 2) * w)
        for j in range(N_KV):
            q4 = jnp.concatenate([q_ref[q_rows, (j * pairs + b) * LANES:(j * pairs + b + 1) * LANES]
                                  for b in range(pairs)], axis=0)
            acc = None
            for parity in range(2):
                k_op, v_op, sink = k_ops[j][parity], v_ops[j][parity], sinks[j][parity]
                s_prev = lax.dot_general(q4, k_op[prev_rows], nt, preferred_element_type=F32)
                s_own = lax.dot_general(q4, k_op[own_rows], nt, preferred_element_type=F32)
                if i == 0:
                    s_prev = jnp.where(has_prev, s_prev, NEG)
                s = jnp.where(from_prev, s_prev, s_own) + bias_ref[j, parity]
                m = jnp.maximum(jnp.max(s, axis=-1, keepdims=True), sink)
                p = jnp.exp(s - m)
                denom = jnp.sum(p, axis=-1, keepdims=True) + jnp.exp(sink - m)
                o = (_dot(jnp.where(from_prev, p, 0.0).astype(BF16), v_op[prev_rows])
                     + _dot(jnp.where(from_prev, 0.0, p).astype(BF16), v_op[own_rows])) / denom
                acc = o if acc is None else acc + o
            for b in range(pairs):
                o_ref[q_rows, (j * pairs + b) * LANES:(j * pairs + b + 1) * LANES] = acc[b * w:(b + 1) * w].astype(BF16)


def _attn_prompt(q, kv, bias, sinks, batch, seq, cast_ws):
    nb = seq // WINDOW
    assert nb % ATTN_BLOCKS == 0
    ns = nb // ATTN_BLOCKS
    rows = ATTN_BLOCKS * WINDOW
    d = q.shape[1]
    casts = [_side_cast_specs(w, batch * ns, lambda b, n: b * ns + n) for w in cast_ws]
    in_specs = [
        pl.BlockSpec(memory_space=pltpu.SMEM),
        pl.BlockSpec((rows, d), lambda b, n: (b * ns + n, 0)),
        pl.BlockSpec((WINDOW, 2 * KV_DIM), lambda b, n: (b * nb + jnp.maximum(n * ATTN_BLOCKS - 1, 0), 0)),
        pl.BlockSpec((rows, 2 * KV_DIM), lambda b, n: (b * ns + n, 0)),
        _const_spec(bias.shape),
    ] + [c[0] for c in casts]
    return pl.pallas_call(
        _attn_prompt_kernel,
        grid=(batch, ns),
        in_specs=in_specs,
        out_specs=[pl.BlockSpec((rows, d), lambda b, n: (b * ns + n, 0))] + [c[1] for c in casts],
        out_shape=[jax.ShapeDtypeStruct((batch * seq, d), BF16)] + [c[2] for c in casts],
        compiler_params=_params("arbitrary", "arbitrary"),
        name="attn_prompt",
    )(sinks, q, kv, kv, bias, *cast_ws)


SAMPLE_GROUP = 8


def _attn_sample_kernel(q_ref, ckt_ref, cvt_ref, new_ref, newt_ref, bias_ref, sink_ref, o_ref, skt_ref, svt_ref):
    gb = q_ref.shape[0]
    steps = new_ref.shape[1]
    rows = steps * GROUP
    pad = (-steps) % SUBLANES
    ri = lax.broadcasted_iota(I32, (rows, WINDOW + steps + pad), 0)
    si = lax.broadcasted_iota(I32, (rows, WINDOW + steps + pad), 1)
    dist = ri // GROUP + WINDOW - si
    mask = (dist >= 0) & (dist < WINDOW)
    mask_c, mask_n = mask[:, :WINDOW], mask[:, WINDOW:]
    zeros = jnp.zeros((pad, 2 * KV_DIM), F32)

    def scores(b, j):
        new = jnp.concatenate([new_ref[b], zeros], axis=0).astype(BF16)
        q = q_ref[b, j]
        s_c = _dot(q, ckt_ref[b, j].astype(BF16))
        s_n = lax.dot_general(q, new[:, j * HEAD_DIM:(j + 1) * HEAD_DIM], (((1,), (1,)), ((), ())),
                              preferred_element_type=F32)
        bias = bias_ref[j]
        return (jnp.where(mask_c, s_c + bias[:, :WINDOW], NEG), jnp.where(mask_n, s_n + bias[:, WINDOW:], NEG), new)

    tasks = [(b, j) for b in range(gb) for j in range(N_KV)]
    all_scores = [scores(b, j) for b, j in tasks]
    for (b, j), (s_c, s_n, new) in zip(tasks, all_scores):
        sink = sink_ref[j]
        m = jnp.maximum(jnp.maximum(jnp.max(s_c, axis=-1, keepdims=True), jnp.max(s_n, axis=-1, keepdims=True)), sink)
        p_c = jnp.exp(s_c - m)
        p_n = jnp.exp(s_n - m)
        denom = jnp.sum(p_c, axis=-1, keepdims=True) + jnp.sum(p_n, axis=-1, keepdims=True) + jnp.exp(sink - m)
        pv = lax.dot_general(p_c.astype(BF16), cvt_ref[b, j].astype(BF16), (((1,), (1,)), ((), ())),
                             preferred_element_type=F32)
        pv = pv + _dot(p_n.astype(BF16), new[:, KV_DIM + j * HEAD_DIM:KV_DIM + (j + 1) * HEAD_DIM])
        o_ref[b, j] = (pv / denom).astype(BF16)

    lane = lax.broadcasted_iota(I32, (HEAD_DIM, WINDOW), 1)
    for b in range(gb):
        newt = jnp.concatenate([newt_ref[b], jnp.zeros((2 * KV_DIM, WINDOW - steps), F32)], axis=1)
        for j in range(N_KV):
            for src_ref, dst_ref, base in ((ckt_ref, skt_ref, 0), (cvt_ref, svt_ref, KV_DIM)):
                fresh = newt[base + j * HEAD_DIM:base + (j + 1) * HEAD_DIM, :]
                dst_ref[b, j] = pltpu.roll(jnp.where(lane < steps, fresh, src_ref[b, j]), WINDOW - steps, axis=1)


def _attn_sample(q4, ckt, cvt, new, newt, bias_rows, sink_rows):
    nb, _, rows, hd = q4.shape
    steps = new.shape[1]
    keys = bias_rows.shape[2]
    gb = SAMPLE_GROUP
    cache_spec = pl.BlockSpec((gb, N_KV, hd, WINDOW), lambda i: (i, 0, 0, 0))
    in_specs = [
        pl.BlockSpec((gb, N_KV, rows, hd), lambda i: (i, 0, 0, 0)),
        cache_spec, cache_spec,
        pl.BlockSpec((gb, steps, 2 * KV_DIM), lambda i: (i, 0, 0)),
        pl.BlockSpec((gb, 2 * KV_DIM, steps), lambda i: (i, 0, 0)),
        pl.BlockSpec((N_KV, rows, keys), lambda i: (0, 0, 0)),
        pl.BlockSpec((N_KV, rows, 1), lambda i: (0, 0, 0)),
    ]
    return pl.pallas_call(
        _attn_sample_kernel,
        grid=(nb // gb,),
        in_specs=in_specs,
        out_specs=[pl.BlockSpec((gb, N_KV, rows, hd), lambda i: (i, 0, 0, 0)), cache_spec, cache_spec],
        out_shape=[jax.ShapeDtypeStruct(q4.shape, BF16), jax.ShapeDtypeStruct(ckt.shape, F32),
                   jax.ShapeDtypeStruct(cvt.shape, F32)],
        compiler_params=_params("arbitrary"),
        name="attn_sample",
    )(q4, ckt, cvt, new, newt, bias_rows, sink_rows)


def _split_bf16(x):
    hi = x.astype(BF16)
    lo = (x - hi.astype(F32)).astype(BF16)
    return hi, lo


ROUTER_PARTS = 2


def _oproj_router_kernel(op_ref, os_ref, h2p_ref, h2s_ref, wo_ref, bo_ref, gf_ref, wrh_ref, wrl_ref,
                         h3_ref, xn_ref, route_ref, counts_ref):
    tm = h3_ref.shape[0]
    parts = ROUTER_PARTS
    pr = tm // parts
    o = _two_part_tile(op_ref, os_ref)
    h2 = _two_part_tile(h2p_ref, h2s_ref)
    lane = lax.broadcasted_iota(I32, (pr, LANES), 1)
    lanef = lane.astype(F32)
    ri = lax.broadcasted_iota(I32, (pr, pr), 0)
    ci = lax.broadcasted_iota(I32, (pr, pr), 1)
    earlier = jnp.where(ri > ci, 1.0, 0.0).astype(BF16)

    picks, ranks, gates = [], [], []
    cnt = jnp.zeros((1, LANES), F32)
    for p in range(parts):
        rows = slice(p * pr, (p + 1) * pr)
        h3 = h2[rows, :] + _dot(o[rows, :], wo_ref[...]) + bo_ref[...]
        h3_ref[rows, :] = h3
        u_hi, u_lo = _split_bf16(_rms(h3, gf_ref[...]))
        xn_ref[rows, :] = u_hi
        logits =_dot(u_hi, wrh_ref[...]) + (_dot(u_lo, wrh_ref[...]) + _dot(u_hi, wrl_ref[...]))
        logits = jnp.where(lane < N_EXPERTS, logits, -jnp.inf)
        v1 = jnp.max(logits, axis=-1, keepdims=True)
        e1 = jnp.min(jnp.where(logits == v1, lanef, float(LANES)), axis=-1, keepdims=True)
        rest = jnp.where(lanef == e1, -jnp.inf, logits)
        v2 = jnp.max(rest, axis=-1, keepdims=True)
        e2 = jnp.min(jnp.where(rest == v2, lanef, float(LANES)), axis=-1, keepdims=True)
        ex = jnp.exp(v2 - v1)
        gates.append((1.0 / (1.0 + ex), ex / (1.0 + ex)))
        pick1 = lanef == e1
        pick2 = lanef == e2
        picks.append((pick1, pick2))
        sel = jnp.where(pick1 | pick2, 1.0, 0.0)
        ranks.append(_dot(earlier, sel.astype(BF16)) + cnt)
        cnt = cnt + jnp.sum(sel, axis=0, keepdims=True)

    seg = jnp.floor((cnt + (SUBLANES - 1)) * (1.0 / SUBLANES)) * SUBLANES
    ek = lax.broadcasted_iota(I32, (LANES, LANES), 0)
    el = lax.broadcasted_iota(I32, (LANES, LANES), 1)
    lower_experts = jnp.where(ek < el, 1.0, 0.0).astype(BF16)
    seg_start = _dot(jnp.broadcast_to(seg, (SUBLANES, LANES)).astype(BF16), lower_experts)[0:1, :]
    for p in range(parts):
        local = ranks[p] + seg_start
        lr1 = jnp.sum(jnp.where(picks[p][0], local, 0.0), axis=-1, keepdims=True)
        lr2 = jnp.sum(jnp.where(picks[p][1], local, 0.0), axis=-1, keepdims=True)
        w1, w2 = gates[p]
        route_ref[p * pr:(p + 1) * pr, :] = jnp.where(
            lane == 0, lr1, jnp.where(lane == 1, lr2, jnp.where(lane == 2, w1, jnp.where(lane == 3, w2, 0.0))))
    counts_ref[0] = jnp.broadcast_to(cnt, (SUBLANES, LANES))


def _oproj_router(o_p, o_s, h2_p, h2_s, w):
    d = h2_p.shape[1]
    n = h2_p.shape[0] + h2_s.shape[0]
    tm = ROW_TILE
    row = lambda width: pl.BlockSpec((tm, width), lambda i: (i, 0))
    in_specs = (_two_part_specs(o_p.shape[0], o_s.shape[0], d) + _two_part_specs(h2_p.shape[0], h2_s.shape[0], d) + [
        _const_spec((d, d)), _const_spec((1, d)), _const_spec((1, d)),
        _const_spec((d, LANES)), _const_spec((d, LANES))])
    return pl.pallas_call(
        _oproj_router_kernel,
        grid=(n // tm,),
        in_specs=in_specs,
        out_specs=[row(d), row(d), row(LANES), pl.BlockSpec((1, SUBLANES, LANES), lambda i: (i, 0, 0))],
        out_shape=[jax.ShapeDtypeStruct((n, d), F32), jax.ShapeDtypeStruct((n, d), BF16),
                   jax.ShapeDtypeStruct((n, LANES), F32), jax.ShapeDtypeStruct((n // tm, SUBLANES, LANES), F32)],
        compiler_params=_params("arbitrary"),
        name="oproj_router",
    )(o_p, o_s, h2_p, h2_s, w["wo"], w["bo"], w["gf"], w["wr_hi"], w["wr_lo"])


LOCAL_ROWS = 1152
XS_WIDTH = D_MODEL + LANES
SEG_TABLE = 3 * N_EXPERTS
SEG_PIECES = tuple(SUBLANES << b for b in reversed(range(7)))


def _segment_copies(src_ref, src_start, dst_ref, dst_start, length, sem, act):
    for piece in SEG_PIECES:
        done = lax.div(length, 2 * piece) * (2 * piece)

        @pl.when(lax.rem(lax.div(length, piece), 2) != 0)
        def _():
            s = pl.multiple_of(src_start + done, SUBLANES)
            t = pl.multiple_of(dst_start + done, SUBLANES)
            act(pltpu.make_async_copy(src_ref.at[pl.ds(s, piece), :], dst_ref.at[pl.ds(t, piece), :], sem))


def _sort_place_kernel(tbl_ref, tail_ref, nu_ref, xn_ref, route_ref, xs_ref, stage, zeros, sems):
    j = pl.program_id(0)
    tm = xn_ref.shape[0]
    xn = xn_ref[...]
    rt = jnp.transpose(route_ref[...])
    row = lax.broadcasted_iota(I32, (LOCAL_ROWS, tm), 0).astype(F32)
    m1 = row == rt[0:1, :]
    m2 = row == rt[1:2, :]
    onehot = jnp.where(m1 | m2, 1.0, 0.0).astype(BF16)
    slot = lax.rem(j, 2)
    buf = stage.at[slot]
    buf[:, 0:D_MODEL] = _dot(onehot, xn)
    gate = jnp.sum(jnp.where(m1, rt[2:3, :], 0.0) + jnp.where(m2, rt[3:4, :], 0.0), axis=-1, keepdims=True)
    buf[:, D_MODEL:XS_WIDTH] = jnp.broadcast_to(gate, (LOCAL_ROWS, LANES))

    def segments(tile, tile_slot, act):
        for e in range(N_EXPERTS):
            base = tile * SEG_TABLE
            _segment_copies(stage.at[tile_slot], tbl_ref[base + e], xs_ref, tbl_ref[base + 2 * N_EXPERTS + e],
                            tbl_ref[base + N_EXPERTS + e], sems.at[tile_slot], act)

    segments(j, slot, lambda cp: cp.start())

    @pl.when(j > 0)
    def _():
        segments(j - 1, 1 - slot, lambda cp: cp.wait())

    @pl.when(j == pl.num_programs(0) - 1)
    def _():
        segments(j, slot, lambda cp: cp.wait())
        zeros[...] = jnp.zeros_like(zeros)
        sem = sems.at[0]

        def tails(act):
            for e in range(N_EXPERTS):
                _segment_copies(zeros, 0, xs_ref, tail_ref[e], tail_ref[N_EXPERTS + e], sem, act)

        def unused(act):
            def body(i, c):
                t = pl.multiple_of(i * MOE_TILE, MOE_TILE)
                act(pltpu.make_async_copy(zeros, xs_ref.at[pl.ds(t, MOE_TILE), :], sem))
                return c
            lax.fori_loop(nu_ref[0], xs_ref.shape[0] // MOE_TILE, body, 0)

        tails(lambda cp: cp.start())
        unused(lambda cp: cp.start())
        tails(lambda cp: cp.wait())
        unused(lambda cp: cp.wait())


def _sort_place(xn, route, tbl, tail, n_used, p_rows):
    n, d = xn.shape
    tm = ROW_TILE
    grid_spec = pltpu.PrefetchScalarGridSpec(
        num_scalar_prefetch=3,
        grid=(n // tm,),
        in_specs=[pl.BlockSpec((tm, d), lambda j, *_: (j, 0)),
                  pl.BlockSpec((tm, LANES), lambda j, *_: (j, 0))],
        out_specs=pl.BlockSpec(memory_space=pl.ANY),
        scratch_shapes=[pltpu.VMEM((2, LOCAL_ROWS, XS_WIDTH), F32), pltpu.VMEM((MOE_TILE, XS_WIDTH), F32),
                        pltpu.SemaphoreType.DMA((2,))],
    )
    return pl.pallas_call(
        _sort_place_kernel,
        grid_spec=grid_spec,
        out_shape=jax.ShapeDtypeStruct((p_rows, XS_WIDTH), F32),
        compiler_params=_params("arbitrary"),
        name="moe_sort_place",
    )(tbl, tail, n_used, xn, route)


def _moe_kernel(te_ref, nu_ref, xs_ref, wg_ref, wu_ref, wd_ref, y_ref):
    i = pl.program_id(0)
    c = pl.program_id(1)
    used = i < nu_ref[0]

    @pl.when(used)
    def _():
        xn = xs_ref[:, 0:D_MODEL].astype(BF16)
        g = _dot(xn, wg_ref[0])
        up = _dot(xn, wu_ref[0])
        mid = (g * _sigmoid(g) * up).astype(BF16)
        y = _dot(mid, wd_ref[0]) * xs_ref[:, D_MODEL:D_MODEL + 1]

        @pl.when(c == 0)
        def _():
            y_ref[...] = y

        @pl.when(c > 0)
        def _():
            y_ref[...] = y_ref[...] + y

    @pl.when(jnp.logical_not(used) & (c == 0))
    def _():
        y_ref[...] = jnp.zeros_like(y_ref)


def _moe_experts(xs, tile_expert, n_used, wg, wu, wd):
    p_rows = xs.shape[0]
    d = D_MODEL
    tm = MOE_TILE
    n_tiles = p_rows // tm
    ch = D_EXPERT // MOE_CHUNKS
    last = MOE_CHUNKS - 1

    def chunk(i, c, nu):
        return jnp.where(i < nu[0], c, last)

    grid_spec = pltpu.PrefetchScalarGridSpec(
        num_scalar_prefetch=2,
        grid=(n_tiles, MOE_CHUNKS),
        in_specs=[
            pl.BlockSpec((tm, XS_WIDTH), lambda i, c, te, nu: (i, 0)),
            pl.BlockSpec((1, d, ch), lambda i, c, te, nu: (te[i], 0, chunk(i, c, nu))),
            pl.BlockSpec((1, d, ch), lambda i, c, te, nu: (te[i], 0, chunk(i, c, nu))),
            pl.BlockSpec((1, ch, d), lambda i, c, te, nu: (te[i], chunk(i, c, nu), 0)),
        ],
        out_specs=pl.BlockSpec((tm, d), lambda i, c, te, nu: (i, 0)),
    )
    return pl.pallas_call(
        _moe_kernel,
        grid_spec=grid_spec,
        out_shape=jax.ShapeDtypeStruct((p_rows, d), F32),
        compiler_params=_params("arbitrary", "arbitrary"),
        name="moe_experts",
    )(tile_expert, n_used, xs, wg, wu, wd)


def _combine_kernel(tbl_ref, h3_ref, route_ref, ys_ref, g_ref, outp_ref, outs_ref, ybuf, sems):
    j = pl.program_id(0)
    last = pl.num_programs(0) - 1
    tm = h3_ref.shape[0]
    slot = lax.rem(j, 2)

    def segments(tile, tile_slot, act):
        for e in range(N_EXPERTS):
            base = tile * SEG_TABLE
            _segment_copies(ys_ref, tbl_ref[base + 2 * N_EXPERTS + e], ybuf.at[tile_slot], tbl_ref[base + e],
                            tbl_ref[base + N_EXPERTS + e], sems.at[tile_slot], act)

    @pl.when(j == 0)
    def _():
        ybuf[...] = jnp.zeros_like(ybuf)
        segments(j, slot, lambda cp: cp.start())

    @pl.when(j < last)
    def _():
        segments(j + 1, 1 - slot, lambda cp: cp.start())

    segments(j, slot, lambda cp: cp.wait())

    route = route_ref[...]
    col = lax.broadcasted_iota(I32, (tm, LOCAL_ROWS), 1).astype(F32)
    picks = jnp.where((col == route[:, 0:1]) | (col == route[:, 1:2]), 1.0, 0.0).astype(BF16)
    h4 = h3_ref[...] + _dot(picks, ybuf[slot].astype(BF16))
    out = _rms(h4, g_ref[...])

    @pl.when(j < last)
    def _():
        outp_ref[...] = out

    @pl.when(j == last)
    def _():
        outs_ref[...] = out


def _combine_final(h3, route, tbl, ys, g_final, n_prompt):
    n, d = h3.shape
    tm = ROW_TILE
    n_sample = n - n_prompt
    assert n_sample == tm and n_prompt % tm == 0
    last_prompt = n_prompt // tm - 1
    grid_spec = pltpu.PrefetchScalarGridSpec(
        num_scalar_prefetch=1,
        grid=(n // tm,),
        in_specs=[pl.BlockSpec((tm, d), lambda j, *_: (j, 0)),
                  pl.BlockSpec((tm, LANES), lambda j, *_: (j, 0)),
                  pl.BlockSpec(memory_space=pl.ANY),
                  pl.BlockSpec((1, d), lambda j, *_: (0, 0))],
        out_specs=[pl.BlockSpec((tm, d), lambda j, *_: (jnp.minimum(j, last_prompt), 0)),
                   pl.BlockSpec((tm, d), lambda j, *_: (0, 0))],
        scratch_shapes=[pltpu.VMEM((2, LOCAL_ROWS, d), F32), pltpu.SemaphoreType.DMA((2,))],
    )
    return pl.pallas_call(
        _combine_kernel,
        grid_spec=grid_spec,
        out_shape=[jax.ShapeDtypeStruct((n_prompt, d), F32), jax.ShapeDtypeStruct((n_sample, d), F32)],
        compiler_params=_params("arbitrary"),
        name="moe_combine",
    )(tbl, h3, route, ys, g_final)


def _row(v):
    return v.reshape(1, -1).astype(F32)


def _routing_tables(counts):
    n_row_tiles = counts.shape[0]
    cnt = counts[:, 0, :N_EXPERTS].astype(I32)
    seg = ((cnt + SUBLANES - 1) // SUBLANES) * SUBLANES
    local_start = jnp.cumsum(seg, axis=1) - seg
    rows = jnp.sum(seg, axis=0)
    padded = ((rows + MOE_TILE - 1) // MOE_TILE) * MOE_TILE
    ends = jnp.cumsum(padded)
    starts = ends - padded
    sorted_start = starts[None, :] + jnp.cumsum(seg, axis=0) - seg
    tbl = jnp.concatenate([local_start, seg, sorted_start], axis=1).reshape(-1).astype(I32)
    tail = jnp.concatenate([starts + rows, padded - rows]).astype(I32)
    max_rows = n_row_tiles * (TOP_K * ROW_TILE + N_EXPERTS * (SUBLANES - 1)) + N_EXPERTS * (MOE_TILE - SUBLANES)
    n_tiles = -(-max_rows // MOE_TILE)
    tile_start = jnp.arange(n_tiles, dtype=I32) * MOE_TILE
    tile_expert = jnp.minimum(jnp.sum((tile_start[:, None] >= ends[None, :]).astype(I32), axis=1), N_EXPERTS - 1)
    n_used = (ends[-1] // MOE_TILE).reshape(1).astype(I32)
    return tbl, tail, tile_expert.astype(I32), n_used, n_tiles


def kernel(x_prompt, x_sample, state_conv, state_h, cache_k, cache_v, g_mix, g_ffn, g_kv, g_final, a_w_gate, a_b_gate, a_w_in, a_b_in, a_conv_w, a_conv_b, a_w_r, a_b_r, a_w_i, a_b_i, a_lam, a_w_out, a_b_out, w_kv, b_kv, rel_bias, b_w_q, b_b_q, b_sinks, b_w_o, b_b_o, f_w_gate, f_w_up, f_w_down, m_w_router, m_w_gate, m_w_up, m_w_down):
    bp, seq, d = x_prompt.shape
    bs, steps, _ = x_sample.shape
    n_prompt = bp * seq
    n_sample = bs * steps
    n = n_prompt + n_sample
    assert seq % MIX_TILE == 0 and seq % WINDOW == 0 and n_prompt % ROW_TILE == 0
    assert n_sample == ROW_TILE and bs % SAMPLE_GROUP == 0

    mix_w = dict(g=_row(g_mix[0]), wg=a_w_gate[0].astype(BF16), bg=_row(a_b_gate[0]),
                 win=a_w_in[0].astype(BF16), bin=_row(a_b_in[0]), cw=a_conv_w[0], cb=_row(a_conv_b[0]),
                 wr=a_w_r[0].astype(BF16), br=_row(a_b_r[0]), wi=a_w_i[0].astype(BF16), bi=_row(a_b_i[0]),
                 lam=_row(a_lam[0]), wout=a_w_out[0].astype(BF16), bout=_row(a_b_out[0]))

    ffn_w = dict(gf=_row(g_ffn[0]), wg=f_w_gate[0].astype(BF16), wu=f_w_up[0].astype(BF16),
                 wd=f_w_down[0].astype(BF16), gkv=_row(g_kv), wkv=w_kv.astype(BF16), bkv=_row(b_kv),
                 gq=_row(g_mix[1]), wq=b_w_q[0].astype(BF16), bq=_row(b_b_q[0]))

    h2_p, kv_p, q_p, p_conv, p_h = _layer0_prompt(x_prompt, mix_w, ffn_w)
    h1_s, s_conv_tm, s_h = _mixer_sample(jnp.transpose(x_sample, (1, 0, 2)),
                                         jnp.transpose(state_conv[0], (1, 0, 2)), state_h[0], mix_w)
    h2_s, kv_s, q_s = _ffn_rows(jnp.transpose(h1_s, (1, 0, 2)).reshape(n_sample, d), ffn_w)

    bias, bias_rows = _bias_band(rel_bias, steps)
    sinks = b_sinks[0].astype(F32)
    o_p, moe_wg, moe_wu, moe_wd = _attn_prompt(q_p, kv_p, bias, sinks, bp, seq,
                                               [m_w_gate[0], m_w_up[0], m_w_down[0]])
    q4 = q_s.reshape(bs, steps, N_KV, GROUP, HEAD_DIM).transpose(0, 2, 1, 3, 4)
    q4 = q4.reshape(bs, N_KV, steps * GROUP, HEAD_DIM)
    kv_new = kv_s.reshape(bs, steps, 2 * KV_DIM)
    sink_rows = jnp.broadcast_to(sinks.reshape(N_KV, 1, GROUP), (N_KV, steps, GROUP)).reshape(N_KV, steps * GROUP, 1)
    o4, s_kt, s_vt = _attn_sample(q4, jnp.transpose(cache_k, (0, 2, 3, 1)), jnp.transpose(cache_v, (0, 2, 3, 1)),
                                  kv_new, jnp.transpose(kv_new, (0, 2, 1)), bias_rows, sink_rows)
    o_s = o4.reshape(bs, N_KV, steps, GROUP, HEAD_DIM).transpose(0, 2, 1, 3, 4).reshape(n_sample, d)

    wr_pad = jnp.zeros((d, LANES), F32).at[:, :N_EXPERTS].set(m_w_router[0])
    wr_hi = wr_pad.astype(BF16)
    wr_lo = (wr_pad - wr_hi.astype(F32)).astype(BF16)
    h3, xn, route, counts = _oproj_router(
        o_p, o_s, h2_p, h2_s,
        dict(wo=b_w_o[0].astype(BF16), bo=_row(b_b_o[0]), gf=_row(g_ffn[1]), wr_hi=wr_hi, wr_lo=wr_lo))

    tbl, tail, tile_expert, n_used, n_tiles = _routing_tables(counts)
    xs = _sort_place(xn, route, tbl, tail, n_used, n_tiles * MOE_TILE)
    ys = _moe_experts(xs, tile_expert, n_used, moe_wg, moe_wu, moe_wd)
    y_p, y_s = _combine_final(h3, route, tbl, ys, _row(g_final), n_prompt)

    y_prompt = y_p.reshape(bp, seq, d)
    y_sample = y_s.reshape(bs, steps, d)
    kv_last = jnp.stack([kv_p[b * seq + seq - WINDOW:(b + 1) * seq] for b in range(bp)])
    p_k = kv_last[:, :, :KV_DIM].reshape(bp, WINDOW, N_KV, HEAD_DIM)
    p_v = kv_last[:, :, KV_DIM:].reshape(bp, WINDOW, N_KV, HEAD_DIM)
    s_k = jnp.transpose(s_kt, (0, 3, 1, 2))
    s_v = jnp.transpose(s_vt, (0, 3, 1, 2))
    return (y_prompt, y_sample, p_conv[None], p_h.reshape(1, bp, LRU_WIDTH), p_k, p_v,
            jnp.transpose(s_conv_tm, (1, 0, 2))[None], s_h[None], s_k, s_v)
```

```python
import functools
import math

import jax
import jax.numpy as jnp
from jax import lax
from jax.experimental import pallas as pl
from jax.experimental.pallas import tpu as pltpu

D_MODEL = 1024
LRU_WIDTH = D_MODEL
LRU_BLOCK_W = 256
LRU_BLOCKS = LRU_WIDTH // LRU_BLOCK_W
CONV_W = 4
LRU_C = 8.0
HEAD_DIM = 64
N_HEADS = D_MODEL // HEAD_DIM
N_KV = 2
GROUP = N_HEADS // N_KV
KV_DIM = N_KV * HEAD_DIM
WINDOW = 128
NUM_BUCKETS = 32
MAX_DISTANCE = 128
D_FF = 3 * D_MODEL
N_EXPERTS = 8
TOP_K = 2
D_EXPERT = 7 * D_MODEL // 2
EPS = 1e-6
NEG = -1e30

BF16 = jnp.bfloat16
F32 = jnp.float32
I32 = jnp.int32

SUBLANES = 8
LANES = 128
VMEM_LIMIT_BYTES = 56 * 1024 * 1024

ROW_TILE = 512
MIX_TILE = 256
FF_CHUNK = 1024
MOE_TILE = 512
MOE_CHUNKS = 2


def _params(*semantics):
    return pltpu.CompilerParams(dimension_semantics=semantics, vmem_limit_bytes=VMEM_LIMIT_BYTES)


def _const_spec(shape):
    zeros = (0,) * len(shape)
    return pl.BlockSpec(shape, lambda *_: zeros, pipeline_mode=pl.Buffered(1))


def _dot(a, b):
    return jnp.dot(a, b, preferred_element_type=F32)


def _rms(x, g):
    ms = jnp.mean(x * x, axis=-1, keepdims=True)
    return x * lax.rsqrt(ms + EPS) * g


def _sigmoid(x):
    return 1.0 / (1.0 + jnp.exp(-x))


def _gelu_tanh(x):
    return 0.5 * x * (1.0 + jnp.tanh(0.7978845608028654 * (x + 0.044715 * (x * x * x))))


def _log_sigmoid(x):
    return jnp.minimum(x, 0.0) - jnp.log1p(jnp.exp(-jnp.abs(x)))


def _lru_gates(xc, wr_ref, br, wi_ref, bi, lam):
    xcb = xc.astype(BF16)
    rs, gs = [], []
    for n in range(LRU_BLOCKS):
        xn = xcb[:, n * LRU_BLOCK_W:(n + 1) * LRU_BLOCK_W]
        rs.append(_dot(xn, wr_ref[n]))
        gs.append(_dot(xn, wi_ref[n]))
    r = _sigmoid(jnp.concatenate(rs, axis=1) + br)
    i = _sigmoid(jnp.concatenate(gs, axis=1) + bi)
    log_a = LRU_C * r * _log_sigmoid(lam)
    a = jnp.exp(log_a)
    mult = jnp.sqrt(1.0 - a * a)
    return a, mult * (i * xc)


def _interleave(order, *stage_generators):
    results = [None] * len(stage_generators)
    finished = set()

    def advance(idx):
        if idx in finished:
            return
        try:
            next(stage_generators[idx])
        except StopIteration as done:
            results[idx] = done.value
            finished.add(idx)

    for idx in order:
        advance(idx)
    while len(finished) < len(stage_generators):
        for idx in range(len(stage_generators)):
            advance(idx)
    return results


def _mixer_stages(x, g_ref, wg_ref, bg_ref, win_ref, bin_ref, cw_ref, cb_ref,
                  wr_ref, br_ref, wi_ref, bi_ref, lam_ref, wout_ref, bout_ref, xr_buf, h_carry):
    tt = x.shape[0]
    pad = SUBLANES
    u = _rms(x, g_ref[...]).astype(BF16)
    gate = _gelu_tanh(_dot(u, wg_ref[...]) + bg_ref[...])
    xr = _dot(u, win_ref[...]) + bin_ref[...]
    yield
    xr_buf[pad:pad + tt, :] = xr
    xc = cb_ref[...] + cw_ref[CONV_W - 1:CONV_W, :] * xr
    for k in range(CONV_W - 1):
        back = CONV_W - 1 - k
        xc = xc + cw_ref[k:k + 1, :] * xr_buf[pad - back:pad - back + tt, :]
    xr_buf[0:pad, :] = xr[tt - pad:tt, :]
    yield

    a, b = _lru_gates(xc, wr_ref, br_ref[...], wi_ref, bi_ref[...], lam_ref[...])
    yield

    groups = tt // SUBLANES
    a3 = a.reshape(groups, SUBLANES, LRU_WIDTH)
    b3 = b.reshape(groups, SUBLANES, LRU_WIDTH)
    row = lax.broadcasted_iota(I32, (1, SUBLANES, LRU_WIDTH), 1)
    step = 1
    while step < SUBLANES:
        keep = row >= step
        a_prev = jnp.where(keep, pltpu.roll(a3, step, axis=1), 1.0)
        b_prev = jnp.where(keep, pltpu.roll(b3, step, axis=1), 0.0)
        b3 = b3 + a3 * b_prev
        a3 = a3 * a_prev
        step *= 2
    yield
    h_prev = h_carry[0:1, :]
    hs = []
    for gi in range(groups):
        hg = b3[gi] + a3[gi] * h_prev
        hs.append(hg)
        h_prev = hg[SUBLANES - 1:SUBLANES, :]
        if gi + 1 == groups // 2:
            yield
    h = jnp.concatenate(hs, axis=0)
    h_carry[0:1, :] = h_prev
    yield
    y = _dot((h * gate).astype(BF16), wout_ref[...]) + bout_ref[...]
    return x + y, xr[tt - (CONV_W - 1):tt, :], h_prev


LAYER0_ORDER = (0, 1, 0, 0, 1, 0, 0, 1, 0, 0)
N_MIX_REFS = 14
N_FFN_REFS = 10


def _ffn_stages(h1, gf_ref, wg_ref, wu_ref, wd_ref, gkv_ref, wkv_ref, bkv_ref, gq_ref, wq_ref, bq_ref):
    u = _rms(h1, gf_ref[...]).astype(BF16)
    acc = h1
    n_chunks = D_FF // FF_CHUNK
    for c in range(n_chunks):
        cols = slice(c * FF_CHUNK, (c + 1) * FF_CHUNK)
        g = _dot(u, wg_ref[:, cols])
        up = _dot(u, wu_ref[:, cols])
        mid = (g * _sigmoid(g) * up).astype(BF16)
        acc = acc + _dot(mid, wd_ref[cols, :])
        if c + 1 < n_chunks:
            yield
    kv = _dot(_rms(acc, gkv_ref[...]).astype(BF16), wkv_ref[...]) + bkv_ref[...]
    q = _dot(_rms(acc, gq_ref[...]).astype(BF16), wq_ref[...]) + bq_ref[...]
    return acc, kv, (q * (HEAD_DIM ** -0.5)).astype(BF16)


def _layer0_prompt_kernel(*refs, n_tiles, tiles_per_seq):
    x_ref = refs[0]
    mix_refs = refs[1:1 + N_MIX_REFS]
    ffn_refs = refs[1 + N_MIX_REFS:1 + N_MIX_REFS + N_FFN_REFS]
    h2_ref, kv_ref, q_ref, conv_ref, hlast_ref, xr_buf, h_carry, h1_buf = refs[1 + N_MIX_REFS + N_FFN_REFS:]
    s = pl.program_id(0)
    slot = lax.rem(s, 2)

    @pl.when(s == 0)
    def _():
        h1_buf[...] = jnp.zeros_like(h1_buf)

    @pl.when(lax.rem(s, tiles_per_seq) == 0)
    def _():
        xr_buf[0:SUBLANES, :] = jnp.zeros((SUBLANES, LRU_WIDTH), F32)
        h_carry[...] = jnp.zeros_like(h_carry)

    (h1, conv_tail, h_last), (h2, kv, q) = _interleave(
        LAYER0_ORDER,
        _mixer_stages(x_ref[...], *mix_refs, xr_buf, h_carry), _ffn_stages(h1_buf[1 - slot], *ffn_refs))
    h2_ref[...] = h2
    kv_ref[...] = kv
    q_ref[...] = q
    h1_buf[slot] = h1

    @pl.when(s < n_tiles)
    def _():
        b = s // tiles_per_seq
        conv_ref[b] = conv_tail
        hlast_ref[b] = h_last


def _layer0_prompt(x, mix_w, ffn_w):
    b, t, d = x.shape
    tt = MIX_TILE
    n_tiles = (b * t) // tt
    vec = lambda n: _const_spec((1, n))
    mix_specs = [
        vec(d), _const_spec((d, LRU_WIDTH)), vec(LRU_WIDTH), _const_spec((d, LRU_WIDTH)), vec(LRU_WIDTH),
        _const_spec((CONV_W, LRU_WIDTH)), vec(LRU_WIDTH),
        _const_spec((LRU_BLOCKS, LRU_BLOCK_W, LRU_BLOCK_W)), vec(LRU_WIDTH),
        _const_spec((LRU_BLOCKS, LRU_BLOCK_W, LRU_BLOCK_W)), vec(LRU_WIDTH),
        vec(LRU_WIDTH), _const_spec((LRU_WIDTH, d)), vec(d),
    ]
    ffn_specs = [
        vec(d), _const_spec((d, D_FF)), _const_spec((d, D_FF)), _const_spec((D_FF, d)),
        vec(d), _const_spec((d, 2 * KV_DIM)), vec(2 * KV_DIM), vec(d), _const_spec((d, d)), vec(d),
    ]
    assert len(mix_specs) == N_MIX_REFS and len(ffn_specs) == N_FFN_REFS
    prev = lambda s: (jnp.maximum(s - 1, 0), 0)
    whole = lambda shape: pl.BlockSpec(shape, lambda s: (0,) * len(shape))
    return pl.pallas_call(
        functools.partial(_layer0_prompt_kernel, n_tiles=n_tiles, tiles_per_seq=t // tt),
        grid=(n_tiles + 1,),
        in_specs=[pl.BlockSpec((tt, d), lambda s: (jnp.minimum(s, n_tiles - 1), 0))] + mix_specs + ffn_specs,
        out_specs=[pl.BlockSpec((tt, d), prev), pl.BlockSpec((tt, 2 * KV_DIM), prev), pl.BlockSpec((tt, d), prev),
                   whole((b, CONV_W - 1, LRU_WIDTH)), whole((b, 1, LRU_WIDTH))],
        out_shape=[jax.ShapeDtypeStruct((b * t, d), F32), jax.ShapeDtypeStruct((b * t, 2 * KV_DIM), F32),
                   jax.ShapeDtypeStruct((b * t, d), BF16),
                   jax.ShapeDtypeStruct((b, CONV_W - 1, LRU_WIDTH), F32),
                   jax.ShapeDtypeStruct((b, 1, LRU_WIDTH), F32)],
        scratch_shapes=[pltpu.VMEM((SUBLANES + tt, LRU_WIDTH), F32), pltpu.VMEM((SUBLANES, LRU_WIDTH), F32),
                        pltpu.VMEM((2, tt, d), F32)],
        compiler_params=_params("arbitrary"),
        name="layer0_prompt",
    )(x.reshape(b * t, d), *[mix_w[k] for k in MIX_KEYS], *[ffn_w[k] for k in FFN_KEYS])


MIX_KEYS = ("g", "wg", "bg", "win", "bin", "cw", "cb", "wr", "br", "wi", "bi", "lam", "wout", "bout")
FFN_KEYS = ("gf", "wg", "wu", "wd", "gkv", "wkv", "bkv", "gq", "wq", "bq")


def _mixer_sample_kernel(x_ref, cs_ref, h0_ref, g_ref, wg_ref, bg_ref, win_ref, bin_ref, cw_ref, cb_ref,
                         wr_ref, br_ref, wi_ref, bi_ref, lam_ref, wout_ref, bout_ref,
                         h1_ref, conv_ref, hlast_ref):
    steps, nb, d = x_ref.shape
    x = x_ref[...].reshape(steps * nb, d)
    u = _rms(x, g_ref[...]).astype(BF16)
    gate = _gelu_tanh(_dot(u, wg_ref[...]) + bg_ref[...])
    xr = _dot(u, win_ref[...]) + bin_ref[...]
    xpad = [cs_ref[k] for k in range(CONV_W - 1)] + [xr[s * nb:(s + 1) * nb, :] for s in range(steps)]
    xcs = []
    for s in range(steps):
        acc = cb_ref[...] + cw_ref[0:1, :] * xpad[s]
        for k in range(1, CONV_W):
            acc = acc + cw_ref[k:k + 1, :] * xpad[s + k]
        xcs.append(acc)
    for k in range(CONV_W - 1):
        conv_ref[k] = xpad[steps + k]
    xc = jnp.concatenate(xcs, axis=0)
    a, b = _lru_gates(xc, wr_ref, br_ref[...], wi_ref, bi_ref[...], lam_ref[...])
    h = h0_ref[...]
    hs = []
    for s in range(steps):
        h = a[s * nb:(s + 1) * nb, :] * h + b[s * nb:(s + 1) * nb, :]
        hs.append(h)
    hlast_ref[...] = h
    hcat = jnp.concatenate(hs, axis=0)
    y = _dot((hcat * gate).astype(BF16), wout_ref[...]) + bout_ref[...]
    h1_ref[...] = (x + y).reshape(steps, nb, d)


def _mixer_sample(x_tm, cs_tm, h0, w):
    steps, nb, d = x_tm.shape
    full = lambda shape: pl.BlockSpec(shape, lambda i: (0,) * len(shape))
    vec = lambda n: full((1, n))
    in_specs = [
        full((steps, nb, d)), full((CONV_W - 1, nb, LRU_WIDTH)), full((nb, LRU_WIDTH)),
        vec(d), full((d, LRU_WIDTH)), vec(LRU_WIDTH), full((d, LRU_WIDTH)), vec(LRU_WIDTH),
        full((CONV_W, LRU_WIDTH)), vec(LRU_WIDTH),
        full((LRU_BLOCKS, LRU_BLOCK_W, LRU_BLOCK_W)), vec(LRU_WIDTH),
        full((LRU_BLOCKS, LRU_BLOCK_W, LRU_BLOCK_W)), vec(LRU_WIDTH),
        vec(LRU_WIDTH), full((LRU_WIDTH, d)), vec(d),
    ]
    out_specs = [full((steps, nb, d)), full((CONV_W - 1, nb, LRU_WIDTH)), full((nb, LRU_WIDTH))]
    out_shape = [
        jax.ShapeDtypeStruct((steps, nb, d), F32),
        jax.ShapeDtypeStruct((CONV_W - 1, nb, LRU_WIDTH), F32),
        jax.ShapeDtypeStruct((nb, LRU_WIDTH), F32),
    ]
    return pl.pallas_call(
        _mixer_sample_kernel, grid=(1,), in_specs=in_specs, out_specs=out_specs, out_shape=out_shape,
        compiler_params=_params("arbitrary"), name="mixer_sample",
    )(x_tm, cs_tm, h0, w["g"], w["wg"], w["bg"], w["win"], w["bin"], w["cw"], w["cb"], w["wr"], w["br"],
      w["wi"], w["bi"], w["lam"], w["wout"], w["bout"])


def _two_part_specs(n_prompt, n_sample, width):
    assert n_sample == ROW_TILE and n_prompt % ROW_TILE == 0
    last_prompt = n_prompt // ROW_TILE - 1
    return [pl.BlockSpec((ROW_TILE, width), lambda i: (jnp.minimum(i, last_prompt), 0)),
            pl.BlockSpec((ROW_TILE, width), lambda i: (0, 0))]


def _two_part_tile(prompt_ref, sample_ref):
    is_sample = pl.program_id(0) == pl.num_programs(0) - 1
    return jnp.where(is_sample, sample_ref[...], prompt_ref[...])


def _side_cast_specs(w, n_steps, step_of):
    ne, rows, cols = w.shape
    per_expert = n_steps // ne
    blk = rows // per_expert
    assert per_expert * ne == n_steps and blk * per_expert == rows and blk % (2 * SUBLANES) == 0
    index = lambda *ids: (step_of(*ids) // per_expert, step_of(*ids) % per_expert, 0)
    spec = pl.BlockSpec((1, blk, cols), index)
    return spec, spec, jax.ShapeDtypeStruct(w.shape, BF16)


def _ffn_rows_kernel(h1_ref, *refs):
    h2_ref, kv_ref, q_ref = refs[N_FFN_REFS:]
    (h2_ref[...], kv_ref[...], q_ref[...]), = _interleave((), _ffn_stages(h1_ref[...], *refs[:N_FFN_REFS]))


def _ffn_rows(h1, w):
    n, d = h1.shape
    tm = ROW_TILE
    row = lambda width: pl.BlockSpec((tm, width), lambda i: (i, 0))
    vec = lambda width: _const_spec((1, width))
    in_specs = [
        row(d), vec(d), _const_spec((d, D_FF)), _const_spec((d, D_FF)), _const_spec((D_FF, d)),
        vec(d), _const_spec((d, 2 * KV_DIM)), vec(2 * KV_DIM), vec(d), _const_spec((d, d)), vec(d),
    ]
    return pl.pallas_call(
        _ffn_rows_kernel,
        grid=(n // tm,),
        in_specs=in_specs,
        out_specs=[row(d), row(2 * KV_DIM), row(d)],
        out_shape=[jax.ShapeDtypeStruct((n, d), F32), jax.ShapeDtypeStruct((n, 2 * KV_DIM), F32),
                   jax.ShapeDtypeStruct((n, d), BF16)],
        compiler_params=_params("arbitrary"),
        name="ffn_rows",
    )(h1, *[w[k] for k in FFN_KEYS])


def _head_of(j, parity, pair):
    return j * GROUP + 2 * pair + parity


def _distance_bias(dist, table_ref, head):
    max_exact = NUM_BUCKETS // 2
    n = jnp.maximum(dist, 0)
    large = jnp.full(dist.shape, max_exact, I32)
    for step in range(1, NUM_BUCKETS - max_exact):
        threshold = math.ceil(max_exact * (MAX_DISTANCE / max_exact) ** (step / (NUM_BUCKETS - max_exact)))
        large = large + jnp.where(n >= threshold, 1, 0)
    bucket = jnp.where(n < max_exact, n, large)
    acc = jnp.zeros(dist.shape, F32)
    for bkt in range(NUM_BUCKETS):
        acc = jnp.where(bucket == bkt, table_ref[bkt, head], acc)
    return acc


def _bias_band_kernel(table_ref, folded_ref, rows_ref):
    w = WINDOW
    ci = lax.broadcasted_iota(I32, (w, w), 0)
    qi = lax.broadcasted_iota(I32, (w, w), 1)
    folded_dist = jnp.where(ci > qi, qi + w - ci, qi - ci)
    for j in range(N_KV):
        for parity in range(2):
            for pair in range(GROUP // 2):
                folded_ref[j, parity, :, pair * w:(pair + 1) * w] = _distance_bias(
                    folded_dist, table_ref, _head_of(j, parity, pair))
    n_rows, n_keys = rows_ref.shape[1], rows_ref.shape[2]
    ri = lax.broadcasted_iota(I32, (GROUP, n_keys), 0)
    si = lax.broadcasted_iota(I32, (GROUP, n_keys), 1)
    for j in range(N_KV):
        for t in range(n_rows // GROUP):
            acc = jnp.zeros((GROUP, n_keys), F32)
            for g in range(GROUP):
                acc = jnp.where(ri == g, _distance_bias(t + w - si, table_ref, j * GROUP + g), acc)
            rows_ref[j, t * GROUP:(t + 1) * GROUP, :] = acc


def _bias_band(rel_table, steps):
    folded = (N_KV, 2, WINDOW, (GROUP // 2) * WINDOW)
    rows = (N_KV, steps * GROUP, WINDOW + steps + (-steps) % SUBLANES)
    return pl.pallas_call(
        _bias_band_kernel,
        grid=(1,),
        in_specs=[pl.BlockSpec(memory_space=pltpu.SMEM)],
        out_specs=[pl.BlockSpec(folded, lambda i: (0, 0, 0, 0)), pl.BlockSpec(rows, lambda i: (0, 0, 0))],
        out_shape=[jax.ShapeDtypeStruct(folded, F32), jax.ShapeDtypeStruct(rows, F32)],
        compiler_params=_params("arbitrary"),
        name="bias_band",
    )(rel_table)


def _softmax_sink_pv(s, sink, v):
    m = jnp.maximum(jnp.max(s, axis=-1, keepdims=True), sink)
    p = jnp.exp(s - m)
    denom = jnp.sum(p, axis=-1, keepdims=True) + jnp.exp(sink - m)
    return _dot(p.astype(BF16), v) / denom


ATTN_BLOCKS = 4


def _attn_prompt_kernel(sink_ref, q_ref, kvp_ref, kvc_ref, bias_ref, *refs):
    n_casts = (len(refs) - 1) // 2
    o_ref = refs[n_casts]
    for src_ref, dst_ref in zip(refs[:n_casts], refs[n_casts + 1:]):
        dst_ref[...] = src_ref[...].astype(BF16)
    w = WINDOW
    pairs = GROUP // 2
    assert 2 * HEAD_DIM == LANES and KV_DIM == LANES
    ci = lax.broadcasted_iota(I32, (w, pairs * w), 0)
    qi = lax.broadcasted_iota(I32, (w, pairs * w), 1) % w
    from_prev = ci > qi
    has_prev = pl.program_id(1) > 0

    kv = jnp.concatenate([kvp_ref[...], kvc_ref[...]], axis=0)
    low = lax.broadcasted_iota(I32, (kv.shape[0], LANES), 1) < HEAD_DIM

    def halves(x):
        swapped = pltpu.roll(x, HEAD_DIM, axis=1)
        return (((jnp.where(low, x, 0.0)).astype(BF16), (jnp.where(low, 0.0, swapped)).astype(BF16)),
                ((jnp.where(low, swapped, 0.0)).astype(BF16), (jnp.where(low, 0.0, x)).astype(BF16)))

    k_ops = halves(kv[:, 0:KV_DIM])
    v_ops = halves(kv[:, KV_DIM:2 * KV_DIM])

    sinks = [[jnp.concatenate([jnp.full((1, w), sink_ref[_head_of(j, parity, b)], F32) for b in range(pairs)], axis=1)
              for parity in range(2)] for j in range(N_KV)]
    nt = (((1,), (1,)), ((), ()))
    tn = (((0,), (0,)), ((), ()))
    for i in range(ATTN_BLOCKS):
        q_rows = slice(i * w, (i + 1) * w)
        prev_rows = slice(i * w, (i + 1) * w)
        own_rows = slice((i + 1) * w, (i + 2) * w)
        for j in range(N_KV):
            q4 = jnp.concatenate([q_ref[q_rows, (j * pairs + b) * LANES:(j * pairs + b + 1) * LANES]
                                  for b in range(pairs)], axis=0)
            acc = None
            for parity in range(2):
                k_op, v_op, sink = k_ops[j][parity], v_ops[j][parity], sinks[j][parity]
                s_prev = lax.dot_general(k_op[prev_rows], q4, nt, preferred_element_type=F32)
                s_own = lax.dot_general(k_op[own_rows], q4, nt, preferred_element_type=F32)
                if i == 0:
                    s_prev = jnp.where(has_prev, s_prev, NEG)
                s = jnp.where(from_prev, s_prev, s_own) + bias_ref[j, parity]
                m = jnp.maximum(jnp.max(s, axis=0, keepdims=True), sink)
                p = jnp.exp(s - m)
                denom = jnp.sum(p, axis=0, keepdims=True) + jnp.exp(sink - m)
                p = p * (1.0 / denom)
                o = (lax.dot_general(jnp.where(from_prev, p, 0.0).astype(BF16), v_op[prev_rows], tn,
                                     preferred_element_type=F32)
                     + lax.dot_general(jnp.where(from_prev, 0.0, p).astype(BF16), v_op[own_rows], tn,
                                       preferred_element_type=F32))
                acc = o if acc is None else acc + o
            for b in range(pairs):
                o_ref[q_rows, (j * pairs + b) * LANES:(j * pairs + b + 1) * LANES] = acc[b * w:(b + 1) * w].astype(BF16)


def _attn_prompt(q, kv, bias, sinks, batch, seq, cast_ws):
    nb = seq // WINDOW
    assert nb % ATTN_BLOCKS == 0
    ns = nb // ATTN_BLOCKS
    rows = ATTN_BLOCKS * WINDOW
    d = q.shape[1]
    casts = [_side_cast_specs(w, batch * ns, lambda b, n: b * ns + n) for w in cast_ws]
    in_specs = [
        pl.BlockSpec(memory_space=pltpu.SMEM),
        pl.BlockSpec((rows, d), lambda b, n: (b * ns + n, 0)),
        pl.BlockSpec((WINDOW, 2 * KV_DIM), lambda b, n: (b * nb + jnp.maximum(n * ATTN_BLOCKS - 1, 0), 0)),
        pl.BlockSpec((rows, 2 * KV_DIM), lambda b, n: (b * ns + n, 0)),
        _const_spec(bias.shape),
    ] + [c[0] for c in casts]
    return pl.pallas_call(
        _attn_prompt_kernel,
        grid=(batch, ns),
        in_specs=in_specs,
        out_specs=[pl.BlockSpec((rows, d), lambda b, n: (b * ns + n, 0))] + [c[1] for c in casts],
        out_shape=[jax.ShapeDtypeStruct((batch * seq, d), BF16)] + [c[2] for c in casts],
        compiler_params=_params("arbitrary", "arbitrary"),
        name="attn_prompt",
    )(sinks, q, kv, kv, bias, *cast_ws)


SAMPLE_GROUP = 8


def _attn_sample_kernel(q_ref, ckt_ref, cvt_ref, new_ref, newt_ref, bias_ref, sink_ref, o_ref, skt_ref, svt_ref):
    gb = q_ref.shape[0]
    steps = new_ref.shape[1]
    rows = steps * GROUP
    pad = (-steps) % SUBLANES
    ri = lax.broadcasted_iota(I32, (rows, WINDOW + steps + pad), 0)
    si = lax.broadcasted_iota(I32, (rows, WINDOW + steps + pad), 1)
    dist = ri // GROUP + WINDOW - si
    mask = (dist >= 0) & (dist < WINDOW)
    mask_c, mask_n = mask[:, :WINDOW], mask[:, WINDOW:]
    zeros = jnp.zeros((pad, 2 * KV_DIM), F32)

    def scores(b, j):
        new = jnp.concatenate([new_ref[b], zeros], axis=0).astype(BF16)
        q = q_ref[b, j]
        s_c = _dot(q, ckt_ref[b, j].astype(BF16))
        s_n = lax.dot_general(q, new[:, j * HEAD_DIM:(j + 1) * HEAD_DIM], (((1,), (1,)), ((), ())),
                              preferred_element_type=F32)
        bias = bias_ref[j]
        return (jnp.where(mask_c, s_c + bias[:, :WINDOW], NEG), jnp.where(mask_n, s_n + bias[:, WINDOW:], NEG), new)

    tasks = [(b, j) for b in range(gb) for j in range(N_KV)]
    all_scores = [scores(b, j) for b, j in tasks]
    for (b, j), (s_c, s_n, new) in zip(tasks, all_scores):
        sink = sink_ref[j]
        m = jnp.maximum(jnp.maximum(jnp.max(s_c, axis=-1, keepdims=True), jnp.max(s_n, axis=-1, keepdims=True)), sink)
        p_c = jnp.exp(s_c - m)
        p_n = jnp.exp(s_n - m)
        denom = jnp.sum(p_c, axis=-1, keepdims=True) + jnp.sum(p_n, axis=-1, keepdims=True) + jnp.exp(sink - m)
        pv = lax.dot_general(p_c.astype(BF16), cvt_ref[b, j].astype(BF16), (((1,), (1,)), ((), ())),
                             preferred_element_type=F32)
        pv = pv + _dot(p_n.astype(BF16), new[:, KV_DIM + j * HEAD_DIM:KV_DIM + (j + 1) * HEAD_DIM])
        o_ref[b, j] = (pv / denom).astype(BF16)

    lane = lax.broadcasted_iota(I32, (HEAD_DIM, WINDOW), 1)
    for b in range(gb):
        newt = jnp.concatenate([newt_ref[b], jnp.zeros((2 * KV_DIM, WINDOW - steps), F32)], axis=1)
        for j in range(N_KV):
            for src_ref, dst_ref, base in ((ckt_ref, skt_ref, 0), (cvt_ref, svt_ref, KV_DIM)):
                fresh = newt[base + j * HEAD_DIM:base + (j + 1) * HEAD_DIM, :]
                dst_ref[b, j] = pltpu.roll(jnp.where(lane < steps, fresh, src_ref[b, j]), WINDOW - steps, axis=1)


def _attn_sample(q4, ckt, cvt, new, newt, bias_rows, sink_rows):
    nb, _, rows, hd = q4.shape
    steps = new.shape[1]
    keys = bias_rows.shape[2]
    gb = SAMPLE_GROUP
    cache_spec = pl.BlockSpec((gb, N_KV, hd, WINDOW), lambda i: (i, 0, 0, 0))
    in_specs = [
        pl.BlockSpec((gb, N_KV, rows, hd), lambda i: (i, 0, 0, 0)),
        cache_spec, cache_spec,
        pl.BlockSpec((gb, steps, 2 * KV_DIM), lambda i: (i, 0, 0)),
        pl.BlockSpec((gb, 2 * KV_DIM, steps), lambda i: (i, 0, 0)),
        pl.BlockSpec((N_KV, rows, keys), lambda i: (0, 0, 0)),
        pl.BlockSpec((N_KV, rows, 1), lambda i: (0, 0, 0)),
    ]
    return pl.pallas_call(
        _attn_sample_kernel,
        grid=(nb // gb,),
        in_specs=in_specs,
        out_specs=[pl.BlockSpec((gb, N_KV, rows, hd), lambda i: (i, 0, 0, 0)), cache_spec, cache_spec],
        out_shape=[jax.ShapeDtypeStruct(q4.shape, BF16), jax.ShapeDtypeStruct(ckt.shape, F32),
                   jax.ShapeDtypeStruct(cvt.shape, F32)],
        compiler_params=_params("arbitrary"),
        name="attn_sample",
    )(q4, ckt, cvt, new, newt, bias_rows, sink_rows)


def _split_bf16(x):
    hi = x.astype(BF16)
    lo = (x - hi.astype(F32)).astype(BF16)
    return hi, lo


ROUTER_PARTS = 2


def _oproj_router_kernel(op_ref, os_ref, h2p_ref, h2s_ref, wo_ref, bo_ref, gf_ref, wrh_ref, wrl_ref,
                         h3_ref, xn_ref, route_ref, counts_ref):
    tm = h3_ref.shape[0]
    parts = ROUTER_PARTS
    pr = tm // parts
    o = _two_part_tile(op_ref, os_ref)
    h2 = _two_part_tile(h2p_ref, h2s_ref)
    lane = lax.broadcasted_iota(I32, (pr, LANES), 1)
    lanef = lane.astype(F32)
    ri = lax.broadcasted_iota(I32, (pr, pr), 0)
    ci = lax.broadcasted_iota(I32, (pr, pr), 1)
    earlier = jnp.where(ri > ci, 1.0, 0.0).astype(BF16)

    picks, ranks, gates = [], [], []
    cnt = jnp.zeros((1, LANES), F32)
    for p in range(parts):
        rows = slice(p * pr, (p + 1) * pr)
        h3 = h2[rows, :] + _dot(o[rows, :], wo_ref[...]) + bo_ref[...]
        h3_ref[rows, :] = h3
        u_hi, u_lo = _split_bf16(_rms(h3, gf_ref[...]))
        xn_ref[rows, :] = u_hi
        logits =_dot(u_hi, wrh_ref[...]) + (_dot(u_lo, wrh_ref[...]) + _dot(u_hi, wrl_ref[...]))
        logits = jnp.where(lane < N_EXPERTS, logits, -jnp.inf)
        v1 = jnp.max(logits, axis=-1, keepdims=True)
        e1 = jnp.min(jnp.where(logits == v1, lanef, float(LANES)), axis=-1, keepdims=True)
        rest = jnp.where(lanef == e1, -jnp.inf, logits)
        v2 = jnp.max(rest, axis=-1, keepdims=True)
        e2 = jnp.min(jnp.where(rest == v2, lanef, float(LANES)), axis=-1, keepdims=True)
        ex = jnp.exp(v2 - v1)
        gates.append((1.0 / (1.0 + ex), ex / (1.0 + ex)))
        pick1 = lanef == e1
        pick2 = lanef == e2
        picks.append((pick1, pick2))
        sel = jnp.where(pick1 | pick2, 1.0, 0.0)
        ranks.append(_dot(earlier, sel.astype(BF16)) + cnt)
        cnt = cnt + jnp.sum(sel, axis=0, keepdims=True)

    seg = jnp.floor((cnt + (SUBLANES - 1)) * (1.0 / SUBLANES)) * SUBLANES
    ek = lax.broadcasted_iota(I32, (LANES, LANES), 0)
    el = lax.broadcasted_iota(I32, (LANES, LANES), 1)
    lower_experts = jnp.where(ek < el, 1.0, 0.0).astype(BF16)
    seg_start = _dot(jnp.broadcast_to(seg, (SUBLANES, LANES)).astype(BF16), lower_experts)[0:1, :]
    for p in range(parts):
        local = ranks[p] + seg_start
        lr1 = jnp.sum(jnp.where(picks[p][0], local, 0.0), axis=-1, keepdims=True)
        lr2 = jnp.sum(jnp.where(picks[p][1], local, 0.0), axis=-1, keepdims=True)
        w1, w2 = gates[p]
        route_ref[p * pr:(p + 1) * pr, :] = jnp.where(
            lane == 0, lr1, jnp.where(lane == 1, lr2, jnp.where(lane == 2, w1, jnp.where(lane == 3, w2, 0.0))))
    counts_ref[0] = jnp.broadcast_to(cnt, (SUBLANES, LANES))


def _oproj_router(o_p, o_s, h2_p, h2_s, w):
    d = h2_p.shape[1]
    n = h2_p.shape[0] + h2_s.shape[0]
    tm = ROW_TILE
    row = lambda width: pl.BlockSpec((tm, width), lambda i: (i, 0))
    in_specs = (_two_part_specs(o_p.shape[0], o_s.shape[0], d) + _two_part_specs(h2_p.shape[0], h2_s.shape[0], d) + [
        _const_spec((d, d)), _const_spec((1, d)), _const_spec((1, d)),
        _const_spec((d, LANES)), _const_spec((d, LANES))])
    return pl.pallas_call(
        _oproj_router_kernel,
        grid=(n // tm,),
        in_specs=in_specs,
        out_specs=[row(d), row(d), row(LANES), pl.BlockSpec((1, SUBLANES, LANES), lambda i: (i, 0, 0))],
        out_shape=[jax.ShapeDtypeStruct((n, d), F32), jax.ShapeDtypeStruct((n, d), BF16),
                   jax.ShapeDtypeStruct((n, LANES), F32), jax.ShapeDtypeStruct((n // tm, SUBLANES, LANES), F32)],
        compiler_params=_params("arbitrary"),
        name="oproj_router",
    )(o_p, o_s, h2_p, h2_s, w["wo"], w["bo"], w["gf"], w["wr_hi"], w["wr_lo"])


LOCAL_ROWS = 1152
XS_WIDTH = D_MODEL + LANES
SEG_TABLE = 3 * N_EXPERTS
SEG_PIECES = tuple(SUBLANES << b for b in reversed(range(7)))


def _segment_copies(src_ref, src_start, dst_ref, dst_start, length, sem, act):
    for piece in SEG_PIECES:
        done = lax.div(length, 2 * piece) * (2 * piece)

        @pl.when(lax.rem(lax.div(length, piece), 2) != 0)
        def _():
            s = pl.multiple_of(src_start + done, SUBLANES)
            t = pl.multiple_of(dst_start + done, SUBLANES)
            act(pltpu.make_async_copy(src_ref.at[pl.ds(s, piece), :], dst_ref.at[pl.ds(t, piece), :], sem))


def _sort_place_kernel(tbl_ref, tail_ref, nu_ref, xn_ref, route_ref, xs_ref, stage, zeros, sems):
    j = pl.program_id(0)
    tm = xn_ref.shape[0]
    xn = xn_ref[...]
    rt = jnp.transpose(route_ref[...])
    row = lax.broadcasted_iota(I32, (LOCAL_ROWS, tm), 0).astype(F32)
    m1 = row == rt[0:1, :]
    m2 = row == rt[1:2, :]
    onehot = jnp.where(m1 | m2, 1.0, 0.0).astype(BF16)
    slot = lax.rem(j, 2)
    buf = stage.at[slot]
    buf[:, 0:D_MODEL] = _dot(onehot, xn)
    gate = jnp.sum(jnp.where(m1, rt[2:3, :], 0.0) + jnp.where(m2, rt[3:4, :], 0.0), axis=-1, keepdims=True)
    buf[:, D_MODEL:XS_WIDTH] = jnp.broadcast_to(gate, (LOCAL_ROWS, LANES))

    def segments(tile, tile_slot, act):
        for e in range(N_EXPERTS):
            base = tile * SEG_TABLE
            _segment_copies(stage.at[tile_slot], tbl_ref[base + e], xs_ref, tbl_ref[base + 2 * N_EXPERTS + e],
                            tbl_ref[base + N_EXPERTS + e], sems.at[tile_slot], act)

    segments(j, slot, lambda cp: cp.start())

    @pl.when(j > 0)
    def _():
        segments(j - 1, 1 - slot, lambda cp: cp.wait())

    @pl.when(j == pl.num_programs(0) - 1)
    def _():
        segments(j, slot, lambda cp: cp.wait())
        zeros[...] = jnp.zeros_like(zeros)
        sem = sems.at[0]

        def tails(act):
            for e in range(N_EXPERTS):
                _segment_copies(zeros, 0, xs_ref, tail_ref[e], tail_ref[N_EXPERTS + e], sem, act)

        def unused(act):
            def body(i, c):
                t = pl.multiple_of(i * MOE_TILE, MOE_TILE)
                act(pltpu.make_async_copy(zeros, xs_ref.at[pl.ds(t, MOE_TILE), :], sem))
                return c
            lax.fori_loop(nu_ref[0], xs_ref.shape[0] // MOE_TILE, body, 0)

        tails(lambda cp: cp.start())
        unused(lambda cp: cp.start())
        tails(lambda cp: cp.wait())
        unused(lambda cp: cp.wait())


def _sort_place(xn, route, tbl, tail, n_used, p_rows):
    n, d = xn.shape
    tm = ROW_TILE
    grid_spec = pltpu.PrefetchScalarGridSpec(
        num_scalar_prefetch=3,
        grid=(n // tm,),
        in_specs=[pl.BlockSpec((tm, d), lambda j, *_: (j, 0)),
                  pl.BlockSpec((tm, LANES), lambda j, *_: (j, 0))],
        out_specs=pl.BlockSpec(memory_space=pl.ANY),
        scratch_shapes=[pltpu.VMEM((2, LOCAL_ROWS, XS_WIDTH), F32), pltpu.VMEM((MOE_TILE, XS_WIDTH), F32),
                        pltpu.SemaphoreType.DMA((2,))],
    )
    return pl.pallas_call(
        _sort_place_kernel,
        grid_spec=grid_spec,
        out_shape=jax.ShapeDtypeStruct((p_rows, XS_WIDTH), F32),
        compiler_params=_params("arbitrary"),
        name="moe_sort_place",
    )(tbl, tail, n_used, xn, route)


def _moe_kernel(te_ref, nu_ref, xs_ref, wg_ref, wu_ref, wd_ref, y_ref):
    i = pl.program_id(0)
    c = pl.program_id(1)
    used = i < nu_ref[0]

    @pl.when(used)
    def _():
        xn = xs_ref[:, 0:D_MODEL].astype(BF16)
        g = _dot(xn, wg_ref[0])
        up = _dot(xn, wu_ref[0])
        mid = (g * _sigmoid(g) * up).astype(BF16)
        y = _dot(mid, wd_ref[0]) * xs_ref[:, D_MODEL:D_MODEL + 1]

        @pl.when(c == 0)
        def _():
            y_ref[...] = y

        @pl.when(c > 0)
        def _():
            y_ref[...] = y_ref[...] + y

    @pl.when(jnp.logical_not(used) & (c == 0))
    def _():
        y_ref[...] = jnp.zeros_like(y_ref)


def _moe_experts(xs, tile_expert, n_used, wg, wu, wd):
    p_rows = xs.shape[0]
    d = D_MODEL
    tm = MOE_TILE
    n_tiles = p_rows // tm
    ch = D_EXPERT // MOE_CHUNKS
    last = MOE_CHUNKS - 1

    def chunk(i, c, nu):
        return jnp.where(i < nu[0], c, last)

    grid_spec = pltpu.PrefetchScalarGridSpec(
        num_scalar_prefetch=2,
        grid=(n_tiles, MOE_CHUNKS),
        in_specs=[
            pl.BlockSpec((tm, XS_WIDTH), lambda i, c, te, nu: (i, 0)),
            pl.BlockSpec((1, d, ch), lambda i, c, te, nu: (te[i], 0, chunk(i, c, nu))),
            pl.BlockSpec((1, d, ch), lambda i, c, te, nu: (te[i], 0, chunk(i, c, nu))),
            pl.BlockSpec((1, ch, d), lambda i, c, te, nu: (te[i], chunk(i, c, nu), 0)),
        ],
        out_specs=pl.BlockSpec((tm, d), lambda i, c, te, nu: (i, 0)),
    )
    return pl.pallas_call(
        _moe_kernel,
        grid_spec=grid_spec,
        out_shape=jax.ShapeDtypeStruct((p_rows, d), F32),
        compiler_params=_params("arbitrary", "arbitrary"),
        name="moe_experts",
    )(tile_expert, n_used, xs, wg, wu, wd)


def _combine_kernel(tbl_ref, h3_ref, route_ref, ys_ref, g_ref, outp_ref, outs_ref, ybuf, sems):
    j = pl.program_id(0)
    last = pl.num_programs(0) - 1
    tm = h3_ref.shape[0]
    slot = lax.rem(j, 2)

    def segments(tile, tile_slot, act):
        for e in range(N_EXPERTS):
            base = tile * SEG_TABLE
            _segment_copies(ys_ref, tbl_ref[base + 2 * N_EXPERTS + e], ybuf.at[tile_slot], tbl_ref[base + e],
                            tbl_ref[base + N_EXPERTS + e], sems.at[tile_slot], act)

    @pl.when(j == 0)
    def _():
        ybuf[...] = jnp.zeros_like(ybuf)
        segments(j, slot, lambda cp: cp.start())

    @pl.when(j < last)
    def _():
        segments(j + 1, 1 - slot, lambda cp: cp.start())

    segments(j, slot, lambda cp: cp.wait())

    route = route_ref[...]
    col = lax.broadcasted_iota(I32, (tm, LOCAL_ROWS), 1).astype(F32)
    picks = jnp.where((col == route[:, 0:1]) | (col == route[:, 1:2]), 1.0, 0.0).astype(BF16)
    h4 = h3_ref[...] + _dot(picks, ybuf[slot].astype(BF16))
    out = _rms(h4, g_ref[...])

    @pl.when(j < last)
    def _():
        outp_ref[...] = out

    @pl.when(j == last)
    def _():
        outs_ref[...] = out


def _combine_final(h3, route, tbl, ys, g_final, n_prompt):
    n, d = h3.shape
    tm = ROW_TILE
    n_sample = n - n_prompt
    assert n_sample == tm and n_prompt % tm == 0
    last_prompt = n_prompt // tm - 1
    grid_spec = pltpu.PrefetchScalarGridSpec(
        num_scalar_prefetch=1,
        grid=(n // tm,),
        in_specs=[pl.BlockSpec((tm, d), lambda j, *_: (j, 0)),
                  pl.BlockSpec((tm, LANES), lambda j, *_: (j, 0)),
                  pl.BlockSpec(memory_space=pl.ANY),
                  pl.BlockSpec((1, d), lambda j, *_: (0, 0))],
        out_specs=[pl.BlockSpec((tm, d), lambda j, *_: (jnp.minimum(j, last_prompt), 0)),
                   pl.BlockSpec((tm, d), lambda j, *_: (0, 0))],
        scratch_shapes=[pltpu.VMEM((2, LOCAL_ROWS, d), F32), pltpu.SemaphoreType.DMA((2,))],
    )
    return pl.pallas_call(
        _combine_kernel,
        grid_spec=grid_spec,
        out_shape=[jax.ShapeDtypeStruct((n_prompt, d), F32), jax.ShapeDtypeStruct((n_sample, d), F32)],
        compiler_params=_params("arbitrary"),
        name="moe_combine",
    )(tbl, h3, route, ys, g_final)


def _row(v):
    return v.reshape(1, -1).astype(F32)


def _routing_tables(counts):
    n_row_tiles = counts.shape[0]
    cnt = counts[:, 0, :N_EXPERTS].astype(I32)
    seg = ((cnt + SUBLANES - 1) // SUBLANES) * SUBLANES
    local_start = jnp.cumsum(seg, axis=1) - seg
    rows = jnp.sum(seg, axis=0)
    padded = ((rows + MOE_TILE - 1) // MOE_TILE) * MOE_TILE
    ends = jnp.cumsum(padded)
    starts = ends - padded
    sorted_start = starts[None, :] + jnp.cumsum(seg, axis=0) - seg
    tbl = jnp.concatenate([local_start, seg, sorted_start], axis=1).reshape(-1).astype(I32)
    tail = jnp.concatenate([starts + rows, padded - rows]).astype(I32)
    max_rows = n_row_tiles * (TOP_K * ROW_TILE + N_EXPERTS * (SUBLANES - 1)) + N_EXPERTS * (MOE_TILE - SUBLANES)
    n_tiles = -(-max_rows // MOE_TILE)
    tile_start = jnp.arange(n_tiles, dtype=I32) * MOE_TILE
    tile_expert = jnp.minimum(jnp.sum((tile_start[:, None] >= ends[None, :]).astype(I32), axis=1), N_EXPERTS - 1)
    n_used = (ends[-1] // MOE_TILE).reshape(1).astype(I32)
    return tbl, tail, tile_expert.astype(I32), n_used, n_tiles


def kernel(x_prompt, x_sample, state_conv, state_h, cache_k, cache_v, g_mix, g_ffn, g_kv, g_final, a_w_gate, a_b_gate, a_w_in, a_b_in, a_conv_w, a_conv_b, a_w_r, a_b_r, a_w_i, a_b_i, a_lam, a_w_out, a_b_out, w_kv, b_kv, rel_bias, b_w_q, b_b_q, b_sinks, b_w_o, b_b_o, f_w_gate, f_w_up, f_w_down, m_w_router, m_w_gate, m_w_up, m_w_down):
    bp, seq, d = x_prompt.shape
    bs, steps, _ = x_sample.shape
    n_prompt = bp * seq
    n_sample = bs * steps
    n = n_prompt + n_sample
    assert seq % MIX_TILE == 0 and seq % WINDOW == 0 and n_prompt % ROW_TILE == 0
    assert n_sample == ROW_TILE and bs % SAMPLE_GROUP == 0

    mix_w = dict(g=_row(g_mix[0]), wg=a_w_gate[0].astype(BF16), bg=_row(a_b_gate[0]),
                 win=a_w_in[0].astype(BF16), bin=_row(a_b_in[0]), cw=a_conv_w[0], cb=_row(a_conv_b[0]),
                 wr=a_w_r[0].astype(BF16), br=_row(a_b_r[0]), wi=a_w_i[0].astype(BF16), bi=_row(a_b_i[0]),
                 lam=_row(a_lam[0]), wout=a_w_out[0].astype(BF16), bout=_row(a_b_out[0]))

    ffn_w = dict(gf=_row(g_ffn[0]), wg=f_w_gate[0].astype(BF16), wu=f_w_up[0].astype(BF16),
                 wd=f_w_down[0].astype(BF16), gkv=_row(g_kv), wkv=w_kv.astype(BF16), bkv=_row(b_kv),
                 gq=_row(g_mix[1]), wq=b_w_q[0].astype(BF16), bq=_row(b_b_q[0]))

    h2_p, kv_p, q_p, p_conv, p_h = _layer0_prompt(x_prompt, mix_w, ffn_w)
    h1_s, s_conv_tm, s_h = _mixer_sample(jnp.transpose(x_sample, (1, 0, 2)),
                                         jnp.transpose(state_conv[0], (1, 0, 2)), state_h[0], mix_w)
    h2_s, kv_s, q_s = _ffn_rows(jnp.transpose(h1_s, (1, 0, 2)).reshape(n_sample, d), ffn_w)

    bias, bias_rows = _bias_band(rel_bias, steps)
    sinks = b_sinks[0].astype(F32)
    o_p, moe_wg, moe_wu, moe_wd = _attn_prompt(q_p, kv_p, bias, sinks, bp, seq,
                                               [m_w_gate[0], m_w_up[0], m_w_down[0]])
    q4 = q_s.reshape(bs, steps, N_KV, GROUP, HEAD_DIM).transpose(0, 2, 1, 3, 4)
    q4 = q4.reshape(bs, N_KV, steps * GROUP, HEAD_DIM)
    kv_new = kv_s.reshape(bs, steps, 2 * KV_DIM)
    sink_rows = jnp.broadcast_to(sinks.reshape(N_KV, 1, GROUP), (N_KV, steps, GROUP)).reshape(N_KV, steps * GROUP, 1)
    o4, s_kt, s_vt = _attn_sample(q4, jnp.transpose(cache_k, (0, 2, 3, 1)), jnp.transpose(cache_v, (0, 2, 3, 1)),
                                  kv_new, jnp.transpose(kv_new, (0, 2, 1)), bias_rows, sink_rows)
    o_s = o4.reshape(bs, N_KV, steps, GROUP, HEAD_DIM).transpose(0, 2, 1, 3, 4).reshape(n_sample, d)

    wr_pad = jnp.zeros((d, LANES), F32).at[:, :N_EXPERTS].set(m_w_router[0])
    wr_hi = wr_pad.astype(BF16)
    wr_lo = (wr_pad - wr_hi.astype(F32)).astype(BF16)
    h3, xn, route, counts = _oproj_router(
        o_p, o_s, h2_p, h2_s,
        dict(wo=b_w_o[0].astype(BF16), bo=_row(b_b_o[0]), gf=_row(g_ffn[1]), wr_hi=wr_hi, wr_lo=wr_lo))

    tbl, tail, tile_expert, n_used, n_tiles = _routing_tables(counts)
    xs = _sort_place(xn, route, tbl, tail, n_used, n_tiles * MOE_TILE)
    ys = _moe_experts(xs, tile_expert, n_used, moe_wg, moe_wu, moe_wd)
    y_p, y_s = _combine_final(h3, route, tbl, ys, _row(g_final), n_prompt)

    y_prompt = y_p.reshape(bp, seq, d)
    y_sample = y_s.reshape(bs, steps, d)
    kv_last = jnp.stack([kv_p[b * seq + seq - WINDOW:(b + 1) * seq] for b in range(bp)])
    p_k = kv_last[:, :, :KV_DIM].reshape(bp, WINDOW, N_KV, HEAD_DIM)
    p_v = kv_last[:, :, KV_DIM:].reshape(bp, WINDOW, N_KV, HEAD_DIM)
    s_k = jnp.transpose(s_kt, (0, 3, 1, 2))
    s_v = jnp.transpose(s_vt, (0, 3, 1, 2))
    return (y_prompt, y_sample, p_conv[None], p_h.reshape(1, bp, LRU_WIDTH), p_k, p_v,
            jnp.transpose(s_conv_tm, (1, 0, 2))[None], s_h[None], s_k, s_v)
```

```python
import functools
import math

import jax
import jax.numpy as jnp
from jax import lax
from jax.experimental import pallas as pl
from jax.experimental.pallas import tpu as pltpu

D_MODEL = 1024
LRU_WIDTH = D_MODEL
LRU_BLOCK_W = 256
LRU_BLOCKS = LRU_WIDTH // LRU_BLOCK_W
CONV_W = 4
LRU_C = 8.0
HEAD_DIM = 64
N_HEADS = D_MODEL // HEAD_DIM
N_KV = 2
GROUP = N_HEADS // N_KV
KV_DIM = N_KV * HEAD_DIM
WINDOW = 128
NUM_BUCKETS = 32
MAX_DISTANCE = 128
D_FF = 3 * D_MODEL
N_EXPERTS = 8
TOP_K = 2
D_EXPERT = 7 * D_MODEL // 2
EPS = 1e-6
NEG = -1e30

BF16 = jnp.bfloat16
F32 = jnp.float32
I32 = jnp.int32

SUBLANES = 8
LANES = 128
VMEM_LIMIT_BYTES = 56 * 1024 * 1024

ROW_TILE = 512
MIX_TILE = 256
FF_CHUNK = 1024
MOE_TILE = 512
MOE_CHUNKS = 2


def _params(*semantics):
    return pltpu.CompilerParams(dimension_semantics=semantics, vmem_limit_bytes=VMEM_LIMIT_BYTES)


def _const_spec(shape):
    zeros = (0,) * len(shape)
    return pl.BlockSpec(shape, lambda *_: zeros, pipeline_mode=pl.Buffered(1))


def _dot(a, b):
    return jnp.dot(a, b, preferred_element_type=F32)


def _rms(x, g):
    ms = jnp.mean(x * x, axis=-1, keepdims=True)
    return x * lax.rsqrt(ms + EPS) * g


def _sigmoid(x):
    return 1.0 / (1.0 + jnp.exp(-x))


def _gelu_tanh(x):
    return 0.5 * x * (1.0 + jnp.tanh(0.7978845608028654 * (x + 0.044715 * (x * x * x))))


def _log_sigmoid(x):
    return jnp.minimum(x, 0.0) - jnp.log1p(jnp.exp(-jnp.abs(x)))


def _lru_gates(xc, wr_ref, br, wi_ref, bi, lam):
    xcb = xc.astype(BF16)
    rs, gs = [], []
    for n in range(LRU_BLOCKS):
        xn = xcb[:, n * LRU_BLOCK_W:(n + 1) * LRU_BLOCK_W]
        rs.append(_dot(xn, wr_ref[n]))
        gs.append(_dot(xn, wi_ref[n]))
    r = _sigmoid(jnp.concatenate(rs, axis=1) + br)
    i = _sigmoid(jnp.concatenate(gs, axis=1) + bi)
    log_a = LRU_C * r * _log_sigmoid(lam)
    a = jnp.exp(log_a)
    mult = jnp.sqrt(1.0 - a * a)
    return a, mult * (i * xc)


def _interleave(order, *stage_generators):
    results = [None] * len(stage_generators)
    finished = set()

    def advance(idx):
        if idx in finished:
            return
        try:
            next(stage_generators[idx])
        except StopIteration as done:
            results[idx] = done.value
            finished.add(idx)

    for idx in order:
        advance(idx)
    while len(finished) < len(stage_generators):
        for idx in range(len(stage_generators)):
            advance(idx)
    return results


def _mixer_stages(x, g_ref, wg_ref, bg_ref, win_ref, bin_ref, cw_ref, cb_ref,
                  wr_ref, br_ref, wi_ref, bi_ref, lam_ref, wout_ref, bout_ref, xr_buf, h_carry):
    tt = x.shape[0]
    pad = SUBLANES
    u = _rms(x, g_ref[...]).astype(BF16)
    gate = _gelu_tanh(_dot(u, wg_ref[...]) + bg_ref[...])
    xr = _dot(u, win_ref[...]) + bin_ref[...]
    yield
    xr_buf[pad:pad + tt, :] = xr
    xc = cb_ref[...] + cw_ref[CONV_W - 1:CONV_W, :] * xr
    for k in range(CONV_W - 1):
        back = CONV_W - 1 - k
        xc = xc + cw_ref[k:k + 1, :] * xr_buf[pad - back:pad - back + tt, :]
    xr_buf[0:pad, :] = xr[tt - pad:tt, :]
    yield

    a, b = _lru_gates(xc, wr_ref, br_ref[...], wi_ref, bi_ref[...], lam_ref[...])
    yield

    groups = tt // SUBLANES
    a3 = a.reshape(groups, SUBLANES, LRU_WIDTH)
    b3 = b.reshape(groups, SUBLANES, LRU_WIDTH)
    row = lax.broadcasted_iota(I32, (1, SUBLANES, LRU_WIDTH), 1)
    step = 1
    while step < SUBLANES:
        keep = row >= step
        a_prev = jnp.where(keep, pltpu.roll(a3, step, axis=1), 1.0)
        b_prev = jnp.where(keep, pltpu.roll(b3, step, axis=1), 0.0)
        b3 = b3 + a3 * b_prev
        a3 = a3 * a_prev
        step *= 2
    yield
    h_prev = h_carry[0:1, :]
    hs = []
    for gi in range(groups):
        hg = b3[gi] + a3[gi] * h_prev
        hs.append(hg)
        h_prev = hg[SUBLANES - 1:SUBLANES, :]
        if gi + 1 == groups // 2:
            yield
    h = jnp.concatenate(hs, axis=0)
    h_carry[0:1, :] = h_prev
    yield
    y = _dot((h * gate).astype(BF16), wout_ref[...]) + bout_ref[...]
    return x + y, xr[tt - (CONV_W - 1):tt, :], h_prev


LAYER0_ORDER = (0, 1, 0, 0, 1, 0, 0, 1, 0, 0)
N_MIX_REFS = 14
N_FFN_REFS = 10


def _ffn_stages(h1, gf_ref, wg_ref, wu_ref, wd_ref, gkv_ref, wkv_ref, bkv_ref, gq_ref, wq_ref, bq_ref):
    u = _rms(h1, gf_ref[...]).astype(BF16)
    acc = h1
    n_chunks = D_FF // FF_CHUNK
    for c in range(n_chunks):
        cols = slice(c * FF_CHUNK, (c + 1) * FF_CHUNK)
        g = _dot(u, wg_ref[:, cols])
        up = _dot(u, wu_ref[:, cols])
        mid = (g * _sigmoid(g) * up).astype(BF16)
        acc = acc + _dot(mid, wd_ref[cols, :])
        if c + 1 < n_chunks:
            yield
    kv = _dot(_rms(acc, gkv_ref[...]).astype(BF16), wkv_ref[...]) + bkv_ref[...]
    q = _dot(_rms(acc, gq_ref[...]).astype(BF16), wq_ref[...]) + bq_ref[...]
    return acc, kv, (q * (HEAD_DIM ** -0.5)).astype(BF16)


def _layer0_prompt_kernel(*refs, n_tiles, tiles_per_seq, n_casts):
    x_ref = refs[0]
    mix_refs = refs[1:1 + N_MIX_REFS]
    ffn_refs = refs[1 + N_MIX_REFS:1 + N_MIX_REFS + N_FFN_REFS]
    rest = refs[1 + N_MIX_REFS + N_FFN_REFS:]
    cast_srcs, rest = rest[:n_casts], rest[n_casts:]
    h2_ref, kv_ref, q_ref, conv_ref, hlast_ref = rest[:5]
    cast_dsts = rest[5:5 + n_casts]
    xr_buf, h_carry, h1_buf = rest[5 + n_casts:]
    for src_ref, dst_ref in zip(cast_srcs, cast_dsts):
        dst_ref[...] = src_ref[...].astype(BF16)
    s = pl.program_id(0)
    slot = lax.rem(s, 2)

    @pl.when(s == 0)
    def _():
        h1_buf[...] = jnp.zeros_like(h1_buf)

    @pl.when(lax.rem(s, tiles_per_seq) == 0)
    def _():
        xr_buf[0:SUBLANES, :] = jnp.zeros((SUBLANES, LRU_WIDTH), F32)
        h_carry[...] = jnp.zeros_like(h_carry)

    (h1, conv_tail, h_last), (h2, kv, q) = _interleave(
        LAYER0_ORDER,
        _mixer_stages(x_ref[...], *mix_refs, xr_buf, h_carry), _ffn_stages(h1_buf[1 - slot], *ffn_refs))
    h2_ref[...] = h2
    kv_ref[...] = kv
    q_ref[...] = q
    h1_buf[slot] = h1

    @pl.when(s < n_tiles)
    def _():
        b = s // tiles_per_seq
        conv_ref[b] = conv_tail
        hlast_ref[b] = h_last


def _layer0_prompt(x, mix_w, ffn_w, cast_ws):
    b, t, d = x.shape
    tt = MIX_TILE
    n_tiles = (b * t) // tt
    casts = [_side_cast_specs(w, n_tiles, lambda s: jnp.minimum(s, n_tiles - 1)) for w in cast_ws]
    vec = lambda n: _const_spec((1, n))
    mix_specs = [
        vec(d), _const_spec((d, LRU_WIDTH)), vec(LRU_WIDTH), _const_spec((d, LRU_WIDTH)), vec(LRU_WIDTH),
        _const_spec((CONV_W, LRU_WIDTH)), vec(LRU_WIDTH),
        _const_spec((LRU_BLOCKS, LRU_BLOCK_W, LRU_BLOCK_W)), vec(LRU_WIDTH),
        _const_spec((LRU_BLOCKS, LRU_BLOCK_W, LRU_BLOCK_W)), vec(LRU_WIDTH),
        vec(LRU_WIDTH), _const_spec((LRU_WIDTH, d)), vec(d),
    ]
    ffn_specs = [
        vec(d), _const_spec((d, D_FF)), _const_spec((d, D_FF)), _const_spec((D_FF, d)),
        vec(d), _const_spec((d, 2 * KV_DIM)), vec(2 * KV_DIM), vec(d), _const_spec((d, d)), vec(d),
    ]
    assert len(mix_specs) == N_MIX_REFS and len(ffn_specs) == N_FFN_REFS
    prev = lambda s: (jnp.maximum(s - 1, 0), 0)
    whole = lambda shape: pl.BlockSpec(shape, lambda s: (0,) * len(shape))
    return pl.pallas_call(
        functools.partial(_layer0_prompt_kernel, n_tiles=n_tiles, tiles_per_seq=t // tt, n_casts=len(casts)),
        grid=(n_tiles + 1,),
        in_specs=([pl.BlockSpec((tt, d), lambda s: (jnp.minimum(s, n_tiles - 1), 0))] + mix_specs + ffn_specs
                  + [c[0] for c in casts]),
        out_specs=[pl.BlockSpec((tt, d), prev), pl.BlockSpec((tt, 2 * KV_DIM), prev), pl.BlockSpec((tt, d), prev),
                   whole((b, CONV_W - 1, LRU_WIDTH)), whole((b, 1, LRU_WIDTH))] + [c[1] for c in casts],
        out_shape=[jax.ShapeDtypeStruct((b * t, d), F32), jax.ShapeDtypeStruct((b * t, 2 * KV_DIM), F32),
                   jax.ShapeDtypeStruct((b * t, d), BF16),
                   jax.ShapeDtypeStruct((b, CONV_W - 1, LRU_WIDTH), F32),
                   jax.ShapeDtypeStruct((b, 1, LRU_WIDTH), F32)] + [c[2] for c in casts],
        scratch_shapes=[pltpu.VMEM((SUBLANES + tt, LRU_WIDTH), F32), pltpu.VMEM((SUBLANES, LRU_WIDTH), F32),
                        pltpu.VMEM((2, tt, d), F32)],
        compiler_params=_params("arbitrary"),
        name="layer0_prompt",
    )(x.reshape(b * t, d), *[mix_w[k] for k in MIX_KEYS], *[ffn_w[k] for k in FFN_KEYS], *cast_ws)


MIX_KEYS = ("g", "wg", "bg", "win", "bin", "cw", "cb", "wr", "br", "wi", "bi", "lam", "wout", "bout")
FFN_KEYS = ("gf", "wg", "wu", "wd", "gkv", "wkv", "bkv", "gq", "wq", "bq")


def _mixer_sample_kernel(x_ref, cs_ref, h0_ref, g_ref, wg_ref, bg_ref, win_ref, bin_ref, cw_ref, cb_ref,
                         wr_ref, br_ref, wi_ref, bi_ref, lam_ref, wout_ref, bout_ref,
                         h1_ref, conv_ref, hlast_ref):
    steps, nb, d = x_ref.shape
    x = x_ref[...].reshape(steps * nb, d)
    u = _rms(x, g_ref[...]).astype(BF16)
    gate = _gelu_tanh(_dot(u, wg_ref[...]) + bg_ref[...])
    xr = _dot(u, win_ref[...]) + bin_ref[...]
    xpad = [cs_ref[k] for k in range(CONV_W - 1)] + [xr[s * nb:(s + 1) * nb, :] for s in range(steps)]
    xcs = []
    for s in range(steps):
        acc = cb_ref[...] + cw_ref[0:1, :] * xpad[s]
        for k in range(1, CONV_W):
            acc = acc + cw_ref[k:k + 1, :] * xpad[s + k]
        xcs.append(acc)
    for k in range(CONV_W - 1):
        conv_ref[k] = xpad[steps + k]
    xc = jnp.concatenate(xcs, axis=0)
    a, b = _lru_gates(xc, wr_ref, br_ref[...], wi_ref, bi_ref[...], lam_ref[...])
    h = h0_ref[...]
    hs = []
    for s in range(steps):
        h = a[s * nb:(s + 1) * nb, :] * h + b[s * nb:(s + 1) * nb, :]
        hs.append(h)
    hlast_ref[...] = h
    hcat = jnp.concatenate(hs, axis=0)
    y = _dot((hcat * gate).astype(BF16), wout_ref[...]) + bout_ref[...]
    h1_ref[...] = (x + y).reshape(steps, nb, d)


def _mixer_sample(x_tm, cs_tm, h0, w):
    steps, nb, d = x_tm.shape
    full = lambda shape: pl.BlockSpec(shape, lambda i: (0,) * len(shape))
    vec = lambda n: full((1, n))
    in_specs = [
        full((steps, nb, d)), full((CONV_W - 1, nb, LRU_WIDTH)), full((nb, LRU_WIDTH)),
        vec(d), full((d, LRU_WIDTH)), vec(LRU_WIDTH), full((d, LRU_WIDTH)), vec(LRU_WIDTH),
        full((CONV_W, LRU_WIDTH)), vec(LRU_WIDTH),
        full((LRU_BLOCKS, LRU_BLOCK_W, LRU_BLOCK_W)), vec(LRU_WIDTH),
        full((LRU_BLOCKS, LRU_BLOCK_W, LRU_BLOCK_W)), vec(LRU_WIDTH),
        vec(LRU_WIDTH), full((LRU_WIDTH, d)), vec(d),
    ]
    out_specs = [full((steps, nb, d)), full((CONV_W - 1, nb, LRU_WIDTH)), full((nb, LRU_WIDTH))]
    out_shape = [
        jax.ShapeDtypeStruct((steps, nb, d), F32),
        jax.ShapeDtypeStruct((CONV_W - 1, nb, LRU_WIDTH), F32),
        jax.ShapeDtypeStruct((nb, LRU_WIDTH), F32),
    ]
    return pl.pallas_call(
        _mixer_sample_kernel, grid=(1,), in_specs=in_specs, out_specs=out_specs, out_shape=out_shape,
        compiler_params=_params("arbitrary"), name="mixer_sample",
    )(x_tm, cs_tm, h0, w["g"], w["wg"], w["bg"], w["win"], w["bin"], w["cw"], w["cb"], w["wr"], w["br"],
      w["wi"], w["bi"], w["lam"], w["wout"], w["bout"])


def _two_part_specs(n_prompt, n_sample, width):
    assert n_sample == ROW_TILE and n_prompt % ROW_TILE == 0
    last_prompt = n_prompt // ROW_TILE - 1
    return [pl.BlockSpec((ROW_TILE, width), lambda i: (jnp.minimum(i, last_prompt), 0)),
            pl.BlockSpec((ROW_TILE, width), lambda i: (0, 0))]


def _two_part_tile(prompt_ref, sample_ref):
    is_sample = pl.program_id(0) == pl.num_programs(0) - 1
    return jnp.where(is_sample, sample_ref[...], prompt_ref[...])


def _side_cast_specs(w, n_steps, step_of):
    ne, rows, cols = w.shape
    per_expert = n_steps // ne
    blk = rows // per_expert
    assert per_expert * ne == n_steps and blk * per_expert == rows and blk % (2 * SUBLANES) == 0
    index = lambda *ids: (step_of(*ids) // per_expert, step_of(*ids) % per_expert, 0)
    spec = pl.BlockSpec((1, blk, cols), index)
    return spec, spec, jax.ShapeDtypeStruct(w.shape, BF16)


def _ffn_rows_kernel(h1_ref, *refs):
    h2_ref, kv_ref, q_ref = refs[N_FFN_REFS:]
    (h2_ref[...], kv_ref[...], q_ref[...]), = _interleave((), _ffn_stages(h1_ref[...], *refs[:N_FFN_REFS]))


def _ffn_rows(h1, w):
    n, d = h1.shape
    tm = ROW_TILE
    row = lambda width: pl.BlockSpec((tm, width), lambda i: (i, 0))
    vec = lambda width: _const_spec((1, width))
    in_specs = [
        row(d), vec(d), _const_spec((d, D_FF)), _const_spec((d, D_FF)), _const_spec((D_FF, d)),
        vec(d), _const_spec((d, 2 * KV_DIM)), vec(2 * KV_DIM), vec(d), _const_spec((d, d)), vec(d),
    ]
    return pl.pallas_call(
        _ffn_rows_kernel,
        grid=(n // tm,),
        in_specs=in_specs,
        out_specs=[row(d), row(2 * KV_DIM), row(d)],
        out_shape=[jax.ShapeDtypeStruct((n, d), F32), jax.ShapeDtypeStruct((n, 2 * KV_DIM), F32),
                   jax.ShapeDtypeStruct((n, d), BF16)],
        compiler_params=_params("arbitrary"),
        name="ffn_rows",
    )(h1, *[w[k] for k in FFN_KEYS])


def _head_of(j, parity, pair):
    return j * GROUP + 2 * pair + parity


def _distance_bias(dist, table_ref, head):
    max_exact = NUM_BUCKETS // 2
    n = jnp.maximum(dist, 0)
    large = jnp.full(dist.shape, max_exact, I32)
    for step in range(1, NUM_BUCKETS - max_exact):
        threshold = math.ceil(max_exact * (MAX_DISTANCE / max_exact) ** (step / (NUM_BUCKETS - max_exact)))
        large = large + jnp.where(n >= threshold, 1, 0)
    bucket = jnp.where(n < max_exact, n, large)
    acc = jnp.zeros(dist.shape, F32)
    for bkt in range(NUM_BUCKETS):
        acc = jnp.where(bucket == bkt, table_ref[bkt, head], acc)
    return acc


def _bias_band_kernel(table_ref, folded_ref, rows_ref):
    w = WINDOW
    ci = lax.broadcasted_iota(I32, (w, w), 0)
    qi = lax.broadcasted_iota(I32, (w, w), 1)
    folded_dist = jnp.where(ci > qi, qi + w - ci, qi - ci)
    for j in range(N_KV):
        for parity in range(2):
            for pair in range(GROUP // 2):
                folded_ref[j, parity, :, pair * w:(pair + 1) * w] = _distance_bias(
                    folded_dist, table_ref, _head_of(j, parity, pair))
    n_rows, n_keys = rows_ref.shape[1], rows_ref.shape[2]
    ri = lax.broadcasted_iota(I32, (GROUP, n_keys), 0)
    si = lax.broadcasted_iota(I32, (GROUP, n_keys), 1)
    for j in range(N_KV):
        for t in range(n_rows // GROUP):
            acc = jnp.zeros((GROUP, n_keys), F32)
            for g in range(GROUP):
                acc = jnp.where(ri == g, _distance_bias(t + w - si, table_ref, j * GROUP + g), acc)
            rows_ref[j, t * GROUP:(t + 1) * GROUP, :] = acc


def _bias_band(rel_table, steps):
    folded = (N_KV, 2, WINDOW, (GROUP // 2) * WINDOW)
    rows = (N_KV, steps * GROUP, WINDOW + steps + (-steps) % SUBLANES)
    return pl.pallas_call(
        _bias_band_kernel,
        grid=(1,),
        in_specs=[pl.BlockSpec(memory_space=pltpu.SMEM)],
        out_specs=[pl.BlockSpec(folded, lambda i: (0, 0, 0, 0)), pl.BlockSpec(rows, lambda i: (0, 0, 0))],
        out_shape=[jax.ShapeDtypeStruct(folded, F32), jax.ShapeDtypeStruct(rows, F32)],
        compiler_params=_params("arbitrary"),
        name="bias_band",
    )(rel_table)


def _softmax_sink_pv(s, sink, v):
    m = jnp.maximum(jnp.max(s, axis=-1, keepdims=True), sink)
    p = jnp.exp(s - m)
    denom = jnp.sum(p, axis=-1, keepdims=True) + jnp.exp(sink - m)
    return _dot(p.astype(BF16), v) / denom


ATTN_BLOCKS = 4


def _attn_prompt_kernel(sink_ref, q_ref, kvp_ref, kvc_ref, bias_ref, *refs):
    n_casts = (len(refs) - 1) // 2
    o_ref = refs[n_casts]
    for src_ref, dst_ref in zip(refs[:n_casts], refs[n_casts + 1:]):
        dst_ref[...] = src_ref[...].astype(BF16)
    w = WINDOW
    pairs = GROUP // 2
    assert 2 * HEAD_DIM == LANES and KV_DIM == LANES
    ci = lax.broadcasted_iota(I32, (w, pairs * w), 0)
    qi = lax.broadcasted_iota(I32, (w, pairs * w), 1) % w
    from_prev = ci > qi
    has_prev = pl.program_id(1) > 0

    kv = jnp.concatenate([kvp_ref[...], kvc_ref[...]], axis=0)
    low = lax.broadcasted_iota(I32, (kv.shape[0], LANES), 1) < HEAD_DIM

    def halves(x):
        swapped = pltpu.roll(x, HEAD_DIM, axis=1)
        return (((jnp.where(low, x, 0.0)).astype(BF16), (jnp.where(low, 0.0, swapped)).astype(BF16)),
                ((jnp.where(low, swapped, 0.0)).astype(BF16), (jnp.where(low, 0.0, x)).astype(BF16)))

    k_ops = halves(kv[:, 0:KV_DIM])
    v_ops = halves(kv[:, KV_DIM:2 * KV_DIM])

    sinks = [[jnp.concatenate([jnp.full((1, w), sink_ref[_head_of(j, parity, b)], F32) for b in range(pairs)], axis=1)
              for parity in range(2)] for j in range(N_KV)]
    nt = (((1,), (1,)), ((), ()))
    tn = (((0,), (0,)), ((), ()))
    for i in range(ATTN_BLOCKS):
        q_rows = slice(i * w, (i + 1) * w)
        prev_rows = slice(i * w, (i + 1) * w)
        own_rows = slice((i + 1) * w, (i + 2) * w)
        for j in range(N_KV):
            q4 = jnp.concatenate([q_ref[q_rows, (j * pairs + b) * LANES:(j * pairs + b + 1) * LANES]
                                  for b in range(pairs)], axis=0)
            acc = None
            for parity in range(2):
                k_op, v_op, sink = k_ops[j][parity], v_ops[j][parity], sinks[j][parity]
                s_prev = lax.dot_general(k_op[prev_rows], q4, nt, preferred_element_type=F32)
                s_own = lax.dot_general(k_op[own_rows], q4, nt, preferred_element_type=F32)
                if i == 0:
                    s_prev = jnp.where(has_prev, s_prev, NEG)
                s = jnp.where(from_prev, s_prev, s_own) + bias_ref[j, parity]
                m = jnp.maximum(jnp.max(s, axis=0, keepdims=True), sink)
                p = jnp.exp(s - m)
                denom = jnp.sum(p, axis=0, keepdims=True) + jnp.exp(sink - m)
                p = p * (1.0 / denom)
                o = (lax.dot_general(jnp.where(from_prev, p, 0.0).astype(BF16), v_op[prev_rows], tn,
                                     preferred_element_type=F32)
                     + lax.dot_general(jnp.where(from_prev, 0.0, p).astype(BF16), v_op[own_rows], tn,
                                       preferred_element_type=F32))
                acc = o if acc is None else acc + o
            for b in range(pairs):
                o_ref[q_rows, (j * pairs + b) * LANES:(j * pairs + b + 1) * LANES] = acc[b * w:(b + 1) * w].astype(BF16)


def _attn_prompt(q, kv, bias, sinks, batch, seq, cast_ws):
    nb = seq // WINDOW
    assert nb % ATTN_BLOCKS == 0
    ns = nb // ATTN_BLOCKS
    rows = ATTN_BLOCKS * WINDOW
    d = q.shape[1]
    casts = [_side_cast_specs(w, batch * ns, lambda b, n: b * ns + n) for w in cast_ws]
    in_specs = [
        pl.BlockSpec(memory_space=pltpu.SMEM),
        pl.BlockSpec((rows, d), lambda b, n: (b * ns + n, 0)),
        pl.BlockSpec((WINDOW, 2 * KV_DIM), lambda b, n: (b * nb + jnp.maximum(n * ATTN_BLOCKS - 1, 0), 0)),
        pl.BlockSpec((rows, 2 * KV_DIM), lambda b, n: (b * ns + n, 0)),
        _const_spec(bias.shape),
    ] + [c[0] for c in casts]
    return pl.pallas_call(
        _attn_prompt_kernel,
        grid=(batch, ns),
        in_specs=in_specs,
        out_specs=[pl.BlockSpec((rows, d), lambda b, n: (b * ns + n, 0))] + [c[1] for c in casts],
        out_shape=[jax.ShapeDtypeStruct((batch * seq, d), BF16)] + [c[2] for c in casts],
        compiler_params=_params("arbitrary", "arbitrary"),
        name="attn_prompt",
    )(sinks, q, kv, kv, bias, *cast_ws)


SAMPLE_GROUP = 8


def _attn_sample_kernel(q_ref, ckt_ref, cvt_ref, new_ref, newt_ref, bias_ref, sink_ref, o_ref, skt_ref, svt_ref):
    gb = q_ref.shape[0]
    steps = new_ref.shape[1]
    rows = steps * GROUP
    pad = (-steps) % SUBLANES
    ri = lax.broadcasted_iota(I32, (rows, WINDOW + steps + pad), 0)
    si = lax.broadcasted_iota(I32, (rows, WINDOW + steps + pad), 1)
    dist = ri // GROUP + WINDOW - si
    mask = (dist >= 0) & (dist < WINDOW)
    mask_c, mask_n = mask[:, :WINDOW], mask[:, WINDOW:]
    zeros = jnp.zeros((pad, 2 * KV_DIM), F32)

    def scores(b, j):
        new = jnp.concatenate([new_ref[b], zeros], axis=0).astype(BF16)
        q = q_ref[b, j]
        s_c = _dot(q, ckt_ref[b, j].astype(BF16))
        s_n = lax.dot_general(q, new[:, j * HEAD_DIM:(j + 1) * HEAD_DIM], (((1,), (1,)), ((), ())),
                              preferred_element_type=F32)
        bias = bias_ref[j]
        return (jnp.where(mask_c, s_c + bias[:, :WINDOW], NEG), jnp.where(mask_n, s_n + bias[:, WINDOW:], NEG), new)

    tasks = [(b, j) for b in range(gb) for j in range(N_KV)]
    all_scores = [scores(b, j) for b, j in tasks]
    for (b, j), (s_c, s_n, new) in zip(tasks, all_scores):
        sink = sink_ref[j]
        m = jnp.maximum(jnp.maximum(jnp.max(s_c, axis=-1, keepdims=True), jnp.max(s_n, axis=-1, keepdims=True)), sink)
        p_c = jnp.exp(s_c - m)
        p_n = jnp.exp(s_n - m)
        denom = jnp.sum(p_c, axis=-1, keepdims=True) + jnp.sum(p_n, axis=-1, keepdims=True) + jnp.exp(sink - m)
        pv = lax.dot_general(p_c.astype(BF16), cvt_ref[b, j].astype(BF16), (((1,), (1,)), ((), ())),
                             preferred_element_type=F32)
        pv = pv + _dot(p_n.astype(BF16), new[:, KV_DIM + j * HEAD_DIM:KV_DIM + (j + 1) * HEAD_DIM])
        o_ref[b, j] = (pv / denom).astype(BF16)

    lane = lax.broadcasted_iota(I32, (HEAD_DIM, WINDOW), 1)
    for b in range(gb):
        newt = jnp.concatenate([newt_ref[b], jnp.zeros((2 * KV_DIM, WINDOW - steps), F32)], axis=1)
        for j in range(N_KV):
            for src_ref, dst_ref, base in ((ckt_ref, skt_ref, 0), (cvt_ref, svt_ref, KV_DIM)):
                fresh = newt[base + j * HEAD_DIM:base + (j + 1) * HEAD_DIM, :]
                dst_ref[b, j] = pltpu.roll(jnp.where(lane < steps, fresh, src_ref[b, j]), WINDOW - steps, axis=1)


def _attn_sample(q4, ckt, cvt, new, newt, bias_rows, sink_rows):
    nb, _, rows, hd = q4.shape
    steps = new.shape[1]
    keys = bias_rows.shape[2]
    gb = SAMPLE_GROUP
    cache_spec = pl.BlockSpec((gb, N_KV, hd, WINDOW), lambda i: (i, 0, 0, 0))
    in_specs = [
        pl.BlockSpec((gb, N_KV, rows, hd), lambda i: (i, 0, 0, 0)),
        cache_spec, cache_spec,
        pl.BlockSpec((gb, steps, 2 * KV_DIM), lambda i: (i, 0, 0)),
        pl.BlockSpec((gb, 2 * KV_DIM, steps), lambda i: (i, 0, 0)),
        pl.BlockSpec((N_KV, rows, keys), lambda i: (0, 0, 0)),
        pl.BlockSpec((N_KV, rows, 1), lambda i: (0, 0, 0)),
    ]
    return pl.pallas_call(
        _attn_sample_kernel,
        grid=(nb // gb,),
        in_specs=in_specs,
        out_specs=[pl.BlockSpec((gb, N_KV, rows, hd), lambda i: (i, 0, 0, 0)), cache_spec, cache_spec],
        out_shape=[jax.ShapeDtypeStruct(q4.shape, BF16), jax.ShapeDtypeStruct(ckt.shape, F32),
                   jax.ShapeDtypeStruct(cvt.shape, F32)],
        compiler_params=_params("arbitrary"),
        name="attn_sample",
    )(q4, ckt, cvt, new, newt, bias_rows, sink_rows)


def _split_bf16(x):
    hi = x.astype(BF16)
    lo = (x - hi.astype(F32)).astype(BF16)
    return hi, lo


ROUTER_PARTS = 2


def _oproj_router_kernel(op_ref, os_ref, h2p_ref, h2s_ref, wo_ref, bo_ref, gf_ref, wrh_ref, wrl_ref,
                         h3_ref, xn_ref, route_ref, counts_ref):
    tm = h3_ref.shape[0]
    parts = ROUTER_PARTS
    pr = tm // parts
    o = _two_part_tile(op_ref, os_ref)
    h2 = _two_part_tile(h2p_ref, h2s_ref)
    lane = lax.broadcasted_iota(I32, (pr, LANES), 1)
    lanef = lane.astype(F32)
    ri = lax.broadcasted_iota(I32, (pr, pr), 0)
    ci = lax.broadcasted_iota(I32, (pr, pr), 1)
    earlier = jnp.where(ri > ci, 1.0, 0.0).astype(BF16)

    picks, ranks, gates = [], [], []
    cnt = jnp.zeros((1, LANES), F32)
    for p in range(parts):
        rows = slice(p * pr, (p + 1) * pr)
        h3 = h2[rows, :] + _dot(o[rows, :], wo_ref[...]) + bo_ref[...]
        h3_ref[rows, :] = h3
        u_hi, u_lo = _split_bf16(_rms(h3, gf_ref[...]))
        xn_ref[rows, :] = u_hi
        logits =_dot(u_hi, wrh_ref[...]) + (_dot(u_lo, wrh_ref[...]) + _dot(u_hi, wrl_ref[...]))
        logits = jnp.where(lane < N_EXPERTS, logits, -jnp.inf)
        v1 = jnp.max(logits, axis=-1, keepdims=True)
        e1 = jnp.min(jnp.where(logits == v1, lanef, float(LANES)), axis=-1, keepdims=True)
        rest = jnp.where(lanef == e1, -jnp.inf, logits)
        v2 = jnp.max(rest, axis=-1, keepdims=True)
        e2 = jnp.min(jnp.where(rest == v2, lanef, float(LANES)), axis=-1, keepdims=True)
        ex = jnp.exp(v2 - v1)
        gates.append((1.0 / (1.0 + ex), ex / (1.0 + ex)))
        pick1 = lanef == e1
        pick2 = lanef == e2
        picks.append((pick1, pick2))
        sel = jnp.where(pick1 | pick2, 1.0, 0.0)
        ranks.append(_dot(earlier, sel.astype(BF16)) + cnt)
        cnt = cnt + jnp.sum(sel, axis=0, keepdims=True)

    seg = jnp.floor((cnt + (SUBLANES - 1)) * (1.0 / SUBLANES)) * SUBLANES
    ek = lax.broadcasted_iota(I32, (LANES, LANES), 0)
    el = lax.broadcasted_iota(I32, (LANES, LANES), 1)
    lower_experts = jnp.where(ek < el, 1.0, 0.0).astype(BF16)
    seg_start = _dot(jnp.broadcast_to(seg, (SUBLANES, LANES)).astype(BF16), lower_experts)[0:1, :]
    for p in range(parts):
        local = ranks[p] + seg_start
        lr1 = jnp.sum(jnp.where(picks[p][0], local, 0.0), axis=-1, keepdims=True)
        lr2 = jnp.sum(jnp.where(picks[p][1], local, 0.0), axis=-1, keepdims=True)
        w1, w2 = gates[p]
        route_ref[p * pr:(p + 1) * pr, :] = jnp.where(
            lane == 0, lr1, jnp.where(lane == 1, lr2, jnp.where(lane == 2, w1, jnp.where(lane == 3, w2, 0.0))))
    counts_ref[0] = jnp.broadcast_to(cnt, (SUBLANES, LANES))


def _oproj_router(o_p, o_s, h2_p, h2_s, w):
    d = h2_p.shape[1]
    n = h2_p.shape[0] + h2_s.shape[0]
    tm = ROW_TILE
    row = lambda width: pl.BlockSpec((tm, width), lambda i: (i, 0))
    in_specs = (_two_part_specs(o_p.shape[0], o_s.shape[0], d) + _two_part_specs(h2_p.shape[0], h2_s.shape[0], d) + [
        _const_spec((d, d)), _const_spec((1, d)), _const_spec((1, d)),
        _const_spec((d, LANES)), _const_spec((d, LANES))])
    return pl.pallas_call(
        _oproj_router_kernel,
        grid=(n // tm,),
        in_specs=in_specs,
        out_specs=[row(d), row(d), row(LANES), pl.BlockSpec((1, SUBLANES, LANES), lambda i: (i, 0, 0))],
        out_shape=[jax.ShapeDtypeStruct((n, d), F32), jax.ShapeDtypeStruct((n, d), BF16),
                   jax.ShapeDtypeStruct((n, LANES), F32), jax.ShapeDtypeStruct((n // tm, SUBLANES, LANES), F32)],
        compiler_params=_params("arbitrary"),
        name="oproj_router",
    )(o_p, o_s, h2_p, h2_s, w["wo"], w["bo"], w["gf"], w["wr_hi"], w["wr_lo"])


LOCAL_ROWS = 1152
XS_WIDTH = D_MODEL + LANES
SEG_TABLE = 3 * N_EXPERTS
SEG_PIECES = tuple(SUBLANES << b for b in reversed(range(7)))


def _segment_copies(src_ref, src_start, dst_ref, dst_start, length, sem, act):
    for piece in SEG_PIECES:
        done = lax.div(length, 2 * piece) * (2 * piece)

        @pl.when(lax.rem(lax.div(length, piece), 2) != 0)
        def _():
            s = pl.multiple_of(src_start + done, SUBLANES)
            t = pl.multiple_of(dst_start + done, SUBLANES)
            act(pltpu.make_async_copy(src_ref.at[pl.ds(s, piece), :], dst_ref.at[pl.ds(t, piece), :], sem))


def _sort_place_kernel(tbl_ref, tail_ref, nu_ref, xn_ref, route_ref, xs_ref, stage, zeros, sems):
    j = pl.program_id(0)
    tm = xn_ref.shape[0]
    xn = xn_ref[...]
    rt = jnp.transpose(route_ref[...])
    row = lax.broadcasted_iota(I32, (LOCAL_ROWS, tm), 0).astype(F32)
    m1 = row == rt[0:1, :]
    m2 = row == rt[1:2, :]
    onehot = jnp.where(m1 | m2, 1.0, 0.0).astype(BF16)
    slot = lax.rem(j, 2)
    buf = stage.at[slot]
    buf[:, 0:D_MODEL] = _dot(onehot, xn)
    gate = jnp.sum(jnp.where(m1, rt[2:3, :], 0.0) + jnp.where(m2, rt[3:4, :], 0.0), axis=-1, keepdims=True)
    buf[:, D_MODEL:XS_WIDTH] = jnp.broadcast_to(gate, (LOCAL_ROWS, LANES))

    def segments(tile, tile_slot, act):
        for e in range(N_EXPERTS):
            base = tile * SEG_TABLE
            _segment_copies(stage.at[tile_slot], tbl_ref[base + e], xs_ref, tbl_ref[base + 2 * N_EXPERTS + e],
                            tbl_ref[base + N_EXPERTS + e], sems.at[tile_slot], act)

    segments(j, slot, lambda cp: cp.start())

    @pl.when(j > 0)
    def _():
        segments(j - 1, 1 - slot, lambda cp: cp.wait())

    @pl.when(j == pl.num_programs(0) - 1)
    def _():
        segments(j, slot, lambda cp: cp.wait())
        zeros[...] = jnp.zeros_like(zeros)
        sem = sems.at[0]

        def tails(act):
            for e in range(N_EXPERTS):
                _segment_copies(zeros, 0, xs_ref, tail_ref[e], tail_ref[N_EXPERTS + e], sem, act)

        def unused(act):
            def body(i, c):
                t = pl.multiple_of(i * MOE_TILE, MOE_TILE)
                act(pltpu.make_async_copy(zeros, xs_ref.at[pl.ds(t, MOE_TILE), :], sem))
                return c
            lax.fori_loop(nu_ref[0], xs_ref.shape[0] // MOE_TILE, body, 0)

        tails(lambda cp: cp.start())
        unused(lambda cp: cp.start())
        tails(lambda cp: cp.wait())
        unused(lambda cp: cp.wait())


def _sort_place(xn, route, tbl, tail, n_used, p_rows):
    n, d = xn.shape
    tm = ROW_TILE
    grid_spec = pltpu.PrefetchScalarGridSpec(
        num_scalar_prefetch=3,
        grid=(n // tm,),
        in_specs=[pl.BlockSpec((tm, d), lambda j, *_: (j, 0)),
                  pl.BlockSpec((tm, LANES), lambda j, *_: (j, 0))],
        out_specs=pl.BlockSpec(memory_space=pl.ANY),
        scratch_shapes=[pltpu.VMEM((2, LOCAL_ROWS, XS_WIDTH), F32), pltpu.VMEM((MOE_TILE, XS_WIDTH), F32),
                        pltpu.SemaphoreType.DMA((2,))],
    )
    return pl.pallas_call(
        _sort_place_kernel,
        grid_spec=grid_spec,
        out_shape=jax.ShapeDtypeStruct((p_rows, XS_WIDTH), F32),
        compiler_params=_params("arbitrary"),
        name="moe_sort_place",
    )(tbl, tail, n_used, xn, route)


def _moe_kernel(te_ref, nu_ref, xs_ref, wg_ref, wu_ref, wd_ref, y_ref):
    i = pl.program_id(0)
    c = pl.program_id(1)
    used = i < nu_ref[0]

    @pl.when(used)
    def _():
        xn = xs_ref[:, 0:D_MODEL].astype(BF16)
        g = _dot(xn, wg_ref[0])
        up = _dot(xn, wu_ref[0])
        mid = (g * _sigmoid(g) * up).astype(BF16)
        y = _dot(mid, wd_ref[0]) * xs_ref[:, D_MODEL:D_MODEL + 1]

        @pl.when(c == 0)
        def _():
            y_ref[...] = y

        @pl.when(c > 0)
        def _():
            y_ref[...] = y_ref[...] + y

    @pl.when(jnp.logical_not(used) & (c == 0))
    def _():
        y_ref[...] = jnp.zeros_like(y_ref)


def _moe_experts(xs, tile_expert, n_used, wg, wu, wd):
    p_rows = xs.shape[0]
    d = D_MODEL
    tm = MOE_TILE
    n_tiles = p_rows // tm
    ch = D_EXPERT // MOE_CHUNKS
    last = MOE_CHUNKS - 1

    def chunk(i, c, nu):
        return jnp.where(i < nu[0], c, last)

    grid_spec = pltpu.PrefetchScalarGridSpec(
        num_scalar_prefetch=2,
        grid=(n_tiles, MOE_CHUNKS),
        in_specs=[
            pl.BlockSpec((tm, XS_WIDTH), lambda i, c, te, nu: (i, 0)),
            pl.BlockSpec((1, d, ch), lambda i, c, te, nu: (te[i], 0, chunk(i, c, nu))),
            pl.BlockSpec((1, d, ch), lambda i, c, te, nu: (te[i], 0, chunk(i, c, nu))),
            pl.BlockSpec((1, ch, d), lambda i, c, te, nu: (te[i], chunk(i, c, nu), 0)),
        ],
        out_specs=pl.BlockSpec((tm, d), lambda i, c, te, nu: (i, 0)),
    )
    return pl.pallas_call(
        _moe_kernel,
        grid_spec=grid_spec,
        out_shape=jax.ShapeDtypeStruct((p_rows, d), F32),
        compiler_params=_params("arbitrary", "arbitrary"),
        name="moe_experts",
    )(tile_expert, n_used, xs, wg, wu, wd)


def _combine_kernel(tbl_ref, h3_ref, route_ref, ys_ref, g_ref, outp_ref, outs_ref, ybuf, sems):
    j = pl.program_id(0)
    last = pl.num_programs(0) - 1
    tm = h3_ref.shape[0]
    slot = lax.rem(j, 2)

    def segments(tile, tile_slot, act):
        for e in range(N_EXPERTS):
            base = tile * SEG_TABLE
            _segment_copies(ys_ref, tbl_ref[base + 2 * N_EXPERTS + e], ybuf.at[tile_slot], tbl_ref[base + e],
                            tbl_ref[base + N_EXPERTS + e], sems.at[tile_slot], act)

    @pl.when(j == 0)
    def _():
        ybuf[...] = jnp.zeros_like(ybuf)
        segments(j, slot, lambda cp: cp.start())

    @pl.when(j < last)
    def _():
        segments(j + 1, 1 - slot, lambda cp: cp.start())

    segments(j, slot, lambda cp: cp.wait())

    route = route_ref[...]
    col = lax.broadcasted_iota(I32, (tm, LOCAL_ROWS), 1).astype(F32)
    picks = jnp.where((col == route[:, 0:1]) | (col == route[:, 1:2]), 1.0, 0.0).astype(BF16)
    h4 = h3_ref[...] + _dot(picks, ybuf[slot].astype(BF16))
    out = _rms(h4, g_ref[...])

    @pl.when(j < last)
    def _():
        outp_ref[...] = out

    @pl.when(j == last)
    def _():
        outs_ref[...] = out


def _combine_final(h3, route, tbl, ys, g_final, n_prompt):
    n, d = h3.shape
    tm = ROW_TILE
    n_sample = n - n_prompt
    assert n_sample == tm and n_prompt % tm == 0
    last_prompt = n_prompt // tm - 1
    grid_spec = pltpu.PrefetchScalarGridSpec(
        num_scalar_prefetch=1,
        grid=(n // tm,),
        in_specs=[pl.BlockSpec((tm, d), lambda j, *_: (j, 0)),
                  pl.BlockSpec((tm, LANES), lambda j, *_: (j, 0)),
                  pl.BlockSpec(memory_space=pl.ANY),
                  pl.BlockSpec((1, d), lambda j, *_: (0, 0))],
        out_specs=[pl.BlockSpec((tm, d), lambda j, *_: (jnp.minimum(j, last_prompt), 0)),
                   pl.BlockSpec((tm, d), lambda j, *_: (0, 0))],
        scratch_shapes=[pltpu.VMEM((2, LOCAL_ROWS, d), F32), pltpu.SemaphoreType.DMA((2,))],
    )
    return pl.pallas_call(
        _combine_kernel,
        grid_spec=grid_spec,
        out_shape=[jax.ShapeDtypeStruct((n_prompt, d), F32), jax.ShapeDtypeStruct((n_sample, d), F32)],
        compiler_params=_params("arbitrary"),
        name="moe_combine",
    )(tbl, h3, route, ys, g_final)


def _row(v):
    return v.reshape(1, -1).astype(F32)


def _routing_tables(counts):
    n_row_tiles = counts.shape[0]
    cnt = counts[:, 0, :N_EXPERTS].astype(I32)
    seg = ((cnt + SUBLANES - 1) // SUBLANES) * SUBLANES
    local_start = jnp.cumsum(seg, axis=1) - seg
    rows = jnp.sum(seg, axis=0)
    padded = ((rows + MOE_TILE - 1) // MOE_TILE) * MOE_TILE
    ends = jnp.cumsum(padded)
    starts = ends - padded
    sorted_start = starts[None, :] + jnp.cumsum(seg, axis=0) - seg
    tbl = jnp.concatenate([local_start, seg, sorted_start], axis=1).reshape(-1).astype(I32)
    tail = jnp.concatenate([starts + rows, padded - rows]).astype(I32)
    max_rows = n_row_tiles * (TOP_K * ROW_TILE + N_EXPERTS * (SUBLANES - 1)) + N_EXPERTS * (MOE_TILE - SUBLANES)
    n_tiles = -(-max_rows // MOE_TILE)
    tile_start = jnp.arange(n_tiles, dtype=I32) * MOE_TILE
    tile_expert = jnp.minimum(jnp.sum((tile_start[:, None] >= ends[None, :]).astype(I32), axis=1), N_EXPERTS - 1)
    n_used = (ends[-1] // MOE_TILE).reshape(1).astype(I32)
    return tbl, tail, tile_expert.astype(I32), n_used, n_tiles


def kernel(x_prompt, x_sample, state_conv, state_h, cache_k, cache_v, g_mix, g_ffn, g_kv, g_final, a_w_gate, a_b_gate, a_w_in, a_b_in, a_conv_w, a_conv_b, a_w_r, a_b_r, a_w_i, a_b_i, a_lam, a_w_out, a_b_out, w_kv, b_kv, rel_bias, b_w_q, b_b_q, b_sinks, b_w_o, b_b_o, f_w_gate, f_w_up, f_w_down, m_w_router, m_w_gate, m_w_up, m_w_down):
    bp, seq, d = x_prompt.shape
    bs, steps, _ = x_sample.shape
    n_prompt = bp * seq
    n_sample = bs * steps
    n = n_prompt + n_sample
    assert seq % MIX_TILE == 0 and seq % WINDOW == 0 and n_prompt % ROW_TILE == 0
    assert n_sample == ROW_TILE and bs % SAMPLE_GROUP == 0

    mix_w = dict(g=_row(g_mix[0]), wg=a_w_gate[0].astype(BF16), bg=_row(a_b_gate[0]),
                 win=a_w_in[0].astype(BF16), bin=_row(a_b_in[0]), cw=a_conv_w[0], cb=_row(a_conv_b[0]),
                 wr=a_w_r[0].astype(BF16), br=_row(a_b_r[0]), wi=a_w_i[0].astype(BF16), bi=_row(a_b_i[0]),
                 lam=_row(a_lam[0]), wout=a_w_out[0].astype(BF16), bout=_row(a_b_out[0]))

    ffn_w = dict(gf=_row(g_ffn[0]), wg=f_w_gate[0].astype(BF16), wu=f_w_up[0].astype(BF16),
                 wd=f_w_down[0].astype(BF16), gkv=_row(g_kv), wkv=w_kv.astype(BF16), bkv=_row(b_kv),
                 gq=_row(g_mix[1]), wq=b_w_q[0].astype(BF16), bq=_row(b_b_q[0]))

    h2_p, kv_p, q_p, p_conv, p_h, moe_wg, moe_wu = _layer0_prompt(x_prompt, mix_w, ffn_w, [m_w_gate[0], m_w_up[0]])
    h1_s, s_conv_tm, s_h = _mixer_sample(jnp.transpose(x_sample, (1, 0, 2)),
                                         jnp.transpose(state_conv[0], (1, 0, 2)), state_h[0], mix_w)
    h2_s, kv_s, q_s = _ffn_rows(jnp.transpose(h1_s, (1, 0, 2)).reshape(n_sample, d), ffn_w)

    bias, bias_rows = _bias_band(rel_bias, steps)
    sinks = b_sinks[0].astype(F32)
    o_p, moe_wd = _attn_prompt(q_p, kv_p, bias, sinks, bp, seq, [m_w_down[0]])
    q4 = q_s.reshape(bs, steps, N_KV, GROUP, HEAD_DIM).transpose(0, 2, 1, 3, 4)
    q4 = q4.reshape(bs, N_KV, steps * GROUP, HEAD_DIM)
    kv_new = kv_s.reshape(bs, steps, 2 * KV_DIM)
    sink_rows = jnp.broadcast_to(sinks.reshape(N_KV, 1, GROUP), (N_KV, steps, GROUP)).reshape(N_KV, steps * GROUP, 1)
    o4, s_kt, s_vt = _attn_sample(q4, jnp.transpose(cache_k, (0, 2, 3, 1)), jnp.transpose(cache_v, (0, 2, 3, 1)),
                                  kv_new, jnp.transpose(kv_new, (0, 2, 1)), bias_rows, sink_rows)
    o_s = o4.reshape(bs, N_KV, steps, GROUP, HEAD_DIM).transpose(0, 2, 1, 3, 4).reshape(n_sample, d)

    wr_pad = jnp.zeros((d, LANES), F32).at[:, :N_EXPERTS].set(m_w_router[0])
    wr_hi = wr_pad.astype(BF16)
    wr_lo = (wr_pad - wr_hi.astype(F32)).astype(BF16)
    h3, xn, route, counts = _oproj_router(
        o_p, o_s, h2_p, h2_s,
        dict(wo=b_w_o[0].astype(BF16), bo=_row(b_b_o[0]), gf=_row(g_ffn[1]), wr_hi=wr_hi, wr_lo=wr_lo))

    tbl, tail, tile_expert, n_used, n_tiles = _routing_tables(counts)
    xs = _sort_place(xn, route, tbl, tail, n_used, n_tiles * MOE_TILE)
    ys = _moe_experts(xs, tile_expert, n_used, moe_wg, moe_wu, moe_wd)
    y_p, y_s = _combine_final(h3, route, tbl, ys, _row(g_final), n_prompt)

    y_prompt = y_p.reshape(bp, seq, d)
    y_sample = y_s.reshape(bs, steps, d)
    kv_last = jnp.stack([kv_p[b * seq + seq - WINDOW:(b + 1) * seq] for b in range(bp)])
    p_k = kv_last[:, :, :KV_DIM].reshape(bp, WINDOW, N_KV, HEAD_DIM)
    p_v = kv_last[:, :, KV_DIM:].reshape(bp, WINDOW, N_KV, HEAD_DIM)
    s_k = jnp.transpose(s_kt, (0, 3, 1, 2))
    s_v = jnp.transpose(s_vt, (0, 3, 1, 2))
    return (y_prompt, y_sample, p_conv[None], p_h.reshape(1, bp, LRU_WIDTH), p_k, p_v,
            jnp.transpose(s_conv_tm, (1, 0, 2))[None], s_h[None], s_k, s_v)
```

```python
import functools
import math

import jax
import jax.numpy as jnp
from jax import lax
from jax.experimental import pallas as pl
from jax.experimental.pallas import tpu as pltpu

D_MODEL = 1024
LRU_WIDTH = D_MODEL
LRU_BLOCK_W = 256
LRU_BLOCKS = LRU_WIDTH // LRU_BLOCK_W
CONV_W = 4
LRU_C = 8.0
HEAD_DIM = 64
N_HEADS = D_MODEL // HEAD_DIM
N_KV = 2
GROUP = N_HEADS // N_KV
KV_DIM = N_KV * HEAD_DIM
WINDOW = 128
NUM_BUCKETS = 32
MAX_DISTANCE = 128
D_FF = 3 * D_MODEL
N_EXPERTS = 8
TOP_K = 2
D_EXPERT = 7 * D_MODEL // 2
EPS = 1e-6
NEG = -1e30

BF16 = jnp.bfloat16
F32 = jnp.float32
I32 = jnp.int32

SUBLANES = 8
LANES = 128
VMEM_LIMIT_BYTES = 56 * 1024 * 1024

ROW_TILE = 512
MIX_TILE = 256
FF_CHUNK = 1024
MOE_TILE = 512
MOE_CHUNKS = 2


def _params(*semantics):
    return pltpu.CompilerParams(dimension_semantics=semantics, vmem_limit_bytes=VMEM_LIMIT_BYTES)


def _const_spec(shape):
    zeros = (0,) * len(shape)
    return pl.BlockSpec(shape, lambda *_: zeros, pipeline_mode=pl.Buffered(1))


def _dot(a, b):
    return jnp.dot(a, b, preferred_element_type=F32)


def _rms(x, g):
    ms = jnp.mean(x * x, axis=-1, keepdims=True)
    return x * lax.rsqrt(ms + EPS) * g


def _sigmoid(x):
    return 1.0 / (1.0 + jnp.exp(-x))


def _gelu_tanh(x):
    return 0.5 * x * (1.0 + jnp.tanh(0.7978845608028654 * (x + 0.044715 * (x * x * x))))


def _log_sigmoid(x):
    return jnp.minimum(x, 0.0) - jnp.log1p(jnp.exp(-jnp.abs(x)))


def _lru_gates(xc, wr_ref, br, wi_ref, bi, lam):
    xcb = xc.astype(BF16)
    rs, gs = [], []
    for n in range(LRU_BLOCKS):
        xn = xcb[:, n * LRU_BLOCK_W:(n + 1) * LRU_BLOCK_W]
        rs.append(_dot(xn, wr_ref[n]))
        gs.append(_dot(xn, wi_ref[n]))
    r = _sigmoid(jnp.concatenate(rs, axis=1) + br)
    i = _sigmoid(jnp.concatenate(gs, axis=1) + bi)
    log_a = LRU_C * r * _log_sigmoid(lam)
    a = jnp.exp(log_a)
    mult = jnp.sqrt(1.0 - a * a)
    return a, mult * (i * xc)


def _interleave(order, *stage_generators):
    results = [None] * len(stage_generators)
    finished = set()

    def advance(idx):
        if idx in finished:
            return
        try:
            next(stage_generators[idx])
        except StopIteration as done:
            results[idx] = done.value
            finished.add(idx)

    for idx in order:
        advance(idx)
    while len(finished) < len(stage_generators):
        for idx in range(len(stage_generators)):
            advance(idx)
    return results


def _mixer_stages(x, g_ref, wg_ref, bg_ref, win_ref, bin_ref, cw_ref, cb_ref,
                  wr_ref, br_ref, wi_ref, bi_ref, lam_ref, wout_ref, bout_ref, xr_buf, h_carry):
    tt = x.shape[0]
    pad = SUBLANES
    u = _rms(x, g_ref[...]).astype(BF16)
    gate = _gelu_tanh(_dot(u, wg_ref[...]) + bg_ref[...])
    xr = _dot(u, win_ref[...]) + bin_ref[...]
    yield
    xr_buf[pad:pad + tt, :] = xr
    xc = cb_ref[...] + cw_ref[CONV_W - 1:CONV_W, :] * xr
    for k in range(CONV_W - 1):
        back = CONV_W - 1 - k
        xc = xc + cw_ref[k:k + 1, :] * xr_buf[pad - back:pad - back + tt, :]
    xr_buf[0:pad, :] = xr[tt - pad:tt, :]
    yield

    a, b = _lru_gates(xc, wr_ref, br_ref[...], wi_ref, bi_ref[...], lam_ref[...])
    yield

    groups = tt // SUBLANES
    a3 = a.reshape(groups, SUBLANES, LRU_WIDTH)
    b3 = b.reshape(groups, SUBLANES, LRU_WIDTH)
    row = lax.broadcasted_iota(I32, (1, SUBLANES, LRU_WIDTH), 1)
    step = 1
    while step < SUBLANES:
        keep = row >= step
        a_prev = jnp.where(keep, pltpu.roll(a3, step, axis=1), 1.0)
        b_prev = jnp.where(keep, pltpu.roll(b3, step, axis=1), 0.0)
        b3 = b3 + a3 * b_prev
        a3 = a3 * a_prev
        step *= 2
    yield
    h_prev = h_carry[0:1, :]
    hs = []
    for gi in range(groups):
        hg = b3[gi] + a3[gi] * h_prev
        hs.append(hg)
        h_prev = hg[SUBLANES - 1:SUBLANES, :]
        if gi + 1 == groups // 2:
            yield
    h = jnp.concatenate(hs, axis=0)
    h_carry[0:1, :] = h_prev
    yield
    y = _dot((h * gate).astype(BF16), wout_ref[...]) + bout_ref[...]
    return x + y, xr[tt - (CONV_W - 1):tt, :], h_prev


LAYER0_ORDER = (0, 1, 0, 0, 1, 0, 0, 1, 0, 0)
N_MIX_REFS = 14
N_FFN_REFS = 10


def _ffn_stages(h1, gf_ref, wg_ref, wu_ref, wd_ref, gkv_ref, wkv_ref, bkv_ref, gq_ref, wq_ref, bq_ref):
    u = _rms(h1, gf_ref[...]).astype(BF16)
    acc = h1
    n_chunks = D_FF // FF_CHUNK
    for c in range(n_chunks):
        cols = slice(c * FF_CHUNK, (c + 1) * FF_CHUNK)
        g = _dot(u, wg_ref[:, cols])
        up = _dot(u, wu_ref[:, cols])
        mid = (g * _sigmoid(g) * up).astype(BF16)
        acc = acc + _dot(mid, wd_ref[cols, :])
        if c + 1 < n_chunks:
            yield
    kv = _dot(_rms(acc, gkv_ref[...]).astype(BF16), wkv_ref[...]) + bkv_ref[...]
    q = _dot(_rms(acc, gq_ref[...]).astype(BF16), wq_ref[...]) + bq_ref[...]
    return acc, kv, (q * (HEAD_DIM ** -0.5)).astype(BF16)


def _layer0_prompt_kernel(*refs, n_tiles, tiles_per_seq, n_casts):
    x_ref = refs[0]
    mix_refs = refs[1:1 + N_MIX_REFS]
    ffn_refs = refs[1 + N_MIX_REFS:1 + N_MIX_REFS + N_FFN_REFS]
    rest = refs[1 + N_MIX_REFS + N_FFN_REFS:]
    cast_srcs, rest = rest[:n_casts], rest[n_casts:]
    h2_ref, kv_ref, q_ref, conv_ref, hlast_ref = rest[:5]
    cast_dsts = rest[5:5 + n_casts]
    xr_buf, h_carry, h1_buf = rest[5 + n_casts:]
    for src_ref, dst_ref in zip(cast_srcs, cast_dsts):
        dst_ref[...] = src_ref[...].astype(BF16)
    s = pl.program_id(0)
    slot = lax.rem(s, 2)

    @pl.when(s == 0)
    def _():
        h1_buf[...] = jnp.zeros_like(h1_buf)

    @pl.when(lax.rem(s, tiles_per_seq) == 0)
    def _():
        xr_buf[0:SUBLANES, :] = jnp.zeros((SUBLANES, LRU_WIDTH), F32)
        h_carry[...] = jnp.zeros_like(h_carry)

    (h1, conv_tail, h_last), (h2, kv, q) = _interleave(
        LAYER0_ORDER,
        _mixer_stages(x_ref[...], *mix_refs, xr_buf, h_carry), _ffn_stages(h1_buf[1 - slot], *ffn_refs))
    h2_ref[...] = h2
    kv_ref[...] = kv
    q_ref[...] = q
    h1_buf[slot] = h1

    @pl.when(s < n_tiles)
    def _():
        b = s // tiles_per_seq
        conv_ref[b] = conv_tail
        hlast_ref[b] = h_last


def _layer0_prompt(x, mix_w, ffn_w, cast_ws):
    b, t, d = x.shape
    tt = MIX_TILE
    n_tiles = (b * t) // tt
    casts = [_side_cast_specs(w, n_tiles, lambda s: jnp.minimum(s, n_tiles - 1)) for w in cast_ws]
    vec = lambda n: _const_spec((1, n))
    mix_specs = [
        vec(d), _const_spec((d, LRU_WIDTH)), vec(LRU_WIDTH), _const_spec((d, LRU_WIDTH)), vec(LRU_WIDTH),
        _const_spec((CONV_W, LRU_WIDTH)), vec(LRU_WIDTH),
        _const_spec((LRU_BLOCKS, LRU_BLOCK_W, LRU_BLOCK_W)), vec(LRU_WIDTH),
        _const_spec((LRU_BLOCKS, LRU_BLOCK_W, LRU_BLOCK_W)), vec(LRU_WIDTH),
        vec(LRU_WIDTH), _const_spec((LRU_WIDTH, d)), vec(d),
    ]
    ffn_specs = [
        vec(d), _const_spec((d, D_FF)), _const_spec((d, D_FF)), _const_spec((D_FF, d)),
        vec(d), _const_spec((d, 2 * KV_DIM)), vec(2 * KV_DIM), vec(d), _const_spec((d, d)), vec(d),
    ]
    assert len(mix_specs) == N_MIX_REFS and len(ffn_specs) == N_FFN_REFS
    prev = lambda s: (jnp.maximum(s - 1, 0), 0)
    whole = lambda shape: pl.BlockSpec(shape, lambda s: (0,) * len(shape))
    return pl.pallas_call(
        functools.partial(_layer0_prompt_kernel, n_tiles=n_tiles, tiles_per_seq=t // tt, n_casts=len(casts)),
        grid=(n_tiles + 1,),
        in_specs=([pl.BlockSpec((tt, d), lambda s: (jnp.minimum(s, n_tiles - 1), 0))] + mix_specs + ffn_specs
                  + [c[0] for c in casts]),
        out_specs=[pl.BlockSpec((tt, d), prev), pl.BlockSpec((tt, 2 * KV_DIM), prev), pl.BlockSpec((tt, d), prev),
                   whole((b, CONV_W - 1, LRU_WIDTH)), whole((b, 1, LRU_WIDTH))] + [c[1] for c in casts],
        out_shape=[jax.ShapeDtypeStruct((b * t, d), F32), jax.ShapeDtypeStruct((b * t, 2 * KV_DIM), F32),
                   jax.ShapeDtypeStruct((b * t, d), BF16),
                   jax.ShapeDtypeStruct((b, CONV_W - 1, LRU_WIDTH), F32),
                   jax.ShapeDtypeStruct((b, 1, LRU_WIDTH), F32)] + [c[2] for c in casts],
        scratch_shapes=[pltpu.VMEM((SUBLANES + tt, LRU_WIDTH), F32), pltpu.VMEM((SUBLANES, LRU_WIDTH), F32),
                        pltpu.VMEM((2, tt, d), F32)],
        compiler_params=_params("arbitrary"),
        name="layer0_prompt",
    )(x.reshape(b * t, d), *[mix_w[k] for k in MIX_KEYS], *[ffn_w[k] for k in FFN_KEYS], *cast_ws)


MIX_KEYS = ("g", "wg", "bg", "win", "bin", "cw", "cb", "wr", "br", "wi", "bi", "lam", "wout", "bout")
FFN_KEYS = ("gf", "wg", "wu", "wd", "gkv", "wkv", "bkv", "gq", "wq", "bq")


def _mixer_sample_kernel(x_ref, cs_ref, h0_ref, g_ref, wg_ref, bg_ref, win_ref, bin_ref, cw_ref, cb_ref,
                         wr_ref, br_ref, wi_ref, bi_ref, lam_ref, wout_ref, bout_ref,
                         h1_ref, conv_ref, hlast_ref):
    steps, nb, d = x_ref.shape
    x = x_ref[...].reshape(steps * nb, d)
    u = _rms(x, g_ref[...]).astype(BF16)
    gate = _gelu_tanh(_dot(u, wg_ref[...]) + bg_ref[...])
    xr = _dot(u, win_ref[...]) + bin_ref[...]
    xpad = [cs_ref[k] for k in range(CONV_W - 1)] + [xr[s * nb:(s + 1) * nb, :] for s in range(steps)]
    xcs = []
    for s in range(steps):
        acc = cb_ref[...] + cw_ref[0:1, :] * xpad[s]
        for k in range(1, CONV_W):
            acc = acc + cw_ref[k:k + 1, :] * xpad[s + k]
        xcs.append(acc)
    for k in range(CONV_W - 1):
        conv_ref[k] = xpad[steps + k]
    xc = jnp.concatenate(xcs, axis=0)
    a, b = _lru_gates(xc, wr_ref, br_ref[...], wi_ref, bi_ref[...], lam_ref[...])
    h = h0_ref[...]
    hs = []
    for s in range(steps):
        h = a[s * nb:(s + 1) * nb, :] * h + b[s * nb:(s + 1) * nb, :]
        hs.append(h)
    hlast_ref[...] = h
    hcat = jnp.concatenate(hs, axis=0)
    y = _dot((hcat * gate).astype(BF16), wout_ref[...]) + bout_ref[...]
    h1_ref[...] = (x + y).reshape(steps, nb, d)


def _mixer_sample(x_tm, cs_tm, h0, w):
    steps, nb, d = x_tm.shape
    full = lambda shape: pl.BlockSpec(shape, lambda i: (0,) * len(shape))
    vec = lambda n: full((1, n))
    in_specs = [
        full((steps, nb, d)), full((CONV_W - 1, nb, LRU_WIDTH)), full((nb, LRU_WIDTH)),
        vec(d), full((d, LRU_WIDTH)), vec(LRU_WIDTH), full((d, LRU_WIDTH)), vec(LRU_WIDTH),
        full((CONV_W, LRU_WIDTH)), vec(LRU_WIDTH),
        full((LRU_BLOCKS, LRU_BLOCK_W, LRU_BLOCK_W)), vec(LRU_WIDTH),
        full((LRU_BLOCKS, LRU_BLOCK_W, LRU_BLOCK_W)), vec(LRU_WIDTH),
        vec(LRU_WIDTH), full((LRU_WIDTH, d)), vec(d),
    ]
    out_specs = [full((steps, nb, d)), full((CONV_W - 1, nb, LRU_WIDTH)), full((nb, LRU_WIDTH))]
    out_shape = [
        jax.ShapeDtypeStruct((steps, nb, d), F32),
        jax.ShapeDtypeStruct((CONV_W - 1, nb, LRU_WIDTH), F32),
        jax.ShapeDtypeStruct((nb, LRU_WIDTH), F32),
    ]
    return pl.pallas_call(
        _mixer_sample_kernel, grid=(1,), in_specs=in_specs, out_specs=out_specs, out_shape=out_shape,
        compiler_params=_params("arbitrary"), name="mixer_sample",
    )(x_tm, cs_tm, h0, w["g"], w["wg"], w["bg"], w["win"], w["bin"], w["cw"], w["cb"], w["wr"], w["br"],
      w["wi"], w["bi"], w["lam"], w["wout"], w["bout"])


def _two_part_specs(n_prompt, n_sample, width):
    assert n_sample == ROW_TILE and n_prompt % ROW_TILE == 0
    last_prompt = n_prompt // ROW_TILE - 1
    return [pl.BlockSpec((ROW_TILE, width), lambda i: (jnp.minimum(i, last_prompt), 0)),
            pl.BlockSpec((ROW_TILE, width), lambda i: (0, 0))]


def _two_part_tile(prompt_ref, sample_ref):
    is_sample = pl.program_id(0) == pl.num_programs(0) - 1
    return jnp.where(is_sample, sample_ref[...], prompt_ref[...])


def _side_cast_specs(w, n_steps, step_of):
    ne, rows, cols = w.shape
    per_expert = n_steps // ne
    blk = rows // per_expert
    assert per_expert * ne == n_steps and blk * per_expert == rows and blk % (2 * SUBLANES) == 0
    index = lambda *ids: (step_of(*ids) // per_expert, step_of(*ids) % per_expert, 0)
    spec = pl.BlockSpec((1, blk, cols), index)
    return spec, spec, jax.ShapeDtypeStruct(w.shape, BF16)


def _ffn_rows_kernel(h1_ref, *refs):
    h2_ref, kv_ref, q_ref = refs[N_FFN_REFS:]
    (h2_ref[...], kv_ref[...], q_ref[...]), = _interleave((), _ffn_stages(h1_ref[...], *refs[:N_FFN_REFS]))


def _ffn_rows(h1, w):
    n, d = h1.shape
    tm = ROW_TILE
    row = lambda width: pl.BlockSpec((tm, width), lambda i: (i, 0))
    vec = lambda width: _const_spec((1, width))
    in_specs = [
        row(d), vec(d), _const_spec((d, D_FF)), _const_spec((d, D_FF)), _const_spec((D_FF, d)),
        vec(d), _const_spec((d, 2 * KV_DIM)), vec(2 * KV_DIM), vec(d), _const_spec((d, d)), vec(d),
    ]
    return pl.pallas_call(
        _ffn_rows_kernel,
        grid=(n // tm,),
        in_specs=in_specs,
        out_specs=[row(d), row(2 * KV_DIM), row(d)],
        out_shape=[jax.ShapeDtypeStruct((n, d), F32), jax.ShapeDtypeStruct((n, 2 * KV_DIM), F32),
                   jax.ShapeDtypeStruct((n, d), BF16)],
        compiler_params=_params("arbitrary"),
        name="ffn_rows",
    )(h1, *[w[k] for k in FFN_KEYS])


def _head_of(j, parity, pair):
    return j * GROUP + 2 * pair + parity


def _distance_bias(dist, table_ref, head):
    max_exact = NUM_BUCKETS // 2
    n = jnp.maximum(dist, 0)
    large = jnp.full(dist.shape, max_exact, I32)
    for step in range(1, NUM_BUCKETS - max_exact):
        threshold = math.ceil(max_exact * (MAX_DISTANCE / max_exact) ** (step / (NUM_BUCKETS - max_exact)))
        large = large + jnp.where(n >= threshold, 1, 0)
    bucket = jnp.where(n < max_exact, n, large)
    acc = jnp.zeros(dist.shape, F32)
    for bkt in range(NUM_BUCKETS):
        acc = jnp.where(bucket == bkt, table_ref[bkt, head], acc)
    return acc


def _bias_band_kernel(table_ref, folded_ref, rows_ref):
    w = WINDOW
    ci = lax.broadcasted_iota(I32, (w, w), 0)
    qi = lax.broadcasted_iota(I32, (w, w), 1)
    folded_dist = jnp.where(ci > qi, qi + w - ci, qi - ci)
    for j in range(N_KV):
        for parity in range(2):
            for pair in range(GROUP // 2):
                folded_ref[j, parity, :, pair * w:(pair + 1) * w] = _distance_bias(
                    folded_dist, table_ref, _head_of(j, parity, pair))
    n_rows, n_keys = rows_ref.shape[1], rows_ref.shape[2]
    ri = lax.broadcasted_iota(I32, (GROUP, n_keys), 0)
    si = lax.broadcasted_iota(I32, (GROUP, n_keys), 1)
    for j in range(N_KV):
        for t in range(n_rows // GROUP):
            acc = jnp.zeros((GROUP, n_keys), F32)
            for g in range(GROUP):
                acc = jnp.where(ri == g, _distance_bias(t + w - si, table_ref, j * GROUP + g), acc)
            rows_ref[j, t * GROUP:(t + 1) * GROUP, :] = acc


def _bias_band(rel_table, steps):
    folded = (N_KV, 2, WINDOW, (GROUP // 2) * WINDOW)
    rows = (N_KV, steps * GROUP, WINDOW + steps + (-steps) % SUBLANES)
    return pl.pallas_call(
        _bias_band_kernel,
        grid=(1,),
        in_specs=[pl.BlockSpec(memory_space=pltpu.SMEM)],
        out_specs=[pl.BlockSpec(folded, lambda i: (0, 0, 0, 0)), pl.BlockSpec(rows, lambda i: (0, 0, 0))],
        out_shape=[jax.ShapeDtypeStruct(folded, F32), jax.ShapeDtypeStruct(rows, F32)],
        compiler_params=_params("arbitrary"),
        name="bias_band",
    )(rel_table)


def _softmax_sink_pv(s, sink, v):
    m = jnp.maximum(jnp.max(s, axis=-1, keepdims=True), sink)
    p = jnp.exp(s - m)
    denom = jnp.sum(p, axis=-1, keepdims=True) + jnp.exp(sink - m)
    return _dot(p.astype(BF16), v) / denom


ATTN_BLOCKS = 8


def _attn_prompt_kernel(sink_ref, q_ref, kvp_ref, kvc_ref, bias_ref, *refs):
    n_casts = (len(refs) - 1) // 2
    o_ref = refs[n_casts]
    for src_ref, dst_ref in zip(refs[:n_casts], refs[n_casts + 1:]):
        dst_ref[...] = src_ref[...].astype(BF16)
    w = WINDOW
    pairs = GROUP // 2
    assert 2 * HEAD_DIM == LANES and KV_DIM == LANES
    ci = lax.broadcasted_iota(I32, (w, pairs * w), 0)
    qi = lax.broadcasted_iota(I32, (w, pairs * w), 1) % w
    from_prev = ci > qi
    has_prev = pl.program_id(1) > 0

    kv = jnp.concatenate([kvp_ref[...], kvc_ref[...]], axis=0)
    low = lax.broadcasted_iota(I32, (kv.shape[0], LANES), 1) < HEAD_DIM

    def halves(x):
        swapped = pltpu.roll(x, HEAD_DIM, axis=1)
        return (((jnp.where(low, x, 0.0)).astype(BF16), (jnp.where(low, 0.0, swapped)).astype(BF16)),
                ((jnp.where(low, swapped, 0.0)).astype(BF16), (jnp.where(low, 0.0, x)).astype(BF16)))

    k_ops = halves(kv[:, 0:KV_DIM])
    v_ops = halves(kv[:, KV_DIM:2 * KV_DIM])

    sinks = [[jnp.concatenate([jnp.full((1, w), sink_ref[_head_of(j, parity, b)], F32) for b in range(pairs)], axis=1)
              for parity in range(2)] for j in range(N_KV)]
    nt = (((1,), (1,)), ((), ()))
    tn = (((0,), (0,)), ((), ()))
    for i in range(ATTN_BLOCKS):
        q_rows = slice(i * w, (i + 1) * w)
        prev_rows = slice(i * w, (i + 1) * w)
        own_rows = slice((i + 1) * w, (i + 2) * w)
        for j in range(N_KV):
            q4 = jnp.concatenate([q_ref[q_rows, (j * pairs + b) * LANES:(j * pairs + b + 1) * LANES]
                                  for b in range(pairs)], axis=0)
            acc = None
            for parity in range(2):
                k_op, v_op, sink = k_ops[j][parity], v_ops[j][parity], sinks[j][parity]
                s_prev = lax.dot_general(k_op[prev_rows], q4, nt, preferred_element_type=F32)
                s_own = lax.dot_general(k_op[own_rows], q4, nt, preferred_element_type=F32)
                if i == 0:
                    s_prev = jnp.where(has_prev, s_prev, NEG)
                s = jnp.where(from_prev, s_prev, s_own) + bias_ref[j, parity]
                m = jnp.maximum(jnp.max(s, axis=0, keepdims=True), sink)
                p = jnp.exp(s - m)
                denom = jnp.sum(p, axis=0, keepdims=True) + jnp.exp(sink - m)
                p = p * (1.0 / denom)
                o = (lax.dot_general(jnp.where(from_prev, p, 0.0).astype(BF16), v_op[prev_rows], tn,
                                     preferred_element_type=F32)
                     + lax.dot_general(jnp.where(from_prev, 0.0, p).astype(BF16), v_op[own_rows], tn,
                                       preferred_element_type=F32))
                acc = o if acc is None else acc + o
            for b in range(pairs):
                o_ref[q_rows, (j * pairs + b) * LANES:(j * pairs + b + 1) * LANES] = acc[b * w:(b + 1) * w].astype(BF16)


def _attn_prompt(q, kv, bias, sinks, batch, seq, cast_ws):
    nb = seq // WINDOW
    assert nb % ATTN_BLOCKS == 0
    ns = nb // ATTN_BLOCKS
    rows = ATTN_BLOCKS * WINDOW
    d = q.shape[1]
    casts = [_side_cast_specs(w, batch * ns, lambda b, n: b * ns + n) for w in cast_ws]
    in_specs = [
        pl.BlockSpec(memory_space=pltpu.SMEM),
        pl.BlockSpec((rows, d), lambda b, n: (b * ns + n, 0)),
        pl.BlockSpec((WINDOW, 2 * KV_DIM), lambda b, n: (b * nb + jnp.maximum(n * ATTN_BLOCKS - 1, 0), 0)),
        pl.BlockSpec((rows, 2 * KV_DIM), lambda b, n: (b * ns + n, 0)),
        _const_spec(bias.shape),
    ] + [c[0] for c in casts]
    return pl.pallas_call(
        _attn_prompt_kernel,
        grid=(batch, ns),
        in_specs=in_specs,
        out_specs=[pl.BlockSpec((rows, d), lambda b, n: (b * ns + n, 0))] + [c[1] for c in casts],
        out_shape=[jax.ShapeDtypeStruct((batch * seq, d), BF16)] + [c[2] for c in casts],
        compiler_params=_params("arbitrary", "arbitrary"),
        name="attn_prompt",
    )(sinks, q, kv, kv, bias, *cast_ws)


SAMPLE_GROUP = 8


def _attn_sample_kernel(q_ref, ckt_ref, cvt_ref, new_ref, newt_ref, bias_ref, sink_ref, o_ref, skt_ref, svt_ref):
    gb = q_ref.shape[0]
    steps = new_ref.shape[1]
    rows = steps * GROUP
    pad = (-steps) % SUBLANES
    ri = lax.broadcasted_iota(I32, (rows, WINDOW + steps + pad), 0)
    si = lax.broadcasted_iota(I32, (rows, WINDOW + steps + pad), 1)
    dist = ri // GROUP + WINDOW - si
    mask = (dist >= 0) & (dist < WINDOW)
    mask_c, mask_n = mask[:, :WINDOW], mask[:, WINDOW:]
    zeros = jnp.zeros((pad, 2 * KV_DIM), F32)

    def scores(b, j):
        new = jnp.concatenate([new_ref[b], zeros], axis=0).astype(BF16)
        q = q_ref[b, j]
        s_c = _dot(q, ckt_ref[b, j].astype(BF16))
        s_n = lax.dot_general(q, new[:, j * HEAD_DIM:(j + 1) * HEAD_DIM], (((1,), (1,)), ((), ())),
                              preferred_element_type=F32)
        bias = bias_ref[j]
        return (jnp.where(mask_c, s_c + bias[:, :WINDOW], NEG), jnp.where(mask_n, s_n + bias[:, WINDOW:], NEG), new)

    tasks = [(b, j) for b in range(gb) for j in range(N_KV)]
    all_scores = [scores(b, j) for b, j in tasks]
    for (b, j), (s_c, s_n, new) in zip(tasks, all_scores):
        sink = sink_ref[j]
        m = jnp.maximum(jnp.maximum(jnp.max(s_c, axis=-1, keepdims=True), jnp.max(s_n, axis=-1, keepdims=True)), sink)
        p_c = jnp.exp(s_c - m)
        p_n = jnp.exp(s_n - m)
        denom = jnp.sum(p_c, axis=-1, keepdims=True) + jnp.sum(p_n, axis=-1, keepdims=True) + jnp.exp(sink - m)
        pv = lax.dot_general(p_c.astype(BF16), cvt_ref[b, j].astype(BF16), (((1,), (1,)), ((), ())),
                             preferred_element_type=F32)
        pv = pv + _dot(p_n.astype(BF16), new[:, KV_DIM + j * HEAD_DIM:KV_DIM + (j + 1) * HEAD_DIM])
        o_ref[b, j] = (pv / denom).astype(BF16)

    lane = lax.broadcasted_iota(I32, (HEAD_DIM, WINDOW), 1)
    for b in range(gb):
        newt = jnp.concatenate([newt_ref[b], jnp.zeros((2 * KV_DIM, WINDOW - steps), F32)], axis=1)
        for j in range(N_KV):
            for src_ref, dst_ref, base in ((ckt_ref, skt_ref, 0), (cvt_ref, svt_ref, KV_DIM)):
                fresh = newt[base + j * HEAD_DIM:base + (j + 1) * HEAD_DIM, :]
                dst_ref[b, j] = pltpu.roll(jnp.where(lane < steps, fresh, src_ref[b, j]), WINDOW - steps, axis=1)


def _attn_sample(q4, ckt, cvt, new, newt, bias_rows, sink_rows):
    nb, _, rows, hd = q4.shape
    steps = new.shape[1]
    keys = bias_rows.shape[2]
    gb = SAMPLE_GROUP
    cache_spec = pl.BlockSpec((gb, N_KV, hd, WINDOW), lambda i: (i, 0, 0, 0))
    in_specs = [
        pl.BlockSpec((gb, N_KV, rows, hd), lambda i: (i, 0, 0, 0)),
        cache_spec, cache_spec,
        pl.BlockSpec((gb, steps, 2 * KV_DIM), lambda i: (i, 0, 0)),
        pl.BlockSpec((gb, 2 * KV_DIM, steps), lambda i: (i, 0, 0)),
        pl.BlockSpec((N_KV, rows, keys), lambda i: (0, 0, 0)),
        pl.BlockSpec((N_KV, rows, 1), lambda i: (0, 0, 0)),
    ]
    return pl.pallas_call(
        _attn_sample_kernel,
        grid=(nb // gb,),
        in_specs=in_specs,
        out_specs=[pl.BlockSpec((gb, N_KV, rows, hd), lambda i: (i, 0, 0, 0)), cache_spec, cache_spec],
        out_shape=[jax.ShapeDtypeStruct(q4.shape, BF16), jax.ShapeDtypeStruct(ckt.shape, F32),
                   jax.ShapeDtypeStruct(cvt.shape, F32)],
        compiler_params=_params("arbitrary"),
        name="attn_sample",
    )(q4, ckt, cvt, new, newt, bias_rows, sink_rows)


def _split_bf16(x):
    hi = x.astype(BF16)
    lo = (x - hi.astype(F32)).astype(BF16)
    return hi, lo


ROUTER_PARTS = 2


def _oproj_router_kernel(op_ref, os_ref, h2p_ref, h2s_ref, wo_ref, bo_ref, gf_ref, wr3_ref,
                         h3_ref, xn_ref, route_ref, counts_ref):
    tm = h3_ref.shape[0]
    parts = ROUTER_PARTS
    pr = tm // parts
    o = _two_part_tile(op_ref, os_ref)
    h2 = _two_part_tile(h2p_ref, h2s_ref)
    lane = lax.broadcasted_iota(I32, (pr, LANES), 1)
    lanef = lane.astype(F32)
    ri = lax.broadcasted_iota(I32, (pr, pr), 0)
    ci = lax.broadcasted_iota(I32, (pr, pr), 1)
    earlier = jnp.where(ri > ci, 1.0, 0.0).astype(BF16)

    picks, ranks, gates = [], [], []
    cnt = jnp.zeros((1, LANES), F32)
    for p in range(parts):
        rows = slice(p * pr, (p + 1) * pr)
        h3 = h2[rows, :] + _dot(o[rows, :], wo_ref[...]) + bo_ref[...]
        h3_ref[rows, :] = h3
        u_hi, u_lo = _split_bf16(_rms(h3, gf_ref[...]))
        xn_ref[rows, :] = u_hi
        logits = _dot(jnp.concatenate([u_hi, u_lo, u_hi], axis=1), wr3_ref[...])
        logits = jnp.where(lane < N_EXPERTS, logits, -jnp.inf)
        v1 = jnp.max(logits, axis=-1, keepdims=True)
        e1 = jnp.min(jnp.where(logits == v1, lanef, float(LANES)), axis=-1, keepdims=True)
        rest = jnp.where(lanef == e1, -jnp.inf, logits)
        v2 = jnp.max(rest, axis=-1, keepdims=True)
        e2 = jnp.min(jnp.where(rest == v2, lanef, float(LANES)), axis=-1, keepdims=True)
        ex = jnp.exp(v2 - v1)
        gates.append((1.0 / (1.0 + ex), ex / (1.0 + ex)))
        pick1 = lanef == e1
        pick2 = lanef == e2
        picks.append((pick1, pick2))
        sel = jnp.where(pick1 | pick2, 1.0, 0.0)
        ranks.append(_dot(earlier, sel.astype(BF16)) + cnt)
        cnt = cnt + jnp.sum(sel, axis=0, keepdims=True)

    seg = jnp.floor((cnt + (SUBLANES - 1)) * (1.0 / SUBLANES)) * SUBLANES
    ek = lax.broadcasted_iota(I32, (LANES, LANES), 0)
    el = lax.broadcasted_iota(I32, (LANES, LANES), 1)
    lower_experts = jnp.where(ek < el, 1.0, 0.0).astype(BF16)
    seg_start = _dot(jnp.broadcast_to(seg, (SUBLANES, LANES)).astype(BF16), lower_experts)[0:1, :]
    for p in range(parts):
        local = ranks[p] + seg_start
        lr1 = jnp.sum(jnp.where(picks[p][0], local, 0.0), axis=-1, keepdims=True)
        lr2 = jnp.sum(jnp.where(picks[p][1], local, 0.0), axis=-1, keepdims=True)
        w1, w2 = gates[p]
        route_ref[p * pr:(p + 1) * pr, :] = jnp.where(
            lane == 0, lr1, jnp.where(lane == 1, lr2, jnp.where(lane == 2, w1, jnp.where(lane == 3, w2, 0.0))))
    counts_ref[0] = jnp.broadcast_to(cnt, (SUBLANES, LANES))


def _oproj_router(o_p, o_s, h2_p, h2_s, w):
    d = h2_p.shape[1]
    n = h2_p.shape[0] + h2_s.shape[0]
    tm = ROW_TILE
    row = lambda width: pl.BlockSpec((tm, width), lambda i: (i, 0))
    in_specs = (_two_part_specs(o_p.shape[0], o_s.shape[0], d) + _two_part_specs(h2_p.shape[0], h2_s.shape[0], d) + [
        _const_spec((d, d)), _const_spec((1, d)), _const_spec((1, d)), _const_spec((3 * d, LANES))])
    return pl.pallas_call(
        _oproj_router_kernel,
        grid=(n // tm,),
        in_specs=in_specs,
        out_specs=[row(d), row(d), row(LANES), pl.BlockSpec((1, SUBLANES, LANES), lambda i: (i, 0, 0))],
        out_shape=[jax.ShapeDtypeStruct((n, d), F32), jax.ShapeDtypeStruct((n, d), BF16),
                   jax.ShapeDtypeStruct((n, LANES), F32), jax.ShapeDtypeStruct((n // tm, SUBLANES, LANES), F32)],
        compiler_params=_params("arbitrary"),
        name="oproj_router",
    )(o_p, o_s, h2_p, h2_s, w["wo"], w["bo"], w["gf"], w["wr3"])


LOCAL_ROWS = 1152
XS_WIDTH = D_MODEL + LANES
SEG_TABLE = 3 * N_EXPERTS
SEG_PIECES = tuple(SUBLANES << b for b in reversed(range(7)))


def _segment_copies(src_ref, src_start, dst_ref, dst_start, length, sem, act):
    for piece in SEG_PIECES:
        done = lax.div(length, 2 * piece) * (2 * piece)

        @pl.when(lax.rem(lax.div(length, piece), 2) != 0)
        def _():
            s = pl.multiple_of(src_start + done, SUBLANES)
            t = pl.multiple_of(dst_start + done, SUBLANES)
            act(pltpu.make_async_copy(src_ref.at[pl.ds(s, piece), :], dst_ref.at[pl.ds(t, piece), :], sem))


def _sort_place_kernel(tbl_ref, tail_ref, nu_ref, xn_ref, route_ref, xs_ref, stage, zeros, sems):
    j = pl.program_id(0)
    tm = xn_ref.shape[0]
    xn = xn_ref[...]
    rt = jnp.transpose(route_ref[...])
    row = lax.broadcasted_iota(I32, (LOCAL_ROWS, tm), 0).astype(F32)
    m1 = row == rt[0:1, :]
    m2 = row == rt[1:2, :]
    onehot = jnp.where(m1 | m2, 1.0, 0.0).astype(BF16)
    slot = lax.rem(j, 2)
    buf = stage.at[slot]
    buf[:, 0:D_MODEL] = _dot(onehot, xn)
    gate = jnp.sum(jnp.where(m1, rt[2:3, :], 0.0) + jnp.where(m2, rt[3:4, :], 0.0), axis=-1, keepdims=True)
    buf[:, D_MODEL:XS_WIDTH] = jnp.broadcast_to(gate, (LOCAL_ROWS, LANES))

    def segments(tile, tile_slot, act):
        for e in range(N_EXPERTS):
            base = tile * SEG_TABLE
            _segment_copies(stage.at[tile_slot], tbl_ref[base + e], xs_ref, tbl_ref[base + 2 * N_EXPERTS + e],
                            tbl_ref[base + N_EXPERTS + e], sems.at[tile_slot], act)

    segments(j, slot, lambda cp: cp.start())

    @pl.when(j > 0)
    def _():
        segments(j - 1, 1 - slot, lambda cp: cp.wait())

    @pl.when(j == pl.num_programs(0) - 1)
    def _():
        segments(j, slot, lambda cp: cp.wait())
        zeros[...] = jnp.zeros_like(zeros)
        sem = sems.at[0]

        def tails(act):
            for e in range(N_EXPERTS):
                _segment_copies(zeros, 0, xs_ref, tail_ref[e], tail_ref[N_EXPERTS + e], sem, act)

        def unused(act):
            def body(i, c):
                t = pl.multiple_of(i * MOE_TILE, MOE_TILE)
                act(pltpu.make_async_copy(zeros, xs_ref.at[pl.ds(t, MOE_TILE), :], sem))
                return c
            lax.fori_loop(nu_ref[0], xs_ref.shape[0] // MOE_TILE, body, 0)

        tails(lambda cp: cp.start())
        unused(lambda cp: cp.start())
        tails(lambda cp: cp.wait())
        unused(lambda cp: cp.wait())


def _sort_place(xn, route, tbl, tail, n_used, p_rows):
    n, d = xn.shape
    tm = ROW_TILE
    grid_spec = pltpu.PrefetchScalarGridSpec(
        num_scalar_prefetch=3,
        grid=(n // tm,),
        in_specs=[pl.BlockSpec((tm, d), lambda j, *_: (j, 0)),
                  pl.BlockSpec((tm, LANES), lambda j, *_: (j, 0))],
        out_specs=pl.BlockSpec(memory_space=pl.ANY),
        scratch_shapes=[pltpu.VMEM((2, LOCAL_ROWS, XS_WIDTH), F32), pltpu.VMEM((MOE_TILE, XS_WIDTH), F32),
                        pltpu.SemaphoreType.DMA((2,))],
    )
    return pl.pallas_call(
        _sort_place_kernel,
        grid_spec=grid_spec,
        out_shape=jax.ShapeDtypeStruct((p_rows, XS_WIDTH), F32),
        compiler_params=_params("arbitrary"),
        name="moe_sort_place",
    )(tbl, tail, n_used, xn, route)


def _moe_kernel(te_ref, nu_ref, xs_ref, wg_ref, wu_ref, wd_ref, y_ref):
    i = pl.program_id(0)
    c = pl.program_id(1)
    used = i < nu_ref[0]

    @pl.when(used)
    def _():
        xn = xs_ref[:, 0:D_MODEL].astype(BF16)
        g = _dot(xn, wg_ref[0])
        up = _dot(xn, wu_ref[0])
        mid = (g * _sigmoid(g) * up).astype(BF16)
        y = _dot(mid, wd_ref[0]) * xs_ref[:, D_MODEL:D_MODEL + 1]

        @pl.when(c == 0)
        def _():
            y_ref[...] = y

        @pl.when(c > 0)
        def _():
            y_ref[...] = y_ref[...] + y

    @pl.when(jnp.logical_not(used) & (c == 0))
    def _():
        y_ref[...] = jnp.zeros_like(y_ref)


def _moe_experts(xs, tile_expert, n_used, wg, wu, wd):
    p_rows = xs.shape[0]
    d = D_MODEL
    tm = MOE_TILE
    n_tiles = p_rows // tm
    ch = D_EXPERT // MOE_CHUNKS
    last = MOE_CHUNKS - 1

    def chunk(i, c, nu):
        return jnp.where(i < nu[0], c, last)

    grid_spec = pltpu.PrefetchScalarGridSpec(
        num_scalar_prefetch=2,
        grid=(n_tiles, MOE_CHUNKS),
        in_specs=[
            pl.BlockSpec((tm, XS_WIDTH), lambda i, c, te, nu: (i, 0)),
            pl.BlockSpec((1, d, ch), lambda i, c, te, nu: (te[i], 0, chunk(i, c, nu))),
            pl.BlockSpec((1, d, ch), lambda i, c, te, nu: (te[i], 0, chunk(i, c, nu))),
            pl.BlockSpec((1, ch, d), lambda i, c, te, nu: (te[i], chunk(i, c, nu), 0)),
        ],
        out_specs=pl.BlockSpec((tm, d), lambda i, c, te, nu: (i, 0)),
    )
    return pl.pallas_call(
        _moe_kernel,
        grid_spec=grid_spec,
        out_shape=jax.ShapeDtypeStruct((p_rows, d), F32),
        compiler_params=_params("arbitrary", "arbitrary"),
        name="moe_experts",
    )(tile_expert, n_used, xs, wg, wu, wd)


def _combine_kernel(tbl_ref, h3_ref, route_ref, ys_ref, g_ref, outp_ref, outs_ref, ybuf, sems):
    j = pl.program_id(0)
    last = pl.num_programs(0) - 1
    tm = h3_ref.shape[0]
    slot = lax.rem(j, 2)

    def segments(tile, tile_slot, act):
        for e in range(N_EXPERTS):
            base = tile * SEG_TABLE
            _segment_copies(ys_ref, tbl_ref[base + 2 * N_EXPERTS + e], ybuf.at[tile_slot], tbl_ref[base + e],
                            tbl_ref[base + N_EXPERTS + e], sems.at[tile_slot], act)

    @pl.when(j == 0)
    def _():
        ybuf[...] = jnp.zeros_like(ybuf)
        segments(j, slot, lambda cp: cp.start())

    @pl.when(j < last)
    def _():
        segments(j + 1, 1 - slot, lambda cp: cp.start())

    segments(j, slot, lambda cp: cp.wait())

    route = route_ref[...]
    col = lax.broadcasted_iota(I32, (tm, LOCAL_ROWS), 1).astype(F32)
    picks = jnp.where((col == route[:, 0:1]) | (col == route[:, 1:2]), 1.0, 0.0).astype(BF16)
    h4 = h3_ref[...] + _dot(picks, ybuf[slot].astype(BF16))
    out = _rms(h4, g_ref[...])

    @pl.when(j < last)
    def _():
        outp_ref[...] = out

    @pl.when(j == last)
    def _():
        outs_ref[...] = out


def _combine_final(h3, route, tbl, ys, g_final, n_prompt):
    n, d = h3.shape
    tm = ROW_TILE
    n_sample = n - n_prompt
    assert n_sample == tm and n_prompt % tm == 0
    last_prompt = n_prompt // tm - 1
    grid_spec = pltpu.PrefetchScalarGridSpec(
        num_scalar_prefetch=1,
        grid=(n // tm,),
        in_specs=[pl.BlockSpec((tm, d), lambda j, *_: (j, 0)),
                  pl.BlockSpec((tm, LANES), lambda j, *_: (j, 0)),
                  pl.BlockSpec(memory_space=pl.ANY),
                  pl.BlockSpec((1, d), lambda j, *_: (0, 0))],
        out_specs=[pl.BlockSpec((tm, d), lambda j, *_: (jnp.minimum(j, last_prompt), 0)),
                   pl.BlockSpec((tm, d), lambda j, *_: (0, 0))],
        scratch_shapes=[pltpu.VMEM((2, LOCAL_ROWS, d), F32), pltpu.SemaphoreType.DMA((2,))],
    )
    return pl.pallas_call(
        _combine_kernel,
        grid_spec=grid_spec,
        out_shape=[jax.ShapeDtypeStruct((n_prompt, d), F32), jax.ShapeDtypeStruct((n_sample, d), F32)],
        compiler_params=_params("arbitrary"),
        name="moe_combine",
    )(tbl, h3, route, ys, g_final)


def _row(v):
    return v.reshape(1, -1).astype(F32)


def _routing_tables(counts):
    n_row_tiles = counts.shape[0]
    cnt = counts[:, 0, :N_EXPERTS].astype(I32)
    seg = ((cnt + SUBLANES - 1) // SUBLANES) * SUBLANES
    local_start = jnp.cumsum(seg, axis=1) - seg
    rows = jnp.sum(seg, axis=0)
    padded = ((rows + MOE_TILE - 1) // MOE_TILE) * MOE_TILE
    ends = jnp.cumsum(padded)
    starts = ends - padded
    sorted_start = starts[None, :] + jnp.cumsum(seg, axis=0) - seg
    tbl = jnp.concatenate([local_start, seg, sorted_start], axis=1).reshape(-1).astype(I32)
    tail = jnp.concatenate([starts + rows, padded - rows]).astype(I32)
    max_rows = n_row_tiles * (TOP_K * ROW_TILE + N_EXPERTS * (SUBLANES - 1)) + N_EXPERTS * (MOE_TILE - SUBLANES)
    n_tiles = -(-max_rows // MOE_TILE)
    tile_start = jnp.arange(n_tiles, dtype=I32) * MOE_TILE
    tile_expert = jnp.minimum(jnp.sum((tile_start[:, None] >= ends[None, :]).astype(I32), axis=1), N_EXPERTS - 1)
    n_used = (ends[-1] // MOE_TILE).reshape(1).astype(I32)
    return tbl, tail, tile_expert.astype(I32), n_used, n_tiles


def kernel(x_prompt, x_sample, state_conv, state_h, cache_k, cache_v, g_mix, g_ffn, g_kv, g_final, a_w_gate, a_b_gate, a_w_in, a_b_in, a_conv_w, a_conv_b, a_w_r, a_b_r, a_w_i, a_b_i, a_lam, a_w_out, a_b_out, w_kv, b_kv, rel_bias, b_w_q, b_b_q, b_sinks, b_w_o, b_b_o, f_w_gate, f_w_up, f_w_down, m_w_router, m_w_gate, m_w_up, m_w_down):
    bp, seq, d = x_prompt.shape
    bs, steps, _ = x_sample.shape
    n_prompt = bp * seq
    n_sample = bs * steps
    n = n_prompt + n_sample
    assert seq % MIX_TILE == 0 and seq % WINDOW == 0 and n_prompt % ROW_TILE == 0
    assert n_sample == ROW_TILE and bs % SAMPLE_GROUP == 0

    mix_w = dict(g=_row(g_mix[0]), wg=a_w_gate[0].astype(BF16), bg=_row(a_b_gate[0]),
                 win=a_w_in[0].astype(BF16), bin=_row(a_b_in[0]), cw=a_conv_w[0], cb=_row(a_conv_b[0]),
                 wr=a_w_r[0].astype(BF16), br=_row(a_b_r[0]), wi=a_w_i[0].astype(BF16), bi=_row(a_b_i[0]),
                 lam=_row(a_lam[0]), wout=a_w_out[0].astype(BF16), bout=_row(a_b_out[0]))

    ffn_w = dict(gf=_row(g_ffn[0]), wg=f_w_gate[0].astype(BF16), wu=f_w_up[0].astype(BF16),
                 wd=f_w_down[0].astype(BF16), gkv=_row(g_kv), wkv=w_kv.astype(BF16), bkv=_row(b_kv),
                 gq=_row(g_mix[1]), wq=b_w_q[0].astype(BF16), bq=_row(b_b_q[0]))

    h2_p, kv_p, q_p, p_conv, p_h, moe_wg, moe_wu = _layer0_prompt(x_prompt, mix_w, ffn_w, [m_w_gate[0], m_w_up[0]])
    h1_s, s_conv_tm, s_h = _mixer_sample(jnp.transpose(x_sample, (1, 0, 2)),
                                         jnp.transpose(state_conv[0], (1, 0, 2)), state_h[0], mix_w)
    h2_s, kv_s, q_s = _ffn_rows(jnp.transpose(h1_s, (1, 0, 2)).reshape(n_sample, d), ffn_w)

    bias, bias_rows = _bias_band(rel_bias, steps)
    sinks = b_sinks[0].astype(F32)
    o_p, moe_wd = _attn_prompt(q_p, kv_p, bias, sinks, bp, seq, [m_w_down[0]])
    q4 = q_s.reshape(bs, steps, N_KV, GROUP, HEAD_DIM).transpose(0, 2, 1, 3, 4)
    q4 = q4.reshape(bs, N_KV, steps * GROUP, HEAD_DIM)
    kv_new = kv_s.reshape(bs, steps, 2 * KV_DIM)
    sink_rows = jnp.broadcast_to(sinks.reshape(N_KV, 1, GROUP), (N_KV, steps, GROUP)).reshape(N_KV, steps * GROUP, 1)
    o4, s_kt, s_vt = _attn_sample(q4, jnp.transpose(cache_k, (0, 2, 3, 1)), jnp.transpose(cache_v, (0, 2, 3, 1)),
                                  kv_new, jnp.transpose(kv_new, (0, 2, 1)), bias_rows, sink_rows)
    o_s = o4.reshape(bs, N_KV, steps, GROUP, HEAD_DIM).transpose(0, 2, 1, 3, 4).reshape(n_sample, d)

    wr_pad = jnp.zeros((d, LANES), F32).at[:, :N_EXPERTS].set(m_w_router[0])
    wr_hi = wr_pad.astype(BF16)
    wr_lo = (wr_pad - wr_hi.astype(F32)).astype(BF16)
    h3, xn, route, counts = _oproj_router(
        o_p, o_s, h2_p, h2_s,
        dict(wo=b_w_o[0].astype(BF16), bo=_row(b_b_o[0]), gf=_row(g_ffn[1]),
             wr3=jnp.concatenate([wr_hi, wr_hi, wr_lo], axis=0)))

    tbl, tail, tile_expert, n_used, n_tiles = _routing_tables(counts)
    xs = _sort_place(xn, route, tbl, tail, n_used, n_tiles * MOE_TILE)
    ys = _moe_experts(xs, tile_expert, n_used, moe_wg, moe_wu, moe_wd)
    y_p, y_s = _combine_final(h3, route, tbl, ys, _row(g_final), n_prompt)

    y_prompt = y_p.reshape(bp, seq, d)
    y_sample = y_s.reshape(bs, steps, d)
    kv_last = jnp.stack([kv_p[b * seq + seq - WINDOW:(b + 1) * seq] for b in range(bp)])
    p_k = kv_last[:, :, :KV_DIM].reshape(bp, WINDOW, N_KV, HEAD_DIM)
    p_v = kv_last[:, :, KV_DIM:].reshape(bp, WINDOW, N_KV, HEAD_DIM)
    s_k = jnp.transpose(s_kt, (0, 3, 1, 2))
    s_v = jnp.transpose(s_vt, (0, 3, 1, 2))
    return (y_prompt, y_sample, p_conv[None], p_h.reshape(1, bp, LRU_WIDTH), p_k, p_v,
            jnp.transpose(s_conv_tm, (1, 0, 2))[None], s_h[None], s_k, s_v)
```

```python
import functools
import math

import jax
import jax.numpy as jnp
from jax import lax
from jax.experimental import pallas as pl
from jax.experimental.pallas import tpu as pltpu

D_MODEL = 1024
LRU_WIDTH = D_MODEL
LRU_BLOCK_W = 256
LRU_BLOCKS = LRU_WIDTH // LRU_BLOCK_W
CONV_W = 4
LRU_C = 8.0
HEAD_DIM = 64
N_HEADS = D_MODEL // HEAD_DIM
N_KV = 2
GROUP = N_HEADS // N_KV
KV_DIM = N_KV * HEAD_DIM
WINDOW = 128
NUM_BUCKETS = 32
MAX_DISTANCE = 128
D_FF = 3 * D_MODEL
N_EXPERTS = 8
TOP_K = 2
D_EXPERT = 7 * D_MODEL // 2
EPS = 1e-6
NEG = -1e30

BF16 = jnp.bfloat16
F32 = jnp.float32
I32 = jnp.int32

SUBLANES = 8
LANES = 128
VMEM_LIMIT_BYTES = 56 * 1024 * 1024

ROW_TILE = 512
MIX_TILE = 256
FF_CHUNK = 1024
MOE_TILE = 512
MOE_CHUNKS = 2


def _params(*semantics):
    return pltpu.CompilerParams(dimension_semantics=semantics, vmem_limit_bytes=VMEM_LIMIT_BYTES)


def _const_spec(shape):
    zeros = (0,) * len(shape)
    return pl.BlockSpec(shape, lambda *_: zeros, pipeline_mode=pl.Buffered(1))


def _dot(a, b):
    return jnp.dot(a, b, preferred_element_type=F32)


def _rms(x, g):
    ms = jnp.mean(x * x, axis=-1, keepdims=True)
    return x * lax.rsqrt(ms + EPS) * g


def _sigmoid(x):
    return 1.0 / (1.0 + jnp.exp(-x))


def _gelu_tanh(x):
    return 0.5 * x * (1.0 + jnp.tanh(0.7978845608028654 * (x + 0.044715 * (x * x * x))))


def _log_sigmoid(x):
    return jnp.minimum(x, 0.0) - jnp.log1p(jnp.exp(-jnp.abs(x)))


def _lru_gates(xc, wr_ref, br, wi_ref, bi, lam):
    xcb = xc.astype(BF16)
    rs, gs = [], []
    for n in range(LRU_BLOCKS):
        xn = xcb[:, n * LRU_BLOCK_W:(n + 1) * LRU_BLOCK_W]
        rs.append(_dot(xn, wr_ref[n]))
        gs.append(_dot(xn, wi_ref[n]))
    r = _sigmoid(jnp.concatenate(rs, axis=1) + br)
    i = _sigmoid(jnp.concatenate(gs, axis=1) + bi)
    log_a = LRU_C * r * _log_sigmoid(lam)
    a = jnp.exp(log_a)
    mult = jnp.sqrt(1.0 - a * a)
    return a, mult * (i * xc)


def _interleave(order, *stage_generators):
    results = [None] * len(stage_generators)
    finished = set()

    def advance(idx):
        if idx in finished:
            return
        try:
            next(stage_generators[idx])
        except StopIteration as done:
            results[idx] = done.value
            finished.add(idx)

    for idx in order:
        advance(idx)
    while len(finished) < len(stage_generators):
        for idx in range(len(stage_generators)):
            advance(idx)
    return results


def _mixer_stages(x, g_ref, wg_ref, bg_ref, win_ref, bin_ref, cw_ref, cb_ref,
                  wr_ref, br_ref, wi_ref, bi_ref, lam_ref, wout_ref, bout_ref, xr_buf, h_carry):
    tt = x.shape[0]
    pad = SUBLANES
    u = _rms(x, g_ref[...]).astype(BF16)
    gate = _gelu_tanh(_dot(u, wg_ref[...]) + bg_ref[...])
    xr = _dot(u, win_ref[...]) + bin_ref[...]
    yield
    xr_buf[pad:pad + tt, :] = xr
    xc = cb_ref[...] + cw_ref[CONV_W - 1:CONV_W, :] * xr
    for k in range(CONV_W - 1):
        back = CONV_W - 1 - k
        xc = xc + cw_ref[k:k + 1, :] * xr_buf[pad - back:pad - back + tt, :]
    xr_buf[0:pad, :] = xr[tt - pad:tt, :]
    yield

    a, b = _lru_gates(xc, wr_ref, br_ref[...], wi_ref, bi_ref[...], lam_ref[...])
    yield

    groups = tt // SUBLANES
    a3 = a.reshape(groups, SUBLANES, LRU_WIDTH)
    b3 = b.reshape(groups, SUBLANES, LRU_WIDTH)
    row = lax.broadcasted_iota(I32, (1, SUBLANES, LRU_WIDTH), 1)
    step = 1
    while step < SUBLANES:
        keep = row >= step
        a_prev = jnp.where(keep, pltpu.roll(a3, step, axis=1), 1.0)
        b_prev = jnp.where(keep, pltpu.roll(b3, step, axis=1), 0.0)
        b3 = b3 + a3 * b_prev
        a3 = a3 * a_prev
        step *= 2
    yield
    h_prev = h_carry[0:1, :]
    hs = []
    for gi in range(groups):
        hg = b3[gi] + a3[gi] * h_prev
        hs.append(hg)
        h_prev = hg[SUBLANES - 1:SUBLANES, :]
        if gi + 1 == groups // 2:
            yield
    h = jnp.concatenate(hs, axis=0)
    h_carry[0:1, :] = h_prev
    yield
    y = _dot((h * gate).astype(BF16), wout_ref[...]) + bout_ref[...]
    return x + y, xr[tt - (CONV_W - 1):tt, :], h_prev


LAYER0_ORDER = (0, 1, 0, 0, 1, 0, 0, 1, 0, 0)
N_MIX_REFS = 14
N_FFN_REFS = 10


def _ffn_stages(h1, gf_ref, wg_ref, wu_ref, wd_ref, gkv_ref, wkv_ref, bkv_ref, gq_ref, wq_ref, bq_ref):
    u = _rms(h1, gf_ref[...]).astype(BF16)
    acc = h1
    n_chunks = D_FF // FF_CHUNK
    for c in range(n_chunks):
        cols = slice(c * FF_CHUNK, (c + 1) * FF_CHUNK)
        g = _dot(u, wg_ref[:, cols])
        up = _dot(u, wu_ref[:, cols])
        mid = (g * _sigmoid(g) * up).astype(BF16)
        acc = acc + _dot(mid, wd_ref[cols, :])
        if c + 1 < n_chunks:
            yield
    kv = _dot(_rms(acc, gkv_ref[...]).astype(BF16), wkv_ref[...]) + bkv_ref[...]
    q = _dot(_rms(acc, gq_ref[...]).astype(BF16), wq_ref[...]) + bq_ref[...]
    return acc, kv, (q * (HEAD_DIM ** -0.5)).astype(BF16)


def _layer0_prompt_kernel(*refs, n_tiles, tiles_per_seq, n_casts):
    x_ref = refs[0]
    mix_refs = refs[1:1 + N_MIX_REFS]
    ffn_refs = refs[1 + N_MIX_REFS:1 + N_MIX_REFS + N_FFN_REFS]
    rest = refs[1 + N_MIX_REFS + N_FFN_REFS:]
    cast_srcs, rest = rest[:n_casts], rest[n_casts:]
    h2_ref, kv_ref, q_ref, conv_ref, hlast_ref = rest[:5]
    cast_dsts = rest[5:5 + n_casts]
    xr_buf, h_carry, h1_buf = rest[5 + n_casts:]
    for src_ref, dst_ref in zip(cast_srcs, cast_dsts):
        dst_ref[...] = src_ref[...].astype(BF16)
    s = pl.program_id(0)
    slot = lax.rem(s, 2)

    @pl.when(s == 0)
    def _():
        h1_buf[...] = jnp.zeros_like(h1_buf)

    @pl.when(lax.rem(s, tiles_per_seq) == 0)
    def _():
        xr_buf[0:SUBLANES, :] = jnp.zeros((SUBLANES, LRU_WIDTH), F32)
        h_carry[...] = jnp.zeros_like(h_carry)

    (h1, conv_tail, h_last), (h2, kv, q) = _interleave(
        LAYER0_ORDER,
        _mixer_stages(x_ref[...], *mix_refs, xr_buf, h_carry), _ffn_stages(h1_buf[1 - slot], *ffn_refs))
    h2_ref[...] = h2
    kv_ref[...] = kv
    q_ref[...] = q
    h1_buf[slot] = h1

    @pl.when(s < n_tiles)
    def _():
        b = s // tiles_per_seq
        conv_ref[b] = conv_tail
        hlast_ref[b] = h_last


def _layer0_prompt(x, mix_w, ffn_w, cast_ws):
    b, t, d = x.shape
    tt = MIX_TILE
    n_tiles = (b * t) // tt
    casts = [_side_cast_specs(w, n_tiles, lambda s: jnp.minimum(s, n_tiles - 1)) for w in cast_ws]
    vec = lambda n: _const_spec((1, n))
    mix_specs = [
        vec(d), _const_spec((d, LRU_WIDTH)), vec(LRU_WIDTH), _const_spec((d, LRU_WIDTH)), vec(LRU_WIDTH),
        _const_spec((CONV_W, LRU_WIDTH)), vec(LRU_WIDTH),
        _const_spec((LRU_BLOCKS, LRU_BLOCK_W, LRU_BLOCK_W)), vec(LRU_WIDTH),
        _const_spec((LRU_BLOCKS, LRU_BLOCK_W, LRU_BLOCK_W)), vec(LRU_WIDTH),
        vec(LRU_WIDTH), _const_spec((LRU_WIDTH, d)), vec(d),
    ]
    ffn_specs = [
        vec(d), _const_spec((d, D_FF)), _const_spec((d, D_FF)), _const_spec((D_FF, d)),
        vec(d), _const_spec((d, 2 * KV_DIM)), vec(2 * KV_DIM), vec(d), _const_spec((d, d)), vec(d),
    ]
    assert len(mix_specs) == N_MIX_REFS and len(ffn_specs) == N_FFN_REFS
    prev = lambda s: (jnp.maximum(s - 1, 0), 0)
    whole = lambda shape: pl.BlockSpec(shape, lambda s: (0,) * len(shape))
    return pl.pallas_call(
        functools.partial(_layer0_prompt_kernel, n_tiles=n_tiles, tiles_per_seq=t // tt, n_casts=len(casts)),
        grid=(n_tiles + 1,),
        in_specs=([pl.BlockSpec((tt, d), lambda s: (jnp.minimum(s, n_tiles - 1), 0))] + mix_specs + ffn_specs
                  + [c[0] for c in casts]),
        out_specs=[pl.BlockSpec((tt, d), prev), pl.BlockSpec((tt, 2 * KV_DIM), prev), pl.BlockSpec((tt, d), prev),
                   whole((b, CONV_W - 1, LRU_WIDTH)), whole((b, 1, LRU_WIDTH))] + [c[1] for c in casts],
        out_shape=[jax.ShapeDtypeStruct((b * t, d), F32), jax.ShapeDtypeStruct((b * t, 2 * KV_DIM), F32),
                   jax.ShapeDtypeStruct((b * t, d), BF16),
                   jax.ShapeDtypeStruct((b, CONV_W - 1, LRU_WIDTH), F32),
                   jax.ShapeDtypeStruct((b, 1, LRU_WIDTH), F32)] + [c[2] for c in casts],
        scratch_shapes=[pltpu.VMEM((SUBLANES + tt, LRU_WIDTH), F32), pltpu.VMEM((SUBLANES, LRU_WIDTH), F32),
                        pltpu.VMEM((2, tt, d), F32)],
        compiler_params=_params("arbitrary"),
        name="layer0_prompt",
    )(x.reshape(b * t, d), *[mix_w[k] for k in MIX_KEYS], *[ffn_w[k] for k in FFN_KEYS], *cast_ws)


MIX_KEYS = ("g", "wg", "bg", "win", "bin", "cw", "cb", "wr", "br", "wi", "bi", "lam", "wout", "bout")
FFN_KEYS = ("gf", "wg", "wu", "wd", "gkv", "wkv", "bkv", "gq", "wq", "bq")


def _mixer_sample_kernel(x_ref, cs_ref, h0_ref, g_ref, wg_ref, bg_ref, win_ref, bin_ref, cw_ref, cb_ref,
                         wr_ref, br_ref, wi_ref, bi_ref, lam_ref, wout_ref, bout_ref,
                         h1_ref, conv_ref, hlast_ref):
    steps, nb, d = x_ref.shape
    x = x_ref[...].reshape(steps * nb, d)
    u = _rms(x, g_ref[...]).astype(BF16)
    gate = _gelu_tanh(_dot(u, wg_ref[...]) + bg_ref[...])
    xr = _dot(u, win_ref[...]) + bin_ref[...]
    xpad = [cs_ref[k] for k in range(CONV_W - 1)] + [xr[s * nb:(s + 1) * nb, :] for s in range(steps)]
    xcs = []
    for s in range(steps):
        acc = cb_ref[...] + cw_ref[0:1, :] * xpad[s]
        for k in range(1, CONV_W):
            acc = acc + cw_ref[k:k + 1, :] * xpad[s + k]
        xcs.append(acc)
    for k in range(CONV_W - 1):
        conv_ref[k] = xpad[steps + k]
    xc = jnp.concatenate(xcs, axis=0)
    a, b = _lru_gates(xc, wr_ref, br_ref[...], wi_ref, bi_ref[...], lam_ref[...])
    h = h0_ref[...]
    hs = []
    for s in range(steps):
        h = a[s * nb:(s + 1) * nb, :] * h + b[s * nb:(s + 1) * nb, :]
        hs.append(h)
    hlast_ref[...] = h
    hcat = jnp.concatenate(hs, axis=0)
    y = _dot((hcat * gate).astype(BF16), wout_ref[...]) + bout_ref[...]
    h1_ref[...] = (x + y).reshape(steps, nb, d)


def _mixer_sample(x_tm, cs_tm, h0, w):
    steps, nb, d = x_tm.shape
    full = lambda shape: pl.BlockSpec(shape, lambda i: (0,) * len(shape))
    vec = lambda n: full((1, n))
    in_specs = [
        full((steps, nb, d)), full((CONV_W - 1, nb, LRU_WIDTH)), full((nb, LRU_WIDTH)),
        vec(d), full((d, LRU_WIDTH)), vec(LRU_WIDTH), full((d, LRU_WIDTH)), vec(LRU_WIDTH),
        full((CONV_W, LRU_WIDTH)), vec(LRU_WIDTH),
        full((LRU_BLOCKS, LRU_BLOCK_W, LRU_BLOCK_W)), vec(LRU_WIDTH),
        full((LRU_BLOCKS, LRU_BLOCK_W, LRU_BLOCK_W)), vec(LRU_WIDTH),
        vec(LRU_WIDTH), full((LRU_WIDTH, d)), vec(d),
    ]
    out_specs = [full((steps, nb, d)), full((CONV_W - 1, nb, LRU_WIDTH)), full((nb, LRU_WIDTH))]
    out_shape = [
        jax.ShapeDtypeStruct((steps, nb, d), F32),
        jax.ShapeDtypeStruct((CONV_W - 1, nb, LRU_WIDTH), F32),
        jax.ShapeDtypeStruct((nb, LRU_WIDTH), F32),
    ]
    return pl.pallas_call(
        _mixer_sample_kernel, grid=(1,), in_specs=in_specs, out_specs=out_specs, out_shape=out_shape,
        compiler_params=_params("arbitrary"), name="mixer_sample",
    )(x_tm, cs_tm, h0, w["g"], w["wg"], w["bg"], w["win"], w["bin"], w["cw"], w["cb"], w["wr"], w["br"],
      w["wi"], w["bi"], w["lam"], w["wout"], w["bout"])


def _two_part_specs(n_prompt, n_sample, width):
    assert n_sample == ROW_TILE and n_prompt % ROW_TILE == 0
    last_prompt = n_prompt // ROW_TILE - 1
    return [pl.BlockSpec((ROW_TILE, width), lambda i: (jnp.minimum(i, last_prompt), 0)),
            pl.BlockSpec((ROW_TILE, width), lambda i: (0, 0))]


def _two_part_tile(prompt_ref, sample_ref):
    is_sample = pl.program_id(0) == pl.num_programs(0) - 1
    return jnp.where(is_sample, sample_ref[...], prompt_ref[...])


def _side_cast_specs(w, n_steps, step_of):
    ne, rows, cols = w.shape
    per_expert = n_steps // ne
    blk = rows // per_expert
    assert per_expert * ne == n_steps and blk * per_expert == rows and blk % (2 * SUBLANES) == 0
    index = lambda *ids: (step_of(*ids) // per_expert, step_of(*ids) % per_expert, 0)
    spec = pl.BlockSpec((1, blk, cols), index)
    return spec, spec, jax.ShapeDtypeStruct(w.shape, BF16)


def _ffn_rows_kernel(h1_ref, *refs):
    h2_ref, kv_ref, q_ref = refs[N_FFN_REFS:]
    (h2_ref[...], kv_ref[...], q_ref[...]), = _interleave((), _ffn_stages(h1_ref[...], *refs[:N_FFN_REFS]))


def _ffn_rows(h1, w):
    n, d = h1.shape
    tm = ROW_TILE
    row = lambda width: pl.BlockSpec((tm, width), lambda i: (i, 0))
    vec = lambda width: _const_spec((1, width))
    in_specs = [
        row(d), vec(d), _const_spec((d, D_FF)), _const_spec((d, D_FF)), _const_spec((D_FF, d)),
        vec(d), _const_spec((d, 2 * KV_DIM)), vec(2 * KV_DIM), vec(d), _const_spec((d, d)), vec(d),
    ]
    return pl.pallas_call(
        _ffn_rows_kernel,
        grid=(n // tm,),
        in_specs=in_specs,
        out_specs=[row(d), row(2 * KV_DIM), row(d)],
        out_shape=[jax.ShapeDtypeStruct((n, d), F32), jax.ShapeDtypeStruct((n, 2 * KV_DIM), F32),
                   jax.ShapeDtypeStruct((n, d), BF16)],
        compiler_params=_params("arbitrary"),
        name="ffn_rows",
    )(h1, *[w[k] for k in FFN_KEYS])


def _head_of(j, parity, pair):
    return j * GROUP + 2 * pair + parity


def _distance_bias(dist, table_ref, head):
    max_exact = NUM_BUCKETS // 2
    n = jnp.maximum(dist, 0)
    large = jnp.full(dist.shape, max_exact, I32)
    for step in range(1, NUM_BUCKETS - max_exact):
        threshold = math.ceil(max_exact * (MAX_DISTANCE / max_exact) ** (step / (NUM_BUCKETS - max_exact)))
        large = large + jnp.where(n >= threshold, 1, 0)
    bucket = jnp.where(n < max_exact, n, large)
    acc = jnp.zeros(dist.shape, F32)
    for bkt in range(NUM_BUCKETS):
        acc = jnp.where(bucket == bkt, table_ref[bkt, head], acc)
    return acc


def _bias_band_kernel(table_ref, folded_ref, rows_ref):
    w = WINDOW
    ci = lax.broadcasted_iota(I32, (w, w), 0)
    qi = lax.broadcasted_iota(I32, (w, w), 1)
    folded_dist = jnp.where(ci > qi, qi + w - ci, qi - ci)
    for j in range(N_KV):
        for parity in range(2):
            for pair in range(GROUP // 2):
                folded_ref[j, parity, :, pair * w:(pair + 1) * w] = _distance_bias(
                    folded_dist, table_ref, _head_of(j, parity, pair))
    n_rows, n_keys = rows_ref.shape[1], rows_ref.shape[2]
    ri = lax.broadcasted_iota(I32, (GROUP, n_keys), 0)
    si = lax.broadcasted_iota(I32, (GROUP, n_keys), 1)
    for j in range(N_KV):
        for t in range(n_rows // GROUP):
            acc = jnp.zeros((GROUP, n_keys), F32)
            for g in range(GROUP):
                acc = jnp.where(ri == g, _distance_bias(t + w - si, table_ref, j * GROUP + g), acc)
            rows_ref[j, t * GROUP:(t + 1) * GROUP, :] = acc


def _bias_band(rel_table, steps):
    folded = (N_KV, 2, WINDOW, (GROUP // 2) * WINDOW)
    rows = (N_KV, steps * GROUP, WINDOW + steps + (-steps) % SUBLANES)
    return pl.pallas_call(
        _bias_band_kernel,
        grid=(1,),
        in_specs=[pl.BlockSpec(memory_space=pltpu.SMEM)],
        out_specs=[pl.BlockSpec(folded, lambda i: (0, 0, 0, 0)), pl.BlockSpec(rows, lambda i: (0, 0, 0))],
        out_shape=[jax.ShapeDtypeStruct(folded, F32), jax.ShapeDtypeStruct(rows, F32)],
        compiler_params=_params("arbitrary"),
        name="bias_band",
    )(rel_table)


ATTN_BLOCKS = 8


def _attn_prompt_kernel(sink_ref, q_ref, kvp_ref, kvc_ref, bias_ref, *refs):
    n_casts = (len(refs) - 1) // 2
    o_ref = refs[n_casts]
    for src_ref, dst_ref in zip(refs[:n_casts], refs[n_casts + 1:]):
        dst_ref[...] = src_ref[...].astype(BF16)
    w = WINDOW
    pairs = GROUP // 2
    assert 2 * HEAD_DIM == LANES and KV_DIM == LANES
    ci = lax.broadcasted_iota(I32, (w, pairs * w), 0)
    qi = lax.broadcasted_iota(I32, (w, pairs * w), 1) % w
    from_prev = ci > qi
    has_prev = pl.program_id(1) > 0

    kv = jnp.concatenate([kvp_ref[...], kvc_ref[...]], axis=0)
    low = lax.broadcasted_iota(I32, (kv.shape[0], LANES), 1) < HEAD_DIM

    def halves(x):
        swapped = pltpu.roll(x, HEAD_DIM, axis=1)
        return (((jnp.where(low, x, 0.0)).astype(BF16), (jnp.where(low, 0.0, swapped)).astype(BF16)),
                ((jnp.where(low, swapped, 0.0)).astype(BF16), (jnp.where(low, 0.0, x)).astype(BF16)))

    k_ops = halves(kv[:, 0:KV_DIM])
    v_ops = halves(kv[:, KV_DIM:2 * KV_DIM])

    sinks = [[jnp.concatenate([jnp.full((1, w), sink_ref[_head_of(j, parity, b)], F32) for b in range(pairs)], axis=1)
              for parity in range(2)] for j in range(N_KV)]
    nt = (((1,), (1,)), ((), ()))
    tn = (((0,), (0,)), ((), ()))
    for i in range(ATTN_BLOCKS):
        q_rows = slice(i * w, (i + 1) * w)
        prev_rows = slice(i * w, (i + 1) * w)
        own_rows = slice((i + 1) * w, (i + 2) * w)
        for j in range(N_KV):
            q4 = jnp.concatenate([q_ref[q_rows, (j * pairs + b) * LANES:(j * pairs + b + 1) * LANES]
                                  for b in range(pairs)], axis=0)
            acc = None
            for parity in range(2):
                k_op, v_op, sink = k_ops[j][parity], v_ops[j][parity], sinks[j][parity]
                s_prev = lax.dot_general(k_op[prev_rows], q4, nt, preferred_element_type=F32)
                s_own = lax.dot_general(k_op[own_rows], q4, nt, preferred_element_type=F32)
                if i == 0:
                    s_prev = jnp.where(has_prev, s_prev, NEG)
                s = jnp.where(from_prev, s_prev, s_own) + bias_ref[j, parity]
                m = jnp.maximum(jnp.max(s, axis=0, keepdims=True), sink)
                p = jnp.exp(s - m)
                denom = jnp.sum(p, axis=0, keepdims=True) + jnp.exp(sink - m)
                p = p * (1.0 / denom)
                o = (lax.dot_general(jnp.where(from_prev, p, 0.0).astype(BF16), v_op[prev_rows], tn,
                                     preferred_element_type=F32)
                     + lax.dot_general(jnp.where(from_prev, 0.0, p).astype(BF16), v_op[own_rows], tn,
                                       preferred_element_type=F32))
                acc = o if acc is None else acc + o
            for b in range(pairs):
                o_ref[q_rows, (j * pairs + b) * LANES:(j * pairs + b + 1) * LANES] = acc[b * w:(b + 1) * w].astype(BF16)


def _attn_prompt(q, kv, bias, sinks, batch, seq, cast_ws):
    nb = seq // WINDOW
    assert nb % ATTN_BLOCKS == 0
    ns = nb // ATTN_BLOCKS
    rows = ATTN_BLOCKS * WINDOW
    d = q.shape[1]
    casts = [_side_cast_specs(w, batch * ns, lambda b, n: b * ns + n) for w in cast_ws]
    in_specs = [
        pl.BlockSpec(memory_space=pltpu.SMEM),
        pl.BlockSpec((rows, d), lambda b, n: (b * ns + n, 0)),
        pl.BlockSpec((WINDOW, 2 * KV_DIM), lambda b, n: (b * nb + jnp.maximum(n * ATTN_BLOCKS - 1, 0), 0)),
        pl.BlockSpec((rows, 2 * KV_DIM), lambda b, n: (b * ns + n, 0)),
        _const_spec(bias.shape),
    ] + [c[0] for c in casts]
    return pl.pallas_call(
        _attn_prompt_kernel,
        grid=(batch, ns),
        in_specs=in_specs,
        out_specs=[pl.BlockSpec((rows, d), lambda b, n: (b * ns + n, 0))] + [c[1] for c in casts],
        out_shape=[jax.ShapeDtypeStruct((batch * seq, d), BF16)] + [c[2] for c in casts],
        compiler_params=_params("arbitrary", "arbitrary"),
        name="attn_prompt",
    )(sinks, q, kv, kv, bias, *cast_ws)


SAMPLE_GROUP = 8


def _attn_sample_kernel(q_ref, ckt_ref, cvt_ref, new_ref, newt_ref, bias_ref, sink_ref, o_ref, skt_ref, svt_ref):
    gb = q_ref.shape[0]
    steps = new_ref.shape[1]
    rows = steps * GROUP
    pad = (-steps) % SUBLANES
    ri = lax.broadcasted_iota(I32, (rows, WINDOW + steps + pad), 0)
    si = lax.broadcasted_iota(I32, (rows, WINDOW + steps + pad), 1)
    dist = ri // GROUP + WINDOW - si
    mask = (dist >= 0) & (dist < WINDOW)
    mask_c, mask_n = mask[:, :WINDOW], mask[:, WINDOW:]
    zeros = jnp.zeros((pad, 2 * KV_DIM), F32)

    def scores(b, j):
        new = jnp.concatenate([new_ref[b], zeros], axis=0).astype(BF16)
        q = q_ref[b, j]
        s_c = _dot(q, ckt_ref[b, j].astype(BF16))
        s_n = lax.dot_general(q, new[:, j * HEAD_DIM:(j + 1) * HEAD_DIM], (((1,), (1,)), ((), ())),
                              preferred_element_type=F32)
        bias = bias_ref[j]
        return (jnp.where(mask_c, s_c + bias[:, :WINDOW], NEG), jnp.where(mask_n, s_n + bias[:, WINDOW:], NEG), new)

    tasks = [(b, j) for b in range(gb) for j in range(N_KV)]
    all_scores = [scores(b, j) for b, j in tasks]
    for (b, j), (s_c, s_n, new) in zip(tasks, all_scores):
        sink = sink_ref[j]
        m = jnp.maximum(jnp.maximum(jnp.max(s_c, axis=-1, keepdims=True), jnp.max(s_n, axis=-1, keepdims=True)), sink)
        p_c = jnp.exp(s_c - m)
        p_n = jnp.exp(s_n - m)
        denom = jnp.sum(p_c, axis=-1, keepdims=True) + jnp.sum(p_n, axis=-1, keepdims=True) + jnp.exp(sink - m)
        pv = lax.dot_general(p_c.astype(BF16), cvt_ref[b, j].astype(BF16), (((1,), (1,)), ((), ())),
                             preferred_element_type=F32)
        pv = pv + _dot(p_n.astype(BF16), new[:, KV_DIM + j * HEAD_DIM:KV_DIM + (j + 1) * HEAD_DIM])
        o_ref[b, j] = (pv / denom).astype(BF16)

    lane = lax.broadcasted_iota(I32, (HEAD_DIM, WINDOW), 1)
    for b in range(gb):
        newt = jnp.concatenate([newt_ref[b], jnp.zeros((2 * KV_DIM, WINDOW - steps), F32)], axis=1)
        for j in range(N_KV):
            for src_ref, dst_ref, base in ((ckt_ref, skt_ref, 0), (cvt_ref, svt_ref, KV_DIM)):
                fresh = newt[base + j * HEAD_DIM:base + (j + 1) * HEAD_DIM, :]
                dst_ref[b, j] = pltpu.roll(jnp.where(lane < steps, fresh, src_ref[b, j]), WINDOW - steps, axis=1)


def _attn_sample(q4, ckt, cvt, new, newt, bias_rows, sink_rows):
    nb, _, rows, hd = q4.shape
    steps = new.shape[1]
    keys = bias_rows.shape[2]
    gb = SAMPLE_GROUP
    cache_spec = pl.BlockSpec((gb, N_KV, hd, WINDOW), lambda i: (i, 0, 0, 0))
    in_specs = [
        pl.BlockSpec((gb, N_KV, rows, hd), lambda i: (i, 0, 0, 0)),
        cache_spec, cache_spec,
        pl.BlockSpec((gb, steps, 2 * KV_DIM), lambda i: (i, 0, 0)),
        pl.BlockSpec((gb, 2 * KV_DIM, steps), lambda i: (i, 0, 0)),
        pl.BlockSpec((N_KV, rows, keys), lambda i: (0, 0, 0)),
        pl.BlockSpec((N_KV, rows, 1), lambda i: (0, 0, 0)),
    ]
    return pl.pallas_call(
        _attn_sample_kernel,
        grid=(nb // gb,),
        in_specs=in_specs,
        out_specs=[pl.BlockSpec((gb, N_KV, rows, hd), lambda i: (i, 0, 0, 0)), cache_spec, cache_spec],
        out_shape=[jax.ShapeDtypeStruct(q4.shape, BF16), jax.ShapeDtypeStruct(ckt.shape, F32),
                   jax.ShapeDtypeStruct(cvt.shape, F32)],
        compiler_params=_params("arbitrary"),
        name="attn_sample",
    )(q4, ckt, cvt, new, newt, bias_rows, sink_rows)


def _split_bf16(x):
    hi = x.astype(BF16)
    lo = (x - hi.astype(F32)).astype(BF16)
    return hi, lo


ROUTER_PARTS = 2


def _oproj_router_kernel(op_ref, os_ref, h2p_ref, h2s_ref, wo_ref, bo_ref, gf_ref, wr3_ref,
                         h3_ref, xn_ref, route_ref, counts_ref):
    tm = h3_ref.shape[0]
    parts = ROUTER_PARTS
    pr = tm // parts
    o = _two_part_tile(op_ref, os_ref)
    h2 = _two_part_tile(h2p_ref, h2s_ref)
    lane = lax.broadcasted_iota(I32, (pr, LANES), 1)
    lanef = lane.astype(F32)
    ri = lax.broadcasted_iota(I32, (pr, pr), 0)
    ci = lax.broadcasted_iota(I32, (pr, pr), 1)
    earlier = jnp.where(ri > ci, 1.0, 0.0).astype(BF16)

    picks, ranks, gates = [], [], []
    cnt = jnp.zeros((1, LANES), F32)
    for p in range(parts):
        rows = slice(p * pr, (p + 1) * pr)
        h3 = h2[rows, :] + _dot(o[rows, :], wo_ref[...]) + bo_ref[...]
        h3_ref[rows, :] = h3
        u_hi, u_lo = _split_bf16(_rms(h3, gf_ref[...]))
        xn_ref[rows, :] = u_hi
        logits = _dot(jnp.concatenate([u_hi, u_lo, u_hi], axis=1), wr3_ref[...])
        logits = jnp.where(lane < N_EXPERTS, logits, -jnp.inf)
        v1 = jnp.max(logits, axis=-1, keepdims=True)
        e1 = jnp.min(jnp.where(logits == v1, lanef, float(LANES)), axis=-1, keepdims=True)
        rest = jnp.where(lanef == e1, -jnp.inf, logits)
        v2 = jnp.max(rest, axis=-1, keepdims=True)
        e2 = jnp.min(jnp.where(rest == v2, lanef, float(LANES)), axis=-1, keepdims=True)
        ex = jnp.exp(v2 - v1)
        gates.append((1.0 / (1.0 + ex), ex / (1.0 + ex)))
        pick1 = lanef == e1
        pick2 = lanef == e2
        picks.append((pick1, pick2))
        sel = jnp.where(pick1 | pick2, 1.0, 0.0)
        ranks.append(_dot(earlier, sel.astype(BF16)) + cnt)
        cnt = cnt + jnp.sum(sel, axis=0, keepdims=True)

    seg = jnp.floor((cnt + (SUBLANES - 1)) * (1.0 / SUBLANES)) * SUBLANES
    ek = lax.broadcasted_iota(I32, (LANES, LANES), 0)
    el = lax.broadcasted_iota(I32, (LANES, LANES), 1)
    lower_experts = jnp.where(ek < el, 1.0, 0.0).astype(BF16)
    seg_start = _dot(jnp.broadcast_to(seg, (SUBLANES, LANES)).astype(BF16), lower_experts)[0:1, :]
    for p in range(parts):
        local = ranks[p] + seg_start
        lr1 = jnp.sum(jnp.where(picks[p][0], local, 0.0), axis=-1, keepdims=True)
        lr2 = jnp.sum(jnp.where(picks[p][1], local, 0.0), axis=-1, keepdims=True)
        w1, w2 = gates[p]
        route_ref[p * pr:(p + 1) * pr, :] = jnp.where(
            lane == 0, lr1, jnp.where(lane == 1, lr2, jnp.where(lane == 2, w1, jnp.where(lane == 3, w2, 0.0))))
    counts_ref[0] = jnp.broadcast_to(cnt, (SUBLANES, LANES))


def _oproj_router(o_p, o_s, h2_p, h2_s, w):
    d = h2_p.shape[1]
    n = h2_p.shape[0] + h2_s.shape[0]
    tm = ROW_TILE
    row = lambda width: pl.BlockSpec((tm, width), lambda i: (i, 0))
    in_specs = (_two_part_specs(o_p.shape[0], o_s.shape[0], d) + _two_part_specs(h2_p.shape[0], h2_s.shape[0], d) + [
        _const_spec((d, d)), _const_spec((1, d)), _const_spec((1, d)), _const_spec((3 * d, LANES))])
    return pl.pallas_call(
        _oproj_router_kernel,
        grid=(n // tm,),
        in_specs=in_specs,
        out_specs=[row(d), row(d), row(LANES), pl.BlockSpec((1, SUBLANES, LANES), lambda i: (i, 0, 0))],
        out_shape=[jax.ShapeDtypeStruct((n, d), F32), jax.ShapeDtypeStruct((n, d), BF16),
                   jax.ShapeDtypeStruct((n, LANES), F32), jax.ShapeDtypeStruct((n // tm, SUBLANES, LANES), F32)],
        compiler_params=_params("arbitrary"),
        name="oproj_router",
    )(o_p, o_s, h2_p, h2_s, w["wo"], w["bo"], w["gf"], w["wr3"])


LOCAL_ROWS = 1152
XS_WIDTH = D_MODEL + LANES
SEG_TABLE = 3 * N_EXPERTS
SEG_PIECES = tuple(SUBLANES << b for b in reversed(range(7)))


def _segment_copies(src_ref, src_start, dst_ref, dst_start, length, sem, act):
    for piece in SEG_PIECES:
        done = lax.div(length, 2 * piece) * (2 * piece)

        @pl.when(lax.rem(lax.div(length, piece), 2) != 0)
        def _():
            s = pl.multiple_of(src_start + done, SUBLANES)
            t = pl.multiple_of(dst_start + done, SUBLANES)
            act(pltpu.make_async_copy(src_ref.at[pl.ds(s, piece), :], dst_ref.at[pl.ds(t, piece), :], sem))


def _sort_place_kernel(tbl_ref, tail_ref, nu_ref, xn_ref, route_ref, xs_ref, stage, zeros, sems):
    j = pl.program_id(0)
    tm = xn_ref.shape[0]
    xn = xn_ref[...]
    rt = jnp.transpose(route_ref[...])
    row = lax.broadcasted_iota(I32, (LOCAL_ROWS, tm), 0).astype(F32)
    m1 = row == rt[0:1, :]
    m2 = row == rt[1:2, :]
    onehot = jnp.where(m1 | m2, 1.0, 0.0).astype(BF16)
    slot = lax.rem(j, 2)
    buf = stage.at[slot]
    buf[:, 0:D_MODEL] = _dot(onehot, xn)
    gate = jnp.sum(jnp.where(m1, rt[2:3, :], 0.0) + jnp.where(m2, rt[3:4, :], 0.0), axis=-1, keepdims=True)
    buf[:, D_MODEL:XS_WIDTH] = jnp.broadcast_to(gate, (LOCAL_ROWS, LANES))

    def segments(tile, tile_slot, act):
        for e in range(N_EXPERTS):
            base = tile * SEG_TABLE
            _segment_copies(stage.at[tile_slot], tbl_ref[base + e], xs_ref, tbl_ref[base + 2 * N_EXPERTS + e],
                            tbl_ref[base + N_EXPERTS + e], sems.at[tile_slot], act)

    segments(j, slot, lambda cp: cp.start())

    @pl.when(j > 0)
    def _():
        segments(j - 1, 1 - slot, lambda cp: cp.wait())

    @pl.when(j == pl.num_programs(0) - 1)
    def _():
        segments(j, slot, lambda cp: cp.wait())
        zeros[...] = jnp.zeros_like(zeros)
        sem = sems.at[0]

        def tails(act):
            for e in range(N_EXPERTS):
                _segment_copies(zeros, 0, xs_ref, tail_ref[e], tail_ref[N_EXPERTS + e], sem, act)

        def unused(act):
            def body(i, c):
                t = pl.multiple_of(i * MOE_TILE, MOE_TILE)
                act(pltpu.make_async_copy(zeros, xs_ref.at[pl.ds(t, MOE_TILE), :], sem))
                return c
            lax.fori_loop(nu_ref[0], xs_ref.shape[0] // MOE_TILE, body, 0)

        tails(lambda cp: cp.start())
        unused(lambda cp: cp.start())
        tails(lambda cp: cp.wait())
        unused(lambda cp: cp.wait())


def _sort_place(xn, route, tbl, tail, n_used, p_rows):
    n, d = xn.shape
    tm = ROW_TILE
    grid_spec = pltpu.PrefetchScalarGridSpec(
        num_scalar_prefetch=3,
        grid=(n // tm,),
        in_specs=[pl.BlockSpec((tm, d), lambda j, *_: (j, 0)),
                  pl.BlockSpec((tm, LANES), lambda j, *_: (j, 0))],
        out_specs=pl.BlockSpec(memory_space=pl.ANY),
        scratch_shapes=[pltpu.VMEM((2, LOCAL_ROWS, XS_WIDTH), F32), pltpu.VMEM((MOE_TILE, XS_WIDTH), F32),
                        pltpu.SemaphoreType.DMA((2,))],
    )
    return pl.pallas_call(
        _sort_place_kernel,
        grid_spec=grid_spec,
        out_shape=jax.ShapeDtypeStruct((p_rows, XS_WIDTH), F32),
        compiler_params=_params("arbitrary"),
        name="moe_sort_place",
    )(tbl, tail, n_used, xn, route)


def _moe_kernel(te_ref, nu_ref, xs_ref, wg_ref, wu_ref, wd_ref, y_ref):
    i = pl.program_id(0)
    c = pl.program_id(1)
    used = i < nu_ref[0]

    @pl.when(used)
    def _():
        xn = xs_ref[:, 0:D_MODEL].astype(BF16)
        g = _dot(xn, wg_ref[0])
        up = _dot(xn, wu_ref[0])
        mid = (g * _sigmoid(g) * up).astype(BF16)
        y = _dot(mid, wd_ref[0]) * xs_ref[:, D_MODEL:D_MODEL + 1]

        @pl.when(c == 0)
        def _():
            y_ref[...] = y

        @pl.when(c > 0)
        def _():
            y_ref[...] = y_ref[...] + y

    @pl.when(jnp.logical_not(used) & (c == 0))
    def _():
        y_ref[...] = jnp.zeros_like(y_ref)


def _moe_experts(xs, tile_expert, n_used, wg, wu, wd):
    p_rows = xs.shape[0]
    d = D_MODEL
    tm = MOE_TILE
    n_tiles = p_rows // tm
    ch = D_EXPERT // MOE_CHUNKS
    last = MOE_CHUNKS - 1

    def chunk(i, c, nu):
        return jnp.where(i < nu[0], c, last)

    grid_spec = pltpu.PrefetchScalarGridSpec(
        num_scalar_prefetch=2,
        grid=(n_tiles, MOE_CHUNKS),
        in_specs=[
            pl.BlockSpec((tm, XS_WIDTH), lambda i, c, te, nu: (jnp.minimum(i, nu[0] - 1), 0)),
            pl.BlockSpec((1, d, ch), lambda i, c, te, nu: (te[i], 0, chunk(i, c, nu))),
            pl.BlockSpec((1, d, ch), lambda i, c, te, nu: (te[i], 0, chunk(i, c, nu))),
            pl.BlockSpec((1, ch, d), lambda i, c, te, nu: (te[i], chunk(i, c, nu), 0)),
        ],
        out_specs=pl.BlockSpec((tm, d), lambda i, c, te, nu: (i, 0)),
    )
    return pl.pallas_call(
        _moe_kernel,
        grid_spec=grid_spec,
        out_shape=jax.ShapeDtypeStruct((p_rows, d), F32),
        compiler_params=_params("arbitrary", "arbitrary"),
        name="moe_experts",
    )(tile_expert, n_used, xs, wg, wu, wd)


def _combine_kernel(tbl_ref, h3_ref, route_ref, ys_ref, g_ref, outp_ref, outs_ref, ybuf, sems):
    j = pl.program_id(0)
    last = pl.num_programs(0) - 1
    tm = h3_ref.shape[0]
    slot = lax.rem(j, 2)

    def segments(tile, tile_slot, act):
        for e in range(N_EXPERTS):
            base = tile * SEG_TABLE
            _segment_copies(ys_ref, tbl_ref[base + 2 * N_EXPERTS + e], ybuf.at[tile_slot], tbl_ref[base + e],
                            tbl_ref[base + N_EXPERTS + e], sems.at[tile_slot], act)

    @pl.when(j == 0)
    def _():
        ybuf[...] = jnp.zeros_like(ybuf)
        segments(j, slot, lambda cp: cp.start())

    @pl.when(j < last)
    def _():
        segments(j + 1, 1 - slot, lambda cp: cp.start())

    segments(j, slot, lambda cp: cp.wait())

    route = route_ref[...]
    col = lax.broadcasted_iota(I32, (tm, LOCAL_ROWS), 1).astype(F32)
    picks = jnp.where((col == route[:, 0:1]) | (col == route[:, 1:2]), 1.0, 0.0).astype(BF16)
    h4 = h3_ref[...] + _dot(picks, ybuf[slot].astype(BF16))
    out = _rms(h4, g_ref[...])

    @pl.when(j < last)
    def _():
        outp_ref[...] = out

    @pl.when(j == last)
    def _():
        outs_ref[...] = out


def _combine_final(h3, route, tbl, ys, g_final, n_prompt):
    n, d = h3.shape
    tm = ROW_TILE
    n_sample = n - n_prompt
    assert n_sample == tm and n_prompt % tm == 0
    last_prompt = n_prompt // tm - 1
    grid_spec = pltpu.PrefetchScalarGridSpec(
        num_scalar_prefetch=1,
        grid=(n // tm,),
        in_specs=[pl.BlockSpec((tm, d), lambda j, *_: (j, 0)),
                  pl.BlockSpec((tm, LANES), lambda j, *_: (j, 0)),
                  pl.BlockSpec(memory_space=pl.ANY),
                  pl.BlockSpec((1, d), lambda j, *_: (0, 0))],
        out_specs=[pl.BlockSpec((tm, d), lambda j, *_: (jnp.minimum(j, last_prompt), 0)),
                   pl.BlockSpec((tm, d), lambda j, *_: (0, 0))],
        scratch_shapes=[pltpu.VMEM((2, LOCAL_ROWS, d), F32), pltpu.SemaphoreType.DMA((2,))],
    )
    return pl.pallas_call(
        _combine_kernel,
        grid_spec=grid_spec,
        out_shape=[jax.ShapeDtypeStruct((n_prompt, d), F32), jax.ShapeDtypeStruct((n_sample, d), F32)],
        compiler_params=_params("arbitrary"),
        name="moe_combine",
    )(tbl, h3, route, ys, g_final)


def _row(v):
    return v.reshape(1, -1).astype(F32)


def _routing_tables(counts):
    n_row_tiles = counts.shape[0]
    cnt = counts[:, 0, :N_EXPERTS].astype(I32)
    seg = ((cnt + SUBLANES - 1) // SUBLANES) * SUBLANES
    local_start = jnp.cumsum(seg, axis=1) - seg
    rows = jnp.sum(seg, axis=0)
    padded = ((rows + MOE_TILE - 1) // MOE_TILE) * MOE_TILE
    ends = jnp.cumsum(padded)
    starts = ends - padded
    sorted_start = starts[None, :] + jnp.cumsum(seg, axis=0) - seg
    tbl = jnp.concatenate([local_start, seg, sorted_start], axis=1).reshape(-1).astype(I32)
    tail = jnp.concatenate([starts + rows, padded - rows]).astype(I32)
    max_rows = n_row_tiles * (TOP_K * ROW_TILE + N_EXPERTS * (SUBLANES - 1)) + N_EXPERTS * (MOE_TILE - SUBLANES)
    n_tiles = -(-max_rows // MOE_TILE)
    tile_start = jnp.arange(n_tiles, dtype=I32) * MOE_TILE
    tile_start = jnp.minimum(tile_start, ends[-1] - MOE_TILE)
    tile_expert = jnp.minimum(jnp.sum((tile_start[:, None] >= ends[None, :]).astype(I32), axis=1), N_EXPERTS - 1)
    n_used = (ends[-1] // MOE_TILE).reshape(1).astype(I32)
    return tbl, tail, tile_expert.astype(I32), n_used, n_tiles


def kernel(x_prompt, x_sample, state_conv, state_h, cache_k, cache_v, g_mix, g_ffn, g_kv, g_final, a_w_gate, a_b_gate, a_w_in, a_b_in, a_conv_w, a_conv_b, a_w_r, a_b_r, a_w_i, a_b_i, a_lam, a_w_out, a_b_out, w_kv, b_kv, rel_bias, b_w_q, b_b_q, b_sinks, b_w_o, b_b_o, f_w_gate, f_w_up, f_w_down, m_w_router, m_w_gate, m_w_up, m_w_down):
    bp, seq, d = x_prompt.shape
    bs, steps, _ = x_sample.shape
    n_prompt = bp * seq
    n_sample = bs * steps
    n = n_prompt + n_sample
    assert seq % MIX_TILE == 0 and seq % WINDOW == 0 and n_prompt % ROW_TILE == 0
    assert n_sample == ROW_TILE and bs % SAMPLE_GROUP == 0

    mix_w = dict(g=_row(g_mix[0]), wg=a_w_gate[0].astype(BF16), bg=_row(a_b_gate[0]),
                 win=a_w_in[0].astype(BF16), bin=_row(a_b_in[0]), cw=a_conv_w[0], cb=_row(a_conv_b[0]),
                 wr=a_w_r[0].astype(BF16), br=_row(a_b_r[0]), wi=a_w_i[0].astype(BF16), bi=_row(a_b_i[0]),
                 lam=_row(a_lam[0]), wout=a_w_out[0].astype(BF16), bout=_row(a_b_out[0]))

    ffn_w = dict(gf=_row(g_ffn[0]), wg=f_w_gate[0].astype(BF16), wu=f_w_up[0].astype(BF16),
                 wd=f_w_down[0].astype(BF16), gkv=_row(g_kv), wkv=w_kv.astype(BF16), bkv=_row(b_kv),
                 gq=_row(g_mix[1]), wq=b_w_q[0].astype(BF16), bq=_row(b_b_q[0]))

    h2_p, kv_p, q_p, p_conv, p_h, moe_wg, moe_wu = _layer0_prompt(x_prompt, mix_w, ffn_w, [m_w_gate[0], m_w_up[0]])
    h1_s, s_conv_tm, s_h = _mixer_sample(jnp.transpose(x_sample, (1, 0, 2)),
                                         jnp.transpose(state_conv[0], (1, 0, 2)), state_h[0], mix_w)
    h2_s, kv_s, q_s = _ffn_rows(jnp.transpose(h1_s, (1, 0, 2)).reshape(n_sample, d), ffn_w)

    bias, bias_rows = _bias_band(rel_bias, steps)
    sinks = b_sinks[0].astype(F32)
    o_p, moe_wd = _attn_prompt(q_p, kv_p, bias, sinks, bp, seq, [m_w_down[0]])
    q4 = q_s.reshape(bs, steps, N_KV, GROUP, HEAD_DIM).transpose(0, 2, 1, 3, 4)
    q4 = q4.reshape(bs, N_KV, steps * GROUP, HEAD_DIM)
    kv_new = kv_s.reshape(bs, steps, 2 * KV_DIM)
    sink_rows = jnp.broadcast_to(sinks.reshape(N_KV, 1, GROUP), (N_KV, steps, GROUP)).reshape(N_KV, steps * GROUP, 1)
    o4, s_kt, s_vt = _attn_sample(q4, jnp.transpose(cache_k, (0, 2, 3, 1)), jnp.transpose(cache_v, (0, 2, 3, 1)),
                                  kv_new, jnp.transpose(kv_new, (0, 2, 1)), bias_rows, sink_rows)
    o_s = o4.reshape(bs, N_KV, steps, GROUP, HEAD_DIM).transpose(0, 2, 1, 3, 4).reshape(n_sample, d)

    wr_pad = jnp.zeros((d, LANES), F32).at[:, :N_EXPERTS].set(m_w_router[0])
    wr_hi = wr_pad.astype(BF16)
    wr_lo = (wr_pad - wr_hi.astype(F32)).astype(BF16)
    h3, xn, route, counts = _oproj_router(
        o_p, o_s, h2_p, h2_s,
        dict(wo=b_w_o[0].astype(BF16), bo=_row(b_b_o[0]), gf=_row(g_ffn[1]),
             wr3=jnp.concatenate([wr_hi, wr_hi, wr_lo], axis=0)))

    tbl, tail, tile_expert, n_used, n_tiles = _routing_tables(counts)
    xs = _sort_place(xn, route, tbl, tail, n_used, n_tiles * MOE_TILE)
    ys = _moe_experts(xs, tile_expert, n_used, moe_wg, moe_wu, moe_wd)
    y_p, y_s = _combine_final(h3, route, tbl, ys, _row(g_final), n_prompt)

    y_prompt = y_p.reshape(bp, seq, d)
    y_sample = y_s.reshape(bs, steps, d)
    kv_last = jnp.stack([kv_p[b * seq + seq - WINDOW:(b + 1) * seq] for b in range(bp)])
    p_k = kv_last[:, :, :KV_DIM].reshape(bp, WINDOW, N_KV, HEAD_DIM)
    p_v = kv_last[:, :, KV_DIM:].reshape(bp, WINDOW, N_KV, HEAD_DIM)
    s_k = jnp.transpose(s_kt, (0, 3, 1, 2))
    s_v = jnp.transpose(s_vt, (0, 3, 1, 2))
    return (y_prompt, y_sample, p_conv[None], p_h.reshape(1, bp, LRU_WIDTH), p_k, p_v,
            jnp.transpose(s_conv_tm, (1, 0, 2))[None], s_h[None], s_k, s_v)
```

```python
import functools
import math

import jax
import jax.numpy as jnp
from jax import lax
from jax.experimental import pallas as pl
from jax.experimental.pallas import tpu as pltpu

D_MODEL = 1024
LRU_WIDTH = D_MODEL
LRU_BLOCK_W = 256
LRU_BLOCKS = LRU_WIDTH // LRU_BLOCK_W
CONV_W = 4
LRU_C = 8.0
HEAD_DIM = 64
N_HEADS = D_MODEL // HEAD_DIM
N_KV = 2
GROUP = N_HEADS // N_KV
KV_DIM = N_KV * HEAD_DIM
WINDOW = 128
NUM_BUCKETS = 32
MAX_DISTANCE = 128
D_FF = 3 * D_MODEL
N_EXPERTS = 8
TOP_K = 2
D_EXPERT = 7 * D_MODEL // 2
EPS = 1e-6
NEG = -1e30

BF16 = jnp.bfloat16
F32 = jnp.float32
I32 = jnp.int32

SUBLANES = 8
LANES = 128
VMEM_LIMIT_BYTES = 56 * 1024 * 1024

ROW_TILE = 512
MIX_TILE = 256
FF_CHUNK = 1024
MOE_TILE = 512
MOE_CHUNKS = 2


def _params(*semantics):
    return pltpu.CompilerParams(dimension_semantics=semantics, vmem_limit_bytes=VMEM_LIMIT_BYTES)


def _const_spec(shape):
    zeros = (0,) * len(shape)
    return pl.BlockSpec(shape, lambda *_: zeros, pipeline_mode=pl.Buffered(1))


def _dot(a, b):
    return jnp.dot(a, b, preferred_element_type=F32)


def _rms(x, g):
    ms = jnp.mean(x * x, axis=-1, keepdims=True)
    return x * lax.rsqrt(ms + EPS) * g


def _sigmoid(x):
    return 1.0 / (1.0 + jnp.exp(-x))


def _gelu_tanh(x):
    return 0.5 * x * (1.0 + jnp.tanh(0.7978845608028654 * (x + 0.044715 * (x * x * x))))


def _log_sigmoid(x):
    return jnp.minimum(x, 0.0) - jnp.log1p(jnp.exp(-jnp.abs(x)))


def _lru_gates(xc, wr_ref, br, wi_ref, bi, lam):
    xcb = xc.astype(BF16)
    rs, gs = [], []
    for n in range(LRU_BLOCKS):
        xn = xcb[:, n * LRU_BLOCK_W:(n + 1) * LRU_BLOCK_W]
        rs.append(_dot(xn, wr_ref[n]))
        gs.append(_dot(xn, wi_ref[n]))
    r = _sigmoid(jnp.concatenate(rs, axis=1) + br)
    i = _sigmoid(jnp.concatenate(gs, axis=1) + bi)
    log_a = LRU_C * r * _log_sigmoid(lam)
    a = jnp.exp(log_a)
    mult = jnp.sqrt(1.0 - a * a)
    return a, mult * (i * xc)


def _interleave(order, *stage_generators):
    results = [None] * len(stage_generators)
    finished = set()

    def advance(idx):
        if idx in finished:
            return
        try:
            next(stage_generators[idx])
        except StopIteration as done:
            results[idx] = done.value
            finished.add(idx)

    for idx in order:
        advance(idx)
    while len(finished) < len(stage_generators):
        for idx in range(len(stage_generators)):
            advance(idx)
    return results


def _mixer_stages(x, g_ref, wg_ref, bg_ref, win_ref, bin_ref, cw_ref, cb_ref,
                  wr_ref, br_ref, wi_ref, bi_ref, lam_ref, wout_ref, bout_ref, xr_buf, h_carry):
    tt = x.shape[0]
    pad = SUBLANES
    u = _rms(x, g_ref[...]).astype(BF16)
    gate = _gelu_tanh(_dot(u, wg_ref[...]) + bg_ref[...])
    xr = _dot(u, win_ref[...]) + bin_ref[...]
    yield
    xr_buf[pad:pad + tt, :] = xr
    xc = cb_ref[...] + cw_ref[CONV_W - 1:CONV_W, :] * xr
    for k in range(CONV_W - 1):
        back = CONV_W - 1 - k
        xc = xc + cw_ref[k:k + 1, :] * xr_buf[pad - back:pad - back + tt, :]
    xr_buf[0:pad, :] = xr[tt - pad:tt, :]
    yield

    a, b = _lru_gates(xc, wr_ref, br_ref[...], wi_ref, bi_ref[...], lam_ref[...])
    yield

    groups = tt // SUBLANES
    a3 = a.reshape(groups, SUBLANES, LRU_WIDTH)
    b3 = b.reshape(groups, SUBLANES, LRU_WIDTH)
    row = lax.broadcasted_iota(I32, (1, SUBLANES, LRU_WIDTH), 1)
    step = 1
    while step < SUBLANES:
        keep = row >= step
        a_prev = jnp.where(keep, pltpu.roll(a3, step, axis=1), 1.0)
        b_prev = jnp.where(keep, pltpu.roll(b3, step, axis=1), 0.0)
        b3 = b3 + a3 * b_prev
        a3 = a3 * a_prev
        step *= 2
    yield
    h_prev = h_carry[0:1, :]
    hs = []
    for gi in range(groups):
        hg = b3[gi] + a3[gi] * h_prev
        hs.append(hg)
        h_prev = hg[SUBLANES - 1:SUBLANES, :]
        if gi + 1 == groups // 2:
            yield
    h = jnp.concatenate(hs, axis=0)
    h_carry[0:1, :] = h_prev
    yield
    y = _dot((h * gate).astype(BF16), wout_ref[...]) + bout_ref[...]
    return x + y, xr[tt - (CONV_W - 1):tt, :], h_prev


LAYER0_ORDER = (0, 1, 0, 0, 1, 0, 0, 1, 0, 0)
N_MIX_REFS = 14
N_FFN_REFS = 10


def _ffn_stages(h1, gf_ref, wg_ref, wu_ref, wd_ref, gkv_ref, wkv_ref, bkv_ref, gq_ref, wq_ref, bq_ref):
    u = _rms(h1, gf_ref[...]).astype(BF16)
    acc = h1
    n_chunks = D_FF // FF_CHUNK
    for c in range(n_chunks):
        cols = slice(c * FF_CHUNK, (c + 1) * FF_CHUNK)
        g = _dot(u, wg_ref[:, cols])
        up = _dot(u, wu_ref[:, cols])
        mid = (g * _sigmoid(g) * up).astype(BF16)
        acc = acc + _dot(mid, wd_ref[cols, :])
        if c + 1 < n_chunks:
            yield
    kv = _dot(_rms(acc, gkv_ref[...]).astype(BF16), wkv_ref[...]) + bkv_ref[...]
    q = _dot(_rms(acc, gq_ref[...]).astype(BF16), wq_ref[...]) + bq_ref[...]
    return acc, kv, (q * (HEAD_DIM ** -0.5)).astype(BF16)


def _layer0_prompt_kernel(*refs, n_tiles, tiles_per_seq, n_casts):
    x_ref = refs[0]
    mix_refs = refs[1:1 + N_MIX_REFS]
    ffn_refs = refs[1 + N_MIX_REFS:1 + N_MIX_REFS + N_FFN_REFS]
    rest = refs[1 + N_MIX_REFS + N_FFN_REFS:]
    cast_srcs, rest = rest[:n_casts], rest[n_casts:]
    h2_ref, kv_ref, q_ref, conv_ref, hlast_ref = rest[:5]
    cast_dsts = rest[5:5 + n_casts]
    xr_buf, h_carry, h1_buf = rest[5 + n_casts:]
    for src_ref, dst_ref in zip(cast_srcs, cast_dsts):
        dst_ref[...] = src_ref[...].astype(BF16)
    s = pl.program_id(0)
    slot = lax.rem(s, 2)

    @pl.when(s == 0)
    def _():
        h1_buf[...] = jnp.zeros_like(h1_buf)

    @pl.when(lax.rem(s, tiles_per_seq) == 0)
    def _():
        xr_buf[0:SUBLANES, :] = jnp.zeros((SUBLANES, LRU_WIDTH), F32)
        h_carry[...] = jnp.zeros_like(h_carry)

    (h1, conv_tail, h_last), (h2, kv, q) = _interleave(
        LAYER0_ORDER,
        _mixer_stages(x_ref[...], *mix_refs, xr_buf, h_carry), _ffn_stages(h1_buf[1 - slot], *ffn_refs))
    h2_ref[...] = h2
    kv_ref[...] = kv
    q_ref[...] = q
    h1_buf[slot] = h1

    @pl.when(s < n_tiles)
    def _():
        b = s // tiles_per_seq
        conv_ref[b] = conv_tail
        hlast_ref[b] = h_last


def _layer0_prompt(x, mix_w, ffn_w, cast_ws):
    b, t, d = x.shape
    tt = MIX_TILE
    n_tiles = (b * t) // tt
    casts = [_side_cast_specs(w, n_tiles, lambda s: jnp.minimum(s, n_tiles - 1)) for w in cast_ws]
    vec = lambda n: _const_spec((1, n))
    mix_specs = [
        vec(d), _const_spec((d, LRU_WIDTH)), vec(LRU_WIDTH), _const_spec((d, LRU_WIDTH)), vec(LRU_WIDTH),
        _const_spec((CONV_W, LRU_WIDTH)), vec(LRU_WIDTH),
        _const_spec((LRU_BLOCKS, LRU_BLOCK_W, LRU_BLOCK_W)), vec(LRU_WIDTH),
        _const_spec((LRU_BLOCKS, LRU_BLOCK_W, LRU_BLOCK_W)), vec(LRU_WIDTH),
        vec(LRU_WIDTH), _const_spec((LRU_WIDTH, d)), vec(d),
    ]
    ffn_specs = [
        vec(d), _const_spec((d, D_FF)), _const_spec((d, D_FF)), _const_spec((D_FF, d)),
        vec(d), _const_spec((d, 2 * KV_DIM)), vec(2 * KV_DIM), vec(d), _const_spec((d, d)), vec(d),
    ]
    assert len(mix_specs) == N_MIX_REFS and len(ffn_specs) == N_FFN_REFS
    prev = lambda s: (jnp.maximum(s - 1, 0), 0)
    whole = lambda shape: pl.BlockSpec(shape, lambda s: (0,) * len(shape))
    return pl.pallas_call(
        functools.partial(_layer0_prompt_kernel, n_tiles=n_tiles, tiles_per_seq=t // tt, n_casts=len(casts)),
        grid=(n_tiles + 1,),
        in_specs=([pl.BlockSpec((tt, d), lambda s: (jnp.minimum(s, n_tiles - 1), 0))] + mix_specs + ffn_specs
                  + [c[0] for c in casts]),
        out_specs=[pl.BlockSpec((tt, d), prev), pl.BlockSpec((tt, 2 * KV_DIM), prev), pl.BlockSpec((tt, d), prev),
                   whole((b, CONV_W - 1, LRU_WIDTH)), whole((b, 1, LRU_WIDTH))] + [c[1] for c in casts],
        out_shape=[jax.ShapeDtypeStruct((b * t, d), F32), jax.ShapeDtypeStruct((b * t, 2 * KV_DIM), F32),
                   jax.ShapeDtypeStruct((b * t, d), BF16),
                   jax.ShapeDtypeStruct((b, CONV_W - 1, LRU_WIDTH), F32),
                   jax.ShapeDtypeStruct((b, 1, LRU_WIDTH), F32)] + [c[2] for c in casts],
        scratch_shapes=[pltpu.VMEM((SUBLANES + tt, LRU_WIDTH), F32), pltpu.VMEM((SUBLANES, LRU_WIDTH), F32),
                        pltpu.VMEM((2, tt, d), F32)],
        compiler_params=_params("arbitrary"),
        name="layer0_prompt",
    )(x.reshape(b * t, d), *[mix_w[k] for k in MIX_KEYS], *[ffn_w[k] for k in FFN_KEYS], *cast_ws)


MIX_KEYS = ("g", "wg", "bg", "win", "bin", "cw", "cb", "wr", "br", "wi", "bi", "lam", "wout", "bout")
FFN_KEYS = ("gf", "wg", "wu", "wd", "gkv", "wkv", "bkv", "gq", "wq", "bq")


def _mixer_sample_kernel(x_ref, cs_ref, h0_ref, g_ref, wg_ref, bg_ref, win_ref, bin_ref, cw_ref, cb_ref,
                         wr_ref, br_ref, wi_ref, bi_ref, lam_ref, wout_ref, bout_ref,
                         h1_ref, conv_ref, hlast_ref):
    steps, nb, d = x_ref.shape
    x = x_ref[...].reshape(steps * nb, d)
    u = _rms(x, g_ref[...]).astype(BF16)
    gate = _gelu_tanh(_dot(u, wg_ref[...]) + bg_ref[...])
    xr = _dot(u, win_ref[...]) + bin_ref[...]
    xpad = [cs_ref[k] for k in range(CONV_W - 1)] + [xr[s * nb:(s + 1) * nb, :] for s in range(steps)]
    xcs = []
    for s in range(steps):
        acc = cb_ref[...] + cw_ref[0:1, :] * xpad[s]
        for k in range(1, CONV_W):
            acc = acc + cw_ref[k:k + 1, :] * xpad[s + k]
        xcs.append(acc)
    for k in range(CONV_W - 1):
        conv_ref[k] = xpad[steps + k]
    xc = jnp.concatenate(xcs, axis=0)
    a, b = _lru_gates(xc, wr_ref, br_ref[...], wi_ref, bi_ref[...], lam_ref[...])
    h = h0_ref[...]
    hs = []
    for s in range(steps):
        h = a[s * nb:(s + 1) * nb, :] * h + b[s * nb:(s + 1) * nb, :]
        hs.append(h)
    hlast_ref[...] = h
    hcat = jnp.concatenate(hs, axis=0)
    y = _dot((hcat * gate).astype(BF16), wout_ref[...]) + bout_ref[...]
    h1_ref[...] = (x + y).reshape(steps, nb, d)


def _mixer_sample(x_tm, cs_tm, h0, w):
    steps, nb, d = x_tm.shape
    full = lambda shape: pl.BlockSpec(shape, lambda i: (0,) * len(shape))
    vec = lambda n: full((1, n))
    in_specs = [
        full((steps, nb, d)), full((CONV_W - 1, nb, LRU_WIDTH)), full((nb, LRU_WIDTH)),
        vec(d), full((d, LRU_WIDTH)), vec(LRU_WIDTH), full((d, LRU_WIDTH)), vec(LRU_WIDTH),
        full((CONV_W, LRU_WIDTH)), vec(LRU_WIDTH),
        full((LRU_BLOCKS, LRU_BLOCK_W, LRU_BLOCK_W)), vec(LRU_WIDTH),
        full((LRU_BLOCKS, LRU_BLOCK_W, LRU_BLOCK_W)), vec(LRU_WIDTH),
        vec(LRU_WIDTH), full((LRU_WIDTH, d)), vec(d),
    ]
    out_specs = [full((steps, nb, d)), full((CONV_W - 1, nb, LRU_WIDTH)), full((nb, LRU_WIDTH))]
    out_shape = [
        jax.ShapeDtypeStruct((steps, nb, d), F32),
        jax.ShapeDtypeStruct((CONV_W - 1, nb, LRU_WIDTH), F32),
        jax.ShapeDtypeStruct((nb, LRU_WIDTH), F32),
    ]
    return pl.pallas_call(
        _mixer_sample_kernel, grid=(1,), in_specs=in_specs, out_specs=out_specs, out_shape=out_shape,
        compiler_params=_params("arbitrary"), name="mixer_sample",
    )(x_tm, cs_tm, h0, w["g"], w["wg"], w["bg"], w["win"], w["bin"], w["cw"], w["cb"], w["wr"], w["br"],
      w["wi"], w["bi"], w["lam"], w["wout"], w["bout"])


def _two_part_specs(n_prompt, n_sample, width):
    assert n_sample == ROW_TILE and n_prompt % ROW_TILE == 0
    last_prompt = n_prompt // ROW_TILE - 1
    return [pl.BlockSpec((ROW_TILE, width), lambda i: (jnp.minimum(i, last_prompt), 0)),
            pl.BlockSpec((ROW_TILE, width), lambda i: (0, 0))]


def _two_part_tile(prompt_ref, sample_ref):
    is_sample = pl.program_id(0) == pl.num_programs(0) - 1
    return jnp.where(is_sample, sample_ref[...], prompt_ref[...])


def _side_cast_specs(w, n_steps, step_of):
    ne, rows, cols = w.shape
    per_expert = n_steps // ne
    blk = rows // per_expert
    assert per_expert * ne == n_steps and blk * per_expert == rows and blk % (2 * SUBLANES) == 0
    index = lambda *ids: (step_of(*ids) // per_expert, step_of(*ids) % per_expert, 0)
    spec = pl.BlockSpec((1, blk, cols), index)
    return spec, spec, jax.ShapeDtypeStruct(w.shape, BF16)


def _ffn_rows_kernel(h1_ref, *refs):
    h2_ref, kv_ref, q_ref = refs[N_FFN_REFS:]
    (h2_ref[...], kv_ref[...], q_ref[...]), = _interleave((), _ffn_stages(h1_ref[...], *refs[:N_FFN_REFS]))


def _ffn_rows(h1, w):
    n, d = h1.shape
    tm = ROW_TILE
    row = lambda width: pl.BlockSpec((tm, width), lambda i: (i, 0))
    vec = lambda width: _const_spec((1, width))
    in_specs = [
        row(d), vec(d), _const_spec((d, D_FF)), _const_spec((d, D_FF)), _const_spec((D_FF, d)),
        vec(d), _const_spec((d, 2 * KV_DIM)), vec(2 * KV_DIM), vec(d), _const_spec((d, d)), vec(d),
    ]
    return pl.pallas_call(
        _ffn_rows_kernel,
        grid=(n // tm,),
        in_specs=in_specs,
        out_specs=[row(d), row(2 * KV_DIM), row(d)],
        out_shape=[jax.ShapeDtypeStruct((n, d), F32), jax.ShapeDtypeStruct((n, 2 * KV_DIM), F32),
                   jax.ShapeDtypeStruct((n, d), BF16)],
        compiler_params=_params("arbitrary"),
        name="ffn_rows",
    )(h1, *[w[k] for k in FFN_KEYS])


def _head_of(j, parity, pair):
    return j * GROUP + 2 * pair + parity


def _distance_bias(dist, table_ref, head):
    max_exact = NUM_BUCKETS // 2
    n = jnp.maximum(dist, 0)
    large = jnp.full(dist.shape, max_exact, I32)
    for step in range(1, NUM_BUCKETS - max_exact):
        threshold = math.ceil(max_exact * (MAX_DISTANCE / max_exact) ** (step / (NUM_BUCKETS - max_exact)))
        large = large + jnp.where(n >= threshold, 1, 0)
    bucket = jnp.where(n < max_exact, n, large)
    acc = jnp.zeros(dist.shape, F32)
    for bkt in range(NUM_BUCKETS):
        acc = jnp.where(bucket == bkt, table_ref[bkt, head], acc)
    return acc


def _bias_band_kernel(table_ref, folded_ref, rows_ref):
    w = WINDOW
    ci = lax.broadcasted_iota(I32, (w, w), 0)
    qi = lax.broadcasted_iota(I32, (w, w), 1)
    folded_dist = jnp.where(ci > qi, qi + w - ci, qi - ci)
    for j in range(N_KV):
        for parity in range(2):
            for pair in range(GROUP // 2):
                folded_ref[j, parity, :, pair * w:(pair + 1) * w] = _distance_bias(
                    folded_dist, table_ref, _head_of(j, parity, pair))
    n_rows, n_keys = rows_ref.shape[1], rows_ref.shape[2]
    ri = lax.broadcasted_iota(I32, (GROUP, n_keys), 0)
    si = lax.broadcasted_iota(I32, (GROUP, n_keys), 1)
    for j in range(N_KV):
        for t in range(n_rows // GROUP):
            acc = jnp.zeros((GROUP, n_keys), F32)
            for g in range(GROUP):
                acc = jnp.where(ri == g, _distance_bias(t + w - si, table_ref, j * GROUP + g), acc)
            rows_ref[j, t * GROUP:(t + 1) * GROUP, :] = acc


def _bias_band(rel_table, steps):
    folded = (N_KV, 2, WINDOW, (GROUP // 2) * WINDOW)
    rows = (N_KV, steps * GROUP, WINDOW + steps + (-steps) % SUBLANES)
    return pl.pallas_call(
        _bias_band_kernel,
        grid=(1,),
        in_specs=[pl.BlockSpec(memory_space=pltpu.SMEM)],
        out_specs=[pl.BlockSpec(folded, lambda i: (0, 0, 0, 0)), pl.BlockSpec(rows, lambda i: (0, 0, 0))],
        out_shape=[jax.ShapeDtypeStruct(folded, F32), jax.ShapeDtypeStruct(rows, F32)],
        compiler_params=_params("arbitrary"),
        name="bias_band",
    )(rel_table)


ATTN_BLOCKS = 8


def _attn_prompt_kernel(sink_ref, q_ref, kvp_ref, kvc_ref, bias_ref, *refs):
    n_casts = (len(refs) - 1) // 2
    o_ref = refs[n_casts]
    for src_ref, dst_ref in zip(refs[:n_casts], refs[n_casts + 1:]):
        dst_ref[...] = src_ref[...].astype(BF16)
    w = WINDOW
    pairs = GROUP // 2
    assert 2 * HEAD_DIM == LANES and KV_DIM == LANES
    ci = lax.broadcasted_iota(I32, (w, pairs * w), 0)
    qi = lax.broadcasted_iota(I32, (w, pairs * w), 1) % w
    from_prev = ci > qi
    has_prev = pl.program_id(1) > 0

    kv = jnp.concatenate([kvp_ref[...], kvc_ref[...]], axis=0)
    low = lax.broadcasted_iota(I32, (kv.shape[0], LANES), 1) < HEAD_DIM

    def halves(x):
        swapped = pltpu.roll(x, HEAD_DIM, axis=1)
        return (((jnp.where(low, x, 0.0)).astype(BF16), (jnp.where(low, 0.0, swapped)).astype(BF16)),
                ((jnp.where(low, swapped, 0.0)).astype(BF16), (jnp.where(low, 0.0, x)).astype(BF16)))

    k_ops = halves(kv[:, 0:KV_DIM])
    v_ops = halves(kv[:, KV_DIM:2 * KV_DIM])

    sinks = [[jnp.concatenate([jnp.full((1, w), sink_ref[_head_of(j, parity, b)], F32) for b in range(pairs)], axis=1)
              for parity in range(2)] for j in range(N_KV)]
    nt = (((1,), (1,)), ((), ()))
    tn = (((0,), (0,)), ((), ()))
    for i in range(ATTN_BLOCKS):
        q_rows = slice(i * w, (i + 1) * w)
        prev_rows = slice(i * w, (i + 1) * w)
        own_rows = slice((i + 1) * w, (i + 2) * w)
        for j in range(N_KV):
            q4 = jnp.concatenate([q_ref[q_rows, (j * pairs + b) * LANES:(j * pairs + b + 1) * LANES]
                                  for b in range(pairs)], axis=0)
            acc = None
            for parity in range(2):
                k_op, v_op, sink = k_ops[j][parity], v_ops[j][parity], sinks[j][parity]
                s_prev = lax.dot_general(k_op[prev_rows], q4, nt, preferred_element_type=F32)
                s_own = lax.dot_general(k_op[own_rows], q4, nt, preferred_element_type=F32)
                if i == 0:
                    s_prev = jnp.where(has_prev, s_prev, NEG)
                s = jnp.where(from_prev, s_prev, s_own) + bias_ref[j, parity]
                m = jnp.maximum(jnp.max(s, axis=0, keepdims=True), sink)
                p = jnp.exp(s - m)
                denom = jnp.sum(p, axis=0, keepdims=True) + jnp.exp(sink - m)
                p = p * (1.0 / denom)
                o = (lax.dot_general(jnp.where(from_prev, p, 0.0).astype(BF16), v_op[prev_rows], tn,
                                     preferred_element_type=F32)
                     + lax.dot_general(jnp.where(from_prev, 0.0, p).astype(BF16), v_op[own_rows], tn,
                                       preferred_element_type=F32))
                acc = o if acc is None else acc + o
            for b in range(pairs):
                o_ref[q_rows, (j * pairs + b) * LANES:(j * pairs + b + 1) * LANES] = acc[b * w:(b + 1) * w].astype(BF16)


def _attn_prompt(q, kv, bias, sinks, batch, seq, cast_ws):
    nb = seq // WINDOW
    assert nb % ATTN_BLOCKS == 0
    ns = nb // ATTN_BLOCKS
    rows = ATTN_BLOCKS * WINDOW
    d = q.shape[1]
    casts = [_side_cast_specs(w, batch * ns, lambda b, n: b * ns + n) for w in cast_ws]
    in_specs = [
        pl.BlockSpec(memory_space=pltpu.SMEM),
        pl.BlockSpec((rows, d), lambda b, n: (b * ns + n, 0)),
        pl.BlockSpec((WINDOW, 2 * KV_DIM), lambda b, n: (b * nb + jnp.maximum(n * ATTN_BLOCKS - 1, 0), 0)),
        pl.BlockSpec((rows, 2 * KV_DIM), lambda b, n: (b * ns + n, 0)),
        _const_spec(bias.shape),
    ] + [c[0] for c in casts]
    return pl.pallas_call(
        _attn_prompt_kernel,
        grid=(batch, ns),
        in_specs=in_specs,
        out_specs=[pl.BlockSpec((rows, d), lambda b, n: (b * ns + n, 0))] + [c[1] for c in casts],
        out_shape=[jax.ShapeDtypeStruct((batch * seq, d), BF16)] + [c[2] for c in casts],
        compiler_params=_params("arbitrary", "arbitrary"),
        name="attn_prompt",
    )(sinks, q, kv, kv, bias, *cast_ws)


SAMPLE_GROUP = 8


def _attn_sample_kernel(q_ref, ckt_ref, cvt_ref, new_ref, newt_ref, bias_ref, sink_ref, o_ref, skt_ref, svt_ref):
    gb = q_ref.shape[0]
    steps = new_ref.shape[1]
    rows = steps * GROUP
    pad = (-steps) % SUBLANES
    ri = lax.broadcasted_iota(I32, (rows, WINDOW + steps + pad), 0)
    si = lax.broadcasted_iota(I32, (rows, WINDOW + steps + pad), 1)
    dist = ri // GROUP + WINDOW - si
    mask = (dist >= 0) & (dist < WINDOW)
    mask_c, mask_n = mask[:, :WINDOW], mask[:, WINDOW:]
    zeros = jnp.zeros((pad, 2 * KV_DIM), F32)

    def scores(b, j):
        new = jnp.concatenate([new_ref[b], zeros], axis=0).astype(BF16)
        q = q_ref[b, j]
        s_c = _dot(q, ckt_ref[b, j].astype(BF16))
        s_n = lax.dot_general(q, new[:, j * HEAD_DIM:(j + 1) * HEAD_DIM], (((1,), (1,)), ((), ())),
                              preferred_element_type=F32)
        bias = bias_ref[j]
        return (jnp.where(mask_c, s_c + bias[:, :WINDOW], NEG), jnp.where(mask_n, s_n + bias[:, WINDOW:], NEG), new)

    tasks = [(b, j) for b in range(gb) for j in range(N_KV)]
    all_scores = [scores(b, j) for b, j in tasks]
    for (b, j), (s_c, s_n, new) in zip(tasks, all_scores):
        sink = sink_ref[j]
        m = jnp.maximum(jnp.maximum(jnp.max(s_c, axis=-1, keepdims=True), jnp.max(s_n, axis=-1, keepdims=True)), sink)
        p_c = jnp.exp(s_c - m)
        p_n = jnp.exp(s_n - m)
        denom = jnp.sum(p_c, axis=-1, keepdims=True) + jnp.sum(p_n, axis=-1, keepdims=True) + jnp.exp(sink - m)
        pv = lax.dot_general(p_c.astype(BF16), cvt_ref[b, j].astype(BF16), (((1,), (1,)), ((), ())),
                             preferred_element_type=F32)
        pv = pv + _dot(p_n.astype(BF16), new[:, KV_DIM + j * HEAD_DIM:KV_DIM + (j + 1) * HEAD_DIM])
        o_ref[b, j] = (pv / denom).astype(BF16)

    lane = lax.broadcasted_iota(I32, (HEAD_DIM, WINDOW), 1)
    for b in range(gb):
        newt = jnp.concatenate([newt_ref[b], jnp.zeros((2 * KV_DIM, WINDOW - steps), F32)], axis=1)
        for j in range(N_KV):
            for src_ref, dst_ref, base in ((ckt_ref, skt_ref, 0), (cvt_ref, svt_ref, KV_DIM)):
                fresh = newt[base + j * HEAD_DIM:base + (j + 1) * HEAD_DIM, :]
                dst_ref[b, j] = pltpu.roll(jnp.where(lane < steps, fresh, src_ref[b, j]), WINDOW - steps, axis=1)


def _attn_sample(q4, ckt, cvt, new, newt, bias_rows, sink_rows):
    nb, _, rows, hd = q4.shape
    steps = new.shape[1]
    keys = bias_rows.shape[2]
    gb = SAMPLE_GROUP
    cache_spec = pl.BlockSpec((gb, N_KV, hd, WINDOW), lambda i: (i, 0, 0, 0))
    in_specs = [
        pl.BlockSpec((gb, N_KV, rows, hd), lambda i: (i, 0, 0, 0)),
        cache_spec, cache_spec,
        pl.BlockSpec((gb, steps, 2 * KV_DIM), lambda i: (i, 0, 0)),
        pl.BlockSpec((gb, 2 * KV_DIM, steps), lambda i: (i, 0, 0)),
        pl.BlockSpec((N_KV, rows, keys), lambda i: (0, 0, 0)),
        pl.BlockSpec((N_KV, rows, 1), lambda i: (0, 0, 0)),
    ]
    return pl.pallas_call(
        _attn_sample_kernel,
        grid=(nb // gb,),
        in_specs=in_specs,
        out_specs=[pl.BlockSpec((gb, N_KV, rows, hd), lambda i: (i, 0, 0, 0)), cache_spec, cache_spec],
        out_shape=[jax.ShapeDtypeStruct(q4.shape, BF16), jax.ShapeDtypeStruct(ckt.shape, F32),
                   jax.ShapeDtypeStruct(cvt.shape, F32)],
        compiler_params=_params("arbitrary"),
        name="attn_sample",
    )(q4, ckt, cvt, new, newt, bias_rows, sink_rows)


def _split_bf16(x):
    hi = x.astype(BF16)
    lo = (x - hi.astype(F32)).astype(BF16)
    return hi, lo


ROUTER_PARTS = 2


def _oproj_router_kernel(op_ref, os_ref, h2p_ref, h2s_ref, wo_ref, bo_ref, gf_ref, wr3_ref,
                         h3_ref, xn_ref, route_ref, counts_ref):
    tm = h3_ref.shape[0]
    parts = ROUTER_PARTS
    pr = tm // parts
    o = _two_part_tile(op_ref, os_ref)
    h2 = _two_part_tile(h2p_ref, h2s_ref)
    lane = lax.broadcasted_iota(I32, (pr, LANES), 1)
    lanef = lane.astype(F32)
    ri = lax.broadcasted_iota(I32, (pr, pr), 0)
    ci = lax.broadcasted_iota(I32, (pr, pr), 1)
    earlier = jnp.where(ri > ci, 1.0, 0.0).astype(BF16)

    def part_stages(p):
        rows = slice(p * pr, (p + 1) * pr)
        h3 = h2[rows, :] + _dot(o[rows, :], wo_ref[...]) + bo_ref[...]
        h3_ref[rows, :] = h3
        yield
        u_hi, u_lo = _split_bf16(_rms(h3, gf_ref[...]))
        xn_ref[rows, :] = u_hi
        logits = _dot(jnp.concatenate([u_hi, u_lo, u_hi], axis=1), wr3_ref[...])
        yield
        logits = jnp.where(lane < N_EXPERTS, logits, -jnp.inf)
        v1 = jnp.max(logits, axis=-1, keepdims=True)
        e1 = jnp.min(jnp.where(logits == v1, lanef, float(LANES)), axis=-1, keepdims=True)
        rest = jnp.where(lanef == e1, -jnp.inf, logits)
        v2 = jnp.max(rest, axis=-1, keepdims=True)
        e2 = jnp.min(jnp.where(rest == v2, lanef, float(LANES)), axis=-1, keepdims=True)
        yield
        ex = jnp.exp(v2 - v1)
        pick1 = lanef == e1
        pick2 = lanef == e2
        sel = jnp.where(pick1 | pick2, 1.0, 0.0)
        return (pick1, pick2), (1.0 / (1.0 + ex), ex / (1.0 + ex)), sel, _dot(earlier, sel.astype(BF16))

    done = _interleave((), *[part_stages(p) for p in range(parts)])
    picks, ranks, gates = [], [], []
    cnt = jnp.zeros((1, LANES), F32)
    for part_picks, part_gates, sel, rank in done:
        picks.append(part_picks)
        gates.append(part_gates)
        ranks.append(rank + cnt)
        cnt = cnt + jnp.sum(sel, axis=0, keepdims=True)

    seg = jnp.floor((cnt + (SUBLANES - 1)) * (1.0 / SUBLANES)) * SUBLANES
    ek = lax.broadcasted_iota(I32, (LANES, LANES), 0)
    el = lax.broadcasted_iota(I32, (LANES, LANES), 1)
    lower_experts = jnp.where(ek < el, 1.0, 0.0).astype(BF16)
    seg_start = _dot(jnp.broadcast_to(seg, (SUBLANES, LANES)).astype(BF16), lower_experts)[0:1, :]
    for p in range(parts):
        local = ranks[p] + seg_start
        lr1 = jnp.sum(jnp.where(picks[p][0], local, 0.0), axis=-1, keepdims=True)
        lr2 = jnp.sum(jnp.where(picks[p][1], local, 0.0), axis=-1, keepdims=True)
        w1, w2 = gates[p]
        route_ref[p * pr:(p + 1) * pr, :] = jnp.where(
            lane == 0, lr1, jnp.where(lane == 1, lr2, jnp.where(lane == 2, w1, jnp.where(lane == 3, w2, 0.0))))
    counts_ref[0] = jnp.broadcast_to(cnt, (SUBLANES, LANES))


def _oproj_router(o_p, o_s, h2_p, h2_s, w):
    d = h2_p.shape[1]
    n = h2_p.shape[0] + h2_s.shape[0]
    tm = ROW_TILE
    row = lambda width: pl.BlockSpec((tm, width), lambda i: (i, 0))
    in_specs = (_two_part_specs(o_p.shape[0], o_s.shape[0], d) + _two_part_specs(h2_p.shape[0], h2_s.shape[0], d) + [
        _const_spec((d, d)), _const_spec((1, d)), _const_spec((1, d)), _const_spec((3 * d, LANES))])
    return pl.pallas_call(
        _oproj_router_kernel,
        grid=(n // tm,),
        in_specs=in_specs,
        out_specs=[row(d), row(d), row(LANES), pl.BlockSpec((1, SUBLANES, LANES), lambda i: (i, 0, 0))],
        out_shape=[jax.ShapeDtypeStruct((n, d), F32), jax.ShapeDtypeStruct((n, d), BF16),
                   jax.ShapeDtypeStruct((n, LANES), F32), jax.ShapeDtypeStruct((n // tm, SUBLANES, LANES), F32)],
        compiler_params=_params("arbitrary"),
        name="oproj_router",
    )(o_p, o_s, h2_p, h2_s, w["wo"], w["bo"], w["gf"], w["wr3"])


LOCAL_ROWS = 1152
XS_WIDTH = D_MODEL + LANES
SEG_TABLE = 3 * N_EXPERTS
SEG_PIECES = tuple(SUBLANES << b for b in reversed(range(7)))


def _segment_copies(src_ref, src_start, dst_ref, dst_start, length, sem, act):
    for piece in SEG_PIECES:
        done = lax.div(length, 2 * piece) * (2 * piece)

        @pl.when(lax.rem(lax.div(length, piece), 2) != 0)
        def _():
            s = pl.multiple_of(src_start + done, SUBLANES)
            t = pl.multiple_of(dst_start + done, SUBLANES)
            act(pltpu.make_async_copy(src_ref.at[pl.ds(s, piece), :], dst_ref.at[pl.ds(t, piece), :], sem))


def _sort_place_kernel(tbl_ref, tail_ref, nu_ref, xn_ref, route_ref, xs_ref, stage, zeros, sems):
    j = pl.program_id(0)
    tm = xn_ref.shape[0]
    xn = xn_ref[...]
    rt = jnp.transpose(route_ref[...])
    row = lax.broadcasted_iota(I32, (LOCAL_ROWS, tm), 0).astype(F32)
    m1 = row == rt[0:1, :]
    m2 = row == rt[1:2, :]
    onehot = jnp.where(m1 | m2, 1.0, 0.0).astype(BF16)
    slot = lax.rem(j, 2)
    buf = stage.at[slot]
    buf[:, 0:D_MODEL] = _dot(onehot, xn)
    gate = jnp.sum(jnp.where(m1, rt[2:3, :], 0.0) + jnp.where(m2, rt[3:4, :], 0.0), axis=-1, keepdims=True)
    buf[:, D_MODEL:XS_WIDTH] = jnp.broadcast_to(gate, (LOCAL_ROWS, LANES))

    def segments(tile, tile_slot, act):
        for e in range(N_EXPERTS):
            base = tile * SEG_TABLE
            _segment_copies(stage.at[tile_slot], tbl_ref[base + e], xs_ref, tbl_ref[base + 2 * N_EXPERTS + e],
                            tbl_ref[base + N_EXPERTS + e], sems.at[tile_slot], act)

    segments(j, slot, lambda cp: cp.start())

    @pl.when(j > 0)
    def _():
        segments(j - 1, 1 - slot, lambda cp: cp.wait())

    @pl.when(j == pl.num_programs(0) - 1)
    def _():
        segments(j, slot, lambda cp: cp.wait())
        zeros[...] = jnp.zeros_like(zeros)
        sem = sems.at[0]

        def tails(act):
            for e in range(N_EXPERTS):
                _segment_copies(zeros, 0, xs_ref, tail_ref[e], tail_ref[N_EXPERTS + e], sem, act)

        def unused(act):
            def body(i, c):
                t = pl.multiple_of(i * MOE_TILE, MOE_TILE)
                act(pltpu.make_async_copy(zeros, xs_ref.at[pl.ds(t, MOE_TILE), :], sem))
                return c
            lax.fori_loop(nu_ref[0], xs_ref.shape[0] // MOE_TILE, body, 0)

        tails(lambda cp: cp.start())
        unused(lambda cp: cp.start())
        tails(lambda cp: cp.wait())
        unused(lambda cp: cp.wait())


def _sort_place(xn, route, tbl, tail, n_used, p_rows):
    n, d = xn.shape
    tm = ROW_TILE
    grid_spec = pltpu.PrefetchScalarGridSpec(
        num_scalar_prefetch=3,
        grid=(n // tm,),
        in_specs=[pl.BlockSpec((tm, d), lambda j, *_: (j, 0)),
                  pl.BlockSpec((tm, LANES), lambda j, *_: (j, 0))],
        out_specs=pl.BlockSpec(memory_space=pl.ANY),
        scratch_shapes=[pltpu.VMEM((2, LOCAL_ROWS, XS_WIDTH), F32), pltpu.VMEM((MOE_TILE, XS_WIDTH), F32),
                        pltpu.SemaphoreType.DMA((2,))],
    )
    return pl.pallas_call(
        _sort_place_kernel,
        grid_spec=grid_spec,
        out_shape=jax.ShapeDtypeStruct((p_rows, XS_WIDTH), F32),
        compiler_params=_params("arbitrary"),
        name="moe_sort_place",
    )(tbl, tail, n_used, xn, route)


def _moe_kernel(te_ref, nu_ref, xs_ref, wg_ref, wu_ref, wd_ref, y_ref):
    i = pl.program_id(0)
    c = pl.program_id(1)
    used = i < nu_ref[0]

    @pl.when(used)
    def _():
        xn = xs_ref[:, 0:D_MODEL].astype(BF16)
        g = _dot(xn, wg_ref[0])
        up = _dot(xn, wu_ref[0])
        mid = (g * _sigmoid(g) * up).astype(BF16)
        y = _dot(mid, wd_ref[0]) * xs_ref[:, D_MODEL:D_MODEL + 1]

        @pl.when(c == 0)
        def _():
            y_ref[...] = y

        @pl.when(c > 0)
        def _():
            y_ref[...] = y_ref[...] + y

    @pl.when(jnp.logical_not(used) & (c == 0))
    def _():
        y_ref[...] = jnp.zeros_like(y_ref)


def _moe_experts(xs, tile_expert, n_used, wg, wu, wd):
    p_rows = xs.shape[0]
    d = D_MODEL
    tm = MOE_TILE
    n_tiles = p_rows // tm
    ch = D_EXPERT // MOE_CHUNKS
    last = MOE_CHUNKS - 1

    def chunk(i, c, nu):
        return jnp.where(i < nu[0], c, last)

    grid_spec = pltpu.PrefetchScalarGridSpec(
        num_scalar_prefetch=2,
        grid=(n_tiles, MOE_CHUNKS),
        in_specs=[
            pl.BlockSpec((tm, XS_WIDTH), lambda i, c, te, nu: (jnp.minimum(i, nu[0] - 1), 0)),
            pl.BlockSpec((1, d, ch), lambda i, c, te, nu: (te[i], 0, chunk(i, c, nu))),
            pl.BlockSpec((1, d, ch), lambda i, c, te, nu: (te[i], 0, chunk(i, c, nu))),
            pl.BlockSpec((1, ch, d), lambda i, c, te, nu: (te[i], chunk(i, c, nu), 0)),
        ],
        out_specs=pl.BlockSpec((tm, d), lambda i, c, te, nu: (i, 0)),
    )
    return pl.pallas_call(
        _moe_kernel,
        grid_spec=grid_spec,
        out_shape=jax.ShapeDtypeStruct((p_rows, d), F32),
        compiler_params=_params("arbitrary", "arbitrary"),
        name="moe_experts",
    )(tile_expert, n_used, xs, wg, wu, wd)


def _combine_kernel(tbl_ref, h3_ref, route_ref, ys_ref, g_ref, outp_ref, outs_ref, ybuf, sems):
    j = pl.program_id(0)
    last = pl.num_programs(0) - 1
    tm = h3_ref.shape[0]
    slot = lax.rem(j, 2)

    def segments(tile, tile_slot, act):
        for e in range(N_EXPERTS):
            base = tile * SEG_TABLE
            _segment_copies(ys_ref, tbl_ref[base + 2 * N_EXPERTS + e], ybuf.at[tile_slot], tbl_ref[base + e],
                            tbl_ref[base + N_EXPERTS + e], sems.at[tile_slot], act)

    @pl.when(j == 0)
    def _():
        ybuf[...] = jnp.zeros_like(ybuf)
        segments(j, slot, lambda cp: cp.start())

    @pl.when(j < last)
    def _():
        segments(j + 1, 1 - slot, lambda cp: cp.start())

    segments(j, slot, lambda cp: cp.wait())

    route = route_ref[...]
    col = lax.broadcasted_iota(I32, (tm, LOCAL_ROWS), 1).astype(F32)
    picks = jnp.where((col == route[:, 0:1]) | (col == route[:, 1:2]), 1.0, 0.0).astype(BF16)
    h4 = h3_ref[...] + _dot(picks, ybuf[slot].astype(BF16))
    out = _rms(h4, g_ref[...])

    @pl.when(j < last)
    def _():
        outp_ref[...] = out

    @pl.when(j == last)
    def _():
        outs_ref[...] = out


def _combine_final(h3, route, tbl, ys, g_final, n_prompt):
    n, d = h3.shape
    tm = ROW_TILE
    n_sample = n - n_prompt
    assert n_sample == tm and n_prompt % tm == 0
    last_prompt = n_prompt // tm - 1
    grid_spec = pltpu.PrefetchScalarGridSpec(
        num_scalar_prefetch=1,
        grid=(n // tm,),
        in_specs=[pl.BlockSpec((tm, d), lambda j, *_: (j, 0)),
                  pl.BlockSpec((tm, LANES), lambda j, *_: (j, 0)),
                  pl.BlockSpec(memory_space=pl.ANY),
                  pl.BlockSpec((1, d), lambda j, *_: (0, 0))],
        out_specs=[pl.BlockSpec((tm, d), lambda j, *_: (jnp.minimum(j, last_prompt), 0)),
                   pl.BlockSpec((tm, d), lambda j, *_: (0, 0))],
        scratch_shapes=[pltpu.VMEM((2, LOCAL_ROWS, d), F32), pltpu.SemaphoreType.DMA((2,))],
    )
    return pl.pallas_call(
        _combine_kernel,
        grid_spec=grid_spec,
        out_shape=[jax.ShapeDtypeStruct((n_prompt, d), F32), jax.ShapeDtypeStruct((n_sample, d), F32)],
        compiler_params=_params("arbitrary"),
        name="moe_combine",
    )(tbl, h3, route, ys, g_final)


def _row(v):
    return v.reshape(1, -1).astype(F32)


def _routing_tables(counts):
    n_row_tiles = counts.shape[0]
    cnt = counts[:, 0, :N_EXPERTS].astype(I32)
    seg = ((cnt + SUBLANES - 1) // SUBLANES) * SUBLANES
    local_start = jnp.cumsum(seg, axis=1) - seg
    rows = jnp.sum(seg, axis=0)
    padded = ((rows + MOE_TILE - 1) // MOE_TILE) * MOE_TILE
    ends = jnp.cumsum(padded)
    starts = ends - padded
    sorted_start = starts[None, :] + jnp.cumsum(seg, axis=0) - seg
    tbl = jnp.concatenate([local_start, seg, sorted_start], axis=1).reshape(-1).astype(I32)
    tail = jnp.concatenate([starts + rows, padded - rows]).astype(I32)
    max_rows = n_row_tiles * (TOP_K * ROW_TILE + N_EXPERTS * (SUBLANES - 1)) + N_EXPERTS * (MOE_TILE - SUBLANES)
    n_tiles = -(-max_rows // MOE_TILE)
    tile_start = jnp.arange(n_tiles, dtype=I32) * MOE_TILE
    tile_start = jnp.minimum(tile_start, ends[-1] - MOE_TILE)
    tile_expert = jnp.minimum(jnp.sum((tile_start[:, None] >= ends[None, :]).astype(I32), axis=1), N_EXPERTS - 1)
    n_used = (ends[-1] // MOE_TILE).reshape(1).astype(I32)
    return tbl, tail, tile_expert.astype(I32), n_used, n_tiles


def kernel(x_prompt, x_sample, state_conv, state_h, cache_k, cache_v, g_mix, g_ffn, g_kv, g_final, a_w_gate, a_b_gate, a_w_in, a_b_in, a_conv_w, a_conv_b, a_w_r, a_b_r, a_w_i, a_b_i, a_lam, a_w_out, a_b_out, w_kv, b_kv, rel_bias, b_w_q, b_b_q, b_sinks, b_w_o, b_b_o, f_w_gate, f_w_up, f_w_down, m_w_router, m_w_gate, m_w_up, m_w_down):
    bp, seq, d = x_prompt.shape
    bs, steps, _ = x_sample.shape
    n_prompt = bp * seq
    n_sample = bs * steps
    n = n_prompt + n_sample
    assert seq % MIX_TILE == 0 and seq % WINDOW == 0 and n_prompt % ROW_TILE == 0
    assert n_sample == ROW_TILE and bs % SAMPLE_GROUP == 0

    mix_w = dict(g=_row(g_mix[0]), wg=a_w_gate[0].astype(BF16), bg=_row(a_b_gate[0]),
                 win=a_w_in[0].astype(BF16), bin=_row(a_b_in[0]), cw=a_conv_w[0], cb=_row(a_conv_b[0]),
                 wr=a_w_r[0].astype(BF16), br=_row(a_b_r[0]), wi=a_w_i[0].astype(BF16), bi=_row(a_b_i[0]),
                 lam=_row(a_lam[0]), wout=a_w_out[0].astype(BF16), bout=_row(a_b_out[0]))

    ffn_w = dict(gf=_row(g_ffn[0]), wg=f_w_gate[0].astype(BF16), wu=f_w_up[0].astype(BF16),
                 wd=f_w_down[0].astype(BF16), gkv=_row(g_kv), wkv=w_kv.astype(BF16), bkv=_row(b_kv),
                 gq=_row(g_mix[1]), wq=b_w_q[0].astype(BF16), bq=_row(b_b_q[0]))

    h2_p, kv_p, q_p, p_conv, p_h, moe_wg, moe_wu = _layer0_prompt(x_prompt, mix_w, ffn_w, [m_w_gate[0], m_w_up[0]])
    h1_s, s_conv_tm, s_h = _mixer_sample(jnp.transpose(x_sample, (1, 0, 2)),
                                         jnp.transpose(state_conv[0], (1, 0, 2)), state_h[0], mix_w)
    h2_s, kv_s, q_s = _ffn_rows(jnp.transpose(h1_s, (1, 0, 2)).reshape(n_sample, d), ffn_w)

    bias, bias_rows = _bias_band(rel_bias, steps)
    sinks = b_sinks[0].astype(F32)
    o_p, moe_wd = _attn_prompt(q_p, kv_p, bias, sinks, bp, seq, [m_w_down[0]])
    q4 = q_s.reshape(bs, steps, N_KV, GROUP, HEAD_DIM).transpose(0, 2, 1, 3, 4)
    q4 = q4.reshape(bs, N_KV, steps * GROUP, HEAD_DIM)
    kv_new = kv_s.reshape(bs, steps, 2 * KV_DIM)
    sink_rows = jnp.broadcast_to(sinks.reshape(N_KV, 1, GROUP), (N_KV, steps, GROUP)).reshape(N_KV, steps * GROUP, 1)
    o4, s_kt, s_vt = _attn_sample(q4, jnp.transpose(cache_k, (0, 2, 3, 1)), jnp.transpose(cache_v, (0, 2, 3, 1)),
                                  kv_new, jnp.transpose(kv_new, (0, 2, 1)), bias_rows, sink_rows)
    o_s = o4.reshape(bs, N_KV, steps, GROUP, HEAD_DIM).transpose(0, 2, 1, 3, 4).reshape(n_sample, d)

    wr_pad = jnp.zeros((d, LANES), F32).at[:, :N_EXPERTS].set(m_w_router[0])
    wr_hi = wr_pad.astype(BF16)
    wr_lo = (wr_pad - wr_hi.astype(F32)).astype(BF16)
    h3, xn, route, counts = _oproj_router(
        o_p, o_s, h2_p, h2_s,
        dict(wo=b_w_o[0].astype(BF16), bo=_row(b_b_o[0]), gf=_row(g_ffn[1]),
             wr3=jnp.concatenate([wr_hi, wr_hi, wr_lo], axis=0)))

    tbl, tail, tile_expert, n_used, n_tiles = _routing_tables(counts)
    xs = _sort_place(xn, route, tbl, tail, n_used, n_tiles * MOE_TILE)
    ys = _moe_experts(xs, tile_expert, n_used, moe_wg, moe_wu, moe_wd)
    y_p, y_s = _combine_final(h3, route, tbl, ys, _row(g_final), n_prompt)

    y_prompt = y_p.reshape(bp, seq, d)
    y_sample = y_s.reshape(bs, steps, d)
    kv_last = jnp.stack([kv_p[b * seq + seq - WINDOW:(b + 1) * seq] for b in range(bp)])
    p_k = kv_last[:, :, :KV_DIM].reshape(bp, WINDOW, N_KV, HEAD_DIM)
    p_v = kv_last[:, :, KV_DIM:].reshape(bp, WINDOW, N_KV, HEAD_DIM)
    s_k = jnp.transpose(s_kt, (0, 3, 1, 2))
    s_v = jnp.transpose(s_vt, (0, 3, 1, 2))
    return (y_prompt, y_sample, p_conv[None], p_h.reshape(1, bp, LRU_WIDTH), p_k, p_v,
            jnp.transpose(s_conv_tm, (1, 0, 2))[None], s_h[None], s_k, s_v)
```

```python
import functools
import math

import jax
import jax.numpy as jnp
from jax import lax
from jax.experimental import pallas as pl
from jax.experimental.pallas import tpu as pltpu

D_MODEL = 1024
LRU_WIDTH = D_MODEL
LRU_BLOCK_W = 256
LRU_BLOCKS = LRU_WIDTH // LRU_BLOCK_W
CONV_W = 4
LRU_C = 8.0
HEAD_DIM = 64
N_HEADS = D_MODEL // HEAD_DIM
N_KV = 2
GROUP = N_HEADS // N_KV
KV_DIM = N_KV * HEAD_DIM
WINDOW = 128
NUM_BUCKETS = 32
MAX_DISTANCE = 128
D_FF = 3 * D_MODEL
N_EXPERTS = 8
TOP_K = 2
D_EXPERT = 7 * D_MODEL // 2
EPS = 1e-6
NEG = -1e30

BF16 = jnp.bfloat16
F32 = jnp.float32
I32 = jnp.int32

SUBLANES = 8
LANES = 128
VMEM_LIMIT_BYTES = 56 * 1024 * 1024

ROW_TILE = 512
MIX_TILE = 256
FF_CHUNK = 1024
MOE_TILE = 512
MOE_CHUNKS = 2


def _params(*semantics):
    return pltpu.CompilerParams(dimension_semantics=semantics, vmem_limit_bytes=VMEM_LIMIT_BYTES)


def _const_spec(shape):
    zeros = (0,) * len(shape)
    return pl.BlockSpec(shape, lambda *_: zeros, pipeline_mode=pl.Buffered(1))


def _dot(a, b):
    return jnp.dot(a, b, preferred_element_type=F32)


def _rms(x, g):
    ms = jnp.mean(x * x, axis=-1, keepdims=True)
    return x * lax.rsqrt(ms + EPS) * g


def _sigmoid(x):
    return 1.0 / (1.0 + jnp.exp(-x))


def _gelu_tanh(x):
    return 0.5 * x * (1.0 + jnp.tanh(0.7978845608028654 * (x + 0.044715 * (x * x * x))))


def _log_sigmoid(x):
    return jnp.minimum(x, 0.0) - jnp.log1p(jnp.exp(-jnp.abs(x)))


def _lru_gates(xc, wr_ref, br, wi_ref, bi, lam):
    xcb = xc.astype(BF16)
    rs, gs = [], []
    for n in range(LRU_BLOCKS):
        xn = xcb[:, n * LRU_BLOCK_W:(n + 1) * LRU_BLOCK_W]
        rs.append(_dot(xn, wr_ref[n]))
        gs.append(_dot(xn, wi_ref[n]))
    r = _sigmoid(jnp.concatenate(rs, axis=1) + br)
    i = _sigmoid(jnp.concatenate(gs, axis=1) + bi)
    log_a = LRU_C * r * _log_sigmoid(lam)
    a = jnp.exp(log_a)
    mult = jnp.sqrt(1.0 - a * a)
    return a, mult * (i * xc)


def _interleave(order, *stage_generators):
    results = [None] * len(stage_generators)
    finished = set()

    def advance(idx):
        if idx in finished:
            return
        try:
            next(stage_generators[idx])
        except StopIteration as done:
            results[idx] = done.value
            finished.add(idx)

    for idx in order:
        advance(idx)
    while len(finished) < len(stage_generators):
        for idx in range(len(stage_generators)):
            advance(idx)
    return results


def _mixer_stages(x, g_ref, wg_ref, bg_ref, win_ref, bin_ref, cw_ref, cb_ref,
                  wr_ref, br_ref, wi_ref, bi_ref, lam_ref, wout_ref, bout_ref, xr_buf, h_carry):
    tt = x.shape[0]
    pad = SUBLANES
    u = _rms(x, g_ref[...]).astype(BF16)
    gate = _gelu_tanh(_dot(u, wg_ref[...]) + bg_ref[...])
    xr = _dot(u, win_ref[...]) + bin_ref[...]
    yield
    xr_buf[pad:pad + tt, :] = xr
    xc = cb_ref[...] + cw_ref[CONV_W - 1:CONV_W, :] * xr
    for k in range(CONV_W - 1):
        back = CONV_W - 1 - k
        xc = xc + cw_ref[k:k + 1, :] * xr_buf[pad - back:pad - back + tt, :]
    xr_buf[0:pad, :] = xr[tt - pad:tt, :]
    yield

    a, b = _lru_gates(xc, wr_ref, br_ref[...], wi_ref, bi_ref[...], lam_ref[...])
    yield

    groups = tt // SUBLANES
    a3 = a.reshape(groups, SUBLANES, LRU_WIDTH)
    b3 = b.reshape(groups, SUBLANES, LRU_WIDTH)
    row = lax.broadcasted_iota(I32, (1, SUBLANES, LRU_WIDTH), 1)
    step = 1
    while step < SUBLANES:
        keep = row >= step
        a_prev = jnp.where(keep, pltpu.roll(a3, step, axis=1), 1.0)
        b_prev = jnp.where(keep, pltpu.roll(b3, step, axis=1), 0.0)
        b3 = b3 + a3 * b_prev
        a3 = a3 * a_prev
        step *= 2
    yield
    h_prev = h_carry[0:1, :]
    hs = []
    for gi in range(groups):
        hg = b3[gi] + a3[gi] * h_prev
        hs.append(hg)
        h_prev = hg[SUBLANES - 1:SUBLANES, :]
        if gi + 1 == groups // 2:
            yield
    h = jnp.concatenate(hs, axis=0)
    h_carry[0:1, :] = h_prev
    yield
    y = _dot((h * gate).astype(BF16), wout_ref[...]) + bout_ref[...]
    return x + y, xr[tt - (CONV_W - 1):tt, :], h_prev


LAYER0_ORDER = (0, 1, 0, 0, 1, 0, 0, 1, 0, 0)
N_MIX_REFS = 14
N_FFN_REFS = 10


def _ffn_stages(h1, gf_ref, wg_ref, wu_ref, wd_ref, gkv_ref, wkv_ref, bkv_ref, gq_ref, wq_ref, bq_ref):
    u = _rms(h1, gf_ref[...]).astype(BF16)
    acc = h1
    n_chunks = D_FF // FF_CHUNK
    for c in range(n_chunks):
        cols = slice(c * FF_CHUNK, (c + 1) * FF_CHUNK)
        g = _dot(u, wg_ref[:, cols])
        up = _dot(u, wu_ref[:, cols])
        mid = (g * _sigmoid(g) * up).astype(BF16)
        acc = acc + _dot(mid, wd_ref[cols, :])
        if c + 1 < n_chunks:
            yield
    kv = _dot(_rms(acc, gkv_ref[...]).astype(BF16), wkv_ref[...]) + bkv_ref[...]
    q = _dot(_rms(acc, gq_ref[...]).astype(BF16), wq_ref[...]) + bq_ref[...]
    return acc, kv, (q * (HEAD_DIM ** -0.5)).astype(BF16)


def _layer0_prompt_kernel(*refs, n_tiles, tiles_per_seq, n_casts):
    x_ref = refs[0]
    mix_refs = refs[1:1 + N_MIX_REFS]
    ffn_refs = refs[1 + N_MIX_REFS:1 + N_MIX_REFS + N_FFN_REFS]
    rest = refs[1 + N_MIX_REFS + N_FFN_REFS:]
    cast_srcs, rest = rest[:n_casts], rest[n_casts:]
    h2_ref, kv_ref, q_ref, conv_ref, hlast_ref = rest[:5]
    cast_dsts = rest[5:5 + n_casts]
    xr_buf, h_carry, h1_buf = rest[5 + n_casts:]
    for src_ref, dst_ref in zip(cast_srcs, cast_dsts):
        dst_ref[...] = src_ref[...].astype(BF16)
    s = pl.program_id(0)
    slot = lax.rem(s, 2)

    @pl.when(s == 0)
    def _():
        h1_buf[...] = jnp.zeros_like(h1_buf)

    @pl.when(lax.rem(s, tiles_per_seq) == 0)
    def _():
        xr_buf[0:SUBLANES, :] = jnp.zeros((SUBLANES, LRU_WIDTH), F32)
        h_carry[...] = jnp.zeros_like(h_carry)

    (h1, conv_tail, h_last), (h2, kv, q) = _interleave(
        LAYER0_ORDER,
        _mixer_stages(x_ref[...], *mix_refs, xr_buf, h_carry), _ffn_stages(h1_buf[1 - slot], *ffn_refs))
    h2_ref[...] = h2
    kv_ref[...] = kv
    q_ref[...] = q
    h1_buf[slot] = h1

    @pl.when(s < n_tiles)
    def _():
        b = s // tiles_per_seq
        conv_ref[b] = conv_tail
        hlast_ref[b] = h_last


def _layer0_prompt(x, mix_w, ffn_w, cast_ws):
    b, t, d = x.shape
    tt = MIX_TILE
    n_tiles = (b * t) // tt
    casts = [_side_cast_specs(w, n_tiles, lambda s: jnp.minimum(s, n_tiles - 1)) for w in cast_ws]
    vec = lambda n: _const_spec((1, n))
    mix_specs = [
        vec(d), _const_spec((d, LRU_WIDTH)), vec(LRU_WIDTH), _const_spec((d, LRU_WIDTH)), vec(LRU_WIDTH),
        _const_spec((CONV_W, LRU_WIDTH)), vec(LRU_WIDTH),
        _const_spec((LRU_BLOCKS, LRU_BLOCK_W, LRU_BLOCK_W)), vec(LRU_WIDTH),
        _const_spec((LRU_BLOCKS, LRU_BLOCK_W, LRU_BLOCK_W)), vec(LRU_WIDTH),
        vec(LRU_WIDTH), _const_spec((LRU_WIDTH, d)), vec(d),
    ]
    ffn_specs = [
        vec(d), _const_spec((d, D_FF)), _const_spec((d, D_FF)), _const_spec((D_FF, d)),
        vec(d), _const_spec((d, 2 * KV_DIM)), vec(2 * KV_DIM), vec(d), _const_spec((d, d)), vec(d),
    ]
    assert len(mix_specs) == N_MIX_REFS and len(ffn_specs) == N_FFN_REFS
    prev = lambda s: (jnp.maximum(s - 1, 0), 0)
    whole = lambda shape: pl.BlockSpec(shape, lambda s: (0,) * len(shape))
    return pl.pallas_call(
        functools.partial(_layer0_prompt_kernel, n_tiles=n_tiles, tiles_per_seq=t // tt, n_casts=len(casts)),
        grid=(n_tiles + 1,),
        in_specs=([pl.BlockSpec((tt, d), lambda s: (jnp.minimum(s, n_tiles - 1), 0))] + mix_specs + ffn_specs
                  + [c[0] for c in casts]),
        out_specs=[pl.BlockSpec((tt, d), prev), pl.BlockSpec((tt, 2 * KV_DIM), prev), pl.BlockSpec((tt, d), prev),
                   whole((b, CONV_W - 1, LRU_WIDTH)), whole((b, 1, LRU_WIDTH))] + [c[1] for c in casts],
        out_shape=[jax.ShapeDtypeStruct((b * t, d), F32), jax.ShapeDtypeStruct((b * t, 2 * KV_DIM), F32),
                   jax.ShapeDtypeStruct((b * t, d), BF16),
                   jax.ShapeDtypeStruct((b, CONV_W - 1, LRU_WIDTH), F32),
                   jax.ShapeDtypeStruct((b, 1, LRU_WIDTH), F32)] + [c[2] for c in casts],
        scratch_shapes=[pltpu.VMEM((SUBLANES + tt, LRU_WIDTH), F32), pltpu.VMEM((SUBLANES, LRU_WIDTH), F32),
                        pltpu.VMEM((2, tt, d), F32)],
        compiler_params=_params("arbitrary"),
        name="layer0_prompt",
    )(x.reshape(b * t, d), *[mix_w[k] for k in MIX_KEYS], *[ffn_w[k] for k in FFN_KEYS], *cast_ws)


MIX_KEYS = ("g", "wg", "bg", "win", "bin", "cw", "cb", "wr", "br", "wi", "bi", "lam", "wout", "bout")
FFN_KEYS = ("gf", "wg", "wu", "wd", "gkv", "wkv", "bkv", "gq", "wq", "bq")


def _mixer_sample_kernel(x_ref, cs_ref, h0_ref, g_ref, wg_ref, bg_ref, win_ref, bin_ref, cw_ref, cb_ref,
                         wr_ref, br_ref, wi_ref, bi_ref, lam_ref, wout_ref, bout_ref,
                         h1_ref, conv_ref, hlast_ref):
    steps, nb, d = x_ref.shape
    x = x_ref[...].reshape(steps * nb, d)
    u = _rms(x, g_ref[...]).astype(BF16)
    gate = _gelu_tanh(_dot(u, wg_ref[...]) + bg_ref[...])
    xr = _dot(u, win_ref[...]) + bin_ref[...]
    xpad = [cs_ref[k] for k in range(CONV_W - 1)] + [xr[s * nb:(s + 1) * nb, :] for s in range(steps)]
    xcs = []
    for s in range(steps):
        acc = cb_ref[...] + cw_ref[0:1, :] * xpad[s]
        for k in range(1, CONV_W):
            acc = acc + cw_ref[k:k + 1, :] * xpad[s + k]
        xcs.append(acc)
    for k in range(CONV_W - 1):
        conv_ref[k] = xpad[steps + k]
    xc = jnp.concatenate(xcs, axis=0)
    a, b = _lru_gates(xc, wr_ref, br_ref[...], wi_ref, bi_ref[...], lam_ref[...])
    h = h0_ref[...]
    hs = []
    for s in range(steps):
        h = a[s * nb:(s + 1) * nb, :] * h + b[s * nb:(s + 1) * nb, :]
        hs.append(h)
    hlast_ref[...] = h
    hcat = jnp.concatenate(hs, axis=0)
    y = _dot((hcat * gate).astype(BF16), wout_ref[...]) + bout_ref[...]
    h1_ref[...] = (x + y).reshape(steps, nb, d)


def _mixer_sample(x_tm, cs_tm, h0, w):
    steps, nb, d = x_tm.shape
    full = lambda shape: pl.BlockSpec(shape, lambda i: (0,) * len(shape))
    vec = lambda n: full((1, n))
    in_specs = [
        full((steps, nb, d)), full((CONV_W - 1, nb, LRU_WIDTH)), full((nb, LRU_WIDTH)),
        vec(d), full((d, LRU_WIDTH)), vec(LRU_WIDTH), full((d, LRU_WIDTH)), vec(LRU_WIDTH),
        full((CONV_W, LRU_WIDTH)), vec(LRU_WIDTH),
        full((LRU_BLOCKS, LRU_BLOCK_W, LRU_BLOCK_W)), vec(LRU_WIDTH),
        full((LRU_BLOCKS, LRU_BLOCK_W, LRU_BLOCK_W)), vec(LRU_WIDTH),
        vec(LRU_WIDTH), full((LRU_WIDTH, d)), vec(d),
    ]
    out_specs = [full((steps, nb, d)), full((CONV_W - 1, nb, LRU_WIDTH)), full((nb, LRU_WIDTH))]
    out_shape = [
        jax.ShapeDtypeStruct((steps, nb, d), F32),
        jax.ShapeDtypeStruct((CONV_W - 1, nb, LRU_WIDTH), F32),
        jax.ShapeDtypeStruct((nb, LRU_WIDTH), F32),
    ]
    return pl.pallas_call(
        _mixer_sample_kernel, grid=(1,), in_specs=in_specs, out_specs=out_specs, out_shape=out_shape,
        compiler_params=_params("arbitrary"), name="mixer_sample",
    )(x_tm, cs_tm, h0, w["g"], w["wg"], w["bg"], w["win"], w["bin"], w["cw"], w["cb"], w["wr"], w["br"],
      w["wi"], w["bi"], w["lam"], w["wout"], w["bout"])


def _two_part_specs(n_prompt, n_sample, width):
    assert n_sample == ROW_TILE and n_prompt % ROW_TILE == 0
    last_prompt = n_prompt // ROW_TILE - 1
    return [pl.BlockSpec((ROW_TILE, width), lambda i: (jnp.minimum(i, last_prompt), 0)),
            pl.BlockSpec((ROW_TILE, width), lambda i: (0, 0))]


def _two_part_tile(prompt_ref, sample_ref):
    is_sample = pl.program_id(0) == pl.num_programs(0) - 1
    return jnp.where(is_sample, sample_ref[...], prompt_ref[...])


def _side_cast_specs(w, n_steps, step_of):
    ne, rows, cols = w.shape
    per_expert = n_steps // ne
    blk = rows // per_expert
    assert per_expert * ne == n_steps and blk * per_expert == rows and blk % (2 * SUBLANES) == 0
    index = lambda *ids: (step_of(*ids) // per_expert, step_of(*ids) % per_expert, 0)
    spec = pl.BlockSpec((1, blk, cols), index)
    return spec, spec, jax.ShapeDtypeStruct(w.shape, BF16)


def _ffn_rows_kernel(h1_ref, *refs):
    h2_ref, kv_ref, q_ref = refs[N_FFN_REFS:]
    (h2_ref[...], kv_ref[...], q_ref[...]), = _interleave((), _ffn_stages(h1_ref[...], *refs[:N_FFN_REFS]))


def _ffn_rows(h1, w):
    n, d = h1.shape
    tm = ROW_TILE
    row = lambda width: pl.BlockSpec((tm, width), lambda i: (i, 0))
    vec = lambda width: _const_spec((1, width))
    in_specs = [
        row(d), vec(d), _const_spec((d, D_FF)), _const_spec((d, D_FF)), _const_spec((D_FF, d)),
        vec(d), _const_spec((d, 2 * KV_DIM)), vec(2 * KV_DIM), vec(d), _const_spec((d, d)), vec(d),
    ]
    return pl.pallas_call(
        _ffn_rows_kernel,
        grid=(n // tm,),
        in_specs=in_specs,
        out_specs=[row(d), row(2 * KV_DIM), row(d)],
        out_shape=[jax.ShapeDtypeStruct((n, d), F32), jax.ShapeDtypeStruct((n, 2 * KV_DIM), F32),
                   jax.ShapeDtypeStruct((n, d), BF16)],
        compiler_params=_params("arbitrary"),
        name="ffn_rows",
    )(h1, *[w[k] for k in FFN_KEYS])


def _head_of(j, parity, pair):
    return j * GROUP + 2 * pair + parity


def _distance_bias(dist, table_ref, head):
    max_exact = NUM_BUCKETS // 2
    n = jnp.maximum(dist, 0)
    large = jnp.full(dist.shape, max_exact, I32)
    for step in range(1, NUM_BUCKETS - max_exact):
        threshold = math.ceil(max_exact * (MAX_DISTANCE / max_exact) ** (step / (NUM_BUCKETS - max_exact)))
        large = large + jnp.where(n >= threshold, 1, 0)
    bucket = jnp.where(n < max_exact, n, large)
    acc = jnp.zeros(dist.shape, F32)
    for bkt in range(NUM_BUCKETS):
        acc = jnp.where(bucket == bkt, table_ref[bkt, head], acc)
    return acc


def _bias_band_kernel(table_ref, folded_ref, rows_ref):
    w = WINDOW
    ci = lax.broadcasted_iota(I32, (w, w), 0)
    qi = lax.broadcasted_iota(I32, (w, w), 1)
    folded_dist = jnp.where(ci > qi, qi + w - ci, qi - ci)
    for j in range(N_KV):
        for parity in range(2):
            for pair in range(GROUP // 2):
                folded_ref[j, parity, :, pair * w:(pair + 1) * w] = _distance_bias(
                    folded_dist, table_ref, _head_of(j, parity, pair))
    n_rows, n_keys = rows_ref.shape[1], rows_ref.shape[2]
    ri = lax.broadcasted_iota(I32, (GROUP, n_keys), 0)
    si = lax.broadcasted_iota(I32, (GROUP, n_keys), 1)
    for j in range(N_KV):
        for t in range(n_rows // GROUP):
            acc = jnp.zeros((GROUP, n_keys), F32)
            for g in range(GROUP):
                acc = jnp.where(ri == g, _distance_bias(t + w - si, table_ref, j * GROUP + g), acc)
            rows_ref[j, t * GROUP:(t + 1) * GROUP, :] = acc


def _bias_band(rel_table, steps):
    folded = (N_KV, 2, WINDOW, (GROUP // 2) * WINDOW)
    rows = (N_KV, steps * GROUP, WINDOW + steps + (-steps) % SUBLANES)
    return pl.pallas_call(
        _bias_band_kernel,
        grid=(1,),
        in_specs=[pl.BlockSpec(memory_space=pltpu.SMEM)],
        out_specs=[pl.BlockSpec(folded, lambda i: (0, 0, 0, 0)), pl.BlockSpec(rows, lambda i: (0, 0, 0))],
        out_shape=[jax.ShapeDtypeStruct(folded, F32), jax.ShapeDtypeStruct(rows, F32)],
        compiler_params=_params("arbitrary"),
        name="bias_band",
    )(rel_table)


ATTN_BLOCKS = 8


def _attn_prompt_kernel(sink_ref, q_ref, kvp_ref, kvc_ref, bias_ref, *refs):
    n_casts = (len(refs) - 1) // 2
    o_ref = refs[n_casts]
    for src_ref, dst_ref in zip(refs[:n_casts], refs[n_casts + 1:]):
        dst_ref[...] = src_ref[...].astype(BF16)
    w = WINDOW
    pairs = GROUP // 2
    assert 2 * HEAD_DIM == LANES and KV_DIM == LANES
    ci = lax.broadcasted_iota(I32, (w, pairs * w), 0)
    qi = lax.broadcasted_iota(I32, (w, pairs * w), 1) % w
    from_prev = ci > qi
    has_prev = pl.program_id(1) > 0

    kv = jnp.concatenate([kvp_ref[...], kvc_ref[...]], axis=0)
    low = lax.broadcasted_iota(I32, (kv.shape[0], LANES), 1) < HEAD_DIM

    def halves(x):
        swapped = pltpu.roll(x, HEAD_DIM, axis=1)
        return (((jnp.where(low, x, 0.0)).astype(BF16), (jnp.where(low, 0.0, swapped)).astype(BF16)),
                ((jnp.where(low, swapped, 0.0)).astype(BF16), (jnp.where(low, 0.0, x)).astype(BF16)))

    k_ops = halves(kv[:, 0:KV_DIM])
    v_ops = halves(kv[:, KV_DIM:2 * KV_DIM])

    sinks = [[jnp.concatenate([jnp.full((1, w), sink_ref[_head_of(j, parity, b)], F32) for b in range(pairs)], axis=1)
              for parity in range(2)] for j in range(N_KV)]
    nt = (((1,), (1,)), ((), ()))
    tn = (((0,), (0,)), ((), ()))
    for i in range(ATTN_BLOCKS):
        q_rows = slice(i * w, (i + 1) * w)
        prev_rows = slice(i * w, (i + 1) * w)
        own_rows = slice((i + 1) * w, (i + 2) * w)
        for j in range(N_KV):
            q4 = jnp.concatenate([q_ref[q_rows, (j * pairs + b) * LANES:(j * pairs + b + 1) * LANES]
                                  for b in range(pairs)], axis=0)
            acc = None
            for parity in range(2):
                k_op, v_op, sink = k_ops[j][parity], v_ops[j][parity], sinks[j][parity]
                s_prev = lax.dot_general(k_op[prev_rows], q4, nt, preferred_element_type=F32)
                s_own = lax.dot_general(k_op[own_rows], q4, nt, preferred_element_type=F32)
                if i == 0:
                    s_prev = jnp.where(has_prev, s_prev, NEG)
                s = jnp.where(from_prev, s_prev, s_own) + bias_ref[j, parity]
                m = jnp.maximum(jnp.max(s, axis=0, keepdims=True), sink)
                p = jnp.exp(s - m)
                denom = jnp.sum(p, axis=0, keepdims=True) + jnp.exp(sink - m)
                p = p * (1.0 / denom)
                o = (lax.dot_general(jnp.where(from_prev, p, 0.0).astype(BF16), v_op[prev_rows], tn,
                                     preferred_element_type=F32)
                     + lax.dot_general(jnp.where(from_prev, 0.0, p).astype(BF16), v_op[own_rows], tn,
                                       preferred_element_type=F32))
                acc = o if acc is None else acc + o
            for b in range(pairs):
                o_ref[q_rows, (j * pairs + b) * LANES:(j * pairs + b + 1) * LANES] = acc[b * w:(b + 1) * w].astype(BF16)


def _attn_prompt(q, kv, bias, sinks, batch, seq, cast_ws):
    nb = seq // WINDOW
    assert nb % ATTN_BLOCKS == 0
    ns = nb // ATTN_BLOCKS
    rows = ATTN_BLOCKS * WINDOW
    d = q.shape[1]
    casts = [_side_cast_specs(w, batch * ns, lambda b, n: b * ns + n) for w in cast_ws]
    in_specs = [
        pl.BlockSpec(memory_space=pltpu.SMEM),
        pl.BlockSpec((rows, d), lambda b, n: (b * ns + n, 0)),
        pl.BlockSpec((WINDOW, 2 * KV_DIM), lambda b, n: (b * nb + jnp.maximum(n * ATTN_BLOCKS - 1, 0), 0)),
        pl.BlockSpec((rows, 2 * KV_DIM), lambda b, n: (b * ns + n, 0)),
        _const_spec(bias.shape),
    ] + [c[0] for c in casts]
    return pl.pallas_call(
        _attn_prompt_kernel,
        grid=(batch, ns),
        in_specs=in_specs,
        out_specs=[pl.BlockSpec((rows, d), lambda b, n: (b * ns + n, 0))] + [c[1] for c in casts],
        out_shape=[jax.ShapeDtypeStruct((batch * seq, d), BF16)] + [c[2] for c in casts],
        compiler_params=_params("arbitrary", "arbitrary"),
        name="attn_prompt",
    )(sinks, q, kv, kv, bias, *cast_ws)


SAMPLE_GROUP = 8


def _attn_sample_kernel(q_ref, ckt_ref, cvt_ref, new_ref, newt_ref, bias_ref, sink_ref, o_ref, skt_ref, svt_ref):
    gb = q_ref.shape[0]
    steps = new_ref.shape[1]
    rows = steps * GROUP
    pad = (-steps) % SUBLANES
    ri = lax.broadcasted_iota(I32, (rows, WINDOW + steps + pad), 0)
    si = lax.broadcasted_iota(I32, (rows, WINDOW + steps + pad), 1)
    dist = ri // GROUP + WINDOW - si
    mask = (dist >= 0) & (dist < WINDOW)
    mask_c, mask_n = mask[:, :WINDOW], mask[:, WINDOW:]
    zeros = jnp.zeros((pad, 2 * KV_DIM), F32)

    def scores(b, j):
        new = jnp.concatenate([new_ref[b], zeros], axis=0).astype(BF16)
        q = q_ref[b, j]
        s_c = _dot(q, ckt_ref[b, j].astype(BF16))
        s_n = lax.dot_general(q, new[:, j * HEAD_DIM:(j + 1) * HEAD_DIM], (((1,), (1,)), ((), ())),
                              preferred_element_type=F32)
        bias = bias_ref[j]
        return (jnp.where(mask_c, s_c + bias[:, :WINDOW], NEG), jnp.where(mask_n, s_n + bias[:, WINDOW:], NEG), new)

    def task_stages(b, j):
        s_c, s_n, new = scores(b, j)
        yield
        sink = sink_ref[j]
        m = jnp.maximum(jnp.maximum(jnp.max(s_c, axis=-1, keepdims=True), jnp.max(s_n, axis=-1, keepdims=True)), sink)
        p_c = jnp.exp(s_c - m)
        p_n = jnp.exp(s_n - m)
        yield
        denom = jnp.sum(p_c, axis=-1, keepdims=True) + jnp.sum(p_n, axis=-1, keepdims=True) + jnp.exp(sink - m)
        pv = lax.dot_general(p_c.astype(BF16), cvt_ref[b, j].astype(BF16), (((1,), (1,)), ((), ())),
                             preferred_element_type=F32)
        pv = pv + _dot(p_n.astype(BF16), new[:, KV_DIM + j * HEAD_DIM:KV_DIM + (j + 1) * HEAD_DIM])
        o_ref[b, j] = (pv / denom).astype(BF16)

    _interleave((), *[task_stages(b, j) for b in range(gb) for j in range(N_KV)])

    lane = lax.broadcasted_iota(I32, (HEAD_DIM, WINDOW), 1)
    for b in range(gb):
        newt = jnp.concatenate([newt_ref[b], jnp.zeros((2 * KV_DIM, WINDOW - steps), F32)], axis=1)
        for j in range(N_KV):
            for src_ref, dst_ref, base in ((ckt_ref, skt_ref, 0), (cvt_ref, svt_ref, KV_DIM)):
                fresh = newt[base + j * HEAD_DIM:base + (j + 1) * HEAD_DIM, :]
                dst_ref[b, j] = pltpu.roll(jnp.where(lane < steps, fresh, src_ref[b, j]), WINDOW - steps, axis=1)


def _attn_sample(q4, ckt, cvt, new, newt, bias_rows, sink_rows):
    nb, _, rows, hd = q4.shape
    steps = new.shape[1]
    keys = bias_rows.shape[2]
    gb = SAMPLE_GROUP
    cache_spec = pl.BlockSpec((gb, N_KV, hd, WINDOW), lambda i: (i, 0, 0, 0))
    in_specs = [
        pl.BlockSpec((gb, N_KV, rows, hd), lambda i: (i, 0, 0, 0)),
        cache_spec, cache_spec,
        pl.BlockSpec((gb, steps, 2 * KV_DIM), lambda i: (i, 0, 0)),
        pl.BlockSpec((gb, 2 * KV_DIM, steps), lambda i: (i, 0, 0)),
        pl.BlockSpec((N_KV, rows, keys), lambda i: (0, 0, 0)),
        pl.BlockSpec((N_KV, rows, 1), lambda i: (0, 0, 0)),
    ]
    return pl.pallas_call(
        _attn_sample_kernel,
        grid=(nb // gb,),
        in_specs=in_specs,
        out_specs=[pl.BlockSpec((gb, N_KV, rows, hd), lambda i: (i, 0, 0, 0)), cache_spec, cache_spec],
        out_shape=[jax.ShapeDtypeStruct(q4.shape, BF16), jax.ShapeDtypeStruct(ckt.shape, F32),
                   jax.ShapeDtypeStruct(cvt.shape, F32)],
        compiler_params=_params("arbitrary"),
        name="attn_sample",
    )(q4, ckt, cvt, new, newt, bias_rows, sink_rows)


def _split_bf16(x):
    hi = x.astype(BF16)
    lo = (x - hi.astype(F32)).astype(BF16)
    return hi, lo


ROUTER_PARTS = 2


def _oproj_router_kernel(op_ref, os_ref, h2p_ref, h2s_ref, wo_ref, bo_ref, gf_ref, wr3_ref,
                         h3_ref, xn_ref, route_ref, counts_ref):
    tm = h3_ref.shape[0]
    parts = ROUTER_PARTS
    pr = tm // parts
    o = _two_part_tile(op_ref, os_ref)
    h2 = _two_part_tile(h2p_ref, h2s_ref)
    lane = lax.broadcasted_iota(I32, (pr, LANES), 1)
    lanef = lane.astype(F32)
    ri = lax.broadcasted_iota(I32, (pr, pr), 0)
    ci = lax.broadcasted_iota(I32, (pr, pr), 1)
    earlier = jnp.where(ri > ci, 1.0, 0.0).astype(BF16)

    def part_stages(p):
        rows = slice(p * pr, (p + 1) * pr)
        h3 = h2[rows, :] + _dot(o[rows, :], wo_ref[...]) + bo_ref[...]
        h3_ref[rows, :] = h3
        yield
        u_hi, u_lo = _split_bf16(_rms(h3, gf_ref[...]))
        xn_ref[rows, :] = u_hi
        logits = _dot(jnp.concatenate([u_hi, u_lo, u_hi], axis=1), wr3_ref[...])
        yield
        logits = jnp.where(lane < N_EXPERTS, logits, -jnp.inf)
        v1 = jnp.max(logits, axis=-1, keepdims=True)
        e1 = jnp.min(jnp.where(logits == v1, lanef, float(LANES)), axis=-1, keepdims=True)
        rest = jnp.where(lanef == e1, -jnp.inf, logits)
        v2 = jnp.max(rest, axis=-1, keepdims=True)
        e2 = jnp.min(jnp.where(rest == v2, lanef, float(LANES)), axis=-1, keepdims=True)
        yield
        ex = jnp.exp(v2 - v1)
        pick1 = lanef == e1
        pick2 = lanef == e2
        sel = jnp.where(pick1 | pick2, 1.0, 0.0)
        return (pick1, pick2), (1.0 / (1.0 + ex), ex / (1.0 + ex)), sel, _dot(earlier, sel.astype(BF16))

    done = _interleave((), *[part_stages(p) for p in range(parts)])
    picks, ranks, gates = [], [], []
    cnt = jnp.zeros((1, LANES), F32)
    for part_picks, part_gates, sel, rank in done:
        picks.append(part_picks)
        gates.append(part_gates)
        ranks.append(rank + cnt)
        cnt = cnt + jnp.sum(sel, axis=0, keepdims=True)

    seg = jnp.floor((cnt + (SUBLANES - 1)) * (1.0 / SUBLANES)) * SUBLANES
    ek = lax.broadcasted_iota(I32, (LANES, LANES), 0)
    el = lax.broadcasted_iota(I32, (LANES, LANES), 1)
    lower_experts = jnp.where(ek < el, 1.0, 0.0).astype(BF16)
    seg_start = _dot(jnp.broadcast_to(seg, (SUBLANES, LANES)).astype(BF16), lower_experts)[0:1, :]
    for p in range(parts):
        local = ranks[p] + seg_start
        lr1 = jnp.sum(jnp.where(picks[p][0], local, 0.0), axis=-1, keepdims=True)
        lr2 = jnp.sum(jnp.where(picks[p][1], local, 0.0), axis=-1, keepdims=True)
        w1, w2 = gates[p]
        route_ref[p * pr:(p + 1) * pr, :] = jnp.where(
            lane == 0, lr1, jnp.where(lane == 1, lr2, jnp.where(lane == 2, w1, jnp.where(lane == 3, w2, 0.0))))
    counts_ref[0] = jnp.broadcast_to(cnt, (SUBLANES, LANES))


def _oproj_router(o_p, o_s, h2_p, h2_s, w):
    d = h2_p.shape[1]
    n = h2_p.shape[0] + h2_s.shape[0]
    tm = ROW_TILE
    row = lambda width: pl.BlockSpec((tm, width), lambda i: (i, 0))
    in_specs = (_two_part_specs(o_p.shape[0], o_s.shape[0], d) + _two_part_specs(h2_p.shape[0], h2_s.shape[0], d) + [
        _const_spec((d, d)), _const_spec((1, d)), _const_spec((1, d)), _const_spec((3 * d, LANES))])
    return pl.pallas_call(
        _oproj_router_kernel,
        grid=(n // tm,),
        in_specs=in_specs,
        out_specs=[row(d), row(d), row(LANES), pl.BlockSpec((1, SUBLANES, LANES), lambda i: (i, 0, 0))],
        out_shape=[jax.ShapeDtypeStruct((n, d), F32), jax.ShapeDtypeStruct((n, d), BF16),
                   jax.ShapeDtypeStruct((n, LANES), F32), jax.ShapeDtypeStruct((n // tm, SUBLANES, LANES), F32)],
        compiler_params=_params("arbitrary"),
        name="oproj_router",
    )(o_p, o_s, h2_p, h2_s, w["wo"], w["bo"], w["gf"], w["wr3"])


LOCAL_ROWS = 1152
XS_WIDTH = D_MODEL + LANES
SEG_TABLE = 3 * N_EXPERTS
SEG_PIECES = tuple(SUBLANES << b for b in reversed(range(7)))


def _segment_copies(src_ref, src_start, dst_ref, dst_start, length, sem, act):
    for piece in SEG_PIECES:
        done = lax.div(length, 2 * piece) * (2 * piece)

        @pl.when(lax.rem(lax.div(length, piece), 2) != 0)
        def _():
            s = pl.multiple_of(src_start + done, SUBLANES)
            t = pl.multiple_of(dst_start + done, SUBLANES)
            act(pltpu.make_async_copy(src_ref.at[pl.ds(s, piece), :], dst_ref.at[pl.ds(t, piece), :], sem))


def _sort_place_kernel(tbl_ref, tail_ref, nu_ref, xn_ref, route_ref, xs_ref, stage, zeros, sems):
    j = pl.program_id(0)
    tm = xn_ref.shape[0]
    xn = xn_ref[...]
    rt = jnp.transpose(route_ref[...])
    row = lax.broadcasted_iota(I32, (LOCAL_ROWS, tm), 0).astype(F32)
    m1 = row == rt[0:1, :]
    m2 = row == rt[1:2, :]
    onehot = jnp.where(m1 | m2, 1.0, 0.0).astype(BF16)
    slot = lax.rem(j, 2)
    buf = stage.at[slot]
    buf[:, 0:D_MODEL] = _dot(onehot, xn)
    gate = jnp.sum(jnp.where(m1, rt[2:3, :], 0.0) + jnp.where(m2, rt[3:4, :], 0.0), axis=-1, keepdims=True)
    buf[:, D_MODEL:XS_WIDTH] = jnp.broadcast_to(gate, (LOCAL_ROWS, LANES))

    def segments(tile, tile_slot, act):
        for e in range(N_EXPERTS):
            base = tile * SEG_TABLE
            _segment_copies(stage.at[tile_slot], tbl_ref[base + e], xs_ref, tbl_ref[base + 2 * N_EXPERTS + e],
                            tbl_ref[base + N_EXPERTS + e], sems.at[tile_slot], act)

    segments(j, slot, lambda cp: cp.start())

    @pl.when(j > 0)
    def _():
        segments(j - 1, 1 - slot, lambda cp: cp.wait())

    @pl.when(j == pl.num_programs(0) - 1)
    def _():
        segments(j, slot, lambda cp: cp.wait())
        zeros[...] = jnp.zeros_like(zeros)
        sem = sems.at[0]

        def tails(act):
            for e in range(N_EXPERTS):
                _segment_copies(zeros, 0, xs_ref, tail_ref[e], tail_ref[N_EXPERTS + e], sem, act)

        def unused(act):
            def body(i, c):
                t = pl.multiple_of(i * MOE_TILE, MOE_TILE)
                act(pltpu.make_async_copy(zeros, xs_ref.at[pl.ds(t, MOE_TILE), :], sem))
                return c
            lax.fori_loop(nu_ref[0], xs_ref.shape[0] // MOE_TILE, body, 0)

        tails(lambda cp: cp.start())
        unused(lambda cp: cp.start())
        tails(lambda cp: cp.wait())
        unused(lambda cp: cp.wait())


def _sort_place(xn, route, tbl, tail, n_used, p_rows):
    n, d = xn.shape
    tm = ROW_TILE
    grid_spec = pltpu.PrefetchScalarGridSpec(
        num_scalar_prefetch=3,
        grid=(n // tm,),
        in_specs=[pl.BlockSpec((tm, d), lambda j, *_: (j, 0)),
                  pl.BlockSpec((tm, LANES), lambda j, *_: (j, 0))],
        out_specs=pl.BlockSpec(memory_space=pl.ANY),
        scratch_shapes=[pltpu.VMEM((2, LOCAL_ROWS, XS_WIDTH), F32), pltpu.VMEM((MOE_TILE, XS_WIDTH), F32),
                        pltpu.SemaphoreType.DMA((2,))],
    )
    return pl.pallas_call(
        _sort_place_kernel,
        grid_spec=grid_spec,
        out_shape=jax.ShapeDtypeStruct((p_rows, XS_WIDTH), F32),
        compiler_params=_params("arbitrary"),
        name="moe_sort_place",
    )(tbl, tail, n_used, xn, route)


def _moe_kernel(te_ref, nu_ref, xs_ref, wg_ref, wu_ref, wd_ref, y_ref):
    i = pl.program_id(0)
    c = pl.program_id(1)
    used = i < nu_ref[0]

    @pl.when(used)
    def _():
        xn = xs_ref[:, 0:D_MODEL].astype(BF16)
        g = _dot(xn, wg_ref[0])
        up = _dot(xn, wu_ref[0])
        mid = (g * _sigmoid(g) * up).astype(BF16)
        y = _dot(mid, wd_ref[0]) * xs_ref[:, D_MODEL:D_MODEL + 1]

        @pl.when(c == 0)
        def _():
            y_ref[...] = y

        @pl.when(c > 0)
        def _():
            y_ref[...] = y_ref[...] + y

    @pl.when(jnp.logical_not(used) & (c == 0))
    def _():
        y_ref[...] = jnp.zeros_like(y_ref)


def _moe_experts(xs, tile_expert, n_used, wg, wu, wd):
    p_rows = xs.shape[0]
    d = D_MODEL
    tm = MOE_TILE
    n_tiles = p_rows // tm
    ch = D_EXPERT // MOE_CHUNKS
    last = MOE_CHUNKS - 1

    def chunk(i, c, nu):
        return jnp.where(i < nu[0], c, last)

    grid_spec = pltpu.PrefetchScalarGridSpec(
        num_scalar_prefetch=2,
        grid=(n_tiles, MOE_CHUNKS),
        in_specs=[
            pl.BlockSpec((tm, XS_WIDTH), lambda i, c, te, nu: (jnp.minimum(i, nu[0] - 1), 0)),
            pl.BlockSpec((1, d, ch), lambda i, c, te, nu: (te[i], 0, chunk(i, c, nu))),
            pl.BlockSpec((1, d, ch), lambda i, c, te, nu: (te[i], 0, chunk(i, c, nu))),
            pl.BlockSpec((1, ch, d), lambda i, c, te, nu: (te[i], chunk(i, c, nu), 0)),
        ],
        out_specs=pl.BlockSpec((tm, d), lambda i, c, te, nu: (i, 0)),
    )
    return pl.pallas_call(
        _moe_kernel,
        grid_spec=grid_spec,
        out_shape=jax.ShapeDtypeStruct((p_rows, d), F32),
        compiler_params=_params("arbitrary", "arbitrary"),
        name="moe_experts",
    )(tile_expert, n_used, xs, wg, wu, wd)


def _combine_kernel(tbl_ref, h3_ref, route_ref, ys_ref, g_ref, outp_ref, outs_ref, ybuf, sems):
    j = pl.program_id(0)
    last = pl.num_programs(0) - 1
    tm = h3_ref.shape[0]
    slot = lax.rem(j, 2)

    def segments(tile, tile_slot, act):
        for e in range(N_EXPERTS):
            base = tile * SEG_TABLE
            _segment_copies(ys_ref, tbl_ref[base + 2 * N_EXPERTS + e], ybuf.at[tile_slot], tbl_ref[base + e],
                            tbl_ref[base + N_EXPERTS + e], sems.at[tile_slot], act)

    @pl.when(j == 0)
    def _():
        ybuf[...] = jnp.zeros_like(ybuf)
        segments(j, slot, lambda cp: cp.start())

    @pl.when(j < last)
    def _():
        segments(j + 1, 1 - slot, lambda cp: cp.start())

    segments(j, slot, lambda cp: cp.wait())

    route = route_ref[...]
    col = lax.broadcasted_iota(I32, (tm, LOCAL_ROWS), 1).astype(F32)
    picks = jnp.where((col == route[:, 0:1]) | (col == route[:, 1:2]), 1.0, 0.0).astype(BF16)
    h4 = h3_ref[...] + _dot(picks, ybuf[slot].astype(BF16))
    out = _rms(h4, g_ref[...])

    @pl.when(j < last)
    def _():
        outp_ref[...] = out

    @pl.when(j == last)
    def _():
        outs_ref[...] = out


def _combine_final(h3, route, tbl, ys, g_final, n_prompt):
    n, d = h3.shape
    tm = ROW_TILE
    n_sample = n - n_prompt
    assert n_sample == tm and n_prompt % tm == 0
    last_prompt = n_prompt // tm - 1
    grid_spec = pltpu.PrefetchScalarGridSpec(
        num_scalar_prefetch=1,
        grid=(n // tm,),
        in_specs=[pl.BlockSpec((tm, d), lambda j, *_: (j, 0)),
                  pl.BlockSpec((tm, LANES), lambda j, *_: (j, 0)),
                  pl.BlockSpec(memory_space=pl.ANY),
                  pl.BlockSpec((1, d), lambda j, *_: (0, 0))],
        out_specs=[pl.BlockSpec((tm, d), lambda j, *_: (jnp.minimum(j, last_prompt), 0)),
                   pl.BlockSpec((tm, d), lambda j, *_: (0, 0))],
        scratch_shapes=[pltpu.VMEM((2, LOCAL_ROWS, d), F32), pltpu.SemaphoreType.DMA((2,))],
    )
    return pl.pallas_call(
        _combine_kernel,
        grid_spec=grid_spec,
        out_shape=[jax.ShapeDtypeStruct((n_prompt, d), F32), jax.ShapeDtypeStruct((n_sample, d), F32)],
        compiler_params=_params("arbitrary"),
        name="moe_combine",
    )(tbl, h3, route, ys, g_final)


def _row(v):
    return v.reshape(1, -1).astype(F32)


def _routing_tables(counts):
    n_row_tiles = counts.shape[0]
    cnt = counts[:, 0, :N_EXPERTS].astype(I32)
    seg = ((cnt + SUBLANES - 1) // SUBLANES) * SUBLANES
    local_start = jnp.cumsum(seg, axis=1) - seg
    rows = jnp.sum(seg, axis=0)
    padded = ((rows + MOE_TILE - 1) // MOE_TILE) * MOE_TILE
    ends = jnp.cumsum(padded)
    starts = ends - padded
    sorted_start = starts[None, :] + jnp.cumsum(seg, axis=0) - seg
    tbl = jnp.concatenate([local_start, seg, sorted_start], axis=1).reshape(-1).astype(I32)
    tail = jnp.concatenate([starts + rows, padded - rows]).astype(I32)
    max_rows = n_row_tiles * (TOP_K * ROW_TILE + N_EXPERTS * (SUBLANES - 1)) + N_EXPERTS * (MOE_TILE - SUBLANES)
    n_tiles = -(-max_rows // MOE_TILE)
    tile_start = jnp.arange(n_tiles, dtype=I32) * MOE_TILE
    tile_start = jnp.minimum(tile_start, ends[-1] - MOE_TILE)
    tile_expert = jnp.minimum(jnp.sum((tile_start[:, None] >= ends[None, :]).astype(I32), axis=1), N_EXPERTS - 1)
    n_used = (ends[-1] // MOE_TILE).reshape(1).astype(I32)
    return tbl, tail, tile_expert.astype(I32), n_used, n_tiles


def kernel(x_prompt, x_sample, state_conv, state_h, cache_k, cache_v, g_mix, g_ffn, g_kv, g_final, a_w_gate, a_b_gate, a_w_in, a_b_in, a_conv_w, a_conv_b, a_w_r, a_b_r, a_w_i, a_b_i, a_lam, a_w_out, a_b_out, w_kv, b_kv, rel_bias, b_w_q, b_b_q, b_sinks, b_w_o, b_b_o, f_w_gate, f_w_up, f_w_down, m_w_router, m_w_gate, m_w_up, m_w_down):
    bp, seq, d = x_prompt.shape
    bs, steps, _ = x_sample.shape
    n_prompt = bp * seq
    n_sample = bs * steps
    n = n_prompt + n_sample
    assert seq % MIX_TILE == 0 and seq % WINDOW == 0 and n_prompt % ROW_TILE == 0
    assert n_sample == ROW_TILE and bs % SAMPLE_GROUP == 0

    mix_w = dict(g=_row(g_mix[0]), wg=a_w_gate[0].astype(BF16), bg=_row(a_b_gate[0]),
                 win=a_w_in[0].astype(BF16), bin=_row(a_b_in[0]), cw=a_conv_w[0], cb=_row(a_conv_b[0]),
                 wr=a_w_r[0].astype(BF16), br=_row(a_b_r[0]), wi=a_w_i[0].astype(BF16), bi=_row(a_b_i[0]),
                 lam=_row(a_lam[0]), wout=a_w_out[0].astype(BF16), bout=_row(a_b_out[0]))

    ffn_w = dict(gf=_row(g_ffn[0]), wg=f_w_gate[0].astype(BF16), wu=f_w_up[0].astype(BF16),
                 wd=f_w_down[0].astype(BF16), gkv=_row(g_kv), wkv=w_kv.astype(BF16), bkv=_row(b_kv),
                 gq=_row(g_mix[1]), wq=b_w_q[0].astype(BF16), bq=_row(b_b_q[0]))

    h2_p, kv_p, q_p, p_conv, p_h, moe_wg, moe_wu = _layer0_prompt(x_prompt, mix_w, ffn_w, [m_w_gate[0], m_w_up[0]])
    h1_s, s_conv_tm, s_h = _mixer_sample(jnp.transpose(x_sample, (1, 0, 2)),
                                         jnp.transpose(state_conv[0], (1, 0, 2)), state_h[0], mix_w)
    h2_s, kv_s, q_s = _ffn_rows(jnp.transpose(h1_s, (1, 0, 2)).reshape(n_sample, d), ffn_w)

    bias, bias_rows = _bias_band(rel_bias, steps)
    sinks = b_sinks[0].astype(F32)
    o_p, moe_wd = _attn_prompt(q_p, kv_p, bias, sinks, bp, seq, [m_w_down[0]])
    q4 = q_s.reshape(bs, steps, N_KV, GROUP, HEAD_DIM).transpose(0, 2, 1, 3, 4)
    q4 = q4.reshape(bs, N_KV, steps * GROUP, HEAD_DIM)
    kv_new = kv_s.reshape(bs, steps, 2 * KV_DIM)
    sink_rows = jnp.broadcast_to(sinks.reshape(N_KV, 1, GROUP), (N_KV, steps, GROUP)).reshape(N_KV, steps * GROUP, 1)
    o4, s_kt, s_vt = _attn_sample(q4, jnp.transpose(cache_k, (0, 2, 3, 1)), jnp.transpose(cache_v, (0, 2, 3, 1)),
                                  kv_new, jnp.transpose(kv_new, (0, 2, 1)), bias_rows, sink_rows)
    o_s = o4.reshape(bs, N_KV, steps, GROUP, HEAD_DIM).transpose(0, 2, 1, 3, 4).reshape(n_sample, d)

    wr_pad = jnp.zeros((d, LANES), F32).at[:, :N_EXPERTS].set(m_w_router[0])
    wr_hi = wr_pad.astype(BF16)
    wr_lo = (wr_pad - wr_hi.astype(F32)).astype(BF16)
    h3, xn, route, counts = _oproj_router(
        o_p, o_s, h2_p, h2_s,
        dict(wo=b_w_o[0].astype(BF16), bo=_row(b_b_o[0]), gf=_row(g_ffn[1]),
             wr3=jnp.concatenate([wr_hi, wr_hi, wr_lo], axis=0)))

    tbl, tail, tile_expert, n_used, n_tiles = _routing_tables(counts)
    xs = _sort_place(xn, route, tbl, tail, n_used, n_tiles * MOE_TILE)
    ys = _moe_experts(xs, tile_expert, n_used, moe_wg, moe_wu, moe_wd)
    y_p, y_s = _combine_final(h3, route, tbl, ys, _row(g_final), n_prompt)

    y_prompt = y_p.reshape(bp, seq, d)
    y_sample = y_s.reshape(bs, steps, d)
    kv_last = jnp.stack([kv_p[b * seq + seq - WINDOW:(b + 1) * seq] for b in range(bp)])
    p_k = kv_last[:, :, :KV_DIM].reshape(bp, WINDOW, N_KV, HEAD_DIM)
    p_v = kv_last[:, :, KV_DIM:].reshape(bp, WINDOW, N_KV, HEAD_DIM)
    s_k = jnp.transpose(s_kt, (0, 3, 1, 2))
    s_v = jnp.transpose(s_vt, (0, 3, 1, 2))
    return (y_prompt, y_sample, p_conv[None], p_h.reshape(1, bp, LRU_WIDTH), p_k, p_v,
            jnp.transpose(s_conv_tm, (1, 0, 2))[None], s_h[None], s_k, s_v)
```

```python
import functools
import math

import jax
import jax.numpy as jnp
from jax import lax
from jax.experimental import pallas as pl
from jax.experimental.pallas import tpu as pltpu

D_MODEL = 1024
LRU_WIDTH = D_MODEL
LRU_BLOCK_W = 256
LRU_BLOCKS = LRU_WIDTH // LRU_BLOCK_W
CONV_W = 4
LRU_C = 8.0
HEAD_DIM = 64
N_HEADS = D_MODEL // HEAD_DIM
N_KV = 2
GROUP = N_HEADS // N_KV
KV_DIM = N_KV * HEAD_DIM
WINDOW = 128
NUM_BUCKETS = 32
MAX_DISTANCE = 128
D_FF = 3 * D_MODEL
N_EXPERTS = 8
TOP_K = 2
D_EXPERT = 7 * D_MODEL // 2
EPS = 1e-6
NEG = -1e30

BF16 = jnp.bfloat16
F32 = jnp.float32
I32 = jnp.int32

SUBLANES = 8
LANES = 128
VMEM_LIMIT_BYTES = 56 * 1024 * 1024

ROW_TILE = 512
MIX_TILE = 256
FF_CHUNK = 1024
MOE_TILE = 512
MOE_CHUNKS = 2


def _params(*semantics):
    return pltpu.CompilerParams(dimension_semantics=semantics, vmem_limit_bytes=VMEM_LIMIT_BYTES)


def _const_spec(shape):
    zeros = (0,) * len(shape)
    return pl.BlockSpec(shape, lambda *_: zeros, pipeline_mode=pl.Buffered(1))


def _dot(a, b):
    return jnp.dot(a, b, preferred_element_type=F32)


def _rms(x, g):
    ms = jnp.mean(x * x, axis=-1, keepdims=True)
    return x * lax.rsqrt(ms + EPS) * g


def _sigmoid(x):
    return 1.0 / (1.0 + jnp.exp(-x))


def _gelu_tanh(x):
    return 0.5 * x * (1.0 + jnp.tanh(0.7978845608028654 * (x + 0.044715 * (x * x * x))))


def _log_sigmoid(x):
    return jnp.minimum(x, 0.0) - jnp.log1p(jnp.exp(-jnp.abs(x)))


def _lru_gates(xc, wr_ref, br, wi_ref, bi, lam):
    xcb = xc.astype(BF16)
    rs, gs = [], []
    for n in range(LRU_BLOCKS):
        xn = xcb[:, n * LRU_BLOCK_W:(n + 1) * LRU_BLOCK_W]
        rs.append(_dot(xn, wr_ref[n]))
        gs.append(_dot(xn, wi_ref[n]))
    r = _sigmoid(jnp.concatenate(rs, axis=1) + br)
    i = _sigmoid(jnp.concatenate(gs, axis=1) + bi)
    log_a = LRU_C * r * _log_sigmoid(lam)
    a = jnp.exp(log_a)
    mult = jnp.sqrt(1.0 - a * a)
    return a, mult * (i * xc)


def _interleave(order, *stage_generators):
    results = [None] * len(stage_generators)
    finished = set()

    def advance(idx):
        if idx in finished:
            return
        try:
            next(stage_generators[idx])
        except StopIteration as done:
            results[idx] = done.value
            finished.add(idx)

    for idx in order:
        advance(idx)
    while len(finished) < len(stage_generators):
        for idx in range(len(stage_generators)):
            advance(idx)
    return results


def _mixer_stages(x, g_ref, wg_ref, bg_ref, win_ref, bin_ref, cw_ref, cb_ref,
                  wr_ref, br_ref, wi_ref, bi_ref, lam_ref, wout_ref, bout_ref, xr_buf, h_carry):
    tt = x.shape[0]
    pad = SUBLANES
    u = _rms(x, g_ref[...]).astype(BF16)
    gate = _gelu_tanh(_dot(u, wg_ref[...]) + bg_ref[...])
    xr = _dot(u, win_ref[...]) + bin_ref[...]
    yield
    xr_buf[pad:pad + tt, :] = xr
    xc = cb_ref[...] + cw_ref[CONV_W - 1:CONV_W, :] * xr
    for k in range(CONV_W - 1):
        back = CONV_W - 1 - k
        xc = xc + cw_ref[k:k + 1, :] * xr_buf[pad - back:pad - back + tt, :]
    xr_buf[0:pad, :] = xr[tt - pad:tt, :]
    yield

    a, b = _lru_gates(xc, wr_ref, br_ref[...], wi_ref, bi_ref[...], lam_ref[...])
    yield

    groups = tt // SUBLANES
    a3 = a.reshape(groups, SUBLANES, LRU_WIDTH)
    b3 = b.reshape(groups, SUBLANES, LRU_WIDTH)
    row = lax.broadcasted_iota(I32, (1, SUBLANES, LRU_WIDTH), 1)
    step = 1
    while step < SUBLANES:
        keep = row >= step
        a_prev = jnp.where(keep, pltpu.roll(a3, step, axis=1), 1.0)
        b_prev = jnp.where(keep, pltpu.roll(b3, step, axis=1), 0.0)
        b3 = b3 + a3 * b_prev
        a3 = a3 * a_prev
        step *= 2
    yield
    h_prev = h_carry[0:1, :]
    hs = []
    for gi in range(groups):
        hg = b3[gi] + a3[gi] * h_prev
        hs.append(hg)
        h_prev = hg[SUBLANES - 1:SUBLANES, :]
        if gi + 1 == groups // 2:
            yield
    h = jnp.concatenate(hs, axis=0)
    h_carry[0:1, :] = h_prev
    yield
    y = _dot((h * gate).astype(BF16), wout_ref[...]) + bout_ref[...]
    return x + y, xr[tt - (CONV_W - 1):tt, :], h_prev


LAYER0_ORDER = (0, 1, 0, 0, 1, 0, 0, 1, 0, 0)
N_MIX_REFS = 14
N_FFN_REFS = 10


def _ffn_stages(h1, gf_ref, wg_ref, wu_ref, wd_ref, gkv_ref, wkv_ref, bkv_ref, gq_ref, wq_ref, bq_ref):
    u = _rms(h1, gf_ref[...]).astype(BF16)
    acc = h1
    n_chunks = D_FF // FF_CHUNK
    for c in range(n_chunks):
        cols = slice(c * FF_CHUNK, (c + 1) * FF_CHUNK)
        g = _dot(u, wg_ref[:, cols])
        up = _dot(u, wu_ref[:, cols])
        mid = (g * _sigmoid(g) * up).astype(BF16)
        acc = acc + _dot(mid, wd_ref[cols, :])
        if c + 1 < n_chunks:
            yield
    kv = _dot(_rms(acc, gkv_ref[...]).astype(BF16), wkv_ref[...]) + bkv_ref[...]
    q = _dot(_rms(acc, gq_ref[...]).astype(BF16), wq_ref[...]) + bq_ref[...]
    return acc, kv, (q * (HEAD_DIM ** -0.5)).astype(BF16)


def _layer0_prompt_kernel(*refs, n_tiles, tiles_per_seq, n_casts):
    x_ref = refs[0]
    mix_refs = refs[1:1 + N_MIX_REFS]
    ffn_refs = refs[1 + N_MIX_REFS:1 + N_MIX_REFS + N_FFN_REFS]
    rest = refs[1 + N_MIX_REFS + N_FFN_REFS:]
    cast_srcs, rest = rest[:n_casts], rest[n_casts:]
    h2_ref, kv_ref, q_ref, conv_ref, hlast_ref = rest[:5]
    cast_dsts = rest[5:5 + n_casts]
    xr_buf, h_carry, h1_buf = rest[5 + n_casts:]
    for src_ref, dst_ref in zip(cast_srcs, cast_dsts):
        dst_ref[...] = src_ref[...].astype(BF16)
    s = pl.program_id(0)
    slot = lax.rem(s, 2)

    @pl.when(s == 0)
    def _():
        h1_buf[...] = jnp.zeros_like(h1_buf)

    @pl.when(lax.rem(s, tiles_per_seq) == 0)
    def _():
        xr_buf[0:SUBLANES, :] = jnp.zeros((SUBLANES, LRU_WIDTH), F32)
        h_carry[...] = jnp.zeros_like(h_carry)

    (h1, conv_tail, h_last), (h2, kv, q) = _interleave(
        LAYER0_ORDER,
        _mixer_stages(x_ref[...], *mix_refs, xr_buf, h_carry), _ffn_stages(h1_buf[1 - slot], *ffn_refs))
    h2_ref[...] = h2
    kv_ref[...] = kv
    q_ref[...] = q
    h1_buf[slot] = h1

    @pl.when(s < n_tiles)
    def _():
        b = s // tiles_per_seq
        conv_ref[b] = conv_tail
        hlast_ref[b] = h_last


def _layer0_prompt(x, mix_w, ffn_w, cast_ws):
    b, t, d = x.shape
    tt = MIX_TILE
    n_tiles = (b * t) // tt
    casts = [_side_cast_specs(w, n_tiles, lambda s: jnp.minimum(s, n_tiles - 1)) for w in cast_ws]
    vec = lambda n: _const_spec((1, n))
    mix_specs = [
        vec(d), _const_spec((d, LRU_WIDTH)), vec(LRU_WIDTH), _const_spec((d, LRU_WIDTH)), vec(LRU_WIDTH),
        _const_spec((CONV_W, LRU_WIDTH)), vec(LRU_WIDTH),
        _const_spec((LRU_BLOCKS, LRU_BLOCK_W, LRU_BLOCK_W)), vec(LRU_WIDTH),
        _const_spec((LRU_BLOCKS, LRU_BLOCK_W, LRU_BLOCK_W)), vec(LRU_WIDTH),
        vec(LRU_WIDTH), _const_spec((LRU_WIDTH, d)), vec(d),
    ]
    ffn_specs = [
        vec(d), _const_spec((d, D_FF)), _const_spec((d, D_FF)), _const_spec((D_FF, d)),
        vec(d), _const_spec((d, 2 * KV_DIM)), vec(2 * KV_DIM), vec(d), _const_spec((d, d)), vec(d),
    ]
    assert len(mix_specs) == N_MIX_REFS and len(ffn_specs) == N_FFN_REFS
    prev = lambda s: (jnp.maximum(s - 1, 0), 0)
    whole = lambda shape: pl.BlockSpec(shape, lambda s: (0,) * len(shape))
    return pl.pallas_call(
        functools.partial(_layer0_prompt_kernel, n_tiles=n_tiles, tiles_per_seq=t // tt, n_casts=len(casts)),
        grid=(n_tiles + 1,),
        in_specs=([pl.BlockSpec((tt, d), lambda s: (jnp.minimum(s, n_tiles - 1), 0))] + mix_specs + ffn_specs
                  + [c[0] for c in casts]),
        out_specs=[pl.BlockSpec((tt, d), prev), pl.BlockSpec((tt, 2 * KV_DIM), prev), pl.BlockSpec((tt, d), prev),
                   whole((b, CONV_W - 1, LRU_WIDTH)), whole((b, 1, LRU_WIDTH))] + [c[1] for c in casts],
        out_shape=[jax.ShapeDtypeStruct((b * t, d), F32), jax.ShapeDtypeStruct((b * t, 2 * KV_DIM), F32),
                   jax.ShapeDtypeStruct((b * t, d), BF16),
                   jax.ShapeDtypeStruct((b, CONV_W - 1, LRU_WIDTH), F32),
                   jax.ShapeDtypeStruct((b, 1, LRU_WIDTH), F32)] + [c[2] for c in casts],
        scratch_shapes=[pltpu.VMEM((SUBLANES + tt, LRU_WIDTH), F32), pltpu.VMEM((SUBLANES, LRU_WIDTH), F32),
                        pltpu.VMEM((2, tt, d), F32)],
        compiler_params=_params("arbitrary"),
        name="layer0_prompt",
    )(x.reshape(b * t, d), *[mix_w[k] for k in MIX_KEYS], *[ffn_w[k] for k in FFN_KEYS], *cast_ws)


MIX_KEYS = ("g", "wg", "bg", "win", "bin", "cw", "cb", "wr", "br", "wi", "bi", "lam", "wout", "bout")
FFN_KEYS = ("gf", "wg", "wu", "wd", "gkv", "wkv", "bkv", "gq", "wq", "bq")


def _mixer_sample_kernel(x_ref, cs_ref, h0_ref, g_ref, wg_ref, bg_ref, win_ref, bin_ref, cw_ref, cb_ref,
                         wr_ref, br_ref, wi_ref, bi_ref, lam_ref, wout_ref, bout_ref,
                         h1_ref, conv_ref, hlast_ref):
    steps, nb, d = x_ref.shape
    x = x_ref[...].reshape(steps * nb, d)
    u = _rms(x, g_ref[...]).astype(BF16)
    gate = _gelu_tanh(_dot(u, wg_ref[...]) + bg_ref[...])
    xr = _dot(u, win_ref[...]) + bin_ref[...]
    xpad = [cs_ref[k] for k in range(CONV_W - 1)] + [xr[s * nb:(s + 1) * nb, :] for s in range(steps)]
    xcs = []
    for s in range(steps):
        acc = cb_ref[...] + cw_ref[0:1, :] * xpad[s]
        for k in range(1, CONV_W):
            acc = acc + cw_ref[k:k + 1, :] * xpad[s + k]
        xcs.append(acc)
    for k in range(CONV_W - 1):
        conv_ref[k] = xpad[steps + k]
    xc = jnp.concatenate(xcs, axis=0)
    a, b = _lru_gates(xc, wr_ref, br_ref[...], wi_ref, bi_ref[...], lam_ref[...])
    h = h0_ref[...]
    hs = []
    for s in range(steps):
        h = a[s * nb:(s + 1) * nb, :] * h + b[s * nb:(s + 1) * nb, :]
        hs.append(h)
    hlast_ref[...] = h
    hcat = jnp.concatenate(hs, axis=0)
    y = _dot((hcat * gate).astype(BF16), wout_ref[...]) + bout_ref[...]
    h1_ref[...] = (x + y).reshape(steps, nb, d)


def _mixer_sample(x_tm, cs_tm, h0, w):
    steps, nb, d = x_tm.shape
    full = lambda shape: pl.BlockSpec(shape, lambda i: (0,) * len(shape))
    vec = lambda n: full((1, n))
    in_specs = [
        full((steps, nb, d)), full((CONV_W - 1, nb, LRU_WIDTH)), full((nb, LRU_WIDTH)),
        vec(d), full((d, LRU_WIDTH)), vec(LRU_WIDTH), full((d, LRU_WIDTH)), vec(LRU_WIDTH),
        full((CONV_W, LRU_WIDTH)), vec(LRU_WIDTH),
        full((LRU_BLOCKS, LRU_BLOCK_W, LRU_BLOCK_W)), vec(LRU_WIDTH),
        full((LRU_BLOCKS, LRU_BLOCK_W, LRU_BLOCK_W)), vec(LRU_WIDTH),
        vec(LRU_WIDTH), full((LRU_WIDTH, d)), vec(d),
    ]
    out_specs = [full((steps, nb, d)), full((CONV_W - 1, nb, LRU_WIDTH)), full((nb, LRU_WIDTH))]
    out_shape = [
        jax.ShapeDtypeStruct((steps, nb, d), F32),
        jax.ShapeDtypeStruct((CONV_W - 1, nb, LRU_WIDTH), F32),
        jax.ShapeDtypeStruct((nb, LRU_WIDTH), F32),
    ]
    return pl.pallas_call(
        _mixer_sample_kernel, grid=(1,), in_specs=in_specs, out_specs=out_specs, out_shape=out_shape,
        compiler_params=_params("arbitrary"), name="mixer_sample",
    )(x_tm, cs_tm, h0, w["g"], w["wg"], w["bg"], w["win"], w["bin"], w["cw"], w["cb"], w["wr"], w["br"],
      w["wi"], w["bi"], w["lam"], w["wout"], w["bout"])


def _two_part_specs(n_prompt, n_sample, width):
    assert n_sample == ROW_TILE and n_prompt % ROW_TILE == 0
    last_prompt = n_prompt // ROW_TILE - 1
    return [pl.BlockSpec((ROW_TILE, width), lambda i: (jnp.minimum(i, last_prompt), 0)),
            pl.BlockSpec((ROW_TILE, width), lambda i: (0, 0))]


def _two_part_tile(prompt_ref, sample_ref):
    is_sample = pl.program_id(0) == pl.num_programs(0) - 1
    return jnp.where(is_sample, sample_ref[...], prompt_ref[...])


def _side_cast_specs(w, n_steps, step_of):
    ne, rows, cols = w.shape
    per_expert = n_steps // ne
    blk = rows // per_expert
    assert per_expert * ne == n_steps and blk * per_expert == rows and blk % (2 * SUBLANES) == 0
    index = lambda *ids: (step_of(*ids) // per_expert, step_of(*ids) % per_expert, 0)
    spec = pl.BlockSpec((1, blk, cols), index)
    return spec, spec, jax.ShapeDtypeStruct(w.shape, BF16)


def _ffn_rows_kernel(h1_ref, *refs):
    h2_ref, kv_ref, q_ref = refs[N_FFN_REFS:]
    (h2_ref[...], kv_ref[...], q_ref[...]), = _interleave((), _ffn_stages(h1_ref[...], *refs[:N_FFN_REFS]))


def _ffn_rows(h1, w):
    n, d = h1.shape
    tm = ROW_TILE
    row = lambda width: pl.BlockSpec((tm, width), lambda i: (i, 0))
    vec = lambda width: _const_spec((1, width))
    in_specs = [
        row(d), vec(d), _const_spec((d, D_FF)), _const_spec((d, D_FF)), _const_spec((D_FF, d)),
        vec(d), _const_spec((d, 2 * KV_DIM)), vec(2 * KV_DIM), vec(d), _const_spec((d, d)), vec(d),
    ]
    return pl.pallas_call(
        _ffn_rows_kernel,
        grid=(n // tm,),
        in_specs=in_specs,
        out_specs=[row(d), row(2 * KV_DIM), row(d)],
        out_shape=[jax.ShapeDtypeStruct((n, d), F32), jax.ShapeDtypeStruct((n, 2 * KV_DIM), F32),
                   jax.ShapeDtypeStruct((n, d), BF16)],
        compiler_params=_params("arbitrary"),
        name="ffn_rows",
    )(h1, *[w[k] for k in FFN_KEYS])


def _head_of(j, parity, pair):
    return j * GROUP + 2 * pair + parity


def _distance_bias(dist, table_ref, head):
    max_exact = NUM_BUCKETS // 2
    n = jnp.maximum(dist, 0)
    large = jnp.full(dist.shape, max_exact, I32)
    for step in range(1, NUM_BUCKETS - max_exact):
        threshold = math.ceil(max_exact * (MAX_DISTANCE / max_exact) ** (step / (NUM_BUCKETS - max_exact)))
        large = large + jnp.where(n >= threshold, 1, 0)
    bucket = jnp.where(n < max_exact, n, large)
    acc = jnp.zeros(dist.shape, F32)
    for bkt in range(NUM_BUCKETS):
        acc = jnp.where(bucket == bkt, table_ref[bkt, head], acc)
    return acc


def _bias_band_kernel(table_ref, folded_ref, rows_ref):
    w = WINDOW
    ci = lax.broadcasted_iota(I32, (w, w), 0)
    qi = lax.broadcasted_iota(I32, (w, w), 1)
    folded_dist = jnp.where(ci > qi, qi + w - ci, qi - ci)
    for j in range(N_KV):
        for parity in range(2):
            for pair in range(GROUP // 2):
                folded_ref[j, parity, :, pair * w:(pair + 1) * w] = _distance_bias(
                    folded_dist, table_ref, _head_of(j, parity, pair))
    n_rows, n_keys = rows_ref.shape[1], rows_ref.shape[2]
    ri = lax.broadcasted_iota(I32, (GROUP, n_keys), 0)
    si = lax.broadcasted_iota(I32, (GROUP, n_keys), 1)
    for j in range(N_KV):
        for t in range(n_rows // GROUP):
            acc = jnp.zeros((GROUP, n_keys), F32)
            for g in range(GROUP):
                acc = jnp.where(ri == g, _distance_bias(t + w - si, table_ref, j * GROUP + g), acc)
            rows_ref[j, t * GROUP:(t + 1) * GROUP, :] = acc


def _bias_band(rel_table, steps):
    folded = (N_KV, 2, WINDOW, (GROUP // 2) * WINDOW)
    rows = (N_KV, steps * GROUP, WINDOW + steps + (-steps) % SUBLANES)
    return pl.pallas_call(
        _bias_band_kernel,
        grid=(1,),
        in_specs=[pl.BlockSpec(memory_space=pltpu.SMEM)],
        out_specs=[pl.BlockSpec(folded, lambda i: (0, 0, 0, 0)), pl.BlockSpec(rows, lambda i: (0, 0, 0))],
        out_shape=[jax.ShapeDtypeStruct(folded, F32), jax.ShapeDtypeStruct(rows, F32)],
        compiler_params=_params("arbitrary"),
        name="bias_band",
    )(rel_table)


ATTN_BLOCKS = 8


def _attn_prompt_kernel(sink_ref, q_ref, kvp_ref, kvc_ref, bias_ref, *refs):
    n_casts = (len(refs) - 1) // 2
    o_ref = refs[n_casts]
    for src_ref, dst_ref in zip(refs[:n_casts], refs[n_casts + 1:]):
        dst_ref[...] = src_ref[...].astype(BF16)
    w = WINDOW
    pairs = GROUP // 2
    assert 2 * HEAD_DIM == LANES and KV_DIM == LANES
    ci = lax.broadcasted_iota(I32, (w, pairs * w), 0)
    qi = lax.broadcasted_iota(I32, (w, pairs * w), 1) % w
    from_prev = ci > qi
    has_prev = pl.program_id(1) > 0

    kv = jnp.concatenate([kvp_ref[...], kvc_ref[...]], axis=0)
    low = lax.broadcasted_iota(I32, (kv.shape[0], LANES), 1) < HEAD_DIM

    def halves(x):
        swapped = pltpu.roll(x, HEAD_DIM, axis=1)
        return (((jnp.where(low, x, 0.0)).astype(BF16), (jnp.where(low, 0.0, swapped)).astype(BF16)),
                ((jnp.where(low, swapped, 0.0)).astype(BF16), (jnp.where(low, 0.0, x)).astype(BF16)))

    k_ops = halves(kv[:, 0:KV_DIM])
    v_ops = halves(kv[:, KV_DIM:2 * KV_DIM])

    sinks = [[jnp.concatenate([jnp.full((1, w), sink_ref[_head_of(j, parity, b)], F32) for b in range(pairs)], axis=1)
              for parity in range(2)] for j in range(N_KV)]
    nt = (((1,), (1,)), ((), ()))
    tn = (((0,), (0,)), ((), ()))
    for i in range(ATTN_BLOCKS):
        q_rows = slice(i * w, (i + 1) * w)
        prev_rows = slice(i * w, (i + 1) * w)
        own_rows = slice((i + 1) * w, (i + 2) * w)
        for j in range(N_KV):
            q4 = jnp.concatenate([q_ref[q_rows, (j * pairs + b) * LANES:(j * pairs + b + 1) * LANES]
                                  for b in range(pairs)], axis=0)
            acc = None
            for parity in range(2):
                k_op, v_op, sink = k_ops[j][parity], v_ops[j][parity], sinks[j][parity]
                s_prev = lax.dot_general(k_op[prev_rows], q4, nt, preferred_element_type=F32)
                s_own = lax.dot_general(k_op[own_rows], q4, nt, preferred_element_type=F32)
                if i == 0:
                    s_prev = jnp.where(has_prev, s_prev, NEG)
                s = jnp.where(from_prev, s_prev, s_own) + bias_ref[j, parity]
                m = jnp.maximum(jnp.max(s, axis=0, keepdims=True), sink)
                p = jnp.exp(s - m)
                denom = jnp.sum(p, axis=0, keepdims=True) + jnp.exp(sink - m)
                p = p * (1.0 / denom)
                o = (lax.dot_general(jnp.where(from_prev, p, 0.0).astype(BF16), v_op[prev_rows], tn,
                                     preferred_element_type=F32)
                     + lax.dot_general(jnp.where(from_prev, 0.0, p).astype(BF16), v_op[own_rows], tn,
                                       preferred_element_type=F32))
                acc = o if acc is None else acc + o
            for b in range(pairs):
                o_ref[q_rows, (j * pairs + b) * LANES:(j * pairs + b + 1) * LANES] = acc[b * w:(b + 1) * w].astype(BF16)


def _attn_prompt(q, kv, bias, sinks, batch, seq, cast_ws):
    nb = seq // WINDOW
    assert nb % ATTN_BLOCKS == 0
    ns = nb // ATTN_BLOCKS
    rows = ATTN_BLOCKS * WINDOW
    d = q.shape[1]
    casts = [_side_cast_specs(w, batch * ns, lambda b, n: b * ns + n) for w in cast_ws]
    in_specs = [
        pl.BlockSpec(memory_space=pltpu.SMEM),
        pl.BlockSpec((rows, d), lambda b, n: (b * ns + n, 0)),
        pl.BlockSpec((WINDOW, 2 * KV_DIM), lambda b, n: (b * nb + jnp.maximum(n * ATTN_BLOCKS - 1, 0), 0)),
        pl.BlockSpec((rows, 2 * KV_DIM), lambda b, n: (b * ns + n, 0)),
        _const_spec(bias.shape),
    ] + [c[0] for c in casts]
    return pl.pallas_call(
        _attn_prompt_kernel,
        grid=(batch, ns),
        in_specs=in_specs,
        out_specs=[pl.BlockSpec((rows, d), lambda b, n: (b * ns + n, 0))] + [c[1] for c in casts],
        out_shape=[jax.ShapeDtypeStruct((batch * seq, d), BF16)] + [c[2] for c in casts],
        compiler_params=_params("arbitrary", "arbitrary"),
        name="attn_prompt",
    )(sinks, q, kv, kv, bias, *cast_ws)


SAMPLE_GROUP = 8


def _attn_sample_kernel(q_ref, ckt_ref, cvt_ref, new_ref, newt_ref, bias_ref, sink_ref, o_ref, skt_ref, svt_ref):
    gb = q_ref.shape[0]
    steps = new_ref.shape[1]
    rows = steps * GROUP
    pad = (-steps) % SUBLANES
    ri = lax.broadcasted_iota(I32, (rows, WINDOW + steps + pad), 0)
    si = lax.broadcasted_iota(I32, (rows, WINDOW + steps + pad), 1)
    dist = ri // GROUP + WINDOW - si
    mask = (dist >= 0) & (dist < WINDOW)
    mask_c, mask_n = mask[:, :WINDOW], mask[:, WINDOW:]
    zeros = jnp.zeros((pad, 2 * KV_DIM), F32)

    def scores(b, j):
        new = jnp.concatenate([new_ref[b], zeros], axis=0).astype(BF16)
        q = q_ref[b, j]
        s_c = _dot(q, ckt_ref[b, j].astype(BF16))
        s_n = lax.dot_general(q, new[:, j * HEAD_DIM:(j + 1) * HEAD_DIM], (((1,), (1,)), ((), ())),
                              preferred_element_type=F32)
        bias = bias_ref[j]
        return (jnp.where(mask_c, s_c + bias[:, :WINDOW], NEG), jnp.where(mask_n, s_n + bias[:, WINDOW:], NEG), new)

    def task_stages(b, j):
        s_c, s_n, new = scores(b, j)
        yield
        sink = sink_ref[j]
        m = jnp.maximum(jnp.maximum(jnp.max(s_c, axis=-1, keepdims=True), jnp.max(s_n, axis=-1, keepdims=True)), sink)
        p_c = jnp.exp(s_c - m)
        p_n = jnp.exp(s_n - m)
        yield
        denom = jnp.sum(p_c, axis=-1, keepdims=True) + jnp.sum(p_n, axis=-1, keepdims=True) + jnp.exp(sink - m)
        pv = lax.dot_general(p_c.astype(BF16), cvt_ref[b, j].astype(BF16), (((1,), (1,)), ((), ())),
                             preferred_element_type=F32)
        pv = pv + _dot(p_n.astype(BF16), new[:, KV_DIM + j * HEAD_DIM:KV_DIM + (j + 1) * HEAD_DIM])
        o_ref[b, j] = (pv / denom).astype(BF16)

    _interleave((), *[task_stages(b, j) for b in range(gb) for j in range(N_KV)])

    lane = lax.broadcasted_iota(I32, (HEAD_DIM, WINDOW), 1)
    for b in range(gb):
        newt = jnp.concatenate([newt_ref[b], jnp.zeros((2 * KV_DIM, WINDOW - steps), F32)], axis=1)
        for j in range(N_KV):
            for src_ref, dst_ref, base in ((ckt_ref, skt_ref, 0), (cvt_ref, svt_ref, KV_DIM)):
                fresh = newt[base + j * HEAD_DIM:base + (j + 1) * HEAD_DIM, :]
                dst_ref[b, j] = pltpu.roll(jnp.where(lane < steps, fresh, src_ref[b, j]), WINDOW - steps, axis=1)


def _attn_sample(q4, ckt, cvt, new, newt, bias_rows, sink_rows):
    nb, _, rows, hd = q4.shape
    steps = new.shape[1]
    keys = bias_rows.shape[2]
    gb = SAMPLE_GROUP
    cache_spec = pl.BlockSpec((gb, N_KV, hd, WINDOW), lambda i: (i, 0, 0, 0))
    in_specs = [
        pl.BlockSpec((gb, N_KV, rows, hd), lambda i: (i, 0, 0, 0)),
        cache_spec, cache_spec,
        pl.BlockSpec((gb, steps, 2 * KV_DIM), lambda i: (i, 0, 0)),
        pl.BlockSpec((gb, 2 * KV_DIM, steps), lambda i: (i, 0, 0)),
        pl.BlockSpec((N_KV, rows, keys), lambda i: (0, 0, 0)),
        pl.BlockSpec((N_KV, rows, 1), lambda i: (0, 0, 0)),
    ]
    return pl.pallas_call(
        _attn_sample_kernel,
        grid=(nb // gb,),
        in_specs=in_specs,
        out_specs=[pl.BlockSpec((gb, N_KV, rows, hd), lambda i: (i, 0, 0, 0)), cache_spec, cache_spec],
        out_shape=[jax.ShapeDtypeStruct(q4.shape, BF16), jax.ShapeDtypeStruct(ckt.shape, F32),
                   jax.ShapeDtypeStruct(cvt.shape, F32)],
        compiler_params=_params("arbitrary"),
        name="attn_sample",
    )(q4, ckt, cvt, new, newt, bias_rows, sink_rows)


def _split_bf16(x):
    hi = x.astype(BF16)
    lo = (x - hi.astype(F32)).astype(BF16)
    return hi, lo


ROUTER_PARTS = 2


def _oproj_router_kernel(op_ref, os_ref, h2p_ref, h2s_ref, wo_ref, bo_ref, gf_ref, wr3_ref,
                         h3_ref, xn_ref, route_ref, counts_ref):
    tm = h3_ref.shape[0]
    parts = ROUTER_PARTS
    pr = tm // parts
    o = _two_part_tile(op_ref, os_ref)
    h2 = _two_part_tile(h2p_ref, h2s_ref)
    lane = lax.broadcasted_iota(I32, (pr, LANES), 1)
    lanef = lane.astype(F32)
    ri = lax.broadcasted_iota(I32, (pr, pr), 0)
    ci = lax.broadcasted_iota(I32, (pr, pr), 1)
    earlier = jnp.where(ri > ci, 1.0, 0.0).astype(BF16)

    def part_stages(p):
        rows = slice(p * pr, (p + 1) * pr)
        h3 = h2[rows, :] + _dot(o[rows, :], wo_ref[...]) + bo_ref[...]
        h3_ref[rows, :] = h3
        yield
        u_hi, u_lo = _split_bf16(_rms(h3, gf_ref[...]))
        xn_ref[rows, :] = u_hi
        logits = _dot(jnp.concatenate([u_hi, u_lo, u_hi], axis=1), wr3_ref[...])
        yield
        logits = jnp.where(lane < N_EXPERTS, logits, -jnp.inf)
        v1 = jnp.max(logits, axis=-1, keepdims=True)
        e1 = jnp.min(jnp.where(logits == v1, lanef, float(LANES)), axis=-1, keepdims=True)
        rest = jnp.where(lanef == e1, -jnp.inf, logits)
        v2 = jnp.max(rest, axis=-1, keepdims=True)
        e2 = jnp.min(jnp.where(rest == v2, lanef, float(LANES)), axis=-1, keepdims=True)
        yield
        ex = jnp.exp(v2 - v1)
        pick1 = lanef == e1
        pick2 = lanef == e2
        sel = jnp.where(pick1 | pick2, 1.0, 0.0)
        return (pick1, pick2), (1.0 / (1.0 + ex), ex / (1.0 + ex)), sel, _dot(earlier, sel.astype(BF16))

    done = _interleave((), *[part_stages(p) for p in range(parts)])
    picks, ranks, gates = [], [], []
    cnt = jnp.zeros((1, LANES), F32)
    for part_picks, part_gates, sel, rank in done:
        picks.append(part_picks)
        gates.append(part_gates)
        ranks.append(rank + cnt)
        cnt = cnt + jnp.sum(sel, axis=0, keepdims=True)

    seg = jnp.floor((cnt + (SUBLANES - 1)) * (1.0 / SUBLANES)) * SUBLANES
    ek = lax.broadcasted_iota(I32, (LANES, LANES), 0)
    el = lax.broadcasted_iota(I32, (LANES, LANES), 1)
    lower_experts = jnp.where(ek < el, 1.0, 0.0).astype(BF16)
    seg_start = _dot(jnp.broadcast_to(seg, (SUBLANES, LANES)).astype(BF16), lower_experts)[0:1, :]
    for p in range(parts):
        local = ranks[p] + seg_start
        lr1 = jnp.sum(jnp.where(picks[p][0], local, 0.0), axis=-1, keepdims=True)
        lr2 = jnp.sum(jnp.where(picks[p][1], local, 0.0), axis=-1, keepdims=True)
        w1, w2 = gates[p]
        route_ref[p * pr:(p + 1) * pr, :] = jnp.where(
            lane == 0, lr1, jnp.where(lane == 1, lr2, jnp.where(lane == 2, w1, jnp.where(lane == 3, w2, 0.0))))
    counts_ref[0] = jnp.broadcast_to(cnt, (SUBLANES, LANES))


def _oproj_router(o_p, o_s, h2_p, h2_s, w):
    d = h2_p.shape[1]
    n = h2_p.shape[0] + h2_s.shape[0]
    tm = ROW_TILE
    row = lambda width: pl.BlockSpec((tm, width), lambda i: (i, 0))
    in_specs = (_two_part_specs(o_p.shape[0], o_s.shape[0], d) + _two_part_specs(h2_p.shape[0], h2_s.shape[0], d) + [
        _const_spec((d, d)), _const_spec((1, d)), _const_spec((1, d)), _const_spec((3 * d, LANES))])
    return pl.pallas_call(
        _oproj_router_kernel,
        grid=(n // tm,),
        in_specs=in_specs,
        out_specs=[row(d), row(d), row(LANES), pl.BlockSpec((1, SUBLANES, LANES), lambda i: (i, 0, 0))],
        out_shape=[jax.ShapeDtypeStruct((n, d), F32), jax.ShapeDtypeStruct((n, d), BF16),
                   jax.ShapeDtypeStruct((n, LANES), F32), jax.ShapeDtypeStruct((n // tm, SUBLANES, LANES), F32)],
        compiler_params=_params("arbitrary"),
        name="oproj_router",
    )(o_p, o_s, h2_p, h2_s, w["wo"], w["bo"], w["gf"], w["wr3"])


LOCAL_ROWS = 1152
XS_WIDTH = D_MODEL + LANES
SEG_TABLE = 3 * N_EXPERTS
SEG_PIECES = tuple(SUBLANES << b for b in reversed(range(7)))


def _alternating_start():
    issued = []

    def start(copy):
        copy.start(priority=len(issued) % 2)
        issued.append(copy)

    return start


def _segment_copies(src_ref, src_start, dst_ref, dst_start, length, sem, act):
    for piece in SEG_PIECES:
        done = lax.div(length, 2 * piece) * (2 * piece)

        @pl.when(lax.rem(lax.div(length, piece), 2) != 0)
        def _():
            s = pl.multiple_of(src_start + done, SUBLANES)
            t = pl.multiple_of(dst_start + done, SUBLANES)
            act(pltpu.make_async_copy(src_ref.at[pl.ds(s, piece), :], dst_ref.at[pl.ds(t, piece), :], sem))


def _sort_place_kernel(tbl_ref, tail_ref, nu_ref, xn_ref, route_ref, xs_ref, stage, zeros, sems):
    j = pl.program_id(0)
    tm = xn_ref.shape[0]
    xn = xn_ref[...]
    rt = jnp.transpose(route_ref[...])
    row = lax.broadcasted_iota(I32, (LOCAL_ROWS, tm), 0).astype(F32)
    m1 = row == rt[0:1, :]
    m2 = row == rt[1:2, :]
    onehot = jnp.where(m1 | m2, 1.0, 0.0).astype(BF16)
    slot = lax.rem(j, 2)
    buf = stage.at[slot]
    buf[:, 0:D_MODEL] = _dot(onehot, xn)
    gate = jnp.sum(jnp.where(m1, rt[2:3, :], 0.0) + jnp.where(m2, rt[3:4, :], 0.0), axis=-1, keepdims=True)
    buf[:, D_MODEL:XS_WIDTH] = jnp.broadcast_to(gate, (LOCAL_ROWS, LANES))

    def segments(tile, tile_slot, act):
        for e in range(N_EXPERTS):
            base = tile * SEG_TABLE
            _segment_copies(stage.at[tile_slot], tbl_ref[base + e], xs_ref, tbl_ref[base + 2 * N_EXPERTS + e],
                            tbl_ref[base + N_EXPERTS + e], sems.at[tile_slot], act)

    segments(j, slot, _alternating_start())

    @pl.when(j > 0)
    def _():
        segments(j - 1, 1 - slot, lambda cp: cp.wait())

    @pl.when(j == pl.num_programs(0) - 1)
    def _():
        segments(j, slot, lambda cp: cp.wait())
        zeros[...] = jnp.zeros_like(zeros)
        sem = sems.at[0]

        def tails(act):
            for e in range(N_EXPERTS):
                _segment_copies(zeros, 0, xs_ref, tail_ref[e], tail_ref[N_EXPERTS + e], sem, act)

        def unused(act):
            def body(i, c):
                t = pl.multiple_of(i * MOE_TILE, MOE_TILE)
                act(pltpu.make_async_copy(zeros, xs_ref.at[pl.ds(t, MOE_TILE), :], sem))
                return c
            lax.fori_loop(nu_ref[0], xs_ref.shape[0] // MOE_TILE, body, 0)

        tails(lambda cp: cp.start())
        unused(lambda cp: cp.start())
        tails(lambda cp: cp.wait())
        unused(lambda cp: cp.wait())


def _sort_place(xn, route, tbl, tail, n_used, p_rows):
    n, d = xn.shape
    tm = ROW_TILE
    grid_spec = pltpu.PrefetchScalarGridSpec(
        num_scalar_prefetch=3,
        grid=(n // tm,),
        in_specs=[pl.BlockSpec((tm, d), lambda j, *_: (j, 0)),
                  pl.BlockSpec((tm, LANES), lambda j, *_: (j, 0))],
        out_specs=pl.BlockSpec(memory_space=pl.ANY),
        scratch_shapes=[pltpu.VMEM((2, LOCAL_ROWS, XS_WIDTH), F32), pltpu.VMEM((MOE_TILE, XS_WIDTH), F32),
                        pltpu.SemaphoreType.DMA((2,))],
    )
    return pl.pallas_call(
        _sort_place_kernel,
        grid_spec=grid_spec,
        out_shape=jax.ShapeDtypeStruct((p_rows, XS_WIDTH), F32),
        compiler_params=_params("arbitrary"),
        name="moe_sort_place",
    )(tbl, tail, n_used, xn, route)


def _moe_kernel(te_ref, nu_ref, xs_ref, wg_ref, wu_ref, wd_ref, y_ref):
    i = pl.program_id(0)
    c = pl.program_id(1)
    used = i < nu_ref[0]

    @pl.when(used)
    def _():
        xn = xs_ref[:, 0:D_MODEL].astype(BF16)
        g = _dot(xn, wg_ref[0])
        up = _dot(xn, wu_ref[0])
        mid = (g * _sigmoid(g) * up).astype(BF16)
        y = _dot(mid, wd_ref[0]) * xs_ref[:, D_MODEL:D_MODEL + 1]

        @pl.when(c == 0)
        def _():
            y_ref[...] = y

        @pl.when(c > 0)
        def _():
            y_ref[...] = y_ref[...] + y

    @pl.when(jnp.logical_not(used) & (c == 0))
    def _():
        y_ref[...] = jnp.zeros_like(y_ref)


def _moe_experts(xs, tile_expert, n_used, wg, wu, wd):
    p_rows = xs.shape[0]
    d = D_MODEL
    tm = MOE_TILE
    n_tiles = p_rows // tm
    ch = D_EXPERT // MOE_CHUNKS
    last = MOE_CHUNKS - 1

    def chunk(i, c, nu):
        return jnp.where(i < nu[0], c, last)

    grid_spec = pltpu.PrefetchScalarGridSpec(
        num_scalar_prefetch=2,
        grid=(n_tiles, MOE_CHUNKS),
        in_specs=[
            pl.BlockSpec((tm, XS_WIDTH), lambda i, c, te, nu: (jnp.minimum(i, nu[0] - 1), 0)),
            pl.BlockSpec((1, d, ch), lambda i, c, te, nu: (te[i], 0, chunk(i, c, nu))),
            pl.BlockSpec((1, d, ch), lambda i, c, te, nu: (te[i], 0, chunk(i, c, nu))),
            pl.BlockSpec((1, ch, d), lambda i, c, te, nu: (te[i], chunk(i, c, nu), 0)),
        ],
        out_specs=pl.BlockSpec((tm, d), lambda i, c, te, nu: (i, 0)),
    )
    return pl.pallas_call(
        _moe_kernel,
        grid_spec=grid_spec,
        out_shape=jax.ShapeDtypeStruct((p_rows, d), F32),
        compiler_params=_params("arbitrary", "arbitrary"),
        name="moe_experts",
    )(tile_expert, n_used, xs, wg, wu, wd)


def _combine_kernel(tbl_ref, h3_ref, route_ref, ys_ref, g_ref, outp_ref, outs_ref, ybuf, sems):
    j = pl.program_id(0)
    last = pl.num_programs(0) - 1
    tm = h3_ref.shape[0]
    slot = lax.rem(j, 2)

    def segments(tile, tile_slot, act):
        for e in range(N_EXPERTS):
            base = tile * SEG_TABLE
            _segment_copies(ys_ref, tbl_ref[base + 2 * N_EXPERTS + e], ybuf.at[tile_slot], tbl_ref[base + e],
                            tbl_ref[base + N_EXPERTS + e], sems.at[tile_slot], act)

    @pl.when(j == 0)
    def _():
        ybuf[...] = jnp.zeros_like(ybuf)
        segments(j, slot, lambda cp: cp.start())

    @pl.when(j < last)
    def _():
        segments(j + 1, 1 - slot, _alternating_start())

    segments(j, slot, lambda cp: cp.wait())

    route = route_ref[...]
    col = lax.broadcasted_iota(I32, (tm, LOCAL_ROWS), 1).astype(F32)
    picks = jnp.where((col == route[:, 0:1]) | (col == route[:, 1:2]), 1.0, 0.0).astype(BF16)
    h4 = h3_ref[...] + _dot(picks, ybuf[slot].astype(BF16))
    out = _rms(h4, g_ref[...])

    @pl.when(j < last)
    def _():
        outp_ref[...] = out

    @pl.when(j == last)
    def _():
        outs_ref[...] = out


def _combine_final(h3, route, tbl, ys, g_final, n_prompt):
    n, d = h3.shape
    tm = ROW_TILE
    n_sample = n - n_prompt
    assert n_sample == tm and n_prompt % tm == 0
    last_prompt = n_prompt // tm - 1
    grid_spec = pltpu.PrefetchScalarGridSpec(
        num_scalar_prefetch=1,
        grid=(n // tm,),
        in_specs=[pl.BlockSpec((tm, d), lambda j, *_: (j, 0)),
                  pl.BlockSpec((tm, LANES), lambda j, *_: (j, 0)),
                  pl.BlockSpec(memory_space=pl.ANY),
                  pl.BlockSpec((1, d), lambda j, *_: (0, 0))],
        out_specs=[pl.BlockSpec((tm, d), lambda j, *_: (jnp.minimum(j, last_prompt), 0)),
                   pl.BlockSpec((tm, d), lambda j, *_: (0, 0))],
        scratch_shapes=[pltpu.VMEM((2, LOCAL_ROWS, d), F32), pltpu.SemaphoreType.DMA((2,))],
    )
    return pl.pallas_call(
        _combine_kernel,
        grid_spec=grid_spec,
        out_shape=[jax.ShapeDtypeStruct((n_prompt, d), F32), jax.ShapeDtypeStruct((n_sample, d), F32)],
        compiler_params=_params("arbitrary"),
        name="moe_combine",
    )(tbl, h3, route, ys, g_final)


def _row(v):
    return v.reshape(1, -1).astype(F32)


def _routing_tables(counts):
    n_row_tiles = counts.shape[0]
    cnt = counts[:, 0, :N_EXPERTS].astype(I32)
    seg = ((cnt + SUBLANES - 1) // SUBLANES) * SUBLANES
    local_start = jnp.cumsum(seg, axis=1) - seg
    rows = jnp.sum(seg, axis=0)
    padded = ((rows + MOE_TILE - 1) // MOE_TILE) * MOE_TILE
    ends = jnp.cumsum(padded)
    starts = ends - padded
    sorted_start = starts[None, :] + jnp.cumsum(seg, axis=0) - seg
    tbl = jnp.concatenate([local_start, seg, sorted_start], axis=1).reshape(-1).astype(I32)
    tail = jnp.concatenate([starts + rows, padded - rows]).astype(I32)
    max_rows = n_row_tiles * (TOP_K * ROW_TILE + N_EXPERTS * (SUBLANES - 1)) + N_EXPERTS * (MOE_TILE - SUBLANES)
    n_tiles = -(-max_rows // MOE_TILE)
    tile_start = jnp.arange(n_tiles, dtype=I32) * MOE_TILE
    tile_start = jnp.minimum(tile_start, ends[-1] - MOE_TILE)
    tile_expert = jnp.minimum(jnp.sum((tile_start[:, None] >= ends[None, :]).astype(I32), axis=1), N_EXPERTS - 1)
    n_used = (ends[-1] // MOE_TILE).reshape(1).astype(I32)
    return tbl, tail, tile_expert.astype(I32), n_used, n_tiles


def kernel(x_prompt, x_sample, state_conv, state_h, cache_k, cache_v, g_mix, g_ffn, g_kv, g_final, a_w_gate, a_b_gate, a_w_in, a_b_in, a_conv_w, a_conv_b, a_w_r, a_b_r, a_w_i, a_b_i, a_lam, a_w_out, a_b_out, w_kv, b_kv, rel_bias, b_w_q, b_b_q, b_sinks, b_w_o, b_b_o, f_w_gate, f_w_up, f_w_down, m_w_router, m_w_gate, m_w_up, m_w_down):
    bp, seq, d = x_prompt.shape
    bs, steps, _ = x_sample.shape
    n_prompt = bp * seq
    n_sample = bs * steps
    n = n_prompt + n_sample
    assert seq % MIX_TILE == 0 and seq % WINDOW == 0 and n_prompt % ROW_TILE == 0
    assert n_sample == ROW_TILE and bs % SAMPLE_GROUP == 0

    mix_w = dict(g=_row(g_mix[0]), wg=a_w_gate[0].astype(BF16), bg=_row(a_b_gate[0]),
                 win=a_w_in[0].astype(BF16), bin=_row(a_b_in[0]), cw=a_conv_w[0], cb=_row(a_conv_b[0]),
                 wr=a_w_r[0].astype(BF16), br=_row(a_b_r[0]), wi=a_w_i[0].astype(BF16), bi=_row(a_b_i[0]),
                 lam=_row(a_lam[0]), wout=a_w_out[0].astype(BF16), bout=_row(a_b_out[0]))

    ffn_w = dict(gf=_row(g_ffn[0]), wg=f_w_gate[0].astype(BF16), wu=f_w_up[0].astype(BF16),
                 wd=f_w_down[0].astype(BF16), gkv=_row(g_kv), wkv=w_kv.astype(BF16), bkv=_row(b_kv),
                 gq=_row(g_mix[1]), wq=b_w_q[0].astype(BF16), bq=_row(b_b_q[0]))

    h2_p, kv_p, q_p, p_conv, p_h, moe_wg, moe_wu = _layer0_prompt(x_prompt, mix_w, ffn_w, [m_w_gate[0], m_w_up[0]])
    h1_s, s_conv_tm, s_h = _mixer_sample(jnp.transpose(x_sample, (1, 0, 2)),
                                         jnp.transpose(state_conv[0], (1, 0, 2)), state_h[0], mix_w)
    h2_s, kv_s, q_s = _ffn_rows(jnp.transpose(h1_s, (1, 0, 2)).reshape(n_sample, d), ffn_w)

    bias, bias_rows = _bias_band(rel_bias, steps)
    sinks = b_sinks[0].astype(F32)
    o_p, moe_wd = _attn_prompt(q_p, kv_p, bias, sinks, bp, seq, [m_w_down[0]])
    q4 = q_s.reshape(bs, steps, N_KV, GROUP, HEAD_DIM).transpose(0, 2, 1, 3, 4)
    q4 = q4.reshape(bs, N_KV, steps * GROUP, HEAD_DIM)
    kv_new = kv_s.reshape(bs, steps, 2 * KV_DIM)
    sink_rows = jnp.broadcast_to(sinks.reshape(N_KV, 1, GROUP), (N_KV, steps, GROUP)).reshape(N_KV, steps * GROUP, 1)
    o4, s_kt, s_vt = _attn_sample(q4, jnp.transpose(cache_k, (0, 2, 3, 1)), jnp.transpose(cache_v, (0, 2, 3, 1)),
                                  kv_new, jnp.transpose(kv_new, (0, 2, 1)), bias_rows, sink_rows)
    o_s = o4.reshape(bs, N_KV, steps, GROUP, HEAD_DIM).transpose(0, 2, 1, 3, 4).reshape(n_sample, d)

    wr_pad = jnp.zeros((d, LANES), F32).at[:, :N_EXPERTS].set(m_w_router[0])
    wr_hi = wr_pad.astype(BF16)
    wr_lo = (wr_pad - wr_hi.astype(F32)).astype(BF16)
    h3, xn, route, counts = _oproj_router(
        o_p, o_s, h2_p, h2_s,
        dict(wo=b_w_o[0].astype(BF16), bo=_row(b_b_o[0]), gf=_row(g_ffn[1]),
             wr3=jnp.concatenate([wr_hi, wr_hi, wr_lo], axis=0)))

    tbl, tail, tile_expert, n_used, n_tiles = _routing_tables(counts)
    xs = _sort_place(xn, route, tbl, tail, n_used, n_tiles * MOE_TILE)
    ys = _moe_experts(xs, tile_expert, n_used, moe_wg, moe_wu, moe_wd)
    y_p, y_s = _combine_final(h3, route, tbl, ys, _row(g_final), n_prompt)

    y_prompt = y_p.reshape(bp, seq, d)
    y_sample = y_s.reshape(bs, steps, d)
    kv_last = jnp.stack([kv_p[b * seq + seq - WINDOW:(b + 1) * seq] for b in range(bp)])
    p_k = kv_last[:, :, :KV_DIM].reshape(bp, WINDOW, N_KV, HEAD_DIM)
    p_v = kv_last[:, :, KV_DIM:].reshape(bp, WINDOW, N_KV, HEAD_DIM)
    s_k = jnp.transpose(s_kt, (0, 3, 1, 2))
    s_v = jnp.transpose(s_vt, (0, 3, 1, 2))
    return (y_prompt, y_sample, p_conv[None], p_h.reshape(1, bp, LRU_WIDTH), p_k, p_v,
            jnp.transpose(s_conv_tm, (1, 0, 2))[None], s_h[None], s_k, s_v)
```
